```python
import math
import jax, jax.numpy as jnp
from jax import lax
import numpy as np

D_MODEL = 1024
BATCH = 2
SEQ = 8192
DEPTH = 2
DEC_BATCH = 32
DEC_SEQ = 1
PAST_LEN = 8192
PAGE_SIZE = 128

HA = 4
DKA = 64
DVA = 64
HB = 4
DKB = 32
DVB = 64
HC = 4
DKC = 64
DVC = 64
HD = 4
DKD = 64
DVD = 64
MIX_WIDTH = HA * DVA + HB * DVB + HC * DVC + HD * DVD
CONV_W = 4
GDN_CONV_CH = 2 * HA * DKA + HA * DVA
CHUNK = 64
Q_BLOCK = 128
NUM_BUCKETS = 32
MAX_DISTANCE = 128
D_FF = 2816
N_EXPERTS = 8
TOP_K = 2
D_FF_EXPERT = 2816
N_DENSE = (DEPTH + 1) // 2
N_MOE = DEPTH // 2
EPS = 1e-6
IN_SPLITS = (HA * DKA, HA * DKA, HA * DVA, HA, HA, HA * DVA,
             HB * 2 * DKB, HB * 2 * DKB, HB * DVB,
             HC * DKC, HC * DKC, HC * DVC, HC * DVC,
             HD * DKD, HD * DKD, HD * DVD, HD, HD, HD * DVD)
D_IN = sum(IN_SPLITS)

kernel_name = 'hybrid_gdn_diffattn_hgrn2_mlstm_step'


def _rmsnorm(x, g):
    xf = x.astype(jnp.float32)
    y = xf * lax.rsqrt(jnp.mean(xf * xf, axis=-1, keepdims=True) + EPS)
    return (y * g.astype(jnp.float32)).astype(x.dtype)


def _l2norm(x):
    return x * lax.rsqrt(jnp.sum(x * x, axis=-1, keepdims=True) + EPS)


def _swiglu(h, wg, wu, wd):
    return (jax.nn.silu(h @ wg) * (h @ wu)) @ wd


def _short_conv(u, buf, w):
    L = u.shape[1]
    up = jnp.concatenate([buf.astype(u.dtype), u], axis=1)
    out = up[:, 0:L] * w[0]
    for j in range(1, CONV_W):
        out = out + up[:, j:j + L] * w[j]
    return jax.nn.silu(out), up[:, L:]


def _tri(C, k=0):
    return jnp.tril(jnp.ones((C, C), dtype=bool), k)


def _to_chunks(x, C, n, fill):
    pad = n * C - x.shape[1]
    x = jnp.pad(x, [(0, 0), (0, pad)] + [(0, 0)] * (x.ndim - 2), constant_values=fill)
    return jnp.moveaxis(x.reshape(x.shape[0], n, C, *x.shape[2:]), 1, 0)


def _from_chunks(y, L):
    y = jnp.moveaxis(y, 0, 1)
    return y.reshape(y.shape[0], -1, *y.shape[3:])[:, :L]


def _chunked_scan(fn, carry, xs, fills):
    L = xs[0].shape[1]
    C = min(CHUNK, L)
    n = -(-L // C)
    xs_c = tuple(_to_chunks(x, C, n, f) for x, f in zip(xs, fills))
    carry, ys = lax.scan(fn, carry, xs_c)
    return _from_chunks(ys, L), carry


def _gdn_chunk(S, inp):
    q, k, v, beta, g = inp
    C = q.shape[1]
    G = jnp.moveaxis(jnp.cumsum(g, axis=1), 1, 2)
    bh = jnp.moveaxis(beta, 1, 2)
    decay = jnp.exp(jnp.where(_tri(C), G[..., :, None] - G[..., None, :], -jnp.inf))
    M = bh[..., :, None] * jnp.einsum('bthd,bshd->bhts', k, k) * jnp.where(_tri(C, -1), decay, 0.0)
    eG = jnp.exp(G)
    rhs = bh[..., None] * (jnp.moveaxis(v, 1, 2) - eG[..., None] * jnp.einsum('bthd,bhde->bhte', k, S))
    U = lax.linalg.triangular_solve(M, rhs, left_side=True, lower=True, unit_diagonal=True)
    o = eG[..., None] * jnp.einsum('bthd,bhde->bhte', q, S) + jnp.einsum('bhts,bhse->bhte', jnp.einsum('bthd,bshd->bhts', q, k) * decay, U)
    G_last = G[..., -1]
    S_new = jnp.exp(G_last)[..., None, None] * S + jnp.einsum('bhs,bshd,bhse->bhde', jnp.exp(G_last[..., None] - G), k, U)
    return S_new, jnp.moveaxis(o, 1, 2)


def _hgrn_chunk(S, inp):
    q, k, v, lf = inp
    C = q.shape[1]
    b = jnp.cumsum(lf, axis=1)
    decay = jnp.exp(jnp.where(_tri(C)[None, :, :, None, None], b[:, :, None] - b[:, None, :], -jnp.inf))
    A = jnp.einsum('bthd,bshd,btshd->bhts', q, k, decay)
    o = jnp.einsum('bthd,bhde->bthe', q * jnp.exp(b), S) + jnp.einsum('bhts,bshe->bthe', A, v)
    b_last = b[:, -1]
    S_new = jnp.exp(b_last)[..., None] * S + jnp.einsum('bshd,bshe->bhde', k * jnp.exp(b_last[:, None] - b), v)
    return S_new, o


def _mlstm_chunk(carry, inp):
    Cs, n, m = carry
    q, k, v, li, lf = inp
    C = q.shape[1]
    b = jnp.moveaxis(jnp.cumsum(lf, axis=1), 1, 2)
    lih = jnp.moveaxis(li, 1, 2)
    D = jnp.where(_tri(C), b[..., :, None] - b[..., None, :] + lih[..., None, :], -jnp.inf)
    inter = b + m[..., None]
    m_t = jnp.maximum(inter, jnp.max(D, axis=-1))
    w_inter = jnp.exp(inter - m_t)
    qk = jnp.einsum('bthd,bshd->bhts', q, k) * jnp.exp(D - m_t[..., None])
    num = w_inter[..., None] * jnp.einsum('bthd,bhde->bhte', q, Cs) + jnp.einsum('bhts,bshe->bhte', qk, v)
    den = w_inter * jnp.einsum('bthd,bhd->bht', q, n) + jnp.sum(qk, axis=-1)
    h = num / jnp.maximum(jnp.abs(den), jnp.exp(-m_t))[..., None]
    m_new = m_t[..., -1]
    w_end = jnp.exp(b[..., -1:] - b + lih - m_new[..., None])
    d0 = jnp.exp(b[..., -1] + m - m_new)
    C_new = d0[..., None, None] * Cs + jnp.einsum('bhs,bshd,bshe->bhde', w_end, k, v)
    n_new = d0[..., None] * n + jnp.einsum('bhs,bshd->bhd', w_end, k)
    return (C_new, n_new, m_new), jnp.moveaxis(h, 1, 2)


def _t5_bucket(rel):
    n = jnp.maximum(rel, 0)
    max_exact = NUM_BUCKETS // 2
    nf = jnp.maximum(n, 1).astype(jnp.float32)
    large = max_exact + (jnp.log(nf / max_exact) / math.log(MAX_DISTANCE / max_exact) * (NUM_BUCKETS - max_exact)).astype(jnp.int32)
    return jnp.where(n < max_exact, n, jnp.minimum(large, NUM_BUCKETS - 1))


def _diff_attend(q, k, v, q_pos, k_pos, lam, rel_bias):
    s = jnp.einsum('bqhcd,bkhcd->bhcqk', q, k) * DKB ** -0.5
    bias = rel_bias.astype(jnp.float32)[_t5_bucket(q_pos[:, None] - k_pos[None, :])]
    s = s + jnp.transpose(bias, (2, 0, 1))[None, :, None]
    s = jnp.where(k_pos[None, :] <= q_pos[:, None], s, -jnp.inf)
    p = jax.nn.softmax(s, axis=-1)
    attn = p[:, :, 0] - lam * p[:, :, 1]
    return jnp.einsum('bhqk,bkhe->bqhe', attn, v)


def _diff_prompt(q, k, v, lam, rel_bias):
    B, L = q.shape[:2]
    nb = L // Q_BLOCK
    qb = jnp.moveaxis(q.reshape(B, nb, Q_BLOCK, HB, 2, DKB), 1, 0)
    k_pos = jnp.arange(L)

    def one_block(args):
        q_blk, i = args
        return _diff_attend(q_blk, k, v, i * Q_BLOCK + jnp.arange(Q_BLOCK), k_pos, lam, rel_bias)

    o = lax.map(one_block, (qb, jnp.arange(nb)))
    return jnp.moveaxis(o, 0, 1).reshape(B, L, HB, DVB)


def _hgrn_lower_bounds(logits):
    p = jax.nn.softmax(logits.astype(jnp.float32), axis=0)
    cum = jnp.cumsum(p, axis=0)
    return cum - cum[0:1]


def _token_mixers(h, layer, kv_past, conv0, S_gdn0, S_hgrn0, C0, n0, m0,
                  w_in_l, conv_w_l, a_log_l, dt_bias_l, gdn_norm_l,
                  qk_norm_l, lambda_l, subln_l, rel_bias, lb_l, hgrn_norm_l,
                  i_bias_l, f_bias_l, mlstm_norm_l):
    dt = h.dtype
    f32 = jnp.float32
    B, L, _ = h.shape
    cuts = [int(c) for c in np.cumsum(IN_SPLITS)[:-1]]
    (aq, ak, av, aa, ab, ag, bq, bk, bv, cq, cf, ci, cg,
     dq, dk, dv, di, df, dgo) = jnp.split(h @ w_in_l, cuts, axis=-1)

    qkv, conv_new = _short_conv(jnp.concatenate([aq, ak, av], axis=-1), conv0, conv_w_l)
    qkv = qkv.astype(f32)
    qa = _l2norm(qkv[..., :HA * DKA].reshape(B, L, HA, DKA)) * DKA ** -0.5
    ka = _l2norm(qkv[..., HA * DKA:2 * HA * DKA].reshape(B, L, HA, DKA))
    va = qkv[..., 2 * HA * DKA:].reshape(B, L, HA, DVA)
    beta = jax.nn.sigmoid(ab.astype(f32))
    log_alpha = -jnp.exp(a_log_l.astype(f32)) * jax.nn.softplus(aa.astype(f32) + dt_bias_l.astype(f32))
    oa, S_gdn = _chunked_scan(_gdn_chunk, S_gdn0.astype(f32), (qa, ka, va, beta, log_alpha), (0.0,) * 5)
    oa = _rmsnorm(oa, gdn_norm_l) * jax.nn.silu(ag.astype(f32)).reshape(B, L, HA, DVA)

    qb = _rmsnorm(bq.astype(f32).reshape(B, L, HB, 2, DKB), qk_norm_l[0])
    kb = _rmsnorm(bk.astype(f32).reshape(B, L, HB, 2, DKB), qk_norm_l[1])
    vb = bv.astype(f32).reshape(B, L, HB, DVB)
    lam_init = 0.8 - 0.6 * math.exp(-0.3 * layer)
    lam32 = lambda_l.astype(f32)
    lam = jnp.exp(jnp.sum(lam32[0] * lam32[1])) - jnp.exp(jnp.sum(lam32[2] * lam32[3])) + lam_init
    if kv_past is None:
        ob = _diff_prompt(qb, kb, vb, lam, rel_bias)
    else:
        past = kv_past[0].shape[1]
        k_all = jnp.concatenate([kv_past[0].astype(f32), kb], axis=1)
        v_all = jnp.concatenate([kv_past[1].astype(f32), vb], axis=1)
        ob = _diff_attend(qb, k_all, v_all, past + jnp.arange(L), jnp.arange(past + L), lam, rel_bias)
    ob = _rmsnorm(ob, subln_l) * (1.0 - lam_init)

    z = cf.astype(f32)
    lb = lb_l.astype(f32)
    log_f = jnp.log(lb + (1.0 - lb) * jax.nn.sigmoid(z)).reshape(B, L, HC, DKC)
    kc = ((1.0 - lb) * jax.nn.sigmoid(-z)).reshape(B, L, HC, DKC)
    qc = jax.nn.silu(cq.astype(f32)).reshape(B, L, HC, DKC)
    vc = ci.astype(f32).reshape(B, L, HC, DVC)
    oc, S_hgrn = _chunked_scan(_hgrn_chunk, S_hgrn0.astype(f32), (qc, kc, vc, log_f), (0.0,) * 4)
    oc = _rmsnorm(oc, hgrn_norm_l) * jax.nn.sigmoid(cg.astype(f32)).reshape(B, L, HC, DVC)

    qd = dq.astype(f32).reshape(B, L, HD, DKD)
    kd = dk.astype(f32).reshape(B, L, HD, DKD) * DKD ** -0.5
    vd = dv.astype(f32).reshape(B, L, HD, DVD)
    log_i = di.astype(f32) + i_bias_l.astype(f32)
    log_fd = jax.nn.log_sigmoid(df.astype(f32) + f_bias_l.astype(f32))
    od, (C_new, n_new, m_new) = _chunked_scan(
        _mlstm_chunk, (C0.astype(f32), n0.astype(f32), m0.astype(f32)),
        (qd, kd, vd, log_i, log_fd), (0.0, 0.0, 0.0, -jnp.inf, 0.0))
    od = _rmsnorm(od, mlstm_norm_l) * jax.nn.sigmoid(dgo.astype(f32)).reshape(B, L, HD, DVD)

    mix = jnp.concatenate([oa.reshape(B, L, -1), ob.reshape(B, L, -1),
                           oc.reshape(B, L, -1), od.reshape(B, L, -1)], axis=-1).astype(dt)
    new_state = (kb.astype(dt), vb.astype(dt), conv_new, S_gdn.astype(dt), S_hgrn.astype(dt),
                 C_new.astype(dt), n_new.astype(dt), m_new.astype(dt))
    return mix, new_state


def _moe(h, router, wg, wu, wd):
    logits = (h @ router).astype(jnp.float32)
    top_v, top_i = lax.top_k(logits, TOP_K)
    gates = jax.nn.softmax(top_v, axis=-1)
    comb = jnp.einsum('blk,blke->ble', gates, jax.nn.one_hot(top_i, N_EXPERTS, dtype=jnp.float32)).astype(h.dtype)
    out = jnp.zeros_like(h)
    for e in range(N_EXPERTS):
        out = out + comb[..., e:e + 1] * _swiglu(h, wg[e], wu[e], wd[e])
    return out


def setup_inputs(seed: int = 0) -> dict:
    key = jax.random.key(seed)
    ks = jax.random.split(key, 40)
    f32 = jnp.float32

    def nrm(i, shape, scale):
        return jax.random.normal(ks[i], shape, f32) * scale

    def gain(i, shape):
        return 1.0 + nrm(i, shape, 0.02)

    n_pages = PAST_LEN // PAGE_SIZE
    n_used = DEC_BATCH * n_pages
    n_pool = (n_used * 5) // 4
    page_table = jax.random.permutation(ks[2], n_pool)[:n_used].reshape(DEC_BATCH, n_pages).astype(jnp.int32)
    dt_init = jnp.exp(jax.random.uniform(ks[14], (DEPTH, HA), f32, math.log(1e-3), math.log(1e-1)))
    return {
        'x_prompt': nrm(0, (BATCH, SEQ, D_MODEL), 1.0),
        'x_sample': nrm(1, (DEC_BATCH, DEC_SEQ, D_MODEL), 1.0),
        'page_table': page_table,
        'cache_k': nrm(3, (DEPTH, n_pool, PAGE_SIZE, HB, 2, DKB), 1.0),
        'cache_v': nrm(4, (DEPTH, n_pool, PAGE_SIZE, HB, DVB), 1.0),
        'state_gdn_conv': nrm(5, (DEPTH, DEC_BATCH, CONV_W - 1, GDN_CONV_CH), 1.0),
        'state_gdn': nrm(6, (DEPTH, DEC_BATCH, HA, DKA, DVA), 0.3),
        'state_hgrn': nrm(7, (DEPTH, DEC_BATCH, HC, DKC, DVC), 0.3),
        'state_mlstm_C': nrm(8, (DEPTH, DEC_BATCH, HD, DKD, DVD), 0.3),
        'state_mlstm_n': jnp.abs(nrm(9, (DEPTH, DEC_BATCH, HD, DKD), 0.5)),
        'state_mlstm_m': nrm(10, (DEPTH, DEC_BATCH, HD), 1.0),
        'attn_norm_g': gain(11, (DEPTH, D_MODEL)),
        'w_in': nrm(12, (DEPTH, D_MODEL, D_IN), D_MODEL ** -0.5),
        'gdn_conv_w': nrm(13, (DEPTH, CONV_W, GDN_CONV_CH), CONV_W ** -0.5),
        'gdn_a_log': jnp.log(jax.random.uniform(ks[15], (DEPTH, HA), f32, 1.0, 16.0)),
        'gdn_dt_bias': dt_init + jnp.log(-jnp.expm1(-dt_init)),
        'gdn_norm_g': gain(16, (DEPTH, DVA)),
        'diff_qk_norm_g': gain(17, (DEPTH, 2, DKB)),
        'diff_lambda': nrm(18, (DEPTH, 4, DKB), 0.1),
        'diff_subln_g': gain(19, (DEPTH, DVB)),
        'rel_bias': nrm(20, (NUM_BUCKETS, HB), 0.5),
        'hgrn_lb_logits': nrm(21, (DEPTH, HC * DKC), 1.0),
        'hgrn_norm_g': gain(22, (DEPTH, DVC)),
        'mlstm_i_bias': nrm(23, (DEPTH, HD), 0.5),
        'mlstm_f_bias': 3.0 + nrm(24, (DEPTH, HD), 0.5),
        'mlstm_norm_g': gain(25, (DEPTH, DVD)),
        'w_out': nrm(26, (DEPTH, MIX_WIDTH, D_MODEL), MIX_WIDTH ** -0.5),
        'ffn_norm_g': gain(27, (DEPTH, D_MODEL)),
        'ffn_w_gate': nrm(28, (N_DENSE, D_MODEL, D_FF), D_MODEL ** -0.5),
        'ffn_w_up': nrm(29, (N_DENSE, D_MODEL, D_FF), D_MODEL ** -0.5),
        'ffn_w_down': nrm(30, (N_DENSE, D_FF, D_MODEL), D_FF ** -0.5),
        'moe_router': nrm(31, (N_MOE, D_MODEL, N_EXPERTS), D_MODEL ** -0.5),
        'moe_w_gate': nrm(32, (N_MOE, N_EXPERTS, D_MODEL, D_FF_EXPERT), D_MODEL ** -0.5),
        'moe_w_up': nrm(33, (N_MOE, N_EXPERTS, D_MODEL, D_FF_EXPERT), D_MODEL ** -0.5),
        'moe_w_down': nrm(34, (N_MOE, N_EXPERTS, D_FF_EXPERT, D_MODEL), D_FF_EXPERT ** -0.5),
    }


def reference(x_prompt, x_sample, page_table, cache_k, cache_v, state_gdn_conv, state_gdn, state_hgrn,
              state_mlstm_C, state_mlstm_n, state_mlstm_m,
              attn_norm_g, w_in, gdn_conv_w, gdn_a_log, gdn_dt_bias, gdn_norm_g,
              diff_qk_norm_g, diff_lambda, diff_subln_g, rel_bias,
              hgrn_lb_logits, hgrn_norm_g, mlstm_i_bias, mlstm_f_bias, mlstm_norm_g,
              w_out, ffn_norm_g, ffn_w_gate, ffn_w_up, ffn_w_down,
              moe_router, moe_w_gate, moe_w_up, moe_w_down):
    hgrn_lb = _hgrn_lower_bounds(hgrn_lb_logits)
    n_pages = page_table.shape[1]
    dt = x_prompt.dtype
    Bp = x_prompt.shape[0]
    Bs = x_sample.shape[0]

    def run_layer(x, l, kv_past, st):
        mix, new_st = _token_mixers(
            _rmsnorm(x, attn_norm_g[l]), l, kv_past, *st,
            w_in[l], gdn_conv_w[l], gdn_a_log[l], gdn_dt_bias[l], gdn_norm_g[l],
            diff_qk_norm_g[l], diff_lambda[l], diff_subln_g[l], rel_bias,
            hgrn_lb[l], hgrn_norm_g[l], mlstm_i_bias[l], mlstm_f_bias[l], mlstm_norm_g[l])
        x = x + mix @ w_out[l]
        h = _rmsnorm(x, ffn_norm_g[l])
        if l % 2 == 0:
            x = x + _swiglu(h, ffn_w_gate[l // 2], ffn_w_up[l // 2], ffn_w_down[l // 2])
        else:
            x = x + _moe(h, moe_router[l // 2], moe_w_gate[l // 2], moe_w_up[l // 2], moe_w_down[l // 2])
        return x, new_st

    zero_st = (jnp.zeros((Bp, CONV_W - 1, GDN_CONV_CH), dt), jnp.zeros((Bp, HA, DKA, DVA), dt),
               jnp.zeros((Bp, HC, DKC, DVC), dt), jnp.zeros((Bp, HD, DKD, DVD), dt),
               jnp.zeros((Bp, HD, DKD), dt), jnp.zeros((Bp, HD), dt))
    xp = x_prompt
    xs = x_sample
    p_states = []
    s_states = []
    for l in range(DEPTH):
        xp, st_p = run_layer(xp, l, None, zero_st)
        p_states.append(st_p)
        k_past = cache_k[l][page_table].reshape(Bs, n_pages * PAGE_SIZE, HB, 2, DKB)
        v_past = cache_v[l][page_table].reshape(Bs, n_pages * PAGE_SIZE, HB, DVB)
        st_in = (state_gdn_conv[l], state_gdn[l], state_hgrn[l],
                 state_mlstm_C[l], state_mlstm_n[l], state_mlstm_m[l])
        xs, st_s = run_layer(xs, l, (k_past, v_past), st_in)
        s_states.append(st_s)
    kp, vp, convp, gdnp, hgrnp, mCp, mnp, mmp = [jnp.stack(z) for z in zip(*p_states)]
    ks_, vs_, convs, gdns, hgrns, mCs, mns, mms = [jnp.stack(z) for z in zip(*s_states)]
    return (xp, xs, kp, vp, ks_, vs_, convp, convs, gdnp, gdns, hgrnp, hgrns, mCp, mCs, mnp, mns, mmp, mms)
```

```python
import functools
import math

import numpy as np
import jax
import jax.numpy as jnp
from jax import lax
from jax.experimental import pallas as pl
from jax.experimental.pallas import tpu as pltpu

f32 = jnp.float32
bf16 = jnp.bfloat16

D_MODEL = 1024
N_HEADS = 4
HEAD_W = 64
GROUP_W = N_HEADS * HEAD_W
DKB = 32
CONV_W = 4
CHUNK = 64
SUB = 16
NUM_BUCKETS = 32
MAX_DISTANCE = 128
N_EXPERTS = 8
EPS = 1e-6
NEG = -1e30
P_COLS = 4096
GATE_COL = 3840
VMEM_LIMIT = 56 * 1024 * 1024

NN = ((1,), (0,))
NT = ((1,), (1,))
TN = ((0,), (0,))


def _dg(a, b, dims=NN):
    return lax.dot_general(a, b, (dims, ((), ())), preferred_element_type=f32)


def _split(a):
    hi = a.astype(bf16)
    lo = (a - hi.astype(f32)).astype(bf16)
    return hi, lo


def _mm3(a, b, dims=NN):
    ah, al = _split(a)
    bh, bl = _split(b)
    return _dg(ah, bh, dims) + (_dg(ah, bl, dims) + _dg(al, bh, dims))


def _mm2(a, b01, dims=NN):
    ah, al = _split(a)
    return _dg(ah, b01, dims) + _dg(al, b01, dims)


def _mm2l(a01, b, dims=NN):
    bh, bl = _split(b)
    return _dg(a01, bh, dims) + _dg(a01, bl, dims)


def _mm1(a, b, dims=NN):
    return _dg(a.astype(bf16), b.astype(bf16), dims)


def _iota(shape, dim):
    return lax.broadcasted_iota(jnp.int32, shape, dim)


def _group_ones(width, group):
    r = _iota((width, width), 0) // group
    c = _iota((width, width), 1) // group
    return (r == c).astype(bf16)


def _group_sum(x, group):
    return _mm2(x, _group_ones(x.shape[-1], group))


def _silu(x):
    return x * jax.nn.sigmoid(x)


def _softplus(x):
    return jnp.maximum(x, 0.0) + jnp.log1p(jnp.exp(-jnp.abs(x)))


def _stack_cols(xc, n=N_HEADS, rows=HEAD_W):
    return jnp.concatenate([xc[:, h:h + 1] for h in range(n)], axis=0)


def _expand_cols(xc, n=N_HEADS, width=HEAD_W):
    r = xc.shape[0]
    return jnp.concatenate([jnp.broadcast_to(xc[:, h:h + 1], (r, width)) for h in range(n)], axis=1)


def _cat_rows(xr, lo, n=N_HEADS, width=HEAD_W):
    return jnp.concatenate([xr[h:h + 1, lo:lo + width] for h in range(n)], axis=1)


def _head_stack(x, n=N_HEADS, width=HEAD_W):
    lane_head = _iota(x.shape, 1) // width
    return jnp.concatenate([jnp.where(lane_head == h, x, 0.0) for h in range(n)], axis=0)


def _fold_heads(x_sm, n=N_HEADS):
    r = x_sm.shape[0] // n
    out = x_sm[0:r]
    for h in range(1, n):
        out = out + x_sm[h * r:(h + 1) * r]
    return out


def _bd_masks(n=GROUP_W, blk=CHUNK):
    r = _iota((n, n), 0)
    c = _iota((n, n), 1)
    same = (r // blk) == (c // blk)
    lower = same & ((r % blk) >= (c % blk))
    strict = same & ((r % blk) > (c % blk))
    return same, lower, strict


def _cparams(*sem):
    return pltpu.CompilerParams(dimension_semantics=sem, vmem_limit_bytes=VMEM_LIMIT)


def _inproj_kernel(x_ref, g_ref, w_ref, wgt_ref, p_ref, gt_ref, h_scr):
    @pl.when(pl.program_id(1) == 0)
    def _():
        x = x_ref[...]
        ms = jnp.mean(x * x, axis=-1, keepdims=True)
        h = ((x * lax.rsqrt(ms + EPS)) * g_ref[...]).astype(bf16)
        h_scr[...] = h
        gt_ref[...] = _dg(wgt_ref[...], h, NT)
    p_ref[...] = _dg(h_scr[...], w_ref[...], NN)


def _inproj(x, g, w_perm, w_gate_t):
    t = x.shape[0]
    tm = min(512, t)
    tn = 1024
    return pl.pallas_call(
        _inproj_kernel,
        grid=(t // tm, P_COLS // tn),
        in_specs=[
            pl.BlockSpec((tm, D_MODEL), lambda i, j: (i, 0)),
            pl.BlockSpec((1, D_MODEL), lambda i, j: (0, 0)),
            pl.BlockSpec((D_MODEL, tn), lambda i, j: (0, j)),
            pl.BlockSpec((16, D_MODEL), lambda i, j: (0, 0)),
        ],
        out_specs=[
            pl.BlockSpec((tm, tn), lambda i, j: (i, j)),
            pl.BlockSpec((16, tm), lambda i, j: (0, i)),
        ],
        out_shape=[jax.ShapeDtypeStruct((t, P_COLS), f32), jax.ShapeDtypeStruct((16, t), f32)],
        scratch_shapes=[pltpu.VMEM((tm, D_MODEL), bf16)],
        compiler_params=_cparams("arbitrary", "arbitrary"),
        name="inproj",
    )(x, g.reshape(1, D_MODEL), w_perm, w_gate_t)


def _bprep_kernel(q_ref, k_ref, gq_ref, gk_ref, qn_ref, kn_ref):
    def gnorm(x, g):
        ms = _group_sum(x * x, DKB) * (1.0 / DKB)
        return (x * lax.rsqrt(ms + EPS)) * g
    qn_ref[...] = gnorm(q_ref[...], gq_ref[...])
    kn_ref[...] = gnorm(k_ref[...], gk_ref[...])


def _bprep(p, qk_norm_g):
    t = p.shape[0]
    tm = min(512, t)
    gq = jnp.tile(qk_norm_g[0], GROUP_W // DKB).reshape(1, GROUP_W)
    gk = jnp.tile(qk_norm_g[1], GROUP_W // DKB).reshape(1, GROUP_W)
    return pl.pallas_call(
        _bprep_kernel,
        grid=(t // tm,),
        in_specs=[
            pl.BlockSpec((tm, GROUP_W), lambda i: (i, 4)),
            pl.BlockSpec((tm, GROUP_W), lambda i: (i, 5)),
            pl.BlockSpec((1, GROUP_W), lambda i: (0, 0)),
            pl.BlockSpec((1, GROUP_W), lambda i: (0, 0)),
        ],
        out_specs=[pl.BlockSpec((tm, GROUP_W), lambda i: (i, 0))] * 2,
        out_shape=[jax.ShapeDtypeStruct((t, GROUP_W), f32)] * 2,
        compiler_params=_cparams("arbitrary"),
        name="bprep",
    )(p, p, gq, gk)


def _outproj_kernel(oa_ref, ob_ref, oc_ref, od_ref, ag_ref, cg_ref, dg_ref, x_ref, g_ref, w_ref, y_ref, *, b_scale):
    def gnorm(x, g):
        ms = _group_sum(x * x, HEAD_W) * (1.0 / HEAD_W)
        return (x * lax.rsqrt(ms + EPS)) * g
    g = g_ref[...]
    mixes = (
        gnorm(oa_ref[...], g[0:1]) * _silu(ag_ref[...]),
        gnorm(ob_ref[...], g[1:2]) * b_scale,
        gnorm(oc_ref[...], g[2:3]) * jax.nn.sigmoid(cg_ref[...]),
        gnorm(od_ref[...], g[3:4]) * jax.nn.sigmoid(dg_ref[...]),
    )
    y = x_ref[...]
    for i, m in enumerate(mixes):
        y = y + _dg(m.astype(bf16), w_ref[i * GROUP_W:(i + 1) * GROUP_W, :], NN)
    y_ref[...] = y


def _outproj(oa, ob, oc, od, p, x, gains, w_out, b_scale):
    t = x.shape[0]
    tm = min(512, t)
    row = lambda i: (i, 0)
    return pl.pallas_call(
        functools.partial(_outproj_kernel, b_scale=b_scale),
        grid=(t // tm,),
        in_specs=[
            pl.BlockSpec((tm, GROUP_W), row), pl.BlockSpec((tm, GROUP_W), row),
            pl.BlockSpec((tm, GROUP_W), row), pl.BlockSpec((tm, GROUP_W), row),
            pl.BlockSpec((tm, GROUP_W), lambda i: (i, 3)),
            pl.BlockSpec((tm, GROUP_W), lambda i: (i, 10)),
            pl.BlockSpec((tm, GROUP_W), lambda i: (i, 14)),
            pl.BlockSpec((tm, D_MODEL), row),
            pl.BlockSpec((4, GROUP_W), lambda i: (0, 0)),
            pl.BlockSpec((D_MODEL, D_MODEL), lambda i: (0, 0)),
        ],
        out_specs=pl.BlockSpec((tm, D_MODEL), row),
        out_shape=jax.ShapeDtypeStruct((t, D_MODEL), f32),
        compiler_params=_cparams("arbitrary"),
        name="outproj",
    )(oa, ob, oc, od, p, p, p, x, gains, w_out)


def _ffn_kernel(x_ref, g_ref, wg_ref, wu_ref, wd_ref, y_ref, h_scr, acc_scr):
    f = pl.program_id(1)

    @pl.when(f == 0)
    def _():
        x = x_ref[...]
        ms = jnp.mean(x * x, axis=-1, keepdims=True)
        h_scr[...] = ((x * lax.rsqrt(ms + EPS)) * g_ref[...]).astype(bf16)
        acc_scr[...] = x

    h = h_scr[...]
    a = _silu(_dg(h, wg_ref[...])) * _dg(h, wu_ref[...])
    acc_scr[...] += _dg(a.astype(bf16), wd_ref[...])

    @pl.when(f == pl.num_programs(1) - 1)
    def _():
        y_ref[...] = acc_scr[...]


def _ffn(x, g, wg, wu, wd):
    t = x.shape[0]
    d_ff = wg.shape[1]
    tm = min(512, t)
    tf = d_ff // 2
    return pl.pallas_call(
        _ffn_kernel,
        grid=(t // tm, d_ff // tf),
        in_specs=[
            pl.BlockSpec((tm, D_MODEL), lambda i, f: (i, 0)),
            pl.BlockSpec((1, D_MODEL), lambda i, f: (0, 0)),
            pl.BlockSpec((D_MODEL, tf), lambda i, f: (0, f)),
            pl.BlockSpec((D_MODEL, tf), lambda i, f: (0, f)),
            pl.BlockSpec((tf, D_MODEL), lambda i, f: (f, 0)),
        ],
        out_specs=pl.BlockSpec((tm, D_MODEL), lambda i, f: (i, 0)),
        out_shape=jax.ShapeDtypeStruct((t, D_MODEL), f32),
        scratch_shapes=[pltpu.VMEM((tm, D_MODEL), bf16), pltpu.VMEM((tm, D_MODEL), f32)],
        compiler_params=_cparams("arbitrary", "arbitrary"),
        name="ffn",
    )(x, g.reshape(1, D_MODEL), wg, wu, wd)


def _router_kernel(x_ref, g_ref, r_ref, h_ref, comb_ref):
    x = x_ref[...]
    ms = jnp.mean(x * x, axis=-1, keepdims=True)
    h = (x * lax.rsqrt(ms + EPS)) * g_ref[...]
    hb = h.astype(bf16)
    h_ref[...] = hb
    logits = _dg(hb, r_ref[...])
    lane = _iota(logits.shape, 1)
    logits = jnp.where(lane < N_EXPERTS, logits, -jnp.inf)
    v1 = jnp.max(logits, axis=-1, keepdims=True)
    i1 = jnp.min(jnp.where(logits == v1, lane, 128), axis=-1, keepdims=True)
    rest = jnp.where(lane == i1, -jnp.inf, logits)
    v2 = jnp.max(rest, axis=-1, keepdims=True)
    i2 = jnp.min(jnp.where(rest == v2, lane, 128), axis=-1, keepdims=True)
    e2 = jnp.exp(v2 - v1)
    den = 1.0 + e2
    comb_ref[...] = jnp.where(lane == i1, 1.0 / den, 0.0) + jnp.where(lane == i2, e2 / den, 0.0)


def _router(x, g, router_pad):
    t = x.shape[0]
    tm = min(512, t)
    return pl.pallas_call(
        _router_kernel,
        grid=(t // tm,),
        in_specs=[
            pl.BlockSpec((tm, D_MODEL), lambda i: (i, 0)),
            pl.BlockSpec((1, D_MODEL), lambda i: (0, 0)),
            pl.BlockSpec((D_MODEL, 128), lambda i: (0, 0)),
        ],
        out_specs=[pl.BlockSpec((tm, D_MODEL), lambda i: (i, 0)), pl.BlockSpec((tm, 128), lambda i: (i, 0))],
        out_shape=[jax.ShapeDtypeStruct((t, D_MODEL), bf16), jax.ShapeDtypeStruct((t, 128), f32)],
        compiler_params=_cparams("arbitrary"),
        name="router",
    )(x, g.reshape(1, D_MODEL), router_pad)


def _moe_kernel(x_ref, h_ref, comb_ref, wg_ref, wu_ref, wd_ref, y_ref, acc_scr):
    e = pl.program_id(1)
    f = pl.program_id(2)

    @pl.when((e == 0) & (f == 0))
    def _():
        acc_scr[...] = x_ref[...]

    comb = comb_ref[...]
    cw = jnp.sum(jnp.where(_iota(comb.shape, 1) == e, comb, 0.0), axis=-1, keepdims=True)
    h = h_ref[...]
    a = _silu(_dg(h, wg_ref[...])) * _dg(h, wu_ref[...]) * cw
    acc_scr[...] += _dg(a.astype(bf16), wd_ref[...])

    @pl.when((e == pl.num_programs(1) - 1) & (f == pl.num_programs(2) - 1))
    def _():
        y_ref[...] = acc_scr[...]


def _moe(x, h, comb, wg, wu, wd):
    t = x.shape[0]
    n_e, _, d_ff = wg.shape
    tm = min(512, t)
    tf = d_ff // 2
    return pl.pallas_call(
        _moe_kernel,
        grid=(t // tm, n_e, d_ff // tf),
        in_specs=[
            pl.BlockSpec((tm, D_MODEL), lambda i, e, f: (i, 0)),
            pl.BlockSpec((tm, D_MODEL), lambda i, e, f: (i, 0)),
            pl.BlockSpec((tm, 128), lambda i, e, f: (i, 0)),
            pl.BlockSpec((None, D_MODEL, tf), lambda i, e, f: (e, 0, f)),
            pl.BlockSpec((None, D_MODEL, tf), lambda i, e, f: (e, 0, f)),
            pl.BlockSpec((None, tf, D_MODEL), lambda i, e, f: (e, f, 0)),
        ],
        out_specs=pl.BlockSpec((tm, D_MODEL), lambda i, e, f: (i, 0)),
        out_shape=jax.ShapeDtypeStruct((t, D_MODEL), f32),
        scratch_shapes=[pltpu.VMEM((tm, D_MODEL), f32)],
        compiler_params=_cparams("arbitrary", "arbitrary", "arbitrary"),
        name="moe",
    )(x, h, comb, wg, wu, wd)


def _t5_bucket_np(n):
    n = np.maximum(n, 0)
    max_exact = NUM_BUCKETS // 2
    nf = np.maximum(n, 1).astype(np.float32)
    large = max_exact + (np.log(nf / np.float32(max_exact)) / np.float32(math.log(MAX_DISTANCE / max_exact))
                         * np.float32(NUM_BUCKETS - max_exact)).astype(np.int32)
    return np.where(n < max_exact, n, np.minimum(large, NUM_BUCKETS - 1))


def _shifted_bias(rel_bias):
    rb = rel_bias.astype(f32)
    return rb - rb[NUM_BUCKETS - 1:NUM_BUCKETS]


def _attn_kernel(qi_ref, kj_ref, lam_ref, q_ref, k_ref, v_ref, toep_ref, o_ref, qs_scr, m_scr, l_scr, acc_scr, *, tq):
    p = pl.program_id(1)
    i = qi_ref[p]
    j = kj_ref[p]
    n_hc = 2 * N_HEADS

    @pl.when(j == 0)
    def _():
        q = q_ref[...]
        lane_grp = _iota(q.shape, 1) // DKB
        for hc in range(n_hc):
            qs_scr[hc * tq:(hc + 1) * tq, :] = jnp.where(lane_grp == hc, q, 0.0).astype(bf16)
        m_scr[...] = jnp.full(m_scr.shape, NEG, f32)
        l_scr[...] = jnp.zeros(l_scr.shape, f32)
        acc_scr[...] = jnp.zeros(acc_scr.shape, f32)

    def step(near):
        k = k_ref[...].astype(bf16)
        v = v_ref[...].astype(bf16)
        s_all = _dg(qs_scr[...], k, NT)
        for hc in range(n_hc):
            s = s_all[hc * tq:(hc + 1) * tq, :] * (DKB ** -0.5)
            if near:
                s = s + toep_ref[(i - j) * N_HEADS + hc // 2]
            m_old = m_scr[hc]
            m_new = jnp.maximum(m_old, jnp.max(s, axis=-1, keepdims=True))
            pexp = jnp.exp(s - m_new)
            alpha = jnp.exp(m_old - m_new)
            l_scr[hc] = alpha * l_scr[hc] + jnp.sum(pexp, axis=-1, keepdims=True)
            acc_scr[hc] = alpha * acc_scr[hc] + _dg(pexp.astype(bf16), v, NN)
            m_scr[hc] = m_new

    @pl.when(i - j <= 1)
    def _():
        step(True)

    @pl.when(i - j > 1)
    def _():
        step(False)

    @pl.when(j == i)
    def _():
        lam = lam_ref[0]
        lane_head = _iota((tq, GROUP_W), 1) // HEAD_W
        out = jnp.zeros((tq, GROUP_W), f32)
        for h in range(N_HEADS):
            o0 = acc_scr[2 * h] / l_scr[2 * h]
            o1 = acc_scr[2 * h + 1] / l_scr[2 * h + 1]
            out = jnp.where(lane_head == h, o0 - lam * o1, out)
        o_ref[...] = out


def _attn_prompt(qn, kn, p, lam, rel_bias, n_batch, seq):
    tq = min(256, seq)
    nq = seq // tq
    pairs = [(i, j) for i in range(nq) for j in range(i + 1)]
    qi = jnp.asarray(np.array([a for a, _ in pairs], np.int32))
    kj = jnp.asarray(np.array([b for _, b in pairs], np.int32))
    a = np.arange(tq)[:, None]
    b = np.arange(tq)[None, :]
    dist = np.stack([a - b, tq + a - b])
    idx = _t5_bucket_np(dist)
    tab = _shifted_bias(rel_bias)
    toep = jnp.transpose(jnp.take(tab, jnp.asarray(idx), axis=0), (0, 3, 1, 2))
    toep = jnp.where(jnp.asarray(dist >= 0)[:, None], toep, NEG).reshape(2 * N_HEADS, tq, tq)
    grid_spec = pltpu.PrefetchScalarGridSpec(
        num_scalar_prefetch=2,
        grid=(n_batch, len(pairs)),
        in_specs=[
            pl.BlockSpec(memory_space=pltpu.SMEM),
            pl.BlockSpec((tq, GROUP_W), lambda b_, p_, qi_, kj_: (b_ * nq + qi_[p_], 0)),
            pl.BlockSpec((tq, GROUP_W), lambda b_, p_, qi_, kj_: (b_ * nq + kj_[p_], 0)),
            pl.BlockSpec((tq, GROUP_W), lambda b_, p_, qi_, kj_: (b_ * nq + kj_[p_], 6)),
            pl.BlockSpec((2 * N_HEADS, tq, tq), lambda b_, p_, qi_, kj_: (0, 0, 0)),
        ],
        out_specs=pl.BlockSpec((tq, GROUP_W), lambda b_, p_, qi_, kj_: (b_ * nq + qi_[p_], 0)),
        scratch_shapes=[
            pltpu.VMEM((2 * N_HEADS * tq, GROUP_W), bf16),
            pltpu.VMEM((2 * N_HEADS, tq, 1), f32),
            pltpu.VMEM((2 * N_HEADS, tq, 1), f32),
            pltpu.VMEM((2 * N_HEADS, tq, GROUP_W), f32),
        ],
    )
    return pl.pallas_call(
        functools.partial(_attn_kernel, tq=tq),
        grid_spec=grid_spec,
        out_shape=jax.ShapeDtypeStruct((n_batch * seq, GROUP_W), f32),
        compiler_params=_cparams("arbitrary", "arbitrary"),
        name="attn_prompt",
    )(qi, kj, lam.reshape(1), qn, kn, p, toep)


def _attn_decode_kernel(pt_ref, lam_ref, q_ref, kn_ref, vn_ref, blast_ref, bself_ref, *rest, pg, n_pages):
    k_refs = rest[:pg]
    v_refs = rest[pg:2 * pg]
    o_ref, qs_scr, s_scr, aself_scr, acc_scr = rest[2 * pg:]
    t = pl.program_id(1)
    n_steps = n_pages // pg
    n_hc = 2 * N_HEADS
    page = k_refs[0].shape[0]
    scale = DKB ** -0.5
    rnd = lambda z: z.astype(bf16).astype(f32)

    @pl.when(t == 0)
    def _():
        q = jnp.broadcast_to(q_ref[...], (n_hc, GROUP_W))
        keep = (_iota(q.shape, 1) // DKB) == _iota(q.shape, 0)
        qs_scr[...] = jnp.where(keep, q, 0.0)

    @pl.when(t < n_steps)
    def _():
        qs_b = qs_scr[...].astype(bf16)
        parts = []
        for g in range(pg):
            s = _dg(qs_b, k_refs[g][...].astype(bf16), NT) * scale
            is_last = (t * pg + g) == (n_pages - 1)
            parts.append(s + jnp.where(is_last, blast_ref[...], 0.0))
        s_scr[t] = jnp.concatenate(parts, axis=1)

    @pl.when(t == n_steps - 1)
    def _():
        s_all = s_scr[...]
        s_self = jnp.sum(rnd(qs_scr[...]) * rnd(kn_ref[...]), axis=-1, keepdims=True) * scale + bself_ref[...]
        m = jnp.maximum(jnp.max(jnp.max(s_all, axis=2, keepdims=True), axis=0), s_self)
        p = jnp.exp(s_all - m)
        p_self = jnp.exp(s_self - m)
        l = jnp.sum(jnp.sum(p, axis=2, keepdims=True), axis=0) + p_self
        pn = p / l
        pn_self = p_self / l
        lam = lam_ref[0]
        rows = [pn[:, 2 * h:2 * h + 1, :] - lam * pn[:, 2 * h + 1:2 * h + 2, :] for h in range(N_HEADS)]
        s_scr[...] = jnp.concatenate(rows + [jnp.zeros_like(rows[0])] * N_HEADS, axis=1)
        rows_self = [pn_self[2 * h:2 * h + 1] - lam * pn_self[2 * h + 1:2 * h + 2] for h in range(N_HEADS)]
        aself_scr[...] = jnp.concatenate(rows_self + [jnp.zeros_like(rows_self[0])] * N_HEADS, axis=0)
        acc_scr[...] = jnp.zeros(acc_scr.shape, f32)

    @pl.when(t >= n_steps)
    def _():
        a = s_scr[t - n_steps].astype(bf16)
        acc = acc_scr[...]
        for g in range(pg):
            acc = acc + _dg(a[:, g * page:(g + 1) * page], v_refs[g][...].astype(bf16), NN)
        acc_scr[...] = acc

    @pl.when(t == 2 * n_steps - 1)
    def _():
        o = acc_scr[...] + rnd(aself_scr[...]) * rnd(vn_ref[...])
        lane_head = _iota((1, GROUP_W), 1) // HEAD_W
        out = jnp.zeros((1, GROUP_W), f32)
        for h in range(N_HEADS):
            out = jnp.where(lane_head == h, o[h:h + 1], out)
        o_ref[...] = out


def _attn_decode(qn, kn, vn, page_table, cache_k, cache_v, layer, lam, rel_bias):
    n_b, n_pages = page_table.shape
    page = cache_k.shape[2]
    pg = min(8, n_pages)
    n_steps = n_pages // pg
    past = n_pages * page
    tab = _shifted_bias(rel_bias)
    d_last = past - ((n_pages - 1) * page + np.arange(page))
    blast = jnp.repeat(jnp.take(tab, jnp.asarray(_t5_bucket_np(d_last)), axis=0).T, 2, axis=0)
    bself = jnp.repeat(tab[0].reshape(N_HEADS, 1), 2, axis=0)

    def k_spec(g):
        return pl.BlockSpec((None, None, page, GROUP_W),
                            lambda b_, t_, pt: (layer, pt[b_, jnp.minimum(t_, n_steps - 1) * pg + g], 0, 0))

    def v_spec(g):
        return pl.BlockSpec((None, None, page, GROUP_W),
                            lambda b_, t_, pt: (layer, pt[b_, jnp.maximum(t_ - n_steps, 0) * pg + g], 0, 0))

    row = pl.BlockSpec((None, 1, GROUP_W), lambda b_, t_, pt: (b_, 0, 0))
    grid_spec = pltpu.PrefetchScalarGridSpec(
        num_scalar_prefetch=1,
        grid=(n_b, 2 * n_steps),
        in_specs=[pl.BlockSpec(memory_space=pltpu.SMEM), row, row, row,
                  pl.BlockSpec((2 * N_HEADS, page), lambda b_, t_, pt: (0, 0)),
                  pl.BlockSpec((2 * N_HEADS, 1), lambda b_, t_, pt: (0, 0))]
                 + [k_spec(g) for g in range(pg)] + [v_spec(g) for g in range(pg)],
        out_specs=row,
        scratch_shapes=[
            pltpu.VMEM((2 * N_HEADS, GROUP_W), f32),
            pltpu.VMEM((n_steps, 2 * N_HEADS, pg * page), f32),
            pltpu.VMEM((2 * N_HEADS, 1), f32),
            pltpu.VMEM((2 * N_HEADS, GROUP_W), f32),
        ],
    )
    r3 = lambda z: z.reshape(n_b, 1, GROUP_W)
    out = pl.pallas_call(
        functools.partial(_attn_decode_kernel, pg=pg, n_pages=n_pages),
        grid_spec=grid_spec,
        out_shape=jax.ShapeDtypeStruct((n_b, 1, GROUP_W), f32),
        compiler_params=_cparams("arbitrary", "arbitrary"),
        name="attn_decode",
    )(page_table, lam.reshape(1), r3(qn), r3(kn), r3(vn), blast, bself,
      *([cache_k] * pg), *([cache_v] * pg))
    return out.reshape(n_b, GROUP_W)


def _tri(n, dtype=f32):
    return (_iota((n, n), 0) >= _iota((n, n), 1)).astype(dtype)


def _block_tri_t(n, blk):
    r = _iota((n, n), 0)
    c = _iota((n, n), 1)
    return (((r // blk) == (c // blk)) & (r <= c)).astype(bf16)


def _block_tri(n, blk):
    r = _iota((n, n), 0)
    c = _iota((n, n), 1)
    return (((r // blk) == (c // blk)) & (r >= c)).astype(bf16)


def _gdn_kernel(u_ref, gc_ref, gr_ref, cw_ref, alr_ref, dtr_ref, alc_ref, dtc_ref, o_ref, s_out_ref, ext_scr, s_scr, *, tb):
    i = pl.program_id(1)

    @pl.when(i == 0)
    def _():
        ext_scr[0:8, :] = jnp.zeros((8, 3 * GROUP_W), f32)
        s_scr[...] = jnp.zeros(s_scr.shape, f32)

    ext_scr[8:8 + tb, :] = u_ref[...]
    w = cw_ref[...]
    conv = ext_scr[8:8 + tb, :] * w[3:4]
    for jj in range(1, CONV_W):
        conv = conv + ext_scr[8 - jj:8 - jj + tb, :] * w[3 - jj:4 - jj]
    ext_scr[0:8, :] = ext_scr[tb:tb + 8, :]
    qkv = _silu(conv)

    def l2n(x):
        return x * lax.rsqrt(_group_sum(x * x, HEAD_W) + EPS)

    q = l2n(qkv[:, 0:GROUP_W]) * (HEAD_W ** -0.5)
    k = l2n(qkv[:, GROUP_W:2 * GROUP_W])
    v = qkv[:, 2 * GROUP_W:3 * GROUP_W]

    gc = gc_ref[...]
    g_col = -jnp.exp(alr_ref[...]) * _softplus(gc[:, 0:4] + dtr_ref[...])
    beta_col = jax.nn.sigmoid(gc[:, 4:8])
    gr = gr_ref[...]
    g_row = -jnp.exp(alc_ref[...]) * _softplus(gr[0:4, :] + dtc_ref[...])
    g_row8 = jnp.concatenate([g_row, jnp.zeros_like(g_row)], axis=0)
    gcum_row = _mm2(g_row8, _block_tri_t(tb, CHUNK))

    same, lower, strict = _bd_masks()
    tri = _tri(CHUNK, bf16)
    r = _iota((GROUP_W, GROUP_W), 0)
    c = _iota((GROUP_W, GROUP_W), 1)
    eye = (r == c).astype(f32)

    for ch in range(tb // CHUNK):
        lo = ch * CHUNK
        qc, kc, vc = q[lo:lo + CHUNK], k[lo:lo + CHUNK], v[lo:lo + CHUNK]
        gcum = _mm2l(tri, g_col[lo:lo + CHUNK])
        bcol = beta_col[lo:lo + CHUNK]
        g_stack = _stack_cols(gcum)
        b_stack = _stack_cols(bcol)
        g_cat = _cat_rows(gcum_row, lo)
        decay = jnp.exp(jnp.where(lower, g_stack - g_cat, NEG))
        ksm = _head_stack(kc)
        qsm = _head_stack(qc)
        kk = _mm3(ksm, ksm, NT)
        qk = _mm3(qsm, ksm, NT)
        m_bd = b_stack * kk * jnp.where(strict, decay, 0.0)
        x = eye
        for lev in range(6):
            sel = ((r >> (lev + 1)) == (c >> (lev + 1))) & (((r >> lev) & 1) == 1) & (((c >> lev) & 1) == 0)
            ck = jnp.where(sel, m_bd, 0.0)
            x = x - _mm3(_mm3(x, ck), x)
        s_bd = s_scr[...]
        kq_s = _mm3(jnp.concatenate([kc, qc], axis=0), s_bd)
        ks, qs = kq_s[0:CHUNK], kq_s[CHUNK:2 * CHUNK]
        eg_all = _expand_cols(jnp.exp(gcum))
        rhs = _expand_cols(bcol) * (vc - eg_all * ks)
        u_sm = _mm3(x, _head_stack(rhs))
        o_sm = _mm3(qk * decay, u_sm)
        o_ref[lo:lo + CHUNK, :] = eg_all * qs + _fold_heads(o_sm)
        u_all = _fold_heads(u_sm)
        g_last = gcum[CHUNK - 1:CHUNK, :]
        kw = kc * _expand_cols(jnp.exp(g_last - gcum))
        d_stack = jnp.concatenate(
            [jnp.broadcast_to(jnp.exp(g_last[:, h:h + 1]), (HEAD_W, 1)) for h in range(N_HEADS)], axis=0)
        s_scr[...] = d_stack * s_bd + jnp.where(same, _mm3(kw, u_all, TN), 0.0)

    @pl.when(i == pl.num_programs(1) - 1)
    def _():
        s_out_ref[...] = s_scr[...]


def _gdn_prompt(p, gt, conv_w, a_log, dt_bias, n_batch, seq):
    tb = min(256, seq)
    nb = seq // tb
    r14 = lambda z: z.astype(f32).reshape(1, N_HEADS)
    c41 = lambda z: z.astype(f32).reshape(N_HEADS, 1)
    o, s_bd = pl.pallas_call(
        functools.partial(_gdn_kernel, tb=tb),
        grid=(n_batch, nb),
        in_specs=[
            pl.BlockSpec((tb, 3 * GROUP_W), lambda b, i: (b * nb + i, 0)),
            pl.BlockSpec((tb, 128), lambda b, i: (b * nb + i, GATE_COL // 128)),
            pl.BlockSpec((16, tb), lambda b, i: (0, b * nb + i)),
            pl.BlockSpec((CONV_W, 3 * GROUP_W), lambda b, i: (0, 0)),
            pl.BlockSpec((1, N_HEADS), lambda b, i: (0, 0)),
            pl.BlockSpec((1, N_HEADS), lambda b, i: (0, 0)),
            pl.BlockSpec((N_HEADS, 1), lambda b, i: (0, 0)),
            pl.BlockSpec((N_HEADS, 1), lambda b, i: (0, 0)),
        ],
        out_specs=[
            pl.BlockSpec((tb, GROUP_W), lambda b, i: (b * nb + i, 0)),
            pl.BlockSpec((None, GROUP_W, GROUP_W), lambda b, i: (b, 0, 0)),
        ],
        out_shape=[jax.ShapeDtypeStruct((n_batch * seq, GROUP_W), f32),
                   jax.ShapeDtypeStruct((n_batch, GROUP_W, GROUP_W), f32)],
        scratch_shapes=[pltpu.VMEM((tb + 8, 3 * GROUP_W), f32), pltpu.VMEM((GROUP_W, GROUP_W), f32)],
        compiler_params=_cparams("arbitrary", "arbitrary"),
        name="gdn_prompt",
    )(p, p, gt, conv_w.astype(f32), r14(a_log), r14(dt_bias), c41(a_log), c41(dt_bias))
    return o, _bd_diag(s_bd)


def _bd_diag(s_bd):
    n_b = s_bd.shape[0]
    s5 = s_bd.reshape(n_b, N_HEADS, HEAD_W, N_HEADS, HEAD_W)
    return jnp.stack([s5[:, h, :, h, :] for h in range(N_HEADS)], axis=1)


def _hgrn_kernel(q_ref, f_ref, i_ref, lb_ref, o_ref, s_out_ref, st_scr, q_scr, k_scr, b_scr, *, tb):
    blk = pl.program_id(1)

    @pl.when(blk == 0)
    def _():
        st_scr[...] = jnp.zeros(st_scr.shape, f32)

    lb = lb_ref[...]
    z = f_ref[...]
    logf = jnp.log(lb + (1.0 - lb) * jax.nn.sigmoid(z))
    q_scr[...] = _silu(q_ref[...])
    k_scr[...] = (1.0 - lb) * jax.nn.sigmoid(-z)
    b_scr[...] = _mm2l(_block_tri(tb, SUB), logf)

    same, _, _ = _bd_masks()
    ones_bd = _group_ones(GROUP_W, HEAD_W)
    row = _iota((SUB * SUB, GROUP_W), 0)
    tmask = (row % SUB) >= (row // SUB)

    def rep_t(x):
        return jnp.broadcast_to(x[None], (SUB, SUB, GROUP_W)).reshape(SUB * SUB, GROUP_W)

    def rep_j(x):
        return jnp.broadcast_to(x[:, None, :], (SUB, SUB, GROUP_W)).reshape(SUB * SUB, GROUP_W)

    def body(c, carry):
        r0 = pl.multiple_of(c * SUB, SUB)
        qs = q_scr[pl.ds(r0, SUB), :]
        ks = k_scr[pl.ds(r0, SUB), :]
        vs = i_ref[pl.ds(r0, SUB), :]
        bs = b_scr[pl.ds(r0, SUB), :]
        st = st_scr[...]
        o_inter = _mm3(qs * jnp.exp(bs), st, NT)
        wgt = rep_t(qs) * jnp.exp(jnp.where(tmask, rep_t(bs) - rep_j(bs), NEG)) * rep_j(ks)
        a = _mm2(wgt, ones_bd)
        o_diag = jnp.sum((a * rep_j(vs)).reshape(SUB, SUB, GROUP_W), axis=0)
        o_ref[pl.ds(r0, SUB), :] = o_inter + o_diag
        b_last = bs[SUB - 1:SUB, :]
        kw = ks * jnp.exp(b_last - bs)
        st_scr[...] = st * jnp.exp(b_last) + jnp.where(same, _mm3(vs, kw, TN), 0.0)
        return carry

    lax.fori_loop(0, tb // SUB, body, 0)

    @pl.when(blk == pl.num_programs(1) - 1)
    def _():
        s_out_ref[...] = st_scr[...]


def _hgrn_prompt(p, lb, n_batch, seq):
    tb = min(256, seq)
    nb = seq // tb
    blk = lambda col: pl.BlockSpec((tb, GROUP_W), lambda b, i: (b * nb + i, col))
    o, st = pl.pallas_call(
        functools.partial(_hgrn_kernel, tb=tb),
        grid=(n_batch, nb),
        in_specs=[blk(7), blk(8), blk(9), pl.BlockSpec((1, GROUP_W), lambda b, i: (0, 0))],
        out_specs=[
            pl.BlockSpec((tb, GROUP_W), lambda b, i: (b * nb + i, 0)),
            pl.BlockSpec((None, GROUP_W, GROUP_W), lambda b, i: (b, 0, 0)),
        ],
        out_shape=[jax.ShapeDtypeStruct((n_batch * seq, GROUP_W), f32),
                   jax.ShapeDtypeStruct((n_batch, GROUP_W, GROUP_W), f32)],
        scratch_shapes=[pltpu.VMEM((GROUP_W, GROUP_W), f32)] + [pltpu.VMEM((tb, GROUP_W), f32)] * 3,
        compiler_params=_cparams("arbitrary", "arbitrary"),
        name="hgrn_prompt",
    )(p, p, p, lb.astype(f32).reshape(1, GROUP_W))
    return o, jnp.swapaxes(_bd_diag(st), -1, -2)


def _log_sigmoid(x):
    return jnp.minimum(x, 0.0) - jnp.log1p(jnp.exp(-jnp.abs(x)))


def _mlstm_kernel(q_ref, k_ref, v_ref, gc_ref, gr_ref, ibr_ref, fbr_ref, ibc_ref, fbc_ref,
                  o_ref, c_out_ref, n_out_ref, m_out_ref, c_scr, n_scr, m_scr, *, tb):
    blk = pl.program_id(1)

    @pl.when(blk == 0)
    def _():
        c_scr[...] = jnp.zeros(c_scr.shape, f32)
        n_scr[...] = jnp.zeros(n_scr.shape, f32)
        m_scr[...] = jnp.zeros(m_scr.shape, f32)

    q = q_ref[...]
    k = k_ref[...] * (HEAD_W ** -0.5)
    v = v_ref[...]
    gc = gc_ref[...]
    li_col = gc[:, 8:12] + ibr_ref[...]
    lf_col = _log_sigmoid(gc[:, 12:16] + fbr_ref[...])
    gr = gr_ref[...]
    li_row = gr[8:12, :] + ibc_ref[...]
    lf_row = _log_sigmoid(gr[12:16, :] + fbc_ref[...])
    b_row = _mm2(jnp.concatenate([lf_row, jnp.zeros_like(lf_row)], axis=0), _block_tri_t(tb, CHUNK))

    same, lower, _ = _bd_masks()
    tri = _tri(CHUNK, bf16)

    for ch in range(tb // CHUNK):
        lo = ch * CHUNK
        qc, kc, vc = q[lo:lo + CHUNK], k[lo:lo + CHUNK], v[lo:lo + CHUNK]
        b_col = _mm2l(tri, lf_col[lo:lo + CHUNK])
        b_stack = _stack_cols(b_col)
        d_mat = jnp.where(lower, b_stack - _cat_rows(b_row, lo) + _cat_rows(li_row, lo), NEG)
        m_row = m_scr[...]
        m_stack = jnp.concatenate(
            [jnp.broadcast_to(m_row[:, h:h + 1], (CHUNK, 1)) for h in range(N_HEADS)], axis=0)
        inter = b_stack + m_stack
        m_t = jnp.maximum(inter, jnp.max(d_mat, axis=-1, keepdims=True))
        w_inter = jnp.exp(inter - m_t)
        qsm = _head_stack(qc)
        ksm = _head_stack(kc)
        pmat = _mm3(qsm, ksm, NT) * jnp.exp(d_mat - m_t)
        c_bd = c_scr[...]
        n_row = n_scr[...]
        num = w_inter * _mm3(qsm, c_bd) + _mm3(pmat, _head_stack(vc))
        den = w_inter * jnp.sum(qsm * n_row, axis=-1, keepdims=True) + jnp.sum(pmat, axis=-1, keepdims=True)
        h_sm = num / jnp.maximum(jnp.abs(den), jnp.exp(-m_t))
        o_ref[lo:lo + CHUNK, :] = _fold_heads(h_sm)
        m_new = jnp.concatenate(
            [m_t[h * CHUNK + CHUNK - 1:h * CHUNK + CHUNK, :] for h in range(N_HEADS)], axis=1)
        b_last = b_col[CHUNK - 1:CHUNK, :]
        w_end = jnp.exp(b_last - b_col + li_col[lo:lo + CHUNK] - m_new)
        d0 = jnp.exp(b_last + m_row - m_new)
        kw = kc * _expand_cols(w_end)
        d0_stack = jnp.concatenate(
            [jnp.broadcast_to(d0[:, h:h + 1], (HEAD_W, 1)) for h in range(N_HEADS)], axis=0)
        c_scr[...] = d0_stack * c_bd + jnp.where(same, _mm3(kw, vc, TN), 0.0)
        n_scr[...] = _expand_cols(d0) * n_row + jnp.sum(kw, axis=0, keepdims=True)
        m_scr[...] = m_new

    @pl.when(blk == pl.num_programs(1) - 1)
    def _():
        c_out_ref[...] = c_scr[...]
        n_out_ref[...] = n_scr[...]
        m_out_ref[...] = m_scr[...]


def _mlstm_prompt(p, gt, i_bias, f_bias, n_batch, seq):
    tb = min(256, seq)
    nb = seq // tb
    blk = lambda col: pl.BlockSpec((tb, GROUP_W), lambda b, i: (b * nb + i, col))
    r14 = lambda z: z.astype(f32).reshape(1, N_HEADS)
    c41 = lambda z: z.astype(f32).reshape(N_HEADS, 1)
    small = lambda shape: pl.BlockSpec(shape, lambda b, i: (0, 0))
    o, c_bd, n_row, m_row = pl.pallas_call(
        functools.partial(_mlstm_kernel, tb=tb),
        grid=(n_batch, nb),
        in_specs=[blk(11), blk(12), blk(13),
                  pl.BlockSpec((tb, 128), lambda b, i: (b * nb + i, GATE_COL // 128)),
                  pl.BlockSpec((16, tb), lambda b, i: (0, b * nb + i)),
                  small((1, N_HEADS)), small((1, N_HEADS)), small((N_HEADS, 1)), small((N_HEADS, 1))],
        out_specs=[
            pl.BlockSpec((tb, GROUP_W), lambda b, i: (b * nb + i, 0)),
            pl.BlockSpec((None, GROUP_W, GROUP_W), lambda b, i: (b, 0, 0)),
            pl.BlockSpec((None, 1, GROUP_W), lambda b, i: (b, 0, 0)),
            pl.BlockSpec((None, 1, N_HEADS), lambda b, i: (b, 0, 0)),
        ],
        out_shape=[jax.ShapeDtypeStruct((n_batch * seq, GROUP_W), f32),
                   jax.ShapeDtypeStruct((n_batch, GROUP_W, GROUP_W), f32),
                   jax.ShapeDtypeStruct((n_batch, 1, GROUP_W), f32),
                   jax.ShapeDtypeStruct((n_batch, 1, N_HEADS), f32)],
        scratch_shapes=[pltpu.VMEM((GROUP_W, GROUP_W), f32), pltpu.VMEM((1, GROUP_W), f32),
                        pltpu.VMEM((1, N_HEADS), f32)],
        compiler_params=_cparams("arbitrary", "arbitrary"),
        name="mlstm_prompt",
    )(p, p, p, p, gt, r14(i_bias), r14(f_bias), c41(i_bias), c41(f_bias))
    return (o, _bd_diag(c_bd), n_row.reshape(n_batch, N_HEADS, HEAD_W), m_row.reshape(n_batch, N_HEADS))


def _gdn_dec_prep_kernel(u_ref, buf_ref, cw_ref, q_ref, k_ref, v_ref):
    w = cw_ref[...]
    conv = u_ref[...] * w[3:4]
    for jj in range(CONV_W - 1):
        conv = conv + buf_ref[jj] * w[jj:jj + 1]
    qkv = _silu(conv)

    def l2n(x):
        return x * lax.rsqrt(_group_sum(x * x, HEAD_W) + EPS)

    q_ref[...] = l2n(qkv[:, 0:GROUP_W]) * (HEAD_W ** -0.5)
    k_ref[...] = l2n(qkv[:, GROUP_W:2 * GROUP_W])
    v_ref[...] = qkv[:, 2 * GROUP_W:3 * GROUP_W]


def _gdn_dec_prep(p, conv_buf, conv_w):
    n_b = p.shape[0]
    out = jax.ShapeDtypeStruct((n_b, GROUP_W), f32)
    return pl.pallas_call(
        _gdn_dec_prep_kernel,
        grid=(1,),
        in_specs=[pl.BlockSpec((n_b, 3 * GROUP_W), lambda i: (0, 0)),
                  pl.BlockSpec((CONV_W - 1, n_b, 3 * GROUP_W), lambda i: (0, 0, 0)),
                  pl.BlockSpec((CONV_W, 3 * GROUP_W), lambda i: (0, 0))],
        out_specs=[pl.BlockSpec((n_b, GROUP_W), lambda i: (0, 0))] * 3,
        out_shape=[out, out, out],
        compiler_params=_cparams("arbitrary"),
        name="gdn_dec_prep",
    )(p, jnp.swapaxes(conv_buf.astype(f32), 0, 1), conv_w.astype(f32))


def _rec_decode_kernel(gq_ref, gk_ref, gv_ref, ga_ref, gb_ref, al_ref, dtb_ref, sg_ref,
                       cq_ref, cf_ref, ci_ref, lbc_ref, sh_ref,
                       dq_ref, dk_ref, dv_ref, di_ref, df_ref, ib_ref, fb_ref, sc_ref, sn_ref, sm_ref,
                       oa_ref, sg_out, oc_ref, sh_out, od_ref, sc_out, sn_out, sm_out):
    q, k, v = gq_ref[...], gk_ref[...], gv_ref[...]
    s = sg_ref[...]
    g = -jnp.exp(al_ref[...]) * _softplus(ga_ref[...] + dtb_ref[...])
    eg = jnp.exp(g)
    beta = jax.nn.sigmoid(gb_ref[...])
    ks = jnp.sum(k * s, axis=1, keepdims=True)
    qs = jnp.sum(q * s, axis=1, keepdims=True)
    u = beta * (v - eg * ks)
    qk = jnp.sum(q * k, axis=1, keepdims=True)
    oa_ref[...] = eg * qs + qk * u
    sg_out[...] = eg * s + k * u

    lb = lbc_ref[...]
    z = cf_ref[...]
    logf = jnp.log(lb + (1.0 - lb) * jax.nn.sigmoid(z))
    kc = (1.0 - lb) * jax.nn.sigmoid(-z)
    qc = _silu(cq_ref[...])
    vc = ci_ref[...]
    sh = sh_ref[...]
    ef = jnp.exp(logf)
    oc_ref[...] = jnp.sum((qc * ef) * sh, axis=1, keepdims=True) + jnp.sum(qc * kc, axis=1, keepdims=True) * vc
    sh_out[...] = ef * sh + kc * vc

    qd = dq_ref[...]
    kd = dk_ref[...] * (HEAD_W ** -0.5)
    vd = dv_ref[...]
    li = di_ref[...] + ib_ref[...]
    lf = _log_sigmoid(df_ref[...] + fb_ref[...])
    m0 = sm_ref[...]
    cs = sc_ref[...]
    n0 = sn_ref[...]
    inter = lf + m0
    m_t = jnp.maximum(inter, li)
    w_inter = jnp.exp(inter - m_t)
    qkd = jnp.sum(qd * kd, axis=1, keepdims=True) * jnp.exp(li - m_t)
    q_c = jnp.sum(qd.astype(bf16).astype(f32) * cs.astype(bf16).astype(f32), axis=1, keepdims=True)
    num = w_inter * q_c + qkd * vd
    den = w_inter * jnp.sum(qd * n0, axis=1, keepdims=True) + qkd
    od_ref[...] = num / jnp.maximum(jnp.abs(den), jnp.exp(-m_t))
    w_end = jnp.exp(li - m_t)
    d0 = jnp.exp(lf + m0 - m_t)
    sc_out[...] = d0 * cs + (w_end * kd) * vd
    sn_out[...] = d0 * n0 + w_end * kd
    sm_out[...] = m_t


def _rec_decode(p, gq, gk, gv, a_log, dt_bias, lb, i_bias, f_bias, s_gdn, s_hgrn, s_c, s_n, s_m):
    n_b = p.shape[0]
    rows = n_b * N_HEADS
    rb = min(16, rows)
    col = lambda z: z.reshape(rows, HEAD_W, 1)
    vrow = lambda z: z.reshape(rows, 1, HEAD_W)
    sca = lambda z: z.reshape(rows, 1, 1)
    per_head = lambda z: jnp.tile(z.astype(f32), n_b).reshape(rows, 1, 1)
    blockp = lambda b: p[:, b * GROUP_W:(b + 1) * GROUP_W]
    gates = p[:, GATE_COL:GATE_COL + 16]
    lb_col = jnp.tile(lb.astype(f32).reshape(N_HEADS, HEAD_W), (n_b, 1)).reshape(rows, HEAD_W, 1)
    st = lambda z: z.astype(f32).reshape(rows, HEAD_W, HEAD_W)
    args = [col(gq), col(gk), vrow(gv), sca(gates[:, 0:4]), sca(gates[:, 4:8]), per_head(a_log), per_head(dt_bias), st(s_gdn),
            col(blockp(7)), col(blockp(8)), vrow(blockp(9)), lb_col, st(s_hgrn),
            col(blockp(11)), col(blockp(12)), vrow(blockp(13)), sca(gates[:, 8:12]), sca(gates[:, 12:16]),
            per_head(i_bias), per_head(f_bias), st(s_c), col(s_n.astype(f32)), sca(s_m.astype(f32))]

    def spec(a):
        return pl.BlockSpec((rb,) + a.shape[1:], lambda i: (i, 0, 0))

    o_vrow = jax.ShapeDtypeStruct((rows, 1, HEAD_W), f32)
    o_st = jax.ShapeDtypeStruct((rows, HEAD_W, HEAD_W), f32)
    o_col = jax.ShapeDtypeStruct((rows, HEAD_W, 1), f32)
    o_sca = jax.ShapeDtypeStruct((rows, 1, 1), f32)
    outs = [o_vrow, o_st, o_vrow, o_st, o_vrow, o_st, o_col, o_sca]
    res = pl.pallas_call(
        _rec_decode_kernel,
        grid=(rows // rb,),
        in_specs=[spec(a) for a in args],
        out_specs=[spec(a) for a in outs],
        out_shape=outs,
        compiler_params=_cparams("arbitrary"),
        name="rec_decode",
    )(*args)
    oa, sg, oc, sh, od, sc, sn, sm = res
    s4 = lambda z: z.reshape(n_b, N_HEADS, HEAD_W, HEAD_W)
    o2 = lambda z: z.reshape(n_b, GROUP_W)
    return (o2(oa), s4(sg), o2(oc), s4(sh), o2(od), s4(sc),
            sn.reshape(n_b, N_HEADS, HEAD_W), sm.reshape(n_b, N_HEADS))


def _permute_w_in(w):
    d_in = w.shape[1]
    a_gate0 = 3 * GROUP_W
    d_gate0 = d_in - GROUP_W - 8
    main = jnp.concatenate([w[:, 0:a_gate0], w[:, a_gate0 + 8:d_gate0], w[:, d_gate0 + 8:]], axis=1)
    gates = jnp.concatenate([w[:, a_gate0:a_gate0 + 8], w[:, d_gate0:d_gate0 + 8]], axis=1)
    pad = jnp.zeros((w.shape[0], P_COLS - main.shape[1] - 16), w.dtype)
    return jnp.concatenate([main, gates, pad], axis=1).astype(bf16), gates.T.astype(bf16)


def kernel(x_prompt, x_sample, page_table, cache_k, cache_v, state_gdn_conv, state_gdn, state_hgrn, state_mlstm_C, state_mlstm_n, state_mlstm_m, attn_norm_g, w_in, gdn_conv_w, gdn_a_log, gdn_dt_bias, gdn_norm_g, diff_qk_norm_g, diff_lambda, diff_subln_g, rel_bias, hgrn_lb_logits, hgrn_norm_g, mlstm_i_bias, mlstm_f_bias, mlstm_norm_g, w_out, ffn_norm_g, ffn_w_gate, ffn_w_up, ffn_w_down, moe_router, moe_w_gate, moe_w_up, moe_w_down):
    depth = w_in.shape[0]
    n_bp, seq, _ = x_prompt.shape
    n_bs = x_sample.shape[0]
    n_pool, page = cache_k.shape[1], cache_k.shape[2]
    dt = x_prompt.dtype

    lb_p = jax.nn.softmax(hgrn_lb_logits.astype(f32), axis=0)
    lb_cum = jnp.cumsum(lb_p, axis=0)
    hgrn_lb = lb_cum - lb_cum[0:1]
    cache_k4 = cache_k.reshape(depth, n_pool, page, GROUP_W)
    cache_v4 = cache_v.reshape(depth, n_pool, page, GROUP_W)

    xp = x_prompt.reshape(n_bp * seq, D_MODEL)
    xs = x_sample.reshape(n_bs, D_MODEL)
    outs_p, outs_s = [], []
    for l in range(depth):
        w_perm, w_gate_t = _permute_w_in(w_in[l])
        w_out_b = w_out[l].astype(bf16)
        gains = jnp.stack([jnp.tile(g.astype(f32), N_HEADS) for g in
                           (gdn_norm_g[l], diff_subln_g[l], hgrn_norm_g[l], mlstm_norm_g[l])])
        lam_init = 0.8 - 0.6 * math.exp(-0.3 * l)
        lam32 = diff_lambda[l].astype(f32)
        lam = jnp.exp(jnp.sum(lam32[0] * lam32[1])) - jnp.exp(jnp.sum(lam32[2] * lam32[3])) + lam_init
        if l % 2 == 0:
            ffn_w = (ffn_w_gate[l // 2].astype(bf16), ffn_w_up[l // 2].astype(bf16), ffn_w_down[l // 2].astype(bf16))
        else:
            router_pad = jnp.pad(moe_router[l // 2].astype(bf16), ((0, 0), (0, 128 - N_EXPERTS)))
            moe_w = (moe_w_gate[l // 2].astype(bf16), moe_w_up[l // 2].astype(bf16), moe_w_down[l // 2].astype(bf16))

        def channel_mix(x):
            if l % 2 == 0:
                return _ffn(x, ffn_norm_g[l], *ffn_w)
            h, comb = _router(x, ffn_norm_g[l], router_pad)
            return _moe(x, h, comb, *moe_w)

        p, gt = _inproj(xp, attn_norm_g[l], w_perm, w_gate_t)
        qn, kn = _bprep(p, diff_qk_norm_g[l])
        ob = _attn_prompt(qn, kn, p, lam, rel_bias, n_bp, seq)
        oa, s_gdn = _gdn_prompt(p, gt, gdn_conv_w[l], gdn_a_log[l], gdn_dt_bias[l], n_bp, seq)
        oc, s_hgrn = _hgrn_prompt(p, hgrn_lb[l], n_bp, seq)
        od, s_c, s_n, s_m = _mlstm_prompt(p, gt, mlstm_i_bias[l], mlstm_f_bias[l], n_bp, seq)
        xp = _outproj(oa, ob, oc, od, p, xp, gains, w_out_b, 1.0 - lam_init)
        xp = channel_mix(xp)
        p3 = p.reshape(n_bp, seq, P_COLS)
        outs_p.append((
            kn.reshape(n_bp, seq, N_HEADS, 2, DKB).astype(dt),
            p3[:, :, 6 * GROUP_W:7 * GROUP_W].reshape(n_bp, seq, N_HEADS, HEAD_W).astype(dt),
            p3[:, seq - (CONV_W - 1):, 0:3 * GROUP_W].astype(dt),
            s_gdn.astype(dt), s_hgrn.astype(dt), s_c.astype(dt), s_n.astype(dt), s_m.astype(dt)))

        p, gt = _inproj(xs, attn_norm_g[l], w_perm, w_gate_t)
        qn, kn = _bprep(p, diff_qk_norm_g[l])
        vn = p[:, 6 * GROUP_W:7 * GROUP_W]
        ob = _attn_decode(qn, kn, vn, page_table, cache_k4, cache_v4, l, lam, rel_bias)
        u = p[:, 0:3 * GROUP_W]
        gq, gk, gv = _gdn_dec_prep(u, state_gdn_conv[l], gdn_conv_w[l])
        oa, s_gdn, oc, s_hgrn, od, s_c, s_n, s_m = _rec_decode(
            p, gq, gk, gv, gdn_a_log[l], gdn_dt_bias[l], hgrn_lb[l], mlstm_i_bias[l], mlstm_f_bias[l],
            state_gdn[l], state_hgrn[l], state_mlstm_C[l], state_mlstm_n[l], state_mlstm_m[l])
        xs = _outproj(oa, ob, oc, od, p, xs, gains, w_out_b, 1.0 - lam_init)
        xs = channel_mix(xs)
        conv_new = jnp.concatenate([state_gdn_conv[l][:, 1:].astype(dt), u[:, None, :].astype(dt)], axis=1)
        outs_s.append((
            kn.reshape(n_bs, 1, N_HEADS, 2, DKB).astype(dt),
            vn.reshape(n_bs, 1, N_HEADS, HEAD_W).astype(dt),
            conv_new, s_gdn.astype(dt), s_hgrn.astype(dt), s_c.astype(dt), s_n.astype(dt), s_m.astype(dt)))

    kp, vp, convp, gdnp, hgrnp, mcp, mnp_, mmp = [jnp.stack(z) for z in zip(*outs_p)]
    ks_, vs_, convs, gdns, hgrns, mcs, mns, mms = [jnp.stack(z) for z in zip(*outs_s)]
    return (xp.reshape(n_bp, seq, D_MODEL), xs.reshape(n_bs, 1, D_MODEL), kp, vp, ks_, vs_, convp, convs,
            gdnp, gdns, hgrnp, hgrns, mcp, mcs, mnp_, mns, mmp, mms)
```

```python
import functools
import math

import numpy as np
import jax
import jax.numpy as jnp
from jax import lax
from jax.experimental import pallas as pl
from jax.experimental.pallas import tpu as pltpu

f32 = jnp.float32
bf16 = jnp.bfloat16

D_MODEL = 1024
N_HEADS = 4
HEAD_W = 64
GROUP_W = N_HEADS * HEAD_W
DKB = 32
CONV_W = 4
CHUNK = 64
SUB = 16
NUM_BUCKETS = 32
MAX_DISTANCE = 128
N_EXPERTS = 8
EPS = 1e-6
NEG = -1e30
P_COLS = 4096
GATE_COL = 3840
VMEM_LIMIT = 56 * 1024 * 1024

NN = ((1,), (0,))
NT = ((1,), (1,))
TN = ((0,), (0,))


def _dg(a, b, dims=NN):
    return lax.dot_general(a, b, (dims, ((), ())), preferred_element_type=f32)


def _split(a):
    hi = a.astype(bf16)
    lo = (a - hi.astype(f32)).astype(bf16)
    return hi, lo


def _mm3(a, b, dims=NN):
    ah, al = _split(a)
    bh, bl = _split(b)
    return _dg(ah, bh, dims) + (_dg(ah, bl, dims) + _dg(al, bh, dims))


def _mm2(a, b01, dims=NN):
    ah, al = _split(a)
    return _dg(ah, b01, dims) + _dg(al, b01, dims)


def _mm2l(a01, b, dims=NN):
    bh, bl = _split(b)
    return _dg(a01, bh, dims) + _dg(a01, bl, dims)


def _mm1(a, b, dims=NN):
    return _dg(a.astype(bf16), b.astype(bf16), dims)


def _iota(shape, dim):
    return lax.broadcasted_iota(jnp.int32, shape, dim)


def _group_ones(width, group):
    r = _iota((width, width), 0) // group
    c = _iota((width, width), 1) // group
    return (r == c).astype(bf16)


def _group_sum(x, group):
    return _mm2(x, _group_ones(x.shape[-1], group))


def _silu(x):
    return x * jax.nn.sigmoid(x)


def _softplus(x):
    return jnp.maximum(x, 0.0) + jnp.log1p(jnp.exp(-jnp.abs(x)))


def _stack_cols(xc, n=N_HEADS, rows=HEAD_W):
    return jnp.concatenate([xc[:, h:h + 1] for h in range(n)], axis=0)


def _expand_cols(xc, n=N_HEADS, width=HEAD_W):
    r = xc.shape[0]
    return jnp.concatenate([jnp.broadcast_to(xc[:, h:h + 1], (r, width)) for h in range(n)], axis=1)


def _cat_rows(xr, lo, n=N_HEADS, width=HEAD_W):
    return jnp.concatenate([xr[h:h + 1, lo:lo + width] for h in range(n)], axis=1)


def _head_stack(x, n=N_HEADS, width=HEAD_W):
    lane_head = _iota(x.shape, 1) // width
    return jnp.concatenate([jnp.where(lane_head == h, x, 0.0) for h in range(n)], axis=0)


def _fold_heads(x_sm, n=N_HEADS):
    r = x_sm.shape[0] // n
    out = x_sm[0:r]
    for h in range(1, n):
        out = out + x_sm[h * r:(h + 1) * r]
    return out


def _bd_masks(n=GROUP_W, blk=CHUNK):
    r = _iota((n, n), 0)
    c = _iota((n, n), 1)
    same = (r // blk) == (c // blk)
    lower = same & ((r % blk) >= (c % blk))
    strict = same & ((r % blk) > (c % blk))
    return same, lower, strict


def _cparams(*sem):
    return pltpu.CompilerParams(dimension_semantics=sem, vmem_limit_bytes=VMEM_LIMIT)


def _inproj_kernel(x_ref, g_ref, w_ref, wgt_ref, p_ref, gt_ref, h_scr):
    @pl.when(pl.program_id(1) == 0)
    def _():
        x = x_ref[...]
        ms = jnp.mean(x * x, axis=-1, keepdims=True)
        h = ((x * lax.rsqrt(ms + EPS)) * g_ref[...]).astype(bf16)
        h_scr[...] = h
        gt_ref[...] = _dg(wgt_ref[...], h, NT)
    p_ref[...] = _dg(h_scr[...], w_ref[...], NN)


def _inproj(x, g, w_perm, w_gate_t):
    t = x.shape[0]
    tm = min(512, t)
    tn = 1024
    return pl.pallas_call(
        _inproj_kernel,
        grid=(t // tm, P_COLS // tn),
        in_specs=[
            pl.BlockSpec((tm, D_MODEL), lambda i, j: (i, 0)),
            pl.BlockSpec((1, D_MODEL), lambda i, j: (0, 0)),
            pl.BlockSpec((D_MODEL, tn), lambda i, j: (0, j)),
            pl.BlockSpec((16, D_MODEL), lambda i, j: (0, 0)),
        ],
        out_specs=[
            pl.BlockSpec((tm, tn), lambda i, j: (i, j)),
            pl.BlockSpec((16, tm), lambda i, j: (0, i)),
        ],
        out_shape=[jax.ShapeDtypeStruct((t, P_COLS), f32), jax.ShapeDtypeStruct((16, t), f32)],
        scratch_shapes=[pltpu.VMEM((tm, D_MODEL), bf16)],
        compiler_params=_cparams("arbitrary", "arbitrary"),
        name="inproj",
    )(x, g.reshape(1, D_MODEL), w_perm, w_gate_t)


def _qk_gnorm(x, g):
    ms = _group_sum(x * x, DKB) * (1.0 / DKB)
    return (x * lax.rsqrt(ms + EPS)) * g


def _bprep_kernel(q_ref, k_ref, gq_ref, gk_ref, qn_ref, kn_ref):
    qn_ref[...] = _qk_gnorm(q_ref[...], gq_ref[...])
    kn_ref[...] = _qk_gnorm(k_ref[...], gk_ref[...])


def _bprep_t_kernel(q_ref, k_ref, v_ref, gq_ref, gk_ref, qnt_ref, kn_ref, vt_ref):
    qnt_ref[...] = _qk_gnorm(q_ref[...], gq_ref[...]).T
    kn_ref[...] = _qk_gnorm(k_ref[...], gk_ref[...])
    vt_ref[...] = v_ref[...].T


def _bprep(p, qk_norm_g, transposed):
    t = p.shape[0]
    tm = min(512, t)
    gq = jnp.tile(qk_norm_g[0], GROUP_W // DKB).reshape(1, GROUP_W)
    gk = jnp.tile(qk_norm_g[1], GROUP_W // DKB).reshape(1, GROUP_W)
    col = lambda c: pl.BlockSpec((tm, GROUP_W), lambda i: (i, c))
    gain = pl.BlockSpec((1, GROUP_W), lambda i: (0, 0))
    rows = pl.BlockSpec((tm, GROUP_W), lambda i: (i, 0))
    rows_t = pl.BlockSpec((GROUP_W, tm), lambda i: (0, i))
    if transposed:
        return pl.pallas_call(
            _bprep_t_kernel,
            grid=(t // tm,),
            in_specs=[col(4), col(5), col(6), gain, gain],
            out_specs=[rows_t, rows, rows_t],
            out_shape=[jax.ShapeDtypeStruct((GROUP_W, t), f32), jax.ShapeDtypeStruct((t, GROUP_W), f32),
                       jax.ShapeDtypeStruct((GROUP_W, t), f32)],
            compiler_params=_cparams("arbitrary"),
            name="bprep_t",
        )(p, p, p, gq, gk)
    return pl.pallas_call(
        _bprep_kernel,
        grid=(t // tm,),
        in_specs=[col(4), col(5), gain, gain],
        out_specs=[rows, rows],
        out_shape=[jax.ShapeDtypeStruct((t, GROUP_W), f32)] * 2,
        compiler_params=_cparams("arbitrary"),
        name="bprep",
    )(p, p, gq, gk)


def _outproj_kernel(oa_ref, ob_ref, oc_ref, od_ref, ag_ref, cg_ref, dg_ref, x_ref, g_ref, w_ref, y_ref, *, b_scale):
    def gnorm(x, g):
        ms = _group_sum(x * x, HEAD_W) * (1.0 / HEAD_W)
        return (x * lax.rsqrt(ms + EPS)) * g
    g = g_ref[...]
    mixes = (
        gnorm(oa_ref[...], g[0:1]) * _silu(ag_ref[...]),
        gnorm(ob_ref[...], g[1:2]) * b_scale,
        gnorm(oc_ref[...], g[2:3]) * jax.nn.sigmoid(cg_ref[...]),
        gnorm(od_ref[...], g[3:4]) * jax.nn.sigmoid(dg_ref[...]),
    )
    y = x_ref[...]
    for i, m in enumerate(mixes):
        y = y + _dg(m.astype(bf16), w_ref[i * GROUP_W:(i + 1) * GROUP_W, :], NN)
    y_ref[...] = y


def _outproj(oa, ob, oc, od, p, x, gains, w_out, b_scale):
    t = x.shape[0]
    tm = min(512, t)
    row = lambda i: (i, 0)
    return pl.pallas_call(
        functools.partial(_outproj_kernel, b_scale=b_scale),
        grid=(t // tm,),
        in_specs=[
            pl.BlockSpec((tm, GROUP_W), row), pl.BlockSpec((tm, GROUP_W), row),
            pl.BlockSpec((tm, GROUP_W), row), pl.BlockSpec((tm, GROUP_W), row),
            pl.BlockSpec((tm, GROUP_W), lambda i: (i, 3)),
            pl.BlockSpec((tm, GROUP_W), lambda i: (i, 10)),
            pl.BlockSpec((tm, GROUP_W), lambda i: (i, 14)),
            pl.BlockSpec((tm, D_MODEL), row),
            pl.BlockSpec((4, GROUP_W), lambda i: (0, 0)),
            pl.BlockSpec((D_MODEL, D_MODEL), lambda i: (0, 0)),
        ],
        out_specs=pl.BlockSpec((tm, D_MODEL), row),
        out_shape=jax.ShapeDtypeStruct((t, D_MODEL), f32),
        compiler_params=_cparams("arbitrary"),
        name="outproj",
    )(oa, ob, oc, od, p, p, p, x, gains, w_out)


def _ffn_kernel(x_ref, g_ref, wg_ref, wu_ref, wd_ref, y_ref, h_scr, acc_scr):
    f = pl.program_id(1)

    @pl.when(f == 0)
    def _():
        x = x_ref[...]
        ms = jnp.mean(x * x, axis=-1, keepdims=True)
        h_scr[...] = ((x * lax.rsqrt(ms + EPS)) * g_ref[...]).astype(bf16)
        acc_scr[...] = x

    h = h_scr[...]
    a = _silu(_dg(h, wg_ref[...])) * _dg(h, wu_ref[...])
    acc_scr[...] += _dg(a.astype(bf16), wd_ref[...])

    @pl.when(f == pl.num_programs(1) - 1)
    def _():
        y_ref[...] = acc_scr[...]


def _ffn(x, g, wg, wu, wd):
    t = x.shape[0]
    d_ff = wg.shape[1]
    tm = min(512, t)
    tf = d_ff // 2
    return pl.pallas_call(
        _ffn_kernel,
        grid=(t // tm, d_ff // tf),
        in_specs=[
            pl.BlockSpec((tm, D_MODEL), lambda i, f: (i, 0)),
            pl.BlockSpec((1, D_MODEL), lambda i, f: (0, 0)),
            pl.BlockSpec((D_MODEL, tf), lambda i, f: (0, f)),
            pl.BlockSpec((D_MODEL, tf), lambda i, f: (0, f)),
            pl.BlockSpec((tf, D_MODEL), lambda i, f: (f, 0)),
        ],
        out_specs=pl.BlockSpec((tm, D_MODEL), lambda i, f: (i, 0)),
        out_shape=jax.ShapeDtypeStruct((t, D_MODEL), f32),
        scratch_shapes=[pltpu.VMEM((tm, D_MODEL), bf16), pltpu.VMEM((tm, D_MODEL), f32)],
        compiler_params=_cparams("arbitrary", "arbitrary"),
        name="ffn",
    )(x, g.reshape(1, D_MODEL), wg, wu, wd)


def _router_kernel(x_ref, g_ref, r_ref, h_ref, comb_ref):
    x = x_ref[...]
    ms = jnp.mean(x * x, axis=-1, keepdims=True)
    h = (x * lax.rsqrt(ms + EPS)) * g_ref[...]
    hb = h.astype(bf16)
    h_ref[...] = hb
    logits = _dg(hb, r_ref[...])
    lane = _iota(logits.shape, 1)
    logits = jnp.where(lane < N_EXPERTS, logits, -jnp.inf)
    v1 = jnp.max(logits, axis=-1, keepdims=True)
    i1 = jnp.min(jnp.where(logits == v1, lane, 128), axis=-1, keepdims=True)
    rest = jnp.where(lane == i1, -jnp.inf, logits)
    v2 = jnp.max(rest, axis=-1, keepdims=True)
    i2 = jnp.min(jnp.where(rest == v2, lane, 128), axis=-1, keepdims=True)
    e2 = jnp.exp(v2 - v1)
    den = 1.0 + e2
    comb_ref[...] = jnp.where(lane == i1, 1.0 / den, 0.0) + jnp.where(lane == i2, e2 / den, 0.0)


def _router(x, g, router_pad):
    t = x.shape[0]
    tm = min(512, t)
    return pl.pallas_call(
        _router_kernel,
        grid=(t // tm,),
        in_specs=[
            pl.BlockSpec((tm, D_MODEL), lambda i: (i, 0)),
            pl.BlockSpec((1, D_MODEL), lambda i: (0, 0)),
            pl.BlockSpec((D_MODEL, 128), lambda i: (0, 0)),
        ],
        out_specs=[pl.BlockSpec((tm, D_MODEL), lambda i: (i, 0)), pl.BlockSpec((tm, 128), lambda i: (i, 0))],
        out_shape=[jax.ShapeDtypeStruct((t, D_MODEL), bf16), jax.ShapeDtypeStruct((t, 128), f32)],
        compiler_params=_cparams("arbitrary"),
        name="router",
    )(x, g.reshape(1, D_MODEL), router_pad)


def _moe_kernel(x_ref, h_ref, comb_ref, wg_ref, wu_ref, wd_ref, y_ref, acc_scr):
    e = pl.program_id(1)
    f = pl.program_id(2)

    @pl.when((e == 0) & (f == 0))
    def _():
        acc_scr[...] = x_ref[...]

    comb = comb_ref[...]
    cw = jnp.sum(jnp.where(_iota(comb.shape, 1) == e, comb, 0.0), axis=-1, keepdims=True)
    h = h_ref[...]
    a = _silu(_dg(h, wg_ref[...])) * _dg(h, wu_ref[...]) * cw
    acc_scr[...] += _dg(a.astype(bf16), wd_ref[...])

    @pl.when((e == pl.num_programs(1) - 1) & (f == pl.num_programs(2) - 1))
    def _():
        y_ref[...] = acc_scr[...]


def _moe(x, h, comb, wg, wu, wd):
    t = x.shape[0]
    n_e, _, d_ff = wg.shape
    tm = min(512, t)
    tf = d_ff // 2
    return pl.pallas_call(
        _moe_kernel,
        grid=(t // tm, n_e, d_ff // tf),
        in_specs=[
            pl.BlockSpec((tm, D_MODEL), lambda i, e, f: (i, 0)),
            pl.BlockSpec((tm, D_MODEL), lambda i, e, f: (i, 0)),
            pl.BlockSpec((tm, 128), lambda i, e, f: (i, 0)),
            pl.BlockSpec((None, D_MODEL, tf), lambda i, e, f: (e, 0, f)),
            pl.BlockSpec((None, D_MODEL, tf), lambda i, e, f: (e, 0, f)),
            pl.BlockSpec((None, tf, D_MODEL), lambda i, e, f: (e, f, 0)),
        ],
        out_specs=pl.BlockSpec((tm, D_MODEL), lambda i, e, f: (i, 0)),
        out_shape=jax.ShapeDtypeStruct((t, D_MODEL), f32),
        scratch_shapes=[pltpu.VMEM((tm, D_MODEL), f32)],
        compiler_params=_cparams("arbitrary", "arbitrary", "arbitrary"),
        name="moe",
    )(x, h, comb, wg, wu, wd)


def _t5_bucket_np(n):
    n = np.maximum(n, 0)
    max_exact = NUM_BUCKETS // 2
    nf = np.maximum(n, 1).astype(np.float32)
    large = max_exact + (np.log(nf / np.float32(max_exact)) / np.float32(math.log(MAX_DISTANCE / max_exact))
                         * np.float32(NUM_BUCKETS - max_exact)).astype(np.int32)
    return np.where(n < max_exact, n, np.minimum(large, NUM_BUCKETS - 1))


def _shifted_bias(rel_bias):
    rb = rel_bias.astype(f32)
    return rb - rb[NUM_BUCKETS - 1:NUM_BUCKETS]


ACC_ROWS = HEAD_W + 8
LOG2E = math.log2(math.e)


def _attn_kernel(qi_ref, kj_ref, lam_ref, qt_ref, k_ref, vt_ref, toep_ref, o_ref, qs_scr, m_scr, acc_scr, *, tq):
    p = pl.program_id(1)
    i = qi_ref[p]
    j = kj_ref[p]
    n_hc = 2 * N_HEADS
    c2 = (DKB ** -0.5) * LOG2E

    @pl.when(j == 0)
    def _():
        qt = qt_ref[...]
        row_grp = _iota(qt.shape, 0) // DKB
        for hc in range(n_hc):
            qs_scr[:, hc * tq:(hc + 1) * tq] = jnp.where(row_grp == hc, qt, 0.0).astype(bf16)
        m_scr[...] = jnp.full(m_scr.shape, NEG, f32)
        acc_scr[...] = jnp.zeros(acc_scr.shape, f32)

    def step(near):
        tk = k_ref.shape[0]
        st_all = _dg(k_ref[...].astype(bf16), qs_scr[...], NN)
        vt = vt_ref[...]
        ones = jnp.ones((ACC_ROWS - HEAD_W, tk), f32)
        for h in range(N_HEADS):
            vh = jnp.concatenate([vt[h * HEAD_W:(h + 1) * HEAD_W, :], ones], axis=0).astype(bf16)
            for hc in (2 * h, 2 * h + 1):
                s = st_all[:, hc * tq:(hc + 1) * tq] * c2
                if near:
                    s = s + toep_ref[(i - j) * N_HEADS + h]
                m_old = m_scr[hc:hc + 1, :]
                m_new = jnp.maximum(m_old, jnp.max(s, axis=0, keepdims=True))
                pexp = jnp.exp2(s - m_new)
                acc_scr[hc] = jnp.exp2(m_old - m_new) * acc_scr[hc] + _dg(vh, pexp.astype(bf16), NN)
                m_scr[hc:hc + 1, :] = m_new

    @pl.when(i - j <= 1)
    def _():
        step(True)

    @pl.when(i - j > 1)
    def _():
        step(False)

    @pl.when(j == i)
    def _():
        lam = lam_ref[0]
        outs = []
        for h in range(N_HEADS):
            a0 = acc_scr[2 * h]
            a1 = acc_scr[2 * h + 1]
            outs.append(a0[0:HEAD_W] / a0[HEAD_W:HEAD_W + 1] - lam * (a1[0:HEAD_W] / a1[HEAD_W:HEAD_W + 1]))
        o_ref[...] = jnp.concatenate(outs, axis=0).T


def _toeplitz_bias_tiles(rel_bias, t):
    period = 2 * t + 1
    m = np.arange(period)[None, :]
    dist = m - t + np.array([0, t])[:, None]
    tab = _shifted_bias(rel_bias)
    u = jnp.take(tab, jnp.asarray(_t5_bucket_np(dist)), axis=0)
    u = jnp.where(jnp.asarray(dist >= 0)[:, :, None], u * LOG2E, NEG)
    u = jnp.transpose(u, (0, 2, 1)).reshape(2 * N_HEADS, period)
    skew = jnp.tile(u, (1, t))[:, :t * (period - 1)].reshape(2 * N_HEADS, t, period - 1)
    return skew[:, :, t:2 * t]


def _attn_prompt(qnt, kn, vt, lam, rel_bias, n_batch, seq):
    tq = min(256, seq)
    nq = seq // tq
    pairs = [(i, j) for i in range(nq) for j in range(i + 1)]
    qi = jnp.asarray(np.array([a for a, _ in pairs], np.int32))
    kj = jnp.asarray(np.array([b for _, b in pairs], np.int32))
    toep = _toeplitz_bias_tiles(rel_bias, tq)
    grid_spec = pltpu.PrefetchScalarGridSpec(
        num_scalar_prefetch=2,
        grid=(n_batch, len(pairs)),
        in_specs=[
            pl.BlockSpec(memory_space=pltpu.SMEM),
            pl.BlockSpec((GROUP_W, tq), lambda b_, p_, qi_, kj_: (0, b_ * nq + qi_[p_])),
            pl.BlockSpec((tq, GROUP_W), lambda b_, p_, qi_, kj_: (b_ * nq + kj_[p_], 0)),
            pl.BlockSpec((GROUP_W, tq), lambda b_, p_, qi_, kj_: (0, b_ * nq + kj_[p_])),
            pl.BlockSpec((2 * N_HEADS, tq, tq), lambda b_, p_, qi_, kj_: (0, 0, 0)),
        ],
        out_specs=pl.BlockSpec((tq, GROUP_W), lambda b_, p_, qi_, kj_: (b_ * nq + qi_[p_], 0)),
        scratch_shapes=[
            pltpu.VMEM((GROUP_W, 2 * N_HEADS * tq), bf16),
            pltpu.VMEM((2 * N_HEADS, tq), f32),
            pltpu.VMEM((2 * N_HEADS, ACC_ROWS, tq), f32),
        ],
    )
    return pl.pallas_call(
        functools.partial(_attn_kernel, tq=tq),
        grid_spec=grid_spec,
        out_shape=jax.ShapeDtypeStruct((n_batch * seq, GROUP_W), f32),
        compiler_params=_cparams("arbitrary", "arbitrary"),
        name="attn_prompt",
    )(qi, kj, lam.reshape(1), qnt, kn, vt, toep)


def _attn_decode_kernel(pt_ref, lam_ref, q_ref, kn_ref, vn_ref, blast_ref, bself_ref, *rest, pg, n_pages):
    k_refs = rest[:pg]
    v_refs = rest[pg:2 * pg]
    o_ref, qs_scr, s_scr, aself_scr, acc_scr = rest[2 * pg:]
    t = pl.program_id(1)
    n_steps = n_pages // pg
    n_hc = 2 * N_HEADS
    page = k_refs[0].shape[1]
    scale = DKB ** -0.5
    rnd = lambda z: z.astype(bf16).astype(f32)

    @pl.when(t == 0)
    def _():
        q = jnp.broadcast_to(q_ref[...], (n_hc, GROUP_W))
        keep = (_iota(q.shape, 1) // DKB) == _iota(q.shape, 0)
        qs_scr[...] = jnp.where(keep, q, 0.0)

    @pl.when(t < n_steps)
    def _():
        qs_b = qs_scr[...].astype(bf16)
        parts = []
        for g in range(pg):
            s = _dg(qs_b, k_refs[g][...].astype(bf16), NN) * scale
            is_last = (t * pg + g) == (n_pages - 1)
            parts.append(s + jnp.where(is_last, blast_ref[...], 0.0))
        s_scr[t] = jnp.concatenate(parts, axis=1)

    @pl.when(t == n_steps - 1)
    def _():
        s_all = s_scr[...]
        s_self = jnp.sum(rnd(qs_scr[...]) * rnd(kn_ref[...]), axis=-1, keepdims=True) * scale + bself_ref[...]
        m = jnp.maximum(jnp.max(jnp.max(s_all, axis=2, keepdims=True), axis=0), s_self)
        p = jnp.exp(s_all - m)
        p_self = jnp.exp(s_self - m)
        l = jnp.sum(jnp.sum(p, axis=2, keepdims=True), axis=0) + p_self
        pn = p / l
        pn_self = p_self / l
        lam = lam_ref[0]
        rows = [pn[:, 2 * h:2 * h + 1, :] - lam * pn[:, 2 * h + 1:2 * h + 2, :] for h in range(N_HEADS)]
        s_scr[...] = jnp.concatenate(rows + [jnp.zeros_like(rows[0])] * N_HEADS, axis=1)
        rows_self = [pn_self[2 * h:2 * h + 1] - lam * pn_self[2 * h + 1:2 * h + 2] for h in range(N_HEADS)]
        aself_scr[...] = jnp.concatenate(rows_self + [jnp.zeros_like(rows_self[0])] * N_HEADS, axis=0)
        acc_scr[...] = jnp.zeros(acc_scr.shape, f32)

    @pl.when(t >= n_steps)
    def _():
        a = s_scr[t - n_steps].astype(bf16)
        acc = acc_scr[...]
        for g in range(pg):
            acc = acc + _dg(a[:, g * page:(g + 1) * page], v_refs[g][...].astype(bf16), NT)
        acc_scr[...] = acc

    @pl.when(t == 2 * n_steps - 1)
    def _():
        o = acc_scr[...] + rnd(aself_scr[...]) * rnd(vn_ref[...])
        lane_head = _iota((1, GROUP_W), 1) // HEAD_W
        out = jnp.zeros((1, GROUP_W), f32)
        for h in range(N_HEADS):
            out = jnp.where(lane_head == h, o[h:h + 1], out)
        o_ref[...] = out


def _attn_decode(qn, kn, vn, page_table, cache_k, cache_v, layer, lam, rel_bias):
    n_b, n_pages = page_table.shape
    page = cache_k.shape[3]
    pg = min(8, n_pages)
    n_steps = n_pages // pg
    past = n_pages * page
    tab = _shifted_bias(rel_bias)
    d_last = past - ((n_pages - 1) * page + np.arange(page))
    blast = jnp.repeat(jnp.take(tab, jnp.asarray(_t5_bucket_np(d_last)), axis=0).T, 2, axis=0)
    bself = jnp.repeat(tab[0].reshape(N_HEADS, 1), 2, axis=0)

    def k_spec(g):
        return pl.BlockSpec((None, None, GROUP_W, page),
                            lambda b_, t_, pt: (layer, pt[b_, jnp.minimum(t_, n_steps - 1) * pg + g], 0, 0))

    def v_spec(g):
        return pl.BlockSpec((None, None, GROUP_W, page),
                            lambda b_, t_, pt: (layer, pt[b_, jnp.maximum(t_ - n_steps, 0) * pg + g], 0, 0))

    row = pl.BlockSpec((None, 1, GROUP_W), lambda b_, t_, pt: (b_, 0, 0))
    grid_spec = pltpu.PrefetchScalarGridSpec(
        num_scalar_prefetch=1,
        grid=(n_b, 2 * n_steps),
        in_specs=[pl.BlockSpec(memory_space=pltpu.SMEM), row, row, row,
                  pl.BlockSpec((2 * N_HEADS, page), lambda b_, t_, pt: (0, 0)),
                  pl.BlockSpec((2 * N_HEADS, 1), lambda b_, t_, pt: (0, 0))]
                 + [k_spec(g) for g in range(pg)] + [v_spec(g) for g in range(pg)],
        out_specs=row,
        scratch_shapes=[
            pltpu.VMEM((2 * N_HEADS, GROUP_W), f32),
            pltpu.VMEM((n_steps, 2 * N_HEADS, pg * page), f32),
            pltpu.VMEM((2 * N_HEADS, 1), f32),
            pltpu.VMEM((2 * N_HEADS, GROUP_W), f32),
        ],
    )
    r3 = lambda z: z.reshape(n_b, 1, GROUP_W)
    out = pl.pallas_call(
        functools.partial(_attn_decode_kernel, pg=pg, n_pages=n_pages),
        grid_spec=grid_spec,
        out_shape=jax.ShapeDtypeStruct((n_b, 1, GROUP_W), f32),
        compiler_params=_cparams("arbitrary", "arbitrary"),
        name="attn_decode",
    )(page_table, lam.reshape(1), r3(qn), r3(kn), r3(vn), blast, bself,
      *([cache_k] * pg), *([cache_v] * pg))
    return out.reshape(n_b, GROUP_W)


def _tri(n, dtype=f32):
    return (_iota((n, n), 0) >= _iota((n, n), 1)).astype(dtype)


def _block_tri_t(n, blk):
    r = _iota((n, n), 0)
    c = _iota((n, n), 1)
    return (((r // blk) == (c // blk)) & (r <= c)).astype(bf16)


def _block_tri(n, blk):
    r = _iota((n, n), 0)
    c = _iota((n, n), 1)
    return (((r // blk) == (c // blk)) & (r >= c)).astype(bf16)


def _gdn_kernel(u_ref, gc_ref, gr_ref, cw_ref, alr_ref, dtr_ref, alc_ref, dtc_ref, o_ref, s_out_ref, ext_scr, s_scr, *, tb):
    i = pl.program_id(1)

    @pl.when(i == 0)
    def _():
        ext_scr[0:8, :] = jnp.zeros((8, 3 * GROUP_W), f32)
        s_scr[...] = jnp.zeros(s_scr.shape, f32)

    ext_scr[8:8 + tb, :] = u_ref[...]
    w = cw_ref[...]
    conv = ext_scr[8:8 + tb, :] * w[3:4]
    for jj in range(1, CONV_W):
        conv = conv + ext_scr[8 - jj:8 - jj + tb, :] * w[3 - jj:4 - jj]
    ext_scr[0:8, :] = ext_scr[tb:tb + 8, :]
    qkv = _silu(conv)

    def l2n(x):
        return x * lax.rsqrt(_group_sum(x * x, HEAD_W) + EPS)

    q = l2n(qkv[:, 0:GROUP_W]) * (HEAD_W ** -0.5)
    k = l2n(qkv[:, GROUP_W:2 * GROUP_W])
    v = qkv[:, 2 * GROUP_W:3 * GROUP_W]

    gc = gc_ref[...]
    g_col = -jnp.exp(alr_ref[...]) * _softplus(gc[:, 0:4] + dtr_ref[...])
    beta_col = jax.nn.sigmoid(gc[:, 4:8])
    gr = gr_ref[...]
    g_row = -jnp.exp(alc_ref[...]) * _softplus(gr[0:4, :] + dtc_ref[...])
    g_row8 = jnp.concatenate([g_row, jnp.zeros_like(g_row)], axis=0)
    gcum_row = _mm2(g_row8, _block_tri_t(tb, CHUNK))

    same, lower, strict = _bd_masks()
    tri = _tri(CHUNK, bf16)
    r = _iota((GROUP_W, GROUP_W), 0)
    c = _iota((GROUP_W, GROUP_W), 1)
    eye = (r == c).astype(f32)

    for ch in range(tb // CHUNK):
        lo = ch * CHUNK
        qc, kc, vc = q[lo:lo + CHUNK], k[lo:lo + CHUNK], v[lo:lo + CHUNK]
        gcum = _mm2l(tri, g_col[lo:lo + CHUNK])
        bcol = beta_col[lo:lo + CHUNK]
        g_stack = _stack_cols(gcum)
        b_stack = _stack_cols(bcol)
        g_cat = _cat_rows(gcum_row, lo)
        decay = jnp.exp(jnp.where(lower, g_stack - g_cat, NEG))
        ksm = _head_stack(kc)
        qsm = _head_stack(qc)
        kk = _mm3(ksm, ksm, NT)
        qk = _mm3(qsm, ksm, NT)
        m_bd = b_stack * kk * jnp.where(strict, decay, 0.0)
        x = eye
        for lev in range(6):
            sel = ((r >> (lev + 1)) == (c >> (lev + 1))) & (((r >> lev) & 1) == 1) & (((c >> lev) & 1) == 0)
            ck = jnp.where(sel, m_bd, 0.0)
            x = x - _mm3(_mm3(x, ck), x)
        s_bd = s_scr[...]
        kq_s = _mm3(jnp.concatenate([kc, qc], axis=0), s_bd)
        ks, qs = kq_s[0:CHUNK], kq_s[CHUNK:2 * CHUNK]
        eg_all = _expand_cols(jnp.exp(gcum))
        rhs = _expand_cols(bcol) * (vc - eg_all * ks)
        u_sm = _mm3(x, _head_stack(rhs))
        o_sm = _mm3(qk * decay, u_sm)
        o_ref[lo:lo + CHUNK, :] = eg_all * qs + _fold_heads(o_sm)
        u_all = _fold_heads(u_sm)
        g_last = gcum[CHUNK - 1:CHUNK, :]
        kw = kc * _expand_cols(jnp.exp(g_last - gcum))
        d_stack = jnp.concatenate(
            [jnp.broadcast_to(jnp.exp(g_last[:, h:h + 1]), (HEAD_W, 1)) for h in range(N_HEADS)], axis=0)
        s_scr[...] = d_stack * s_bd + jnp.where(same, _mm3(kw, u_all, TN), 0.0)

    @pl.when(i == pl.num_programs(1) - 1)
    def _():
        s_out_ref[...] = s_scr[...]


def _gdn_prompt(p, gt, conv_w, a_log, dt_bias, n_batch, seq):
    tb = min(256, seq)
    nb = seq // tb
    r14 = lambda z: z.astype(f32).reshape(1, N_HEADS)
    c41 = lambda z: z.astype(f32).reshape(N_HEADS, 1)
    o, s_bd = pl.pallas_call(
        functools.partial(_gdn_kernel, tb=tb),
        grid=(n_batch, nb),
        in_specs=[
            pl.BlockSpec((tb, 3 * GROUP_W), lambda b, i: (b * nb + i, 0)),
            pl.BlockSpec((tb, 128), lambda b, i: (b * nb + i, GATE_COL // 128)),
            pl.BlockSpec((16, tb), lambda b, i: (0, b * nb + i)),
            pl.BlockSpec((CONV_W, 3 * GROUP_W), lambda b, i: (0, 0)),
            pl.BlockSpec((1, N_HEADS), lambda b, i: (0, 0)),
            pl.BlockSpec((1, N_HEADS), lambda b, i: (0, 0)),
            pl.BlockSpec((N_HEADS, 1), lambda b, i: (0, 0)),
            pl.BlockSpec((N_HEADS, 1), lambda b, i: (0, 0)),
        ],
        out_specs=[
            pl.BlockSpec((tb, GROUP_W), lambda b, i: (b * nb + i, 0)),
            pl.BlockSpec((None, GROUP_W, GROUP_W), lambda b, i: (b, 0, 0)),
        ],
        out_shape=[jax.ShapeDtypeStruct((n_batch * seq, GROUP_W), f32),
                   jax.ShapeDtypeStruct((n_batch, GROUP_W, GROUP_W), f32)],
        scratch_shapes=[pltpu.VMEM((tb + 8, 3 * GROUP_W), f32), pltpu.VMEM((GROUP_W, GROUP_W), f32)],
        compiler_params=_cparams("arbitrary", "arbitrary"),
        name="gdn_prompt",
    )(p, p, gt, conv_w.astype(f32), r14(a_log), r14(dt_bias), c41(a_log), c41(dt_bias))
    return o, _bd_diag(s_bd)


def _bd_diag(s_bd):
    n_b = s_bd.shape[0]
    s5 = s_bd.reshape(n_b, N_HEADS, HEAD_W, N_HEADS, HEAD_W)
    return jnp.stack([s5[:, h, :, h, :] for h in range(N_HEADS)], axis=1)


def _hgrn_kernel(q_ref, f_ref, i_ref, lb_ref, o_ref, s_out_ref, st_scr, q_scr, k_scr, b_scr, *, tb):
    blk = pl.program_id(1)

    @pl.when(blk == 0)
    def _():
        st_scr[...] = jnp.zeros(st_scr.shape, f32)

    lb = lb_ref[...]
    z = f_ref[...]
    logf = jnp.log(lb + (1.0 - lb) * jax.nn.sigmoid(z))
    q_scr[...] = _silu(q_ref[...])
    k_scr[...] = (1.0 - lb) * jax.nn.sigmoid(-z)
    b_scr[...] = _mm2l(_block_tri(tb, SUB), logf)

    same, _, _ = _bd_masks()
    ones_bd = _group_ones(GROUP_W, HEAD_W)
    row = _iota((SUB * SUB, GROUP_W), 0)
    tmask = (row % SUB) >= (row // SUB)

    def rep_t(x):
        return jnp.broadcast_to(x[None], (SUB, SUB, GROUP_W)).reshape(SUB * SUB, GROUP_W)

    def rep_j(x):
        return jnp.broadcast_to(x[:, None, :], (SUB, SUB, GROUP_W)).reshape(SUB * SUB, GROUP_W)

    def body(c, carry):
        r0 = pl.multiple_of(c * SUB, SUB)
        qs = q_scr[pl.ds(r0, SUB), :]
        ks = k_scr[pl.ds(r0, SUB), :]
        vs = i_ref[pl.ds(r0, SUB), :]
        bs = b_scr[pl.ds(r0, SUB), :]
        st = st_scr[...]
        o_inter = _mm3(qs * jnp.exp(bs), st, NT)
        wgt = rep_t(qs) * jnp.exp(jnp.where(tmask, rep_t(bs) - rep_j(bs), NEG)) * rep_j(ks)
        a = _mm2(wgt, ones_bd)
        o_diag = jnp.sum((a * rep_j(vs)).reshape(SUB, SUB, GROUP_W), axis=0)
        o_ref[pl.ds(r0, SUB), :] = o_inter + o_diag
        b_last = bs[SUB - 1:SUB, :]
        kw = ks * jnp.exp(b_last - bs)
        st_scr[...] = st * jnp.exp(b_last) + jnp.where(same, _mm3(vs, kw, TN), 0.0)
        return carry

    lax.fori_loop(0, tb // SUB, body, 0)

    @pl.when(blk == pl.num_programs(1) - 1)
    def _():
        s_out_ref[...] = st_scr[...]


def _hgrn_prompt(p, lb, n_batch, seq):
    tb = min(256, seq)
    nb = seq // tb
    blk = lambda col: pl.BlockSpec((tb, GROUP_W), lambda b, i: (b * nb + i, col))
    o, st = pl.pallas_call(
        functools.partial(_hgrn_kernel, tb=tb),
        grid=(n_batch, nb),
        in_specs=[blk(7), blk(8), blk(9), pl.BlockSpec((1, GROUP_W), lambda b, i: (0, 0))],
        out_specs=[
            pl.BlockSpec((tb, GROUP_W), lambda b, i: (b * nb + i, 0)),
            pl.BlockSpec((None, GROUP_W, GROUP_W), lambda b, i: (b, 0, 0)),
        ],
        out_shape=[jax.ShapeDtypeStruct((n_batch * seq, GROUP_W), f32),
                   jax.ShapeDtypeStruct((n_batch, GROUP_W, GROUP_W), f32)],
        scratch_shapes=[pltpu.VMEM((GROUP_W, GROUP_W), f32)] + [pltpu.VMEM((tb, GROUP_W), f32)] * 3,
        compiler_params=_cparams("arbitrary", "arbitrary"),
        name="hgrn_prompt",
    )(p, p, p, lb.astype(f32).reshape(1, GROUP_W))
    return o, jnp.swapaxes(_bd_diag(st), -1, -2)


def _log_sigmoid(x):
    return jnp.minimum(x, 0.0) - jnp.log1p(jnp.exp(-jnp.abs(x)))


def _mlstm_kernel(q_ref, k_ref, v_ref, gc_ref, gr_ref, ibr_ref, fbr_ref, ibc_ref, fbc_ref,
                  o_ref, c_out_ref, n_out_ref, m_out_ref, c_scr, n_scr, m_scr, *, tb):
    blk = pl.program_id(1)

    @pl.when(blk == 0)
    def _():
        c_scr[...] = jnp.zeros(c_scr.shape, f32)
        n_scr[...] = jnp.zeros(n_scr.shape, f32)
        m_scr[...] = jnp.zeros(m_scr.shape, f32)

    q = q_ref[...]
    k = k_ref[...] * (HEAD_W ** -0.5)
    v = v_ref[...]
    gc = gc_ref[...]
    li_col = gc[:, 8:12] + ibr_ref[...]
    lf_col = _log_sigmoid(gc[:, 12:16] + fbr_ref[...])
    gr = gr_ref[...]
    li_row = gr[8:12, :] + ibc_ref[...]
    lf_row = _log_sigmoid(gr[12:16, :] + fbc_ref[...])
    b_row = _mm2(jnp.concatenate([lf_row, jnp.zeros_like(lf_row)], axis=0), _block_tri_t(tb, CHUNK))

    same, lower, _ = _bd_masks()
    tri = _tri(CHUNK, bf16)

    for ch in range(tb // CHUNK):
        lo = ch * CHUNK
        qc, kc, vc = q[lo:lo + CHUNK], k[lo:lo + CHUNK], v[lo:lo + CHUNK]
        b_col = _mm2l(tri, lf_col[lo:lo + CHUNK])
        b_stack = _stack_cols(b_col)
        d_mat = jnp.where(lower, b_stack - _cat_rows(b_row, lo) + _cat_rows(li_row, lo), NEG)
        m_row = m_scr[...]
        m_stack = jnp.concatenate(
            [jnp.broadcast_to(m_row[:, h:h + 1], (CHUNK, 1)) for h in range(N_HEADS)], axis=0)
        inter = b_stack + m_stack
        m_t = jnp.maximum(inter, jnp.max(d_mat, axis=-1, keepdims=True))
        w_inter = jnp.exp(inter - m_t)
        qsm = _head_stack(qc)
        ksm = _head_stack(kc)
        pmat = _mm3(qsm, ksm, NT) * jnp.exp(d_mat - m_t)
        c_bd = c_scr[...]
        n_row = n_scr[...]
        num = w_inter * _mm3(qsm, c_bd) + _mm3(pmat, _head_stack(vc))
        den = w_inter * jnp.sum(qsm * n_row, axis=-1, keepdims=True) + jnp.sum(pmat, axis=-1, keepdims=True)
        h_sm = num / jnp.maximum(jnp.abs(den), jnp.exp(-m_t))
        o_ref[lo:lo + CHUNK, :] = _fold_heads(h_sm)
        m_new = jnp.concatenate(
            [m_t[h * CHUNK + CHUNK - 1:h * CHUNK + CHUNK, :] for h in range(N_HEADS)], axis=1)
        b_last = b_col[CHUNK - 1:CHUNK, :]
        w_end = jnp.exp(b_last - b_col + li_col[lo:lo + CHUNK] - m_new)
        d0 = jnp.exp(b_last + m_row - m_new)
        kw = kc * _expand_cols(w_end)
        d0_stack = jnp.concatenate(
            [jnp.broadcast_to(d0[:, h:h + 1], (HEAD_W, 1)) for h in range(N_HEADS)], axis=0)
        c_scr[...] = d0_stack * c_bd + jnp.where(same, _mm3(kw, vc, TN), 0.0)
        n_scr[...] = _expand_cols(d0) * n_row + jnp.sum(kw, axis=0, keepdims=True)
        m_scr[...] = m_new

    @pl.when(blk == pl.num_programs(1) - 1)
    def _():
        c_out_ref[...] = c_scr[...]
        n_out_ref[...] = n_scr[...]
        m_out_ref[...] = m_scr[...]


def _mlstm_prompt(p, gt, i_bias, f_bias, n_batch, seq):
    tb = min(256, seq)
    nb = seq // tb
    blk = lambda col: pl.BlockSpec((tb, GROUP_W), lambda b, i: (b * nb + i, col))
    r14 = lambda z: z.astype(f32).reshape(1, N_HEADS)
    c41 = lambda z: z.astype(f32).reshape(N_HEADS, 1)
    small = lambda shape: pl.BlockSpec(shape, lambda b, i: (0, 0))
    o, c_bd, n_row, m_row = pl.pallas_call(
        functools.partial(_mlstm_kernel, tb=tb),
        grid=(n_batch, nb),
        in_specs=[blk(11), blk(12), blk(13),
                  pl.BlockSpec((tb, 128), lambda b, i: (b * nb + i, GATE_COL // 128)),
                  pl.BlockSpec((16, tb), lambda b, i: (0, b * nb + i)),
                  small((1, N_HEADS)), small((1, N_HEADS)), small((N_HEADS, 1)), small((N_HEADS, 1))],
        out_specs=[
            pl.BlockSpec((tb, GROUP_W), lambda b, i: (b * nb + i, 0)),
            pl.BlockSpec((None, GROUP_W, GROUP_W), lambda b, i: (b, 0, 0)),
            pl.BlockSpec((None, 1, GROUP_W), lambda b, i: (b, 0, 0)),
            pl.BlockSpec((None, 1, N_HEADS), lambda b, i: (b, 0, 0)),
        ],
        out_shape=[jax.ShapeDtypeStruct((n_batch * seq, GROUP_W), f32),
                   jax.ShapeDtypeStruct((n_batch, GROUP_W, GROUP_W), f32),
                   jax.ShapeDtypeStruct((n_batch, 1, GROUP_W), f32),
                   jax.ShapeDtypeStruct((n_batch, 1, N_HEADS), f32)],
        scratch_shapes=[pltpu.VMEM((GROUP_W, GROUP_W), f32), pltpu.VMEM((1, GROUP_W), f32),
                        pltpu.VMEM((1, N_HEADS), f32)],
        compiler_params=_cparams("arbitrary", "arbitrary"),
        name="mlstm_prompt",
    )(p, p, p, p, gt, r14(i_bias), r14(f_bias), c41(i_bias), c41(f_bias))
    return (o, _bd_diag(c_bd), n_row.reshape(n_batch, N_HEADS, HEAD_W), m_row.reshape(n_batch, N_HEADS))


def _gdn_dec_prep_kernel(u_ref, buf_ref, cw_ref, q_ref, k_ref, v_ref):
    w = cw_ref[...]
    conv = u_ref[...] * w[3:4]
    for jj in range(CONV_W - 1):
        conv = conv + buf_ref[jj] * w[jj:jj + 1]
    qkv = _silu(conv)

    def l2n(x):
        return x * lax.rsqrt(_group_sum(x * x, HEAD_W) + EPS)

    q_ref[...] = l2n(qkv[:, 0:GROUP_W]) * (HEAD_W ** -0.5)
    k_ref[...] = l2n(qkv[:, GROUP_W:2 * GROUP_W])
    v_ref[...] = qkv[:, 2 * GROUP_W:3 * GROUP_W]


def _gdn_dec_prep(p, conv_buf, conv_w):
    n_b = p.shape[0]
    out = jax.ShapeDtypeStruct((n_b, GROUP_W), f32)
    return pl.pallas_call(
        _gdn_dec_prep_kernel,
        grid=(1,),
        in_specs=[pl.BlockSpec((n_b, 3 * GROUP_W), lambda i: (0, 0)),
                  pl.BlockSpec((CONV_W - 1, n_b, 3 * GROUP_W), lambda i: (0, 0, 0)),
                  pl.BlockSpec((CONV_W, 3 * GROUP_W), lambda i: (0, 0))],
        out_specs=[pl.BlockSpec((n_b, GROUP_W), lambda i: (0, 0))] * 3,
        out_shape=[out, out, out],
        compiler_params=_cparams("arbitrary"),
        name="gdn_dec_prep",
    )(p, jnp.swapaxes(conv_buf.astype(f32), 0, 1), conv_w.astype(f32))


def _rec_decode_kernel(gq_ref, gk_ref, gv_ref, ga_ref, gb_ref, al_ref, dtb_ref, sg_ref,
                       cq_ref, cf_ref, ci_ref, lbc_ref, sh_ref,
                       dq_ref, dk_ref, dv_ref, di_ref, df_ref, ib_ref, fb_ref, sc_ref, sn_ref, sm_ref,
                       oa_ref, sg_out, oc_ref, sh_out, od_ref, sc_out, sn_out, sm_out):
    q, k, v = gq_ref[...], gk_ref[...], gv_ref[...]
    s = sg_ref[...]
    g = -jnp.exp(al_ref[...]) * _softplus(ga_ref[...] + dtb_ref[...])
    eg = jnp.exp(g)
    beta = jax.nn.sigmoid(gb_ref[...])
    ks = jnp.sum(k * s, axis=1, keepdims=True)
    qs = jnp.sum(q * s, axis=1, keepdims=True)
    u = beta * (v - eg * ks)
    qk = jnp.sum(q * k, axis=1, keepdims=True)
    oa_ref[...] = eg * qs + qk * u
    sg_out[...] = eg * s + k * u

    lb = lbc_ref[...]
    z = cf_ref[...]
    logf = jnp.log(lb + (1.0 - lb) * jax.nn.sigmoid(z))
    kc = (1.0 - lb) * jax.nn.sigmoid(-z)
    qc = _silu(cq_ref[...])
    vc = ci_ref[...]
    sh = sh_ref[...]
    ef = jnp.exp(logf)
    oc_ref[...] = jnp.sum((qc * ef) * sh, axis=1, keepdims=True) + jnp.sum(qc * kc, axis=1, keepdims=True) * vc
    sh_out[...] = ef * sh + kc * vc

    qd = dq_ref[...]
    kd = dk_ref[...] * (HEAD_W ** -0.5)
    vd = dv_ref[...]
    li = di_ref[...] + ib_ref[...]
    lf = _log_sigmoid(df_ref[...] + fb_ref[...])
    m0 = sm_ref[...]
    cs = sc_ref[...]
    n0 = sn_ref[...]
    inter = lf + m0
    m_t = jnp.maximum(inter, li)
    w_inter = jnp.exp(inter - m_t)
    qkd = jnp.sum(qd * kd, axis=1, keepdims=True) * jnp.exp(li - m_t)
    q_c = jnp.sum(qd.astype(bf16).astype(f32) * cs.astype(bf16).astype(f32), axis=1, keepdims=True)
    num = w_inter * q_c + qkd * vd
    den = w_inter * jnp.sum(qd * n0, axis=1, keepdims=True) + qkd
    od_ref[...] = num / jnp.maximum(jnp.abs(den), jnp.exp(-m_t))
    w_end = jnp.exp(li - m_t)
    d0 = jnp.exp(lf + m0 - m_t)
    sc_out[...] = d0 * cs + (w_end * kd) * vd
    sn_out[...] = d0 * n0 + w_end * kd
    sm_out[...] = m_t


def _rec_decode(p, gq, gk, gv, a_log, dt_bias, lb, i_bias, f_bias, s_gdn, s_hgrn, s_c, s_n, s_m):
    n_b = p.shape[0]
    rows = n_b * N_HEADS
    rb = min(16, rows)
    col = lambda z: z.reshape(rows, HEAD_W, 1)
    vrow = lambda z: z.reshape(rows, 1, HEAD_W)
    sca = lambda z: z.reshape(rows, 1, 1)
    per_head = lambda z: jnp.tile(z.astype(f32), n_b).reshape(rows, 1, 1)
    blockp = lambda b: p[:, b * GROUP_W:(b + 1) * GROUP_W]
    gates = p[:, GATE_COL:GATE_COL + 16]
    lb_col = jnp.tile(lb.astype(f32).reshape(N_HEADS, HEAD_W), (n_b, 1)).reshape(rows, HEAD_W, 1)
    st = lambda z: z.astype(f32).reshape(rows, HEAD_W, HEAD_W)
    args = [col(gq), col(gk), vrow(gv), sca(gates[:, 0:4]), sca(gates[:, 4:8]), per_head(a_log), per_head(dt_bias), st(s_gdn),
            col(blockp(7)), col(blockp(8)), vrow(blockp(9)), lb_col, st(s_hgrn),
            col(blockp(11)), col(blockp(12)), vrow(blockp(13)), sca(gates[:, 8:12]), sca(gates[:, 12:16]),
            per_head(i_bias), per_head(f_bias), st(s_c), col(s_n.astype(f32)), sca(s_m.astype(f32))]

    def spec(a):
        return pl.BlockSpec((rb,) + a.shape[1:], lambda i: (i, 0, 0))

    o_vrow = jax.ShapeDtypeStruct((rows, 1, HEAD_W), f32)
    o_st = jax.ShapeDtypeStruct((rows, HEAD_W, HEAD_W), f32)
    o_col = jax.ShapeDtypeStruct((rows, HEAD_W, 1), f32)
    o_sca = jax.ShapeDtypeStruct((rows, 1, 1), f32)
    outs = [o_vrow, o_st, o_vrow, o_st, o_vrow, o_st, o_col, o_sca]
    res = pl.pallas_call(
        _rec_decode_kernel,
        grid=(rows // rb,),
        in_specs=[spec(a) for a in args],
        out_specs=[spec(a) for a in outs],
        out_shape=outs,
        compiler_params=_cparams("arbitrary"),
        name="rec_decode",
    )(*args)
    oa, sg, oc, sh, od, sc, sn, sm = res
    s4 = lambda z: z.reshape(n_b, N_HEADS, HEAD_W, HEAD_W)
    o2 = lambda z: z.reshape(n_b, GROUP_W)
    return (o2(oa), s4(sg), o2(oc), s4(sh), o2(od), s4(sc),
            sn.reshape(n_b, N_HEADS, HEAD_W), sm.reshape(n_b, N_HEADS))


def _permute_w_in(w):
    d_in = w.shape[1]
    a_gate0 = 3 * GROUP_W
    d_gate0 = d_in - GROUP_W - 8
    main = jnp.concatenate([w[:, 0:a_gate0], w[:, a_gate0 + 8:d_gate0], w[:, d_gate0 + 8:]], axis=1)
    gates = jnp.concatenate([w[:, a_gate0:a_gate0 + 8], w[:, d_gate0:d_gate0 + 8]], axis=1)
    pad = jnp.zeros((w.shape[0], P_COLS - main.shape[1] - 16), w.dtype)
    return jnp.concatenate([main, gates, pad], axis=1).astype(bf16), gates.T.astype(bf16)


def kernel(x_prompt, x_sample, page_table, cache_k, cache_v, state_gdn_conv, state_gdn, state_hgrn, state_mlstm_C, state_mlstm_n, state_mlstm_m, attn_norm_g, w_in, gdn_conv_w, gdn_a_log, gdn_dt_bias, gdn_norm_g, diff_qk_norm_g, diff_lambda, diff_subln_g, rel_bias, hgrn_lb_logits, hgrn_norm_g, mlstm_i_bias, mlstm_f_bias, mlstm_norm_g, w_out, ffn_norm_g, ffn_w_gate, ffn_w_up, ffn_w_down, moe_router, moe_w_gate, moe_w_up, moe_w_down):
    depth = w_in.shape[0]
    n_bp, seq, _ = x_prompt.shape
    n_bs = x_sample.shape[0]
    n_pool, page = cache_k.shape[1], cache_k.shape[2]
    dt = x_prompt.dtype

    lb_p = jax.nn.softmax(hgrn_lb_logits.astype(f32), axis=0)
    lb_cum = jnp.cumsum(lb_p, axis=0)
    hgrn_lb = lb_cum - lb_cum[0:1]
    cache_k4 = jnp.transpose(cache_k, (0, 1, 3, 4, 5, 2)).reshape(depth, n_pool, GROUP_W, page)
    cache_v4 = jnp.transpose(cache_v, (0, 1, 3, 4, 2)).reshape(depth, n_pool, GROUP_W, page)

    xp = x_prompt.reshape(n_bp * seq, D_MODEL)
    xs = x_sample.reshape(n_bs, D_MODEL)
    outs_p, outs_s = [], []
    for l in range(depth):
        w_perm, w_gate_t = _permute_w_in(w_in[l])
        w_out_b = w_out[l].astype(bf16)
        gains = jnp.stack([jnp.tile(g.astype(f32), N_HEADS) for g in
                           (gdn_norm_g[l], diff_subln_g[l], hgrn_norm_g[l], mlstm_norm_g[l])])
        lam_init = 0.8 - 0.6 * math.exp(-0.3 * l)
        lam32 = diff_lambda[l].astype(f32)
        lam = jnp.exp(jnp.sum(lam32[0] * lam32[1])) - jnp.exp(jnp.sum(lam32[2] * lam32[3])) + lam_init
        if l % 2 == 0:
            ffn_w = (ffn_w_gate[l // 2].astype(bf16), ffn_w_up[l // 2].astype(bf16), ffn_w_down[l // 2].astype(bf16))
        else:
            router_pad = jnp.pad(moe_router[l // 2].astype(bf16), ((0, 0), (0, 128 - N_EXPERTS)))
            moe_w = (moe_w_gate[l // 2].astype(bf16), moe_w_up[l // 2].astype(bf16), moe_w_down[l // 2].astype(bf16))

        def channel_mix(x):
            if l % 2 == 0:
                return _ffn(x, ffn_norm_g[l], *ffn_w)
            h, comb = _router(x, ffn_norm_g[l], router_pad)
            return _moe(x, h, comb, *moe_w)

        p, gt = _inproj(xp, attn_norm_g[l], w_perm, w_gate_t)
        qnt, kn, vt = _bprep(p, diff_qk_norm_g[l], True)
        ob = _attn_prompt(qnt, kn, vt, lam, rel_bias, n_bp, seq)
        oa, s_gdn = _gdn_prompt(p, gt, gdn_conv_w[l], gdn_a_log[l], gdn_dt_bias[l], n_bp, seq)
        oc, s_hgrn = _hgrn_prompt(p, hgrn_lb[l], n_bp, seq)
        od, s_c, s_n, s_m = _mlstm_prompt(p, gt, mlstm_i_bias[l], mlstm_f_bias[l], n_bp, seq)
        xp = _outproj(oa, ob, oc, od, p, xp, gains, w_out_b, 1.0 - lam_init)
        xp = channel_mix(xp)
        p3 = p.reshape(n_bp, seq, P_COLS)
        outs_p.append((
            kn.reshape(n_bp, seq, N_HEADS, 2, DKB).astype(dt),
            p3[:, :, 6 * GROUP_W:7 * GROUP_W].reshape(n_bp, seq, N_HEADS, HEAD_W).astype(dt),
            p3[:, seq - (CONV_W - 1):, 0:3 * GROUP_W].astype(dt),
            s_gdn.astype(dt), s_hgrn.astype(dt), s_c.astype(dt), s_n.astype(dt), s_m.astype(dt)))

        p, gt = _inproj(xs, attn_norm_g[l], w_perm, w_gate_t)
        qn, kn = _bprep(p, diff_qk_norm_g[l], False)
        vn = p[:, 6 * GROUP_W:7 * GROUP_W]
        ob = _attn_decode(qn, kn, vn, page_table, cache_k4, cache_v4, l, lam, rel_bias)
        u = p[:, 0:3 * GROUP_W]
        gq, gk, gv = _gdn_dec_prep(u, state_gdn_conv[l], gdn_conv_w[l])
        oa, s_gdn, oc, s_hgrn, od, s_c, s_n, s_m = _rec_decode(
            p, gq, gk, gv, gdn_a_log[l], gdn_dt_bias[l], hgrn_lb[l], mlstm_i_bias[l], mlstm_f_bias[l],
            state_gdn[l], state_hgrn[l], state_mlstm_C[l], state_mlstm_n[l], state_mlstm_m[l])
        xs = _outproj(oa, ob, oc, od, p, xs, gains, w_out_b, 1.0 - lam_init)
        xs = channel_mix(xs)
        conv_new = jnp.concatenate([state_gdn_conv[l][:, 1:].astype(dt), u[:, None, :].astype(dt)], axis=1)
        outs_s.append((
            kn.reshape(n_bs, 1, N_HEADS, 2, DKB).astype(dt),
            vn.reshape(n_bs, 1, N_HEADS, HEAD_W).astype(dt),
            conv_new, s_gdn.astype(dt), s_hgrn.astype(dt), s_c.astype(dt), s_n.astype(dt), s_m.astype(dt)))

    kp, vp, convp, gdnp, hgrnp, mcp, mnp_, mmp = [jnp.stack(z) for z in zip(*outs_p)]
    ks_, vs_, convs, gdns, hgrns, mcs, mns, mms = [jnp.stack(z) for z in zip(*outs_s)]
    return (xp.reshape(n_bp, seq, D_MODEL), xs.reshape(n_bs, 1, D_MODEL), kp, vp, ks_, vs_, convp, convs,
            gdnp, gdns, hgrnp, hgrns, mcp, mcs, mnp_, mns, mmp, mms)
```

```python
import functools
import math

import numpy as np
import jax
import jax.numpy as jnp
from jax import lax
from jax.experimental import pallas as pl
from jax.experimental.pallas import tpu as pltpu

f32 = jnp.float32
bf16 = jnp.bfloat16

D_MODEL = 1024
N_HEADS = 4
HEAD_W = 64
GROUP_W = N_HEADS * HEAD_W
DKB = 32
CONV_W = 4
CHUNK = 64
SUB = 16
NUM_BUCKETS = 32
MAX_DISTANCE = 128
N_EXPERTS = 8
EPS = 1e-6
NEG = -1e30
P_COLS = 4096
GATE_COL = 3840
VMEM_LIMIT = 56 * 1024 * 1024

NN = ((1,), (0,))
NT = ((1,), (1,))
TN = ((0,), (0,))


def _dg(a, b, dims=NN):
    return lax.dot_general(a, b, (dims, ((), ())), preferred_element_type=f32)


def _split(a):
    hi = a.astype(bf16)
    lo = (a - hi.astype(f32)).astype(bf16)
    return hi, lo


def _mm3(a, b, dims=NN):
    ah, al = _split(a)
    bh, bl = _split(b)
    return _dg(ah, bh, dims) + (_dg(ah, bl, dims) + _dg(al, bh, dims))


def _mm2(a, b01, dims=NN):
    ah, al = _split(a)
    return _dg(ah, b01, dims) + _dg(al, b01, dims)


def _mm2l(a01, b, dims=NN):
    bh, bl = _split(b)
    return _dg(a01, bh, dims) + _dg(a01, bl, dims)


def _mm1(a, b, dims=NN):
    return _dg(a.astype(bf16), b.astype(bf16), dims)


def _iota(shape, dim):
    return lax.broadcasted_iota(jnp.int32, shape, dim)


def _group_ones(width, group):
    r = _iota((width, width), 0) // group
    c = _iota((width, width), 1) // group
    return (r == c).astype(bf16)


def _group_sum(x, group):
    return _mm2(x, _group_ones(x.shape[-1], group))


def _silu(x):
    return x * jax.nn.sigmoid(x)


def _softplus(x):
    return jnp.maximum(x, 0.0) + jnp.log1p(jnp.exp(-jnp.abs(x)))


def _stack_cols(xc, n=N_HEADS, rows=HEAD_W):
    return jnp.concatenate([xc[:, h:h + 1] for h in range(n)], axis=0)


def _expand_cols(xc, n=N_HEADS, width=HEAD_W):
    r = xc.shape[0]
    return jnp.concatenate([jnp.broadcast_to(xc[:, h:h + 1], (r, width)) for h in range(n)], axis=1)


def _cat_rows(xr, lo, n=N_HEADS, width=HEAD_W):
    return jnp.concatenate([xr[h:h + 1, lo:lo + width] for h in range(n)], axis=1)


def _head_stack(x, n=N_HEADS, width=HEAD_W):
    lane_head = _iota(x.shape, 1) // width
    return jnp.concatenate([jnp.where(lane_head == h, x, 0.0) for h in range(n)], axis=0)


def _fold_heads(x_sm, n=N_HEADS):
    r = x_sm.shape[0] // n
    out = x_sm[0:r]
    for h in range(1, n):
        out = out + x_sm[h * r:(h + 1) * r]
    return out


def _bd_masks(n=GROUP_W, blk=CHUNK):
    r = _iota((n, n), 0)
    c = _iota((n, n), 1)
    same = (r // blk) == (c // blk)
    lower = same & ((r % blk) >= (c % blk))
    strict = same & ((r % blk) > (c % blk))
    return same, lower, strict


def _cparams(*sem):
    return pltpu.CompilerParams(dimension_semantics=sem, vmem_limit_bytes=VMEM_LIMIT)


def _inproj_kernel(x_ref, g_ref, w_ref, wgt_ref, p_ref, gt_ref, h_scr):
    @pl.when(pl.program_id(1) == 0)
    def _():
        x = x_ref[...]
        ms = jnp.mean(x * x, axis=-1, keepdims=True)
        h = ((x * lax.rsqrt(ms + EPS)) * g_ref[...]).astype(bf16)
        h_scr[...] = h
        gt_ref[...] = _dg(wgt_ref[...], h, NT)
    p_ref[...] = _dg(h_scr[...], w_ref[...], NN)


def _inproj(x, g, w_perm, w_gate_t):
    t = x.shape[0]
    tm = min(512, t)
    tn = 1024
    return pl.pallas_call(
        _inproj_kernel,
        grid=(t // tm, P_COLS // tn),
        in_specs=[
            pl.BlockSpec((tm, D_MODEL), lambda i, j: (i, 0)),
            pl.BlockSpec((1, D_MODEL), lambda i, j: (0, 0)),
            pl.BlockSpec((D_MODEL, tn), lambda i, j: (0, j)),
            pl.BlockSpec((16, D_MODEL), lambda i, j: (0, 0)),
        ],
        out_specs=[
            pl.BlockSpec((tm, tn), lambda i, j: (i, j)),
            pl.BlockSpec((16, tm), lambda i, j: (0, i)),
        ],
        out_shape=[jax.ShapeDtypeStruct((t, P_COLS), f32), jax.ShapeDtypeStruct((16, t), f32)],
        scratch_shapes=[pltpu.VMEM((tm, D_MODEL), bf16)],
        compiler_params=_cparams("arbitrary", "arbitrary"),
        name="inproj",
    )(x, g.reshape(1, D_MODEL), w_perm, w_gate_t)


def _qk_gnorm(x, g):
    ms = _group_sum(x * x, DKB) * (1.0 / DKB)
    return (x * lax.rsqrt(ms + EPS)) * g


def _bprep_kernel(q_ref, k_ref, gq_ref, gk_ref, qn_ref, kn_ref):
    qn_ref[...] = _qk_gnorm(q_ref[...], gq_ref[...])
    kn_ref[...] = _qk_gnorm(k_ref[...], gk_ref[...])


def _bprep_t_kernel(q_ref, k_ref, v_ref, gq_ref, gk_ref, qnt_ref, kn_ref, vt_ref):
    qnt_ref[...] = _qk_gnorm(q_ref[...], gq_ref[...]).T
    kn_ref[...] = _qk_gnorm(k_ref[...], gk_ref[...])
    vt_ref[...] = v_ref[...].T


def _bprep(p, qk_norm_g, transposed):
    t = p.shape[0]
    tm = min(512, t)
    gq = jnp.tile(qk_norm_g[0], GROUP_W // DKB).reshape(1, GROUP_W)
    gk = jnp.tile(qk_norm_g[1], GROUP_W // DKB).reshape(1, GROUP_W)
    col = lambda c: pl.BlockSpec((tm, GROUP_W), lambda i: (i, c))
    gain = pl.BlockSpec((1, GROUP_W), lambda i: (0, 0))
    rows = pl.BlockSpec((tm, GROUP_W), lambda i: (i, 0))
    rows_t = pl.BlockSpec((GROUP_W, tm), lambda i: (0, i))
    if transposed:
        return pl.pallas_call(
            _bprep_t_kernel,
            grid=(t // tm,),
            in_specs=[col(4), col(5), col(6), gain, gain],
            out_specs=[rows_t, rows, rows_t],
            out_shape=[jax.ShapeDtypeStruct((GROUP_W, t), f32), jax.ShapeDtypeStruct((t, GROUP_W), f32),
                       jax.ShapeDtypeStruct((GROUP_W, t), f32)],
            compiler_params=_cparams("arbitrary"),
            name="bprep_t",
        )(p, p, p, gq, gk)
    return pl.pallas_call(
        _bprep_kernel,
        grid=(t // tm,),
        in_specs=[col(4), col(5), gain, gain],
        out_specs=[rows, rows],
        out_shape=[jax.ShapeDtypeStruct((t, GROUP_W), f32)] * 2,
        compiler_params=_cparams("arbitrary"),
        name="bprep",
    )(p, p, gq, gk)


def _outproj_kernel(oa_ref, ob_ref, oc_ref, od_ref, ag_ref, cg_ref, dg_ref, x_ref, g_ref, w_ref, y_ref, *, b_scale):
    def gnorm(x, g):
        ms = _group_sum(x * x, HEAD_W) * (1.0 / HEAD_W)
        return (x * lax.rsqrt(ms + EPS)) * g
    g = g_ref[...]
    mixes = (
        gnorm(oa_ref[...], g[0:1]) * _silu(ag_ref[...]),
        gnorm(ob_ref[...], g[1:2]) * b_scale,
        gnorm(oc_ref[...], g[2:3]) * jax.nn.sigmoid(cg_ref[...]),
        gnorm(od_ref[...], g[3:4]) * jax.nn.sigmoid(dg_ref[...]),
    )
    y = x_ref[...]
    for i, m in enumerate(mixes):
        y = y + _dg(m.astype(bf16), w_ref[i * GROUP_W:(i + 1) * GROUP_W, :], NN)
    y_ref[...] = y


def _outproj(oa, ob, oc, od, p, x, gains, w_out, b_scale):
    t = x.shape[0]
    tm = min(512, t)
    row = lambda i: (i, 0)
    return pl.pallas_call(
        functools.partial(_outproj_kernel, b_scale=b_scale),
        grid=(t // tm,),
        in_specs=[
            pl.BlockSpec((tm, GROUP_W), row), pl.BlockSpec((tm, GROUP_W), row),
            pl.BlockSpec((tm, GROUP_W), row), pl.BlockSpec((tm, GROUP_W), row),
            pl.BlockSpec((tm, GROUP_W), lambda i: (i, 3)),
            pl.BlockSpec((tm, GROUP_W), lambda i: (i, 10)),
            pl.BlockSpec((tm, GROUP_W), lambda i: (i, 14)),
            pl.BlockSpec((tm, D_MODEL), row),
            pl.BlockSpec((4, GROUP_W), lambda i: (0, 0)),
            pl.BlockSpec((D_MODEL, D_MODEL), lambda i: (0, 0)),
        ],
        out_specs=pl.BlockSpec((tm, D_MODEL), row),
        out_shape=jax.ShapeDtypeStruct((t, D_MODEL), f32),
        compiler_params=_cparams("arbitrary"),
        name="outproj",
    )(oa, ob, oc, od, p, p, p, x, gains, w_out)


def _ffn_kernel(x_ref, g_ref, wg_ref, wu_ref, wd_ref, y_ref, h_scr, acc_scr):
    f = pl.program_id(1)

    @pl.when(f == 0)
    def _():
        x = x_ref[...]
        ms = jnp.mean(x * x, axis=-1, keepdims=True)
        h_scr[...] = ((x * lax.rsqrt(ms + EPS)) * g_ref[...]).astype(bf16)
        acc_scr[...] = x

    h = h_scr[...]
    a = _silu(_dg(h, wg_ref[...])) * _dg(h, wu_ref[...])
    acc_scr[...] += _dg(a.astype(bf16), wd_ref[...])

    @pl.when(f == pl.num_programs(1) - 1)
    def _():
        y_ref[...] = acc_scr[...]


def _ffn(x, g, wg, wu, wd):
    t = x.shape[0]
    d_ff = wg.shape[1]
    tm = min(512, t)
    tf = d_ff // 2
    return pl.pallas_call(
        _ffn_kernel,
        grid=(t // tm, d_ff // tf),
        in_specs=[
            pl.BlockSpec((tm, D_MODEL), lambda i, f: (i, 0)),
            pl.BlockSpec((1, D_MODEL), lambda i, f: (0, 0)),
            pl.BlockSpec((D_MODEL, tf), lambda i, f: (0, f)),
            pl.BlockSpec((D_MODEL, tf), lambda i, f: (0, f)),
            pl.BlockSpec((tf, D_MODEL), lambda i, f: (f, 0)),
        ],
        out_specs=pl.BlockSpec((tm, D_MODEL), lambda i, f: (i, 0)),
        out_shape=jax.ShapeDtypeStruct((t, D_MODEL), f32),
        scratch_shapes=[pltpu.VMEM((tm, D_MODEL), bf16), pltpu.VMEM((tm, D_MODEL), f32)],
        compiler_params=_cparams("arbitrary", "arbitrary"),
        name="ffn",
    )(x, g.reshape(1, D_MODEL), wg, wu, wd)


def _router_kernel(x_ref, g_ref, r_ref, h_ref, comb_ref):
    x = x_ref[...]
    ms = jnp.mean(x * x, axis=-1, keepdims=True)
    h = (x * lax.rsqrt(ms + EPS)) * g_ref[...]
    hb = h.astype(bf16)
    h_ref[...] = hb
    logits = _dg(hb, r_ref[...])
    lane = _iota(logits.shape, 1)
    logits = jnp.where(lane < N_EXPERTS, logits, -jnp.inf)
    v1 = jnp.max(logits, axis=-1, keepdims=True)
    i1 = jnp.min(jnp.where(logits == v1, lane, 128), axis=-1, keepdims=True)
    rest = jnp.where(lane == i1, -jnp.inf, logits)
    v2 = jnp.max(rest, axis=-1, keepdims=True)
    i2 = jnp.min(jnp.where(rest == v2, lane, 128), axis=-1, keepdims=True)
    e2 = jnp.exp(v2 - v1)
    den = 1.0 + e2
    comb_ref[...] = jnp.where(lane == i1, 1.0 / den, 0.0) + jnp.where(lane == i2, e2 / den, 0.0)


def _router(x, g, router_pad):
    t = x.shape[0]
    tm = min(512, t)
    return pl.pallas_call(
        _router_kernel,
        grid=(t // tm,),
        in_specs=[
            pl.BlockSpec((tm, D_MODEL), lambda i: (i, 0)),
            pl.BlockSpec((1, D_MODEL), lambda i: (0, 0)),
            pl.BlockSpec((D_MODEL, 128), lambda i: (0, 0)),
        ],
        out_specs=[pl.BlockSpec((tm, D_MODEL), lambda i: (i, 0)), pl.BlockSpec((tm, 128), lambda i: (i, 0))],
        out_shape=[jax.ShapeDtypeStruct((t, D_MODEL), bf16), jax.ShapeDtypeStruct((t, 128), f32)],
        compiler_params=_cparams("arbitrary"),
        name="router",
    )(x, g.reshape(1, D_MODEL), router_pad)


def _moe_kernel(x_ref, h_ref, comb_ref, wg_ref, wu_ref, wd_ref, y_ref, acc_scr):
    e = pl.program_id(1)
    f = pl.program_id(2)

    @pl.when((e == 0) & (f == 0))
    def _():
        acc_scr[...] = x_ref[...]

    comb = comb_ref[...]
    cw = jnp.sum(jnp.where(_iota(comb.shape, 1) == e, comb, 0.0), axis=-1, keepdims=True)
    h = h_ref[...]
    a = _silu(_dg(h, wg_ref[...])) * _dg(h, wu_ref[...]) * cw
    acc_scr[...] += _dg(a.astype(bf16), wd_ref[...])

    @pl.when((e == pl.num_programs(1) - 1) & (f == pl.num_programs(2) - 1))
    def _():
        y_ref[...] = acc_scr[...]


def _moe(x, h, comb, wg, wu, wd):
    t = x.shape[0]
    n_e, _, d_ff = wg.shape
    tm = min(512, t)
    tf = d_ff // 2
    return pl.pallas_call(
        _moe_kernel,
        grid=(t // tm, n_e, d_ff // tf),
        in_specs=[
            pl.BlockSpec((tm, D_MODEL), lambda i, e, f: (i, 0)),
            pl.BlockSpec((tm, D_MODEL), lambda i, e, f: (i, 0)),
            pl.BlockSpec((tm, 128), lambda i, e, f: (i, 0)),
            pl.BlockSpec((None, D_MODEL, tf), lambda i, e, f: (e, 0, f)),
            pl.BlockSpec((None, D_MODEL, tf), lambda i, e, f: (e, 0, f)),
            pl.BlockSpec((None, tf, D_MODEL), lambda i, e, f: (e, f, 0)),
        ],
        out_specs=pl.BlockSpec((tm, D_MODEL), lambda i, e, f: (i, 0)),
        out_shape=jax.ShapeDtypeStruct((t, D_MODEL), f32),
        scratch_shapes=[pltpu.VMEM((tm, D_MODEL), f32)],
        compiler_params=_cparams("arbitrary", "arbitrary", "arbitrary"),
        name="moe",
    )(x, h, comb, wg, wu, wd)


def _t5_bucket_np(n):
    n = np.maximum(n, 0)
    max_exact = NUM_BUCKETS // 2
    nf = np.maximum(n, 1).astype(np.float32)
    large = max_exact + (np.log(nf / np.float32(max_exact)) / np.float32(math.log(MAX_DISTANCE / max_exact))
                         * np.float32(NUM_BUCKETS - max_exact)).astype(np.int32)
    return np.where(n < max_exact, n, np.minimum(large, NUM_BUCKETS - 1))


def _shifted_bias(rel_bias):
    rb = rel_bias.astype(f32)
    return rb - rb[NUM_BUCKETS - 1:NUM_BUCKETS]


ACC_ROWS = HEAD_W + 8
LOG2E = math.log2(math.e)


def _attn_kernel(qi_ref, kj_ref, lam_ref, qt_ref, k_ref, vt_ref, toep_ref, o_ref, qs_scr, m_scr, acc_scr, *, tq):
    p = pl.program_id(1)
    i = qi_ref[p]
    j = kj_ref[p]
    n_hc = 2 * N_HEADS
    c2 = (DKB ** -0.5) * LOG2E

    @pl.when(j == 0)
    def _():
        qt = qt_ref[...]
        row_grp = _iota(qt.shape, 0) // DKB
        for hc in range(n_hc):
            qs_scr[:, hc * tq:(hc + 1) * tq] = jnp.where(row_grp == hc, qt, 0.0).astype(bf16)
        m_scr[...] = jnp.full(m_scr.shape, NEG, f32)
        acc_scr[...] = jnp.zeros(acc_scr.shape, f32)

    def step(near):
        tk = k_ref.shape[0]
        st_all = _dg(k_ref[...].astype(bf16), qs_scr[...], NN)
        vt = vt_ref[...]
        ones = jnp.ones((ACC_ROWS - HEAD_W, tk), f32)
        for h in range(N_HEADS):
            vh = jnp.concatenate([vt[h * HEAD_W:(h + 1) * HEAD_W, :], ones], axis=0).astype(bf16)
            for hc in (2 * h, 2 * h + 1):
                s = st_all[:, hc * tq:(hc + 1) * tq] * c2
                if near:
                    s = s + toep_ref[(i - j) * N_HEADS + h]
                m_old = m_scr[hc:hc + 1, :]
                m_new = jnp.maximum(m_old, jnp.max(s, axis=0, keepdims=True))
                pexp = jnp.exp2(s - m_new)
                acc_scr[hc] = jnp.exp2(m_old - m_new) * acc_scr[hc] + _dg(vh, pexp.astype(bf16), NN)
                m_scr[hc:hc + 1, :] = m_new

    @pl.when(i - j <= 1)
    def _():
        step(True)

    @pl.when(i - j > 1)
    def _():
        step(False)

    @pl.when(j == i)
    def _():
        lam = lam_ref[0]
        outs = []
        for h in range(N_HEADS):
            a0 = acc_scr[2 * h]
            a1 = acc_scr[2 * h + 1]
            outs.append(a0[0:HEAD_W] / a0[HEAD_W:HEAD_W + 1] - lam * (a1[0:HEAD_W] / a1[HEAD_W:HEAD_W + 1]))
        o_ref[...] = jnp.concatenate(outs, axis=0).T


def _toeplitz_bias_tiles(rel_bias, t):
    period = 2 * t + 1
    m = np.arange(period)[None, :]
    dist = m - t + np.array([0, t])[:, None]
    tab = _shifted_bias(rel_bias)
    u = jnp.take(tab, jnp.asarray(_t5_bucket_np(dist)), axis=0)
    u = jnp.where(jnp.asarray(dist >= 0)[:, :, None], u * LOG2E, NEG)
    u = jnp.transpose(u, (0, 2, 1)).reshape(2 * N_HEADS, period)
    skew = jnp.tile(u, (1, t))[:, :t * (period - 1)].reshape(2 * N_HEADS, t, period - 1)
    return skew[:, :, t:2 * t]


def _attn_prompt(qnt, kn, vt, lam, rel_bias, n_batch, seq):
    tq = min(512, seq)
    nq = seq // tq
    pairs =[(i, j) for i in range(nq) for j in range(i + 1)]
    qi = jnp.asarray(np.array([a for a, _ in pairs], np.int32))
    kj = jnp.asarray(np.array([b for _, b in pairs], np.int32))
    toep = _toeplitz_bias_tiles(rel_bias, tq)
    grid_spec = pltpu.PrefetchScalarGridSpec(
        num_scalar_prefetch=2,
        grid=(n_batch, len(pairs)),
        in_specs=[
            pl.BlockSpec(memory_space=pltpu.SMEM),
            pl.BlockSpec((GROUP_W, tq), lambda b_, p_, qi_, kj_: (0, b_ * nq + qi_[p_])),
            pl.BlockSpec((tq, GROUP_W), lambda b_, p_, qi_, kj_: (b_ * nq + kj_[p_], 0)),
            pl.BlockSpec((GROUP_W, tq), lambda b_, p_, qi_, kj_: (0, b_ * nq + kj_[p_])),
            pl.BlockSpec((2 * N_HEADS, tq, tq), lambda b_, p_, qi_, kj_: (0, 0, 0)),
        ],
        out_specs=pl.BlockSpec((tq, GROUP_W), lambda b_, p_, qi_, kj_: (b_ * nq + qi_[p_], 0)),
        scratch_shapes=[
            pltpu.VMEM((GROUP_W, 2 * N_HEADS * tq), bf16),
            pltpu.VMEM((2 * N_HEADS, tq), f32),
            pltpu.VMEM((2 * N_HEADS, ACC_ROWS, tq), f32),
        ],
    )
    return pl.pallas_call(
        functools.partial(_attn_kernel, tq=tq),
        grid_spec=grid_spec,
        out_shape=jax.ShapeDtypeStruct((n_batch * seq, GROUP_W), f32),
        compiler_params=_cparams("arbitrary", "arbitrary"),
        name="attn_prompt",
    )(qi, kj, lam.reshape(1), qnt, kn, vt, toep)


def _attn_decode_kernel(pt_ref, lam_ref, q_ref, kn_ref, vn_ref, blast_ref, bself_ref, *rest, pg, n_pages):
    k_refs = rest[:pg]
    v_refs = rest[pg:2 * pg]
    o_ref, qs_scr, s_scr, aself_scr, acc_scr = rest[2 * pg:]
    t = pl.program_id(1)
    n_steps = n_pages // pg
    n_hc = 2 * N_HEADS
    page = k_refs[0].shape[1]
    scale = DKB ** -0.5
    rnd = lambda z: z.astype(bf16).astype(f32)

    @pl.when(t == 0)
    def _():
        q = jnp.broadcast_to(q_ref[...], (n_hc, GROUP_W))
        keep = (_iota(q.shape, 1) // DKB) == _iota(q.shape, 0)
        qs_scr[...] = jnp.where(keep, q, 0.0)

    @pl.when(t < n_steps)
    def _():
        qs_b = qs_scr[...].astype(bf16)
        parts = []
        for g in range(pg):
            s = _dg(qs_b, k_refs[g][...].astype(bf16), NN) * scale
            is_last = (t * pg + g) == (n_pages - 1)
            parts.append(s + jnp.where(is_last, blast_ref[...], 0.0))
        s_scr[t] = jnp.concatenate(parts, axis=1)

    @pl.when(t == n_steps - 1)
    def _():
        s_all = s_scr[...]
        s_self = jnp.sum(rnd(qs_scr[...]) * rnd(kn_ref[...]), axis=-1, keepdims=True) * scale + bself_ref[...]
        m = jnp.maximum(jnp.max(jnp.max(s_all, axis=2, keepdims=True), axis=0), s_self)
        p = jnp.exp(s_all - m)
        p_self = jnp.exp(s_self - m)
        l = jnp.sum(jnp.sum(p, axis=2, keepdims=True), axis=0) + p_self
        pn = p / l
        pn_self = p_self / l
        lam = lam_ref[0]
        rows = [pn[:, 2 * h:2 * h + 1, :] - lam * pn[:, 2 * h + 1:2 * h + 2, :] for h in range(N_HEADS)]
        s_scr[...] = jnp.concatenate(rows + [jnp.zeros_like(rows[0])] * N_HEADS, axis=1)
        rows_self = [pn_self[2 * h:2 * h + 1] - lam * pn_self[2 * h + 1:2 * h + 2] for h in range(N_HEADS)]
        aself_scr[...] = jnp.concatenate(rows_self + [jnp.zeros_like(rows_self[0])] * N_HEADS, axis=0)
        acc_scr[...] = jnp.zeros(acc_scr.shape, f32)

    @pl.when(t >= n_steps)
    def _():
        a = s_scr[t - n_steps].astype(bf16)
        acc = acc_scr[...]
        for g in range(pg):
            acc = acc + _dg(a[:, g * page:(g + 1) * page], v_refs[g][...].astype(bf16), NT)
        acc_scr[...] = acc

    @pl.when(t == 2 * n_steps - 1)
    def _():
        o = acc_scr[...] + rnd(aself_scr[...]) * rnd(vn_ref[...])
        lane_head = _iota((1, GROUP_W), 1) // HEAD_W
        out = jnp.zeros((1, GROUP_W), f32)
        for h in range(N_HEADS):
            out = jnp.where(lane_head == h, o[h:h + 1], out)
        o_ref[...] = out


def _attn_decode(qn, kn, vn, page_table, cache_k, cache_v, layer, lam, rel_bias):
    n_b, n_pages = page_table.shape
    page = cache_k.shape[3]
    pg = min(16, n_pages)
    n_steps = n_pages // pg
    past = n_pages * page
    tab = _shifted_bias(rel_bias)
    d_last = past - ((n_pages - 1) * page + np.arange(page))
    blast = jnp.repeat(jnp.take(tab, jnp.asarray(_t5_bucket_np(d_last)), axis=0).T, 2, axis=0)
    bself = jnp.repeat(tab[0].reshape(N_HEADS, 1), 2, axis=0)

    def k_spec(g):
        return pl.BlockSpec((None, None, GROUP_W, page),
                            lambda b_, t_, pt: (layer, pt[b_, jnp.minimum(t_, n_steps - 1) * pg + g], 0, 0))

    def v_spec(g):
        return pl.BlockSpec((None, None, GROUP_W, page),
                            lambda b_, t_, pt: (layer, pt[b_, jnp.maximum(t_ - n_steps, 0) * pg + g], 0, 0))

    row = pl.BlockSpec((None, 1, GROUP_W), lambda b_, t_, pt: (b_, 0, 0))
    grid_spec = pltpu.PrefetchScalarGridSpec(
        num_scalar_prefetch=1,
        grid=(n_b, 2 * n_steps),
        in_specs=[pl.BlockSpec(memory_space=pltpu.SMEM), row, row, row,
                  pl.BlockSpec((2 * N_HEADS, page), lambda b_, t_, pt: (0, 0)),
                  pl.BlockSpec((2 * N_HEADS, 1), lambda b_, t_, pt: (0, 0))]
                 + [k_spec(g) for g in range(pg)] + [v_spec(g) for g in range(pg)],
        out_specs=row,
        scratch_shapes=[
            pltpu.VMEM((2 * N_HEADS, GROUP_W), f32),
            pltpu.VMEM((n_steps, 2 * N_HEADS, pg * page), f32),
            pltpu.VMEM((2 * N_HEADS, 1), f32),
            pltpu.VMEM((2 * N_HEADS, GROUP_W), f32),
        ],
    )
    r3 = lambda z: z.reshape(n_b, 1, GROUP_W)
    out = pl.pallas_call(
        functools.partial(_attn_decode_kernel, pg=pg, n_pages=n_pages),
        grid_spec=grid_spec,
        out_shape=jax.ShapeDtypeStruct((n_b, 1, GROUP_W), f32),
        compiler_params=_cparams("arbitrary", "arbitrary"),
        name="attn_decode",
    )(page_table, lam.reshape(1), r3(qn), r3(kn), r3(vn), blast, bself,
      *([cache_k] * pg), *([cache_v] * pg))
    return out.reshape(n_b, GROUP_W)


def _tri(n, dtype=f32):
    return (_iota((n, n), 0) >= _iota((n, n), 1)).astype(dtype)


def _block_tri_t(n, blk):
    r = _iota((n, n), 0)
    c = _iota((n, n), 1)
    return (((r // blk) == (c // blk)) & (r <= c)).astype(bf16)


def _block_tri(n, blk):
    r = _iota((n, n), 0)
    c = _iota((n, n), 1)
    return (((r // blk) == (c // blk)) & (r >= c)).astype(bf16)


def _gdn_kernel(u_ref, gc_ref, gr_ref, cw_ref, alr_ref, dtr_ref, alc_ref, dtc_ref, o_ref, s_out_ref, ext_scr, s_scr, *, tb):
    i = pl.program_id(1)

    @pl.when(i == 0)
    def _():
        ext_scr[0:8, :] = jnp.zeros((8, 3 * GROUP_W), f32)
        s_scr[...] = jnp.zeros(s_scr.shape, f32)

    ext_scr[8:8 + tb, :] = u_ref[...]
    w = cw_ref[...]
    conv = ext_scr[8:8 + tb, :] * w[3:4]
    for jj in range(1, CONV_W):
        conv = conv + ext_scr[8 - jj:8 - jj + tb, :] * w[3 - jj:4 - jj]
    ext_scr[0:8, :] = ext_scr[tb:tb + 8, :]
    qkv = _silu(conv)

    def l2n(x):
        return x * lax.rsqrt(_group_sum(x * x, HEAD_W) + EPS)

    q = l2n(qkv[:, 0:GROUP_W]) * (HEAD_W ** -0.5)
    k = l2n(qkv[:, GROUP_W:2 * GROUP_W])
    v = qkv[:, 2 * GROUP_W:3 * GROUP_W]

    gc = gc_ref[...]
    g_col = -jnp.exp(alr_ref[...]) * _softplus(gc[:, 0:4] + dtr_ref[...])
    beta_col = jax.nn.sigmoid(gc[:, 4:8])
    gr = gr_ref[...]
    g_row = -jnp.exp(alc_ref[...]) * _softplus(gr[0:4, :] + dtc_ref[...])
    g_row8 = jnp.concatenate([g_row, jnp.zeros_like(g_row)], axis=0)
    gcum_row = _mm2(g_row8, _block_tri_t(tb, CHUNK))

    same, lower, strict = _bd_masks()
    tri = _tri(CHUNK, bf16)
    r = _iota((GROUP_W, GROUP_W), 0)
    c = _iota((GROUP_W, GROUP_W), 1)
    eye = (r == c).astype(f32)

    chunks = range(tb // CHUNK)
    gcums, qks, m_bds = [], [], []
    for ch in chunks:
        lo = ch * CHUNK
        gcum = _mm2l(tri, g_col[lo:lo + CHUNK])
        g_stack = _stack_cols(gcum)
        g_cat = _cat_rows(gcum_row, lo)
        decay = jnp.exp(jnp.where(lower, g_stack - g_cat, NEG))
        ksm = _head_stack(k[lo:lo + CHUNK])
        kk = _mm1(ksm, ksm, NT)
        qks.append(_mm1(_head_stack(q[lo:lo + CHUNK]), ksm, NT) * decay)
        m_bds.append(_stack_cols(beta_col[lo:lo + CHUNK]) * kk * jnp.where(strict, decay, 0.0))
        gcums.append(gcum)

    def sibling(lev):
        return ((r >> (lev + 1)) == (c >> (lev + 1))) & (((r >> lev) & 1) == 1) & (((c >> lev) & 1) == 0)

    xs = [eye - jnp.where(sibling(0), m, 0.0) for m in m_bds]
    for lev in range(1, 6):
        sel = sibling(lev)
        xs = [x - _mm3(_mm3(x, jnp.where(sel, m, 0.0)), x) for x, m in zip(xs, m_bds)]

    for ch in chunks:
        lo = ch * CHUNK
        qc, kc, vc = q[lo:lo + CHUNK], k[lo:lo + CHUNK], v[lo:lo + CHUNK]
        gcum, bcol = gcums[ch], beta_col[lo:lo + CHUNK]
        s_bd = s_scr[...]
        kq_s = _mm1(jnp.concatenate([kc, qc], axis=0), s_bd)
        ks, qs = kq_s[0:CHUNK], kq_s[CHUNK:2 * CHUNK]
        eg_all = _expand_cols(jnp.exp(gcum))
        rhs = _expand_cols(bcol) * (vc - eg_all * ks)
        u_sm = _mm3(xs[ch], _head_stack(rhs))
        o_sm = _mm1(qks[ch], u_sm)
        o_ref[lo:lo + CHUNK, :] = eg_all * qs + _fold_heads(o_sm)
        u_all = _fold_heads(u_sm)
        g_last = gcum[CHUNK - 1:CHUNK, :]
        kw = kc * _expand_cols(jnp.exp(g_last - gcum))
        d_stack = jnp.concatenate(
            [jnp.broadcast_to(jnp.exp(g_last[:, h:h + 1]), (HEAD_W, 1)) for h in range(N_HEADS)], axis=0)
        s_scr[...] = d_stack * s_bd + jnp.where(same, _mm1(kw, u_all, TN), 0.0)

    @pl.when(i == pl.num_programs(1) - 1)
    def _():
        s_out_ref[...] = s_scr[...]


def _gdn_prompt(p, gt, conv_w, a_log, dt_bias, n_batch, seq):
    tb = min(256, seq)
    nb = seq // tb
    r14 = lambda z: z.astype(f32).reshape(1, N_HEADS)
    c41 = lambda z: z.astype(f32).reshape(N_HEADS, 1)
    o, s_bd = pl.pallas_call(
        functools.partial(_gdn_kernel, tb=tb),
        grid=(n_batch, nb),
        in_specs=[
            pl.BlockSpec((tb, 3 * GROUP_W), lambda b, i: (b * nb + i, 0)),
            pl.BlockSpec((tb, 128), lambda b, i: (b * nb + i, GATE_COL // 128)),
            pl.BlockSpec((16, tb), lambda b, i: (0, b * nb + i)),
            pl.BlockSpec((CONV_W, 3 * GROUP_W), lambda b, i: (0, 0)),
            pl.BlockSpec((1, N_HEADS), lambda b, i: (0, 0)),
            pl.BlockSpec((1, N_HEADS), lambda b, i: (0, 0)),
            pl.BlockSpec((N_HEADS, 1), lambda b, i: (0, 0)),
            pl.BlockSpec((N_HEADS, 1), lambda b, i: (0, 0)),
        ],
        out_specs=[
            pl.BlockSpec((tb, GROUP_W), lambda b, i: (b * nb + i, 0)),
            pl.BlockSpec((None, GROUP_W, GROUP_W), lambda b, i: (b, 0, 0)),
        ],
        out_shape=[jax.ShapeDtypeStruct((n_batch * seq, GROUP_W), f32),
                   jax.ShapeDtypeStruct((n_batch, GROUP_W, GROUP_W), f32)],
        scratch_shapes=[pltpu.VMEM((tb + 8, 3 * GROUP_W), f32), pltpu.VMEM((GROUP_W, GROUP_W), f32)],
        compiler_params=_cparams("arbitrary", "arbitrary"),
        name="gdn_prompt",
    )(p, p, gt, conv_w.astype(f32), r14(a_log), r14(dt_bias), c41(a_log), c41(dt_bias))
    return o, _bd_diag(s_bd)


def _bd_diag(s_bd):
    n_b = s_bd.shape[0]
    s5 = s_bd.reshape(n_b, N_HEADS, HEAD_W, N_HEADS, HEAD_W)
    return jnp.stack([s5[:, h, :, h, :] for h in range(N_HEADS)], axis=1)


def _hgrn_kernel(q_ref, f_ref, i_ref, lb_ref, o_ref, s_out_ref, st_scr, q_scr, k_scr, b_scr, *, tb):
    blk = pl.program_id(1)

    @pl.when(blk == 0)
    def _():
        st_scr[...] = jnp.zeros(st_scr.shape, f32)

    lb = lb_ref[...]
    z = f_ref[...]
    logf = jnp.log(lb + (1.0 - lb) * jax.nn.sigmoid(z))
    q_scr[...] = _silu(q_ref[...])
    k_scr[...] = (1.0 - lb) * jax.nn.sigmoid(-z)
    b_scr[...] = _mm2l(_block_tri(tb, SUB), logf)

    same, _, _ = _bd_masks()
    ones_bd = _group_ones(GROUP_W, HEAD_W)
    row = _iota((SUB * SUB, GROUP_W), 0)
    tmask = (row % SUB) >= (row // SUB)

    def rep_t(x):
        return jnp.broadcast_to(x[None], (SUB, SUB, GROUP_W)).reshape(SUB * SUB, GROUP_W)

    def rep_j(x):
        return jnp.broadcast_to(x[:, None, :], (SUB, SUB, GROUP_W)).reshape(SUB * SUB, GROUP_W)

    def body(c, carry):
        r0 = pl.multiple_of(c * SUB, SUB)
        qs = q_scr[pl.ds(r0, SUB), :]
        ks = k_scr[pl.ds(r0, SUB), :]
        vs = i_ref[pl.ds(r0, SUB), :]
        bs = b_scr[pl.ds(r0, SUB), :]
        st = st_scr[...]
        o_inter = _mm1(qs * jnp.exp(bs), st, NT)
        wgt = rep_t(qs) * jnp.exp(jnp.where(tmask, rep_t(bs) - rep_j(bs), NEG)) * rep_j(ks)
        a = _mm2(wgt, ones_bd)
        o_diag = jnp.sum((a * rep_j(vs)).reshape(SUB, SUB, GROUP_W), axis=0)
        o_ref[pl.ds(r0, SUB), :] = o_inter + o_diag
        b_last = bs[SUB - 1:SUB, :]
        kw = ks * jnp.exp(b_last - bs)
        st_scr[...] = st * jnp.exp(b_last) + jnp.where(same, _mm1(vs, kw, TN), 0.0)
        return carry

    lax.fori_loop(0, tb // SUB, body, 0)

    @pl.when(blk == pl.num_programs(1) - 1)
    def _():
        s_out_ref[...] = st_scr[...]


def _hgrn_prompt(p, lb, n_batch, seq):
    tb = min(256, seq)
    nb = seq // tb
    blk = lambda col: pl.BlockSpec((tb, GROUP_W), lambda b, i: (b * nb + i, col))
    o, st = pl.pallas_call(
        functools.partial(_hgrn_kernel, tb=tb),
        grid=(n_batch, nb),
        in_specs=[blk(7), blk(8), blk(9), pl.BlockSpec((1, GROUP_W), lambda b, i: (0, 0))],
        out_specs=[
            pl.BlockSpec((tb, GROUP_W), lambda b, i: (b * nb + i, 0)),
            pl.BlockSpec((None, GROUP_W, GROUP_W), lambda b, i: (b, 0, 0)),
        ],
        out_shape=[jax.ShapeDtypeStruct((n_batch * seq, GROUP_W), f32),
                   jax.ShapeDtypeStruct((n_batch, GROUP_W, GROUP_W), f32)],
        scratch_shapes=[pltpu.VMEM((GROUP_W, GROUP_W), f32)] + [pltpu.VMEM((tb, GROUP_W), f32)] * 3,
        compiler_params=_cparams("arbitrary", "arbitrary"),
        name="hgrn_prompt",
    )(p, p, p, lb.astype(f32).reshape(1, GROUP_W))
    return o, jnp.swapaxes(_bd_diag(st), -1, -2)


def _log_sigmoid(x):
    return jnp.minimum(x, 0.0) - jnp.log1p(jnp.exp(-jnp.abs(x)))


def _mlstm_kernel(q_ref, k_ref, v_ref, gc_ref, gr_ref, ibr_ref, fbr_ref, ibc_ref, fbc_ref,
                  o_ref, c_out_ref, n_out_ref, m_out_ref, c_scr, n_scr, m_scr, *, tb):
    blk = pl.program_id(1)

    @pl.when(blk == 0)
    def _():
        c_scr[...] = jnp.zeros(c_scr.shape, f32)
        n_scr[...] = jnp.zeros(n_scr.shape, f32)
        m_scr[...] = jnp.zeros(m_scr.shape, f32)

    q = q_ref[...]
    k = k_ref[...] * (HEAD_W ** -0.5)
    v = v_ref[...]
    gc = gc_ref[...]
    li_col = gc[:, 8:12] + ibr_ref[...]
    lf_col = _log_sigmoid(gc[:, 12:16] + fbr_ref[...])
    gr = gr_ref[...]
    li_row = gr[8:12, :] + ibc_ref[...]
    lf_row = _log_sigmoid(gr[12:16, :] + fbc_ref[...])
    b_row = _mm2(jnp.concatenate([lf_row, jnp.zeros_like(lf_row)], axis=0), _block_tri_t(tb, CHUNK))

    same, lower, _ = _bd_masks()
    tri = _tri(CHUNK, bf16)

    for ch in range(tb // CHUNK):
        lo = ch * CHUNK
        qc, kc, vc = q[lo:lo + CHUNK], k[lo:lo + CHUNK], v[lo:lo + CHUNK]
        b_col = _mm2l(tri, lf_col[lo:lo + CHUNK])
        b_stack = _stack_cols(b_col)
        d_mat = jnp.where(lower, b_stack - _cat_rows(b_row, lo) + _cat_rows(li_row, lo), NEG)
        m_row = m_scr[...]
        m_stack = jnp.concatenate(
            [jnp.broadcast_to(m_row[:, h:h + 1], (CHUNK, 1)) for h in range(N_HEADS)], axis=0)
        inter = b_stack + m_stack
        m_t = jnp.maximum(inter, jnp.max(d_mat, axis=-1, keepdims=True))
        w_inter = jnp.exp(inter - m_t)
        qsm = _head_stack(qc)
        ksm = _head_stack(kc)
        pmat = _mm1(qsm, ksm, NT) * jnp.exp(d_mat - m_t)
        c_bd = c_scr[...]
        n_row = n_scr[...]
        num = w_inter * _mm1(qsm, c_bd) + _mm1(pmat, _head_stack(vc))
        den = w_inter * jnp.sum(qsm * n_row, axis=-1, keepdims=True) + jnp.sum(pmat, axis=-1, keepdims=True)
        h_sm = num / jnp.maximum(jnp.abs(den), jnp.exp(-m_t))
        o_ref[lo:lo + CHUNK, :] = _fold_heads(h_sm)
        m_new = jnp.concatenate(
            [m_t[h * CHUNK + CHUNK - 1:h * CHUNK + CHUNK, :] for h in range(N_HEADS)], axis=1)
        b_last = b_col[CHUNK - 1:CHUNK, :]
        w_end = jnp.exp(b_last - b_col + li_col[lo:lo + CHUNK] - m_new)
        d0 = jnp.exp(b_last + m_row - m_new)
        kw = kc * _expand_cols(w_end)
        d0_stack = jnp.concatenate(
            [jnp.broadcast_to(d0[:, h:h + 1], (HEAD_W, 1)) for h in range(N_HEADS)], axis=0)
        c_scr[...] = d0_stack * c_bd + jnp.where(same, _mm1(kw, vc, TN), 0.0)
        n_scr[...] = _expand_cols(d0) * n_row + jnp.sum(kw, axis=0, keepdims=True)
        m_scr[...] = m_new

    @pl.when(blk == pl.num_programs(1) - 1)
    def _():
        c_out_ref[...] = c_scr[...]
        n_out_ref[...] = n_scr[...]
        m_out_ref[...] = m_scr[...]


def _mlstm_prompt(p, gt, i_bias, f_bias, n_batch, seq):
    tb = min(256, seq)
    nb = seq // tb
    blk = lambda col: pl.BlockSpec((tb, GROUP_W), lambda b, i: (b * nb + i, col))
    r14 = lambda z: z.astype(f32).reshape(1, N_HEADS)
    c41 = lambda z: z.astype(f32).reshape(N_HEADS, 1)
    small = lambda shape: pl.BlockSpec(shape, lambda b, i: (0, 0))
    o, c_bd, n_row, m_row = pl.pallas_call(
        functools.partial(_mlstm_kernel, tb=tb),
        grid=(n_batch, nb),
        in_specs=[blk(11), blk(12), blk(13),
                  pl.BlockSpec((tb, 128), lambda b, i: (b * nb + i, GATE_COL // 128)),
                  pl.BlockSpec((16, tb), lambda b, i: (0, b * nb + i)),
                  small((1, N_HEADS)), small((1, N_HEADS)), small((N_HEADS, 1)), small((N_HEADS, 1))],
        out_specs=[
            pl.BlockSpec((tb, GROUP_W), lambda b, i: (b * nb + i, 0)),
            pl.BlockSpec((None, GROUP_W, GROUP_W), lambda b, i: (b, 0, 0)),
            pl.BlockSpec((None, 1, GROUP_W), lambda b, i: (b, 0, 0)),
            pl.BlockSpec((None, 1, N_HEADS), lambda b, i: (b, 0, 0)),
        ],
        out_shape=[jax.ShapeDtypeStruct((n_batch * seq, GROUP_W), f32),
                   jax.ShapeDtypeStruct((n_batch, GROUP_W, GROUP_W), f32),
                   jax.ShapeDtypeStruct((n_batch, 1, GROUP_W), f32),
                   jax.ShapeDtypeStruct((n_batch, 1, N_HEADS), f32)],
        scratch_shapes=[pltpu.VMEM((GROUP_W, GROUP_W), f32), pltpu.VMEM((1, GROUP_W), f32),
                        pltpu.VMEM((1, N_HEADS), f32)],
        compiler_params=_cparams("arbitrary", "arbitrary"),
        name="mlstm_prompt",
    )(p, p, p, p, gt, r14(i_bias), r14(f_bias), c41(i_bias), c41(f_bias))
    return (o, _bd_diag(c_bd), n_row.reshape(n_batch, N_HEADS, HEAD_W), m_row.reshape(n_batch, N_HEADS))


def _gdn_dec_prep_kernel(u_ref, buf_ref, cw_ref, q_ref, k_ref, v_ref):
    w = cw_ref[...]
    conv = u_ref[...] * w[3:4]
    for jj in range(CONV_W - 1):
        conv = conv + buf_ref[jj] * w[jj:jj + 1]
    qkv = _silu(conv)

    def l2n(x):
        return x * lax.rsqrt(_group_sum(x * x, HEAD_W) + EPS)

    q_ref[...] = l2n(qkv[:, 0:GROUP_W]) * (HEAD_W ** -0.5)
    k_ref[...] = l2n(qkv[:, GROUP_W:2 * GROUP_W])
    v_ref[...] = qkv[:, 2 * GROUP_W:3 * GROUP_W]


def _gdn_dec_prep(p, conv_buf, conv_w):
    n_b = p.shape[0]
    out = jax.ShapeDtypeStruct((n_b, GROUP_W), f32)
    return pl.pallas_call(
        _gdn_dec_prep_kernel,
        grid=(1,),
        in_specs=[pl.BlockSpec((n_b, 3 * GROUP_W), lambda i: (0, 0)),
                  pl.BlockSpec((CONV_W - 1, n_b, 3 * GROUP_W), lambda i: (0, 0, 0)),
                  pl.BlockSpec((CONV_W, 3 * GROUP_W), lambda i: (0, 0))],
        out_specs=[pl.BlockSpec((n_b, GROUP_W), lambda i: (0, 0))] * 3,
        out_shape=[out, out, out],
        compiler_params=_cparams("arbitrary"),
        name="gdn_dec_prep",
    )(p, jnp.swapaxes(conv_buf.astype(f32), 0, 1), conv_w.astype(f32))


def _rec_decode_kernel(gq_ref, gk_ref, gv_ref, ga_ref, gb_ref, al_ref, dtb_ref, sg_ref,
                       cq_ref, cf_ref, ci_ref, lbc_ref, sh_ref,
                       dq_ref, dk_ref, dv_ref, di_ref, df_ref, ib_ref, fb_ref, sc_ref, sn_ref, sm_ref,
                       oa_ref, sg_out, oc_ref, sh_out, od_ref, sc_out, sn_out, sm_out):
    q, k, v = gq_ref[...], gk_ref[...], gv_ref[...]
    s = sg_ref[...]
    g = -jnp.exp(al_ref[...]) * _softplus(ga_ref[...] + dtb_ref[...])
    eg = jnp.exp(g)
    beta = jax.nn.sigmoid(gb_ref[...])
    ks = jnp.sum(k * s, axis=1, keepdims=True)
    qs = jnp.sum(q * s, axis=1, keepdims=True)
    u = beta * (v - eg * ks)
    qk = jnp.sum(q * k, axis=1, keepdims=True)
    oa_ref[...] = eg * qs + qk * u
    sg_out[...] = eg * s + k * u

    lb = lbc_ref[...]
    z = cf_ref[...]
    logf = jnp.log(lb + (1.0 - lb) * jax.nn.sigmoid(z))
    kc = (1.0 - lb) * jax.nn.sigmoid(-z)
    qc = _silu(cq_ref[...])
    vc = ci_ref[...]
    sh = sh_ref[...]
    ef = jnp.exp(logf)
    oc_ref[...] = jnp.sum((qc * ef) * sh, axis=1, keepdims=True) + jnp.sum(qc * kc, axis=1, keepdims=True) * vc
    sh_out[...] = ef * sh + kc * vc

    qd = dq_ref[...]
    kd = dk_ref[...] * (HEAD_W ** -0.5)
    vd = dv_ref[...]
    li = di_ref[...] + ib_ref[...]
    lf = _log_sigmoid(df_ref[...] + fb_ref[...])
    m0 = sm_ref[...]
    cs = sc_ref[...]
    n0 = sn_ref[...]
    inter = lf + m0
    m_t = jnp.maximum(inter, li)
    w_inter = jnp.exp(inter - m_t)
    qkd = jnp.sum(qd * kd, axis=1, keepdims=True) * jnp.exp(li - m_t)
    q_c = jnp.sum(qd.astype(bf16).astype(f32) * cs.astype(bf16).astype(f32), axis=1, keepdims=True)
    num = w_inter * q_c + qkd * vd
    den = w_inter * jnp.sum(qd * n0, axis=1, keepdims=True) + qkd
    od_ref[...] = num / jnp.maximum(jnp.abs(den), jnp.exp(-m_t))
    w_end = jnp.exp(li - m_t)
    d0 = jnp.exp(lf + m0 - m_t)
    sc_out[...] = d0 * cs + (w_end * kd) * vd
    sn_out[...] = d0 * n0 + w_end * kd
    sm_out[...] = m_t


def _rec_decode(p, gq, gk, gv, a_log, dt_bias, lb, i_bias, f_bias, s_gdn, s_hgrn, s_c, s_n, s_m):
    n_b = p.shape[0]
    rows = n_b * N_HEADS
    rb = min(16, rows)
    col = lambda z: z.reshape(rows, HEAD_W, 1)
    vrow = lambda z: z.reshape(rows, 1, HEAD_W)
    sca = lambda z: z.reshape(rows, 1, 1)
    per_head = lambda z: jnp.tile(z.astype(f32), n_b).reshape(rows, 1, 1)
    blockp = lambda b: p[:, b * GROUP_W:(b + 1) * GROUP_W]
    gates = p[:, GATE_COL:GATE_COL + 16]
    lb_col = jnp.tile(lb.astype(f32).reshape(N_HEADS, HEAD_W), (n_b, 1)).reshape(rows, HEAD_W, 1)
    st = lambda z: z.astype(f32).reshape(rows, HEAD_W, HEAD_W)
    args = [col(gq), col(gk), vrow(gv), sca(gates[:, 0:4]), sca(gates[:, 4:8]), per_head(a_log), per_head(dt_bias), st(s_gdn),
            col(blockp(7)), col(blockp(8)), vrow(blockp(9)), lb_col, st(s_hgrn),
            col(blockp(11)), col(blockp(12)), vrow(blockp(13)), sca(gates[:, 8:12]), sca(gates[:, 12:16]),
            per_head(i_bias), per_head(f_bias), st(s_c), col(s_n.astype(f32)), sca(s_m.astype(f32))]

    def spec(a):
        return pl.BlockSpec((rb,) + a.shape[1:], lambda i: (i, 0, 0))

    o_vrow = jax.ShapeDtypeStruct((rows, 1, HEAD_W), f32)
    o_st = jax.ShapeDtypeStruct((rows, HEAD_W, HEAD_W), f32)
    o_col = jax.ShapeDtypeStruct((rows, HEAD_W, 1), f32)
    o_sca = jax.ShapeDtypeStruct((rows, 1, 1), f32)
    outs = [o_vrow, o_st, o_vrow, o_st, o_vrow, o_st, o_col, o_sca]
    res = pl.pallas_call(
        _rec_decode_kernel,
        grid=(rows // rb,),
        in_specs=[spec(a) for a in args],
        out_specs=[spec(a) for a in outs],
        out_shape=outs,
        compiler_params=_cparams("arbitrary"),
        name="rec_decode",
    )(*args)
    oa, sg, oc, sh, od, sc, sn, sm = res
    s4 = lambda z: z.reshape(n_b, N_HEADS, HEAD_W, HEAD_W)
    o2 = lambda z: z.reshape(n_b, GROUP_W)
    return (o2(oa), s4(sg), o2(oc), s4(sh), o2(od), s4(sc),
            sn.reshape(n_b, N_HEADS, HEAD_W), sm.reshape(n_b, N_HEADS))


def _permute_w_in(w):
    d_in = w.shape[1]
    a_gate0 = 3 * GROUP_W
    d_gate0 = d_in - GROUP_W - 8
    main = jnp.concatenate([w[:, 0:a_gate0], w[:, a_gate0 + 8:d_gate0], w[:, d_gate0 + 8:]], axis=1)
    gates = jnp.concatenate([w[:, a_gate0:a_gate0 + 8], w[:, d_gate0:d_gate0 + 8]], axis=1)
    pad = jnp.zeros((w.shape[0], P_COLS - main.shape[1] - 16), w.dtype)
    return jnp.concatenate([main, gates, pad], axis=1).astype(bf16), gates.T.astype(bf16)


def kernel(x_prompt, x_sample, page_table, cache_k, cache_v, state_gdn_conv, state_gdn, state_hgrn, state_mlstm_C, state_mlstm_n, state_mlstm_m, attn_norm_g, w_in, gdn_conv_w, gdn_a_log, gdn_dt_bias, gdn_norm_g, diff_qk_norm_g, diff_lambda, diff_subln_g, rel_bias, hgrn_lb_logits, hgrn_norm_g, mlstm_i_bias, mlstm_f_bias, mlstm_norm_g, w_out, ffn_norm_g, ffn_w_gate, ffn_w_up, ffn_w_down, moe_router, moe_w_gate, moe_w_up, moe_w_down):
    depth = w_in.shape[0]
    n_bp, seq, _ = x_prompt.shape
    n_bs = x_sample.shape[0]
    n_pool, page = cache_k.shape[1], cache_k.shape[2]
    dt = x_prompt.dtype

    lb_p = jax.nn.softmax(hgrn_lb_logits.astype(f32), axis=0)
    lb_cum = jnp.cumsum(lb_p, axis=0)
    hgrn_lb = lb_cum - lb_cum[0:1]
    cache_k4 = jnp.transpose(cache_k, (0, 1, 3, 4, 5, 2)).reshape(depth, n_pool, GROUP_W, page)
    cache_v4 = jnp.transpose(cache_v, (0, 1, 3, 4, 2)).reshape(depth, n_pool, GROUP_W, page)

    xp = x_prompt.reshape(n_bp * seq, D_MODEL)
    xs = x_sample.reshape(n_bs, D_MODEL)
    outs_p, outs_s = [], []
    for l in range(depth):
        w_perm, w_gate_t = _permute_w_in(w_in[l])
        w_out_b = w_out[l].astype(bf16)
        gains = jnp.stack([jnp.tile(g.astype(f32), N_HEADS) for g in
                           (gdn_norm_g[l], diff_subln_g[l], hgrn_norm_g[l], mlstm_norm_g[l])])
        lam_init = 0.8 - 0.6 * math.exp(-0.3 * l)
        lam32 = diff_lambda[l].astype(f32)
        lam = jnp.exp(jnp.sum(lam32[0] * lam32[1])) - jnp.exp(jnp.sum(lam32[2] * lam32[3])) + lam_init
        if l % 2 == 0:
            ffn_w = (ffn_w_gate[l // 2].astype(bf16), ffn_w_up[l // 2].astype(bf16), ffn_w_down[l // 2].astype(bf16))
        else:
            router_pad = jnp.pad(moe_router[l // 2].astype(bf16), ((0, 0), (0, 128 - N_EXPERTS)))
            moe_w = (moe_w_gate[l // 2].astype(bf16), moe_w_up[l // 2].astype(bf16), moe_w_down[l // 2].astype(bf16))

        def channel_mix(x):
            if l % 2 == 0:
                return _ffn(x, ffn_norm_g[l], *ffn_w)
            h, comb = _router(x, ffn_norm_g[l], router_pad)
            return _moe(x, h, comb, *moe_w)

        p, gt = _inproj(xp, attn_norm_g[l], w_perm, w_gate_t)
        qnt, kn, vt = _bprep(p, diff_qk_norm_g[l], True)
        ob = _attn_prompt(qnt, kn, vt, lam, rel_bias, n_bp, seq)
        oa, s_gdn = _gdn_prompt(p, gt, gdn_conv_w[l], gdn_a_log[l], gdn_dt_bias[l], n_bp, seq)
        oc, s_hgrn = _hgrn_prompt(p, hgrn_lb[l], n_bp, seq)
        od, s_c, s_n, s_m = _mlstm_prompt(p, gt, mlstm_i_bias[l], mlstm_f_bias[l], n_bp, seq)
        xp = _outproj(oa, ob, oc, od, p, xp, gains, w_out_b, 1.0 - lam_init)
        xp = channel_mix(xp)
        p3 = p.reshape(n_bp, seq, P_COLS)
        outs_p.append((
            kn.reshape(n_bp, seq, N_HEADS, 2, DKB).astype(dt),
            p3[:, :, 6 * GROUP_W:7 * GROUP_W].reshape(n_bp, seq, N_HEADS, HEAD_W).astype(dt),
            p3[:, seq - (CONV_W - 1):, 0:3 * GROUP_W].astype(dt),
            s_gdn.astype(dt), s_hgrn.astype(dt), s_c.astype(dt), s_n.astype(dt), s_m.astype(dt)))

        p, gt = _inproj(xs, attn_norm_g[l], w_perm, w_gate_t)
        qn, kn = _bprep(p, diff_qk_norm_g[l], False)
        vn = p[:, 6 * GROUP_W:7 * GROUP_W]
        ob = _attn_decode(qn, kn, vn, page_table, cache_k4, cache_v4, l, lam, rel_bias)
        u = p[:, 0:3 * GROUP_W]
        gq, gk, gv = _gdn_dec_prep(u, state_gdn_conv[l], gdn_conv_w[l])
        oa, s_gdn, oc, s_hgrn, od, s_c, s_n, s_m = _rec_decode(
            p, gq, gk, gv, gdn_a_log[l], gdn_dt_bias[l], hgrn_lb[l], mlstm_i_bias[l], mlstm_f_bias[l],
            state_gdn[l], state_hgrn[l], state_mlstm_C[l], state_mlstm_n[l], state_mlstm_m[l])
        xs = _outproj(oa, ob, oc, od, p, xs, gains, w_out_b, 1.0 - lam_init)
        xs = channel_mix(xs)
        conv_new = jnp.concatenate([state_gdn_conv[l][:, 1:].astype(dt), u[:, None, :].astype(dt)], axis=1)
        outs_s.append((
            kn.reshape(n_bs, 1, N_HEADS, 2, DKB).astype(dt),
            vn.reshape(n_bs, 1, N_HEADS, HEAD_W).astype(dt),
            conv_new, s_gdn.astype(dt), s_hgrn.astype(dt), s_c.astype(dt), s_n.astype(dt), s_m.astype(dt)))

    kp, vp, convp, gdnp, hgrnp, mcp, mnp_, mmp = [jnp.stack(z) for z in zip(*outs_p)]
    ks_, vs_, convs, gdns, hgrns, mcs, mns, mms = [jnp.stack(z) for z in zip(*outs_s)]
    return (xp.reshape(n_bp, seq, D_MODEL), xs.reshape(n_bs, 1, D_MODEL), kp, vp, ks_, vs_, convp, convs,
            gdnp, gdns, hgrnp, hgrns, mcp, mcs, mnp_, mns, mmp, mms)
```

```python
import functools
import math

import numpy as np
import jax
import jax.numpy as jnp
from jax import lax
from jax.experimental import pallas as pl
from jax.experimental.pallas import tpu as pltpu

f32 = jnp.float32
bf16 = jnp.bfloat16

D_MODEL = 1024
N_HEADS = 4
HEAD_W = 64
GROUP_W = N_HEADS * HEAD_W
DKB = 32
CONV_W = 4
CHUNK = 64
SUB = 16
NUM_BUCKETS = 32
MAX_DISTANCE = 128
N_EXPERTS = 8
EPS = 1e-6
NEG = -1e30
P_COLS = 4096
GATE_COL = 3840
VMEM_LIMIT = 56 * 1024 * 1024

NN = ((1,), (0,))
NT = ((1,), (1,))
TN = ((0,), (0,))


def _dg(a, b, dims=NN):
    return lax.dot_general(a, b, (dims, ((), ())), preferred_element_type=f32)


def _split(a):
    hi = a.astype(bf16)
    lo = (a - hi.astype(f32)).astype(bf16)
    return hi, lo


def _mm3(a, b, dims=NN):
    ah, al = _split(a)
    bh, bl = _split(b)
    return _dg(ah, bh, dims) + (_dg(ah, bl, dims) + _dg(al, bh, dims))


def _mm2(a, b01, dims=NN):
    ah, al = _split(a)
    return _dg(ah, b01, dims) + _dg(al, b01, dims)


def _mm2l(a01, b, dims=NN):
    bh, bl = _split(b)
    return _dg(a01, bh, dims) + _dg(a01, bl, dims)


def _mm1(a, b, dims=NN):
    return _dg(a.astype(bf16), b.astype(bf16), dims)


def _iota(shape, dim):
    return lax.broadcasted_iota(jnp.int32, shape, dim)


def _group_ones(width, group):
    r = _iota((width, width), 0) // group
    c = _iota((width, width), 1) // group
    return (r == c).astype(bf16)


def _group_sum(x, group):
    ones = _group_ones(x.shape[-1], group)
    hi = x.astype(bf16)
    r1 = x - hi.astype(f32)
    mid = r1.astype(bf16)
    lo = (r1 - mid.astype(f32)).astype(bf16)
    return _dg(hi, ones) + (_dg(mid, ones) + _dg(lo, ones))


def _recip(x):
    r = 1.0 / x
    return r * (2.0 - x * r)


def _silu(x):
    return x * jax.nn.sigmoid(x)


def _softplus(x):
    return jnp.maximum(x, 0.0) + jnp.log1p(jnp.exp(-jnp.abs(x)))


def _stack_cols(xc, n=N_HEADS, rows=HEAD_W):
    return jnp.concatenate([xc[:, h:h + 1] for h in range(n)], axis=0)


def _expand_cols(xc, n=N_HEADS, width=HEAD_W):
    r = xc.shape[0]
    return jnp.concatenate([jnp.broadcast_to(xc[:, h:h + 1], (r, width)) for h in range(n)], axis=1)


def _cat_rows(xr, lo, n=N_HEADS, width=HEAD_W):
    return jnp.concatenate([xr[h:h + 1, lo:lo + width] for h in range(n)], axis=1)


def _head_stack(x, n=N_HEADS, width=HEAD_W):
    lane_head = _iota(x.shape, 1) // width
    return jnp.concatenate([jnp.where(lane_head == h, x, 0.0) for h in range(n)], axis=0)


def _fold_heads(x_sm, n=N_HEADS):
    r = x_sm.shape[0] // n
    out = x_sm[0:r]
    for h in range(1, n):
        out = out + x_sm[h * r:(h + 1) * r]
    return out


def _bd_masks(n=GROUP_W, blk=CHUNK):
    r = _iota((n, n), 0)
    c = _iota((n, n), 1)
    same = (r // blk) == (c // blk)
    lower = same & ((r % blk) >= (c % blk))
    strict = same & ((r % blk) > (c % blk))
    return same, lower, strict


def _cparams(*sem):
    return pltpu.CompilerParams(dimension_semantics=sem, vmem_limit_bytes=VMEM_LIMIT)


def _inproj_kernel(x_ref, g_ref, w_ref, wgt_ref, p_ref, gt_ref, h_scr):
    @pl.when(pl.program_id(1) == 0)
    def _():
        x = x_ref[...]
        ms = jnp.mean(x * x, axis=-1, keepdims=True)
        h = ((x * lax.rsqrt(ms + EPS)) * g_ref[...]).astype(bf16)
        h_scr[...] = h
        gt_ref[...] = _dg(wgt_ref[...], h, NT)
    p_ref[...] = _dg(h_scr[...], w_ref[...], NN)


def _inproj(x, g, w_perm, w_gate_t):
    t = x.shape[0]
    tm = min(512, t)
    tn = 1024
    return pl.pallas_call(
        _inproj_kernel,
        grid=(t // tm, P_COLS // tn),
        in_specs=[
            pl.BlockSpec((tm, D_MODEL), lambda i, j: (i, 0)),
            pl.BlockSpec((1, D_MODEL), lambda i, j: (0, 0)),
            pl.BlockSpec((D_MODEL, tn), lambda i, j: (0, j)),
            pl.BlockSpec((16, D_MODEL), lambda i, j: (0, 0)),
        ],
        out_specs=[
            pl.BlockSpec((tm, tn), lambda i, j: (i, j)),
            pl.BlockSpec((16, tm), lambda i, j: (0, i)),
        ],
        out_shape=[jax.ShapeDtypeStruct((t, P_COLS), f32), jax.ShapeDtypeStruct((16, t), f32)],
        scratch_shapes=[pltpu.VMEM((tm, D_MODEL), bf16)],
        compiler_params=_cparams("arbitrary", "arbitrary"),
        name="inproj",
    )(x, g.reshape(1, D_MODEL), w_perm, w_gate_t)


def _qk_gnorm(x, g):
    ms = _group_sum(x * x, DKB) * (1.0 / DKB)
    return (x * lax.rsqrt(ms + EPS)) * g


def _bprep_kernel(q_ref, k_ref, gq_ref, gk_ref, qn_ref, kn_ref):
    qn_ref[...] = _qk_gnorm(q_ref[...], gq_ref[...])
    kn_ref[...] = _qk_gnorm(k_ref[...], gk_ref[...])


def _bprep_t_kernel(q_ref, k_ref, v_ref, gq_ref, gk_ref, qnt_ref, kn_ref, vt_ref):
    qnt_ref[...] = _qk_gnorm(q_ref[...], gq_ref[...]).T
    kn_ref[...] = _qk_gnorm(k_ref[...], gk_ref[...])
    vt_ref[...] = v_ref[...].T


def _bprep(p, qk_norm_g, transposed):
    t = p.shape[0]
    tm = min(512, t)
    gq = jnp.tile(qk_norm_g[0], GROUP_W // DKB).reshape(1, GROUP_W)
    gk = jnp.tile(qk_norm_g[1], GROUP_W // DKB).reshape(1, GROUP_W)
    col = lambda c: pl.BlockSpec((tm, GROUP_W), lambda i: (i, c))
    gain = pl.BlockSpec((1, GROUP_W), lambda i: (0, 0))
    rows = pl.BlockSpec((tm, GROUP_W), lambda i: (i, 0))
    rows_t = pl.BlockSpec((GROUP_W, tm), lambda i: (0, i))
    if transposed:
        return pl.pallas_call(
            _bprep_t_kernel,
            grid=(t // tm,),
            in_specs=[col(4), col(5), col(6), gain, gain],
            out_specs=[rows_t, rows, rows_t],
            out_shape=[jax.ShapeDtypeStruct((GROUP_W, t), f32), jax.ShapeDtypeStruct((t, GROUP_W), f32),
                       jax.ShapeDtypeStruct((GROUP_W, t), f32)],
            compiler_params=_cparams("arbitrary"),
            name="bprep_t",
        )(p, p, p, gq, gk)
    return pl.pallas_call(
        _bprep_kernel,
        grid=(t // tm,),
        in_specs=[col(4), col(5), gain, gain],
        out_specs=[rows, rows],
        out_shape=[jax.ShapeDtypeStruct((t, GROUP_W), f32)] * 2,
        compiler_params=_cparams("arbitrary"),
        name="bprep",
    )(p, p, gq, gk)


def _outproj_kernel(oa_ref, ob_ref, oc_ref, od_ref, ag_ref, cg_ref, dg_ref, x_ref, g_ref, w_ref, y_ref, *, b_scale):
    def gnorm(x, g):
        ms = _group_sum(x * x, HEAD_W) * (1.0 / HEAD_W)
        return (x * lax.rsqrt(ms + EPS)) * g
    g = g_ref[...]
    mixes = (
        gnorm(oa_ref[...], g[0:1]) * _silu(ag_ref[...]),
        gnorm(ob_ref[...], g[1:2]) * b_scale,
        gnorm(oc_ref[...], g[2:3]) * jax.nn.sigmoid(cg_ref[...]),
        gnorm(od_ref[...], g[3:4]) * jax.nn.sigmoid(dg_ref[...]),
    )
    y = x_ref[...]
    for i, m in enumerate(mixes):
        y = y + _dg(m.astype(bf16), w_ref[i * GROUP_W:(i + 1) * GROUP_W, :], NN)
    y_ref[...] = y


def _outproj(oa, ob, oc, od, p, x, gains, w_out, b_scale):
    t = x.shape[0]
    tm = min(512, t)
    row = lambda i: (i, 0)
    return pl.pallas_call(
        functools.partial(_outproj_kernel, b_scale=b_scale),
        grid=(t // tm,),
        in_specs=[
            pl.BlockSpec((tm, GROUP_W), row), pl.BlockSpec((tm, GROUP_W), row),
            pl.BlockSpec((tm, GROUP_W), row), pl.BlockSpec((tm, GROUP_W), row),
            pl.BlockSpec((tm, GROUP_W), lambda i: (i, 3)),
            pl.BlockSpec((tm, GROUP_W), lambda i: (i, 10)),
            pl.BlockSpec((tm, GROUP_W), lambda i: (i, 14)),
            pl.BlockSpec((tm, D_MODEL), row),
            pl.BlockSpec((4, GROUP_W), lambda i: (0, 0)),
            pl.BlockSpec((D_MODEL, D_MODEL), lambda i: (0, 0)),
        ],
        out_specs=pl.BlockSpec((tm, D_MODEL), row),
        out_shape=jax.ShapeDtypeStruct((t, D_MODEL), f32),
        compiler_params=_cparams("arbitrary"),
        name="outproj",
    )(oa, ob, oc, od, p, p, p, x, gains, w_out)


def _ffn_kernel(x_ref, g_ref, wg_ref, wu_ref, wd_ref, y_ref, h_scr, acc_scr):
    f = pl.program_id(1)

    @pl.when(f == 0)
    def _():
        x = x_ref[...]
        ms = jnp.mean(x * x, axis=-1, keepdims=True)
        h_scr[...] = ((x * lax.rsqrt(ms + EPS)) * g_ref[...]).astype(bf16)
        acc_scr[...] = x

    h = h_scr[...]
    a = _silu(_dg(h, wg_ref[...])) * _dg(h, wu_ref[...])
    acc_scr[...] += _dg(a.astype(bf16), wd_ref[...])

    @pl.when(f == pl.num_programs(1) - 1)
    def _():
        y_ref[...] = acc_scr[...]


def _ffn(x, g, wg, wu, wd):
    t = x.shape[0]
    d_ff = wg.shape[1]
    tm = min(512, t)
    tf = d_ff // 2
    return pl.pallas_call(
        _ffn_kernel,
        grid=(t // tm, d_ff // tf),
        in_specs=[
            pl.BlockSpec((tm, D_MODEL), lambda i, f: (i, 0)),
            pl.BlockSpec((1, D_MODEL), lambda i, f: (0, 0)),
            pl.BlockSpec((D_MODEL, tf), lambda i, f: (0, f)),
            pl.BlockSpec((D_MODEL, tf), lambda i, f: (0, f)),
            pl.BlockSpec((tf, D_MODEL), lambda i, f: (f, 0)),
        ],
        out_specs=pl.BlockSpec((tm, D_MODEL), lambda i, f: (i, 0)),
        out_shape=jax.ShapeDtypeStruct((t, D_MODEL), f32),
        scratch_shapes=[pltpu.VMEM((tm, D_MODEL), bf16), pltpu.VMEM((tm, D_MODEL), f32)],
        compiler_params=_cparams("arbitrary", "arbitrary"),
        name="ffn",
    )(x, g.reshape(1, D_MODEL), wg, wu, wd)


def _router_kernel(x_ref, g_ref, r_ref, h_ref, comb_ref):
    x = x_ref[...]
    ms = jnp.mean(x * x, axis=-1, keepdims=True)
    h = (x * lax.rsqrt(ms + EPS)) * g_ref[...]
    hb = h.astype(bf16)
    h_ref[...] = hb
    logits = _dg(hb, r_ref[...])
    lane = _iota(logits.shape, 1)
    logits = jnp.where(lane < N_EXPERTS, logits, -jnp.inf)
    v1 = jnp.max(logits, axis=-1, keepdims=True)
    i1 = jnp.min(jnp.where(logits == v1, lane, 128), axis=-1, keepdims=True)
    rest = jnp.where(lane == i1, -jnp.inf, logits)
    v2 = jnp.max(rest, axis=-1, keepdims=True)
    i2 = jnp.min(jnp.where(rest == v2, lane, 128), axis=-1, keepdims=True)
    e2 = jnp.exp(v2 - v1)
    den = 1.0 + e2
    comb_ref[...] = jnp.where(lane == i1, 1.0 / den, 0.0) + jnp.where(lane == i2, e2 / den, 0.0)


def _router(x, g, router_pad):
    t = x.shape[0]
    tm = min(512, t)
    return pl.pallas_call(
        _router_kernel,
        grid=(t // tm,),
        in_specs=[
            pl.BlockSpec((tm, D_MODEL), lambda i: (i, 0)),
            pl.BlockSpec((1, D_MODEL), lambda i: (0, 0)),
            pl.BlockSpec((D_MODEL, 128), lambda i: (0, 0)),
        ],
        out_specs=[pl.BlockSpec((tm, D_MODEL), lambda i: (i, 0)), pl.BlockSpec((tm, 128), lambda i: (i, 0))],
        out_shape=[jax.ShapeDtypeStruct((t, D_MODEL), bf16), jax.ShapeDtypeStruct((t, 128), f32)],
        compiler_params=_cparams("arbitrary"),
        name="router",
    )(x, g.reshape(1, D_MODEL), router_pad)


def _moe_kernel(x_ref, h_ref, comb_ref, wg_ref, wu_ref, wd_ref, y_ref, acc_scr):
    e = pl.program_id(1)
    f = pl.program_id(2)

    @pl.when((e == 0) & (f == 0))
    def _():
        acc_scr[...] = x_ref[...]

    comb = comb_ref[...]
    cw = jnp.sum(jnp.where(_iota(comb.shape, 1) == e, comb, 0.0), axis=-1, keepdims=True)
    h = h_ref[...]
    a = _silu(_dg(h, wg_ref[...])) * _dg(h, wu_ref[...]) * cw
    acc_scr[...] += _dg(a.astype(bf16), wd_ref[...])

    @pl.when((e == pl.num_programs(1) - 1) & (f == pl.num_programs(2) - 1))
    def _():
        y_ref[...] = acc_scr[...]


def _moe(x, h, comb, wg, wu, wd):
    t = x.shape[0]
    n_e, _, d_ff = wg.shape
    tm = min(512, t)
    tf = d_ff // 2
    return pl.pallas_call(
        _moe_kernel,
        grid=(t // tm, n_e, d_ff // tf),
        in_specs=[
            pl.BlockSpec((tm, D_MODEL), lambda i, e, f: (i, 0)),
            pl.BlockSpec((tm, D_MODEL), lambda i, e, f: (i, 0)),
            pl.BlockSpec((tm, 128), lambda i, e, f: (i, 0)),
            pl.BlockSpec((None, D_MODEL, tf), lambda i, e, f: (e, 0, f)),
            pl.BlockSpec((None, D_MODEL, tf), lambda i, e, f: (e, 0, f)),
            pl.BlockSpec((None, tf, D_MODEL), lambda i, e, f: (e, f, 0)),
        ],
        out_specs=pl.BlockSpec((tm, D_MODEL), lambda i, e, f: (i, 0)),
        out_shape=jax.ShapeDtypeStruct((t, D_MODEL), f32),
        scratch_shapes=[pltpu.VMEM((tm, D_MODEL), f32)],
        compiler_params=_cparams("arbitrary", "arbitrary", "arbitrary"),
        name="moe",
    )(x, h, comb, wg, wu, wd)


def _t5_bucket_np(n):
    n = np.maximum(n, 0)
    max_exact = NUM_BUCKETS // 2
    nf = np.maximum(n, 1).astype(np.float32)
    large = max_exact + (np.log(nf / np.float32(max_exact)) / np.float32(math.log(MAX_DISTANCE / max_exact))
                         * np.float32(NUM_BUCKETS - max_exact)).astype(np.int32)
    return np.where(n < max_exact, n, np.minimum(large, NUM_BUCKETS - 1))


def _shifted_bias(rel_bias):
    rb = rel_bias.astype(f32)
    return rb - rb[NUM_BUCKETS - 1:NUM_BUCKETS]


ACC_ROWS = HEAD_W + 8
LOG2E = math.log2(math.e)


def _attn_kernel(qi_ref, kj_ref, lam_ref, qt_ref, k_ref, vt_ref, toep_ref, o_ref, qs_scr, m_scr, acc_scr, *, tq):
    p = pl.program_id(1)
    i = qi_ref[p]
    j = kj_ref[p]
    n_hc = 2 * N_HEADS
    c2 = (DKB ** -0.5) * LOG2E

    @pl.when(j == 0)
    def _():
        qt = qt_ref[...]
        row_grp = _iota(qt.shape, 0) // DKB
        for hc in range(n_hc):
            qs_scr[:, hc * tq:(hc + 1) * tq] = jnp.where(row_grp == hc, qt, 0.0).astype(bf16)
        m_scr[...] = jnp.full(m_scr.shape, NEG, f32)
        acc_scr[...] = jnp.zeros(acc_scr.shape, f32)

    def step(near):
        tk = k_ref.shape[0]
        st_all = _dg(k_ref[...].astype(bf16), qs_scr[...], NN)
        vt = vt_ref[...]
        ones = jnp.ones((ACC_ROWS - HEAD_W, tk), f32)
        for h in range(N_HEADS):
            vh = jnp.concatenate([vt[h * HEAD_W:(h + 1) * HEAD_W, :], ones], axis=0).astype(bf16)
            for hc in (2 * h, 2 * h + 1):
                s = st_all[:, hc * tq:(hc + 1) * tq] * c2
                if near:
                    s = s + toep_ref[(i - j) * N_HEADS + h]
                m_old = m_scr[hc:hc + 1, :]
                m_new = jnp.maximum(m_old, jnp.max(s, axis=0, keepdims=True))
                pexp = jnp.exp2(s - m_new)
                acc_scr[hc] = jnp.exp2(m_old - m_new) * acc_scr[hc] + _dg(vh, pexp.astype(bf16), NN)
                m_scr[hc:hc + 1, :] = m_new

    @pl.when(i - j <= 1)
    def _():
        step(True)

    @pl.when(i - j > 1)
    def _():
        step(False)

    @pl.when(j == i)
    def _():
        lam = lam_ref[0]
        outs = []
        for h in range(N_HEADS):
            a0 = acc_scr[2 * h]
            a1 = acc_scr[2 * h + 1]
            outs.append(a0[0:HEAD_W] * _recip(a0[HEAD_W:HEAD_W + 1])
                        - lam * (a1[0:HEAD_W] * _recip(a1[HEAD_W:HEAD_W + 1])))
        o_ref[...] = jnp.concatenate(outs, axis=0).T


def _toeplitz_bias_tiles(rel_bias, t):
    period = 2 * t + 1
    m = np.arange(period)[None, :]
    dist = m - t + np.array([0, t])[:, None]
    tab = _shifted_bias(rel_bias)
    u = jnp.take(tab, jnp.asarray(_t5_bucket_np(dist)), axis=0)
    u = jnp.where(jnp.asarray(dist >= 0)[:, :, None], u * LOG2E, NEG)
    u = jnp.transpose(u, (0, 2, 1)).reshape(2 * N_HEADS, period)
    flat = jnp.broadcast_to(u[:, None, :], (2 * N_HEADS, t, period)).reshape(2 * N_HEADS, t * period)
    skew = flat[:, :t * (period - 1)].reshape(2 * N_HEADS, t, period - 1)
    return skew[:, :, t:2 * t]


def _attn_prompt(qnt, kn, vt, lam, rel_bias, n_batch, seq):
    tq = min(512, seq)
    nq = seq // tq
    pairs =[(i, j) for i in range(nq) for j in range(i + 1)]
    qi = jnp.asarray(np.array([a for a, _ in pairs], np.int32))
    kj = jnp.asarray(np.array([b for _, b in pairs], np.int32))
    toep = _toeplitz_bias_tiles(rel_bias, tq)
    grid_spec = pltpu.PrefetchScalarGridSpec(
        num_scalar_prefetch=2,
        grid=(n_batch, len(pairs)),
        in_specs=[
            pl.BlockSpec(memory_space=pltpu.SMEM),
            pl.BlockSpec((GROUP_W, tq), lambda b_, p_, qi_, kj_: (0, b_ * nq + qi_[p_])),
            pl.BlockSpec((tq, GROUP_W), lambda b_, p_, qi_, kj_: (b_ * nq + kj_[p_], 0)),
            pl.BlockSpec((GROUP_W, tq), lambda b_, p_, qi_, kj_: (0, b_ * nq + kj_[p_])),
            pl.BlockSpec((2 * N_HEADS, tq, tq), lambda b_, p_, qi_, kj_: (0, 0, 0)),
        ],
        out_specs=pl.BlockSpec((tq, GROUP_W), lambda b_, p_, qi_, kj_: (b_ * nq + qi_[p_], 0)),
        scratch_shapes=[
            pltpu.VMEM((GROUP_W, 2 * N_HEADS * tq), bf16),
            pltpu.VMEM((2 * N_HEADS, tq), f32),
            pltpu.VMEM((2 * N_HEADS, ACC_ROWS, tq), f32),
        ],
    )
    return pl.pallas_call(
        functools.partial(_attn_kernel, tq=tq),
        grid_spec=grid_spec,
        out_shape=jax.ShapeDtypeStruct((n_batch * seq, GROUP_W), f32),
        compiler_params=_cparams("arbitrary", "arbitrary"),
        name="attn_prompt",
    )(qi, kj, lam.reshape(1), qnt, kn, vt, toep)


def _attn_decode_kernel(pt_ref, lam_ref, q_ref, kn_ref, vn_ref, blast_ref, bself_ref, *rest, pg, n_pages):
    k_refs = rest[:pg]
    v_refs = rest[pg:2 * pg]
    o_ref, qs_scr, s_scr, aself_scr, acc_scr = rest[2 * pg:]
    t = pl.program_id(1)
    n_steps = n_pages // pg
    n_hc = 2 * N_HEADS
    page = k_refs[0].shape[1]
    scale = DKB ** -0.5
    rnd = lambda z: z.astype(bf16).astype(f32)

    @pl.when(t == 0)
    def _():
        q = jnp.broadcast_to(q_ref[...], (n_hc, GROUP_W))
        keep = (_iota(q.shape, 1) // DKB) == _iota(q.shape, 0)
        qs_scr[...] = jnp.where(keep, q, 0.0)

    @pl.when(t < n_steps)
    def _():
        qs_b = qs_scr[...].astype(bf16)
        parts = []
        for g in range(pg):
            s = _dg(qs_b, k_refs[g][...].astype(bf16), NN) * scale
            is_last = (t * pg + g) == (n_pages - 1)
            parts.append(s + jnp.where(is_last, blast_ref[...], 0.0))
        s_scr[t] = jnp.concatenate(parts, axis=1)

    @pl.when(t == n_steps - 1)
    def _():
        s_all = s_scr[...]
        s_self = jnp.sum(rnd(qs_scr[...]) * rnd(kn_ref[...]), axis=-1, keepdims=True) * scale + bself_ref[...]
        m = jnp.maximum(jnp.max(jnp.max(s_all, axis=2, keepdims=True), axis=0), s_self)
        p = jnp.exp(s_all - m)
        p_self = jnp.exp(s_self - m)
        l = jnp.sum(jnp.sum(p, axis=2, keepdims=True), axis=0) + p_self
        inv_l = _recip(l)
        pn = p * inv_l
        pn_self = p_self * inv_l
        lam = lam_ref[0]
        rows = [pn[:, 2 * h:2 * h + 1, :] - lam * pn[:, 2 * h + 1:2 * h + 2, :] for h in range(N_HEADS)]
        s_scr[...] = jnp.concatenate(rows + [jnp.zeros_like(rows[0])] * N_HEADS, axis=1)
        rows_self = [pn_self[2 * h:2 * h + 1] - lam * pn_self[2 * h + 1:2 * h + 2] for h in range(N_HEADS)]
        aself_scr[...] = jnp.concatenate(rows_self + [jnp.zeros_like(rows_self[0])] * N_HEADS, axis=0)
        acc_scr[...] = jnp.zeros(acc_scr.shape, f32)

    @pl.when(t >= n_steps)
    def _():
        a = s_scr[t - n_steps].astype(bf16)
        acc = acc_scr[...]
        for g in range(pg):
            acc = acc + _dg(a[:, g * page:(g + 1) * page], v_refs[g][...].astype(bf16), NT)
        acc_scr[...] = acc

    @pl.when(t == 2 * n_steps - 1)
    def _():
        o = acc_scr[...] + rnd(aself_scr[...]) * rnd(vn_ref[...])
        lane_head = _iota((1, GROUP_W), 1) // HEAD_W
        out = jnp.zeros((1, GROUP_W), f32)
        for h in range(N_HEADS):
            out = jnp.where(lane_head == h, o[h:h + 1], out)
        o_ref[...] = out


def _attn_decode(qn, kn, vn, page_table, cache_k, cache_v, layer, lam, rel_bias):
    n_b, n_pages = page_table.shape
    page = cache_k.shape[3]
    pg = min(16, n_pages)
    n_steps = n_pages // pg
    past = n_pages * page
    tab = _shifted_bias(rel_bias)
    d_last = past - ((n_pages - 1) * page + np.arange(page))
    blast = jnp.repeat(jnp.take(tab, jnp.asarray(_t5_bucket_np(d_last)), axis=0).T, 2, axis=0)
    bself = jnp.repeat(tab[0].reshape(N_HEADS, 1), 2, axis=0)

    def k_spec(g):
        return pl.BlockSpec((None, None, GROUP_W, page),
                            lambda b_, t_, pt: (layer, pt[b_, jnp.minimum(t_, n_steps - 1) * pg + g], 0, 0))

    def v_spec(g):
        return pl.BlockSpec((None, None, GROUP_W, page),
                            lambda b_, t_, pt: (layer, pt[b_, jnp.maximum(t_ - n_steps, 0) * pg + g], 0, 0))

    row = pl.BlockSpec((None, 1, GROUP_W), lambda b_, t_, pt: (b_, 0, 0))
    grid_spec = pltpu.PrefetchScalarGridSpec(
        num_scalar_prefetch=1,
        grid=(n_b, 2 * n_steps),
        in_specs=[pl.BlockSpec(memory_space=pltpu.SMEM), row, row, row,
                  pl.BlockSpec((2 * N_HEADS, page), lambda b_, t_, pt: (0, 0)),
                  pl.BlockSpec((2 * N_HEADS, 1), lambda b_, t_, pt: (0, 0))]
                 + [k_spec(g) for g in range(pg)] + [v_spec(g) for g in range(pg)],
        out_specs=row,
        scratch_shapes=[
            pltpu.VMEM((2 * N_HEADS, GROUP_W), f32),
            pltpu.VMEM((n_steps, 2 * N_HEADS, pg * page), f32),
            pltpu.VMEM((2 * N_HEADS, 1), f32),
            pltpu.VMEM((2 * N_HEADS, GROUP_W), f32),
        ],
    )
    r3 = lambda z: z.reshape(n_b, 1, GROUP_W)
    out = pl.pallas_call(
        functools.partial(_attn_decode_kernel, pg=pg, n_pages=n_pages),
        grid_spec=grid_spec,
        out_shape=jax.ShapeDtypeStruct((n_b, 1, GROUP_W), f32),
        compiler_params=_cparams("arbitrary", "arbitrary"),
        name="attn_decode",
    )(page_table, lam.reshape(1), r3(qn), r3(kn), r3(vn), blast, bself,
      *([cache_k] * pg), *([cache_v] * pg))
    return out.reshape(n_b, GROUP_W)


def _tri(n, dtype=f32):
    return (_iota((n, n), 0) >= _iota((n, n), 1)).astype(dtype)


def _block_tri_t(n, blk):
    r = _iota((n, n), 0)
    c = _iota((n, n), 1)
    return (((r // blk) == (c // blk)) & (r <= c)).astype(bf16)


def _block_tri(n, blk):
    r = _iota((n, n), 0)
    c = _iota((n, n), 1)
    return (((r // blk) == (c // blk)) & (r >= c)).astype(bf16)


def _gdn_kernel(u_ref, gc_ref, gr_ref, cw_ref, alr_ref, dtr_ref, alc_ref, dtc_ref, o_ref, s_out_ref, ext_scr, s_scr, *, tb):
    i = pl.program_id(1)

    @pl.when(i == 0)
    def _():
        ext_scr[0:8, :] = jnp.zeros((8, 3 * GROUP_W), f32)
        s_scr[...] = jnp.zeros(s_scr.shape, f32)

    ext_scr[8:8 + tb, :] = u_ref[...]
    w = cw_ref[...]
    conv = ext_scr[8:8 + tb, :] * w[3:4]
    for jj in range(1, CONV_W):
        conv = conv + ext_scr[8 - jj:8 - jj + tb, :] * w[3 - jj:4 - jj]
    ext_scr[0:8, :] = ext_scr[tb:tb + 8, :]
    qkv = _silu(conv)

    def l2n(x):
        return x * lax.rsqrt(_group_sum(x * x, HEAD_W) + EPS)

    q = l2n(qkv[:, 0:GROUP_W]) * (HEAD_W ** -0.5)
    k = l2n(qkv[:, GROUP_W:2 * GROUP_W])
    v = qkv[:, 2 * GROUP_W:3 * GROUP_W]

    gc = gc_ref[...]
    g_col = -jnp.exp(alr_ref[...]) * _softplus(gc[:, 0:4] + dtr_ref[...])
    beta_col = jax.nn.sigmoid(gc[:, 4:8])
    gr = gr_ref[...]
    g_row = -jnp.exp(alc_ref[...]) * _softplus(gr[0:4, :] + dtc_ref[...])
    g_row8 = jnp.concatenate([g_row, jnp.zeros_like(g_row)], axis=0)
    gcum_row = _mm2(g_row8, _block_tri_t(tb, CHUNK))

    same, lower, strict = _bd_masks()
    tri = _tri(CHUNK, bf16)
    r = _iota((GROUP_W, GROUP_W), 0)
    c = _iota((GROUP_W, GROUP_W), 1)
    eye = (r == c).astype(f32)

    chunks = range(tb // CHUNK)
    gcums, qks, m_bds = [], [], []
    for ch in chunks:
        lo = ch * CHUNK
        gcum = _mm2l(tri, g_col[lo:lo + CHUNK])
        g_stack = _stack_cols(gcum)
        g_cat = _cat_rows(gcum_row, lo)
        decay = jnp.exp(jnp.where(lower, g_stack - g_cat, NEG))
        ksm = _head_stack(k[lo:lo + CHUNK])
        kk = _mm1(ksm, ksm, NT)
        qks.append(_mm1(_head_stack(q[lo:lo + CHUNK]), ksm, NT) * decay)
        m_bds.append(_stack_cols(beta_col[lo:lo + CHUNK]) * kk * jnp.where(strict, decay, 0.0))
        gcums.append(gcum)

    def sibling(lev):
        return ((r >> (lev + 1)) == (c >> (lev + 1))) & (((r >> lev) & 1) == 1) & (((c >> lev) & 1) == 0)

    xs = [eye - jnp.where(sibling(0), m, 0.0) for m in m_bds]
    for lev in range(1, 6):
        sel = sibling(lev)
        xs = [x - _mm3(_mm3(x, jnp.where(sel, m, 0.0)), x) for x, m in zip(xs, m_bds)]

    for ch in chunks:
        lo = ch * CHUNK
        qc, kc, vc = q[lo:lo + CHUNK], k[lo:lo + CHUNK], v[lo:lo + CHUNK]
        gcum, bcol = gcums[ch], beta_col[lo:lo + CHUNK]
        s_bd = s_scr[...]
        kq_s = _mm1(jnp.concatenate([kc, qc], axis=0), s_bd)
        ks, qs = kq_s[0:CHUNK], kq_s[CHUNK:2 * CHUNK]
        eg_all = _expand_cols(jnp.exp(gcum))
        rhs = _expand_cols(bcol) * (vc - eg_all * ks)
        u_sm = _mm3(xs[ch], _head_stack(rhs))
        o_sm = _mm1(qks[ch], u_sm)
        o_ref[lo:lo + CHUNK, :] = eg_all * qs + _fold_heads(o_sm)
        u_all = _fold_heads(u_sm)
        g_last = gcum[CHUNK - 1:CHUNK, :]
        kw = kc * _expand_cols(jnp.exp(g_last - gcum))
        d_stack = jnp.concatenate(
            [jnp.broadcast_to(jnp.exp(g_last[:, h:h + 1]), (HEAD_W, 1)) for h in range(N_HEADS)], axis=0)
        s_scr[...] = d_stack * s_bd + jnp.where(same, _mm1(kw, u_all, TN), 0.0)

    @pl.when(i == pl.num_programs(1) - 1)
    def _():
        s_out_ref[...] = s_scr[...]


def _gdn_prompt(p, gt, conv_w, a_log, dt_bias, n_batch, seq):
    tb = min(256, seq)
    nb = seq // tb
    r14 = lambda z: z.astype(f32).reshape(1, N_HEADS)
    c41 = lambda z: z.astype(f32).reshape(N_HEADS, 1)
    o, s_bd = pl.pallas_call(
        functools.partial(_gdn_kernel, tb=tb),
        grid=(n_batch, nb),
        in_specs=[
            pl.BlockSpec((tb, 3 * GROUP_W), lambda b, i: (b * nb + i, 0)),
            pl.BlockSpec((tb, 128), lambda b, i: (b * nb + i, GATE_COL // 128)),
            pl.BlockSpec((16, tb), lambda b, i: (0, b * nb + i)),
            pl.BlockSpec((CONV_W, 3 * GROUP_W), lambda b, i: (0, 0)),
            pl.BlockSpec((1, N_HEADS), lambda b, i: (0, 0)),
            pl.BlockSpec((1, N_HEADS), lambda b, i: (0, 0)),
            pl.BlockSpec((N_HEADS, 1), lambda b, i: (0, 0)),
            pl.BlockSpec((N_HEADS, 1), lambda b, i: (0, 0)),
        ],
        out_specs=[
            pl.BlockSpec((tb, GROUP_W), lambda b, i: (b * nb + i, 0)),
            pl.BlockSpec((None, GROUP_W, GROUP_W), lambda b, i: (b, 0, 0)),
        ],
        out_shape=[jax.ShapeDtypeStruct((n_batch * seq, GROUP_W), f32),
                   jax.ShapeDtypeStruct((n_batch, GROUP_W, GROUP_W), f32)],
        scratch_shapes=[pltpu.VMEM((tb + 8, 3 * GROUP_W), f32), pltpu.VMEM((GROUP_W, GROUP_W), f32)],
        compiler_params=_cparams("arbitrary", "arbitrary"),
        name="gdn_prompt",
    )(p, p, gt, conv_w.astype(f32), r14(a_log), r14(dt_bias), c41(a_log), c41(dt_bias))
    return o, _bd_diag(s_bd)


def _bd_diag(s_bd):
    n_b = s_bd.shape[0]
    s5 = s_bd.reshape(n_b, N_HEADS, HEAD_W, N_HEADS, HEAD_W)
    return jnp.stack([s5[:, h, :, h, :] for h in range(N_HEADS)], axis=1)


def _hgrn_kernel(q_ref, f_ref, i_ref, lb_ref, o_ref, s_out_ref, st_scr, q_scr, k_scr, b_scr, *, tb):
    blk = pl.program_id(1)

    @pl.when(blk == 0)
    def _():
        st_scr[...] = jnp.zeros(st_scr.shape, f32)

    lb = lb_ref[...]
    z = f_ref[...]
    logf = jnp.log(lb + (1.0 - lb) * jax.nn.sigmoid(z))
    q_scr[...] = _silu(q_ref[...])
    k_scr[...] = (1.0 - lb) * jax.nn.sigmoid(-z)
    b_scr[...] = _mm2l(_block_tri(tb, SUB), logf)

    same, _, _ = _bd_masks()
    ones_bd = _group_ones(GROUP_W, HEAD_W)
    row = _iota((SUB * SUB, GROUP_W), 0)
    tmask = (row % SUB) >= (row // SUB)

    def rep_t(x):
        return jnp.broadcast_to(x[None], (SUB, SUB, GROUP_W)).reshape(SUB * SUB, GROUP_W)

    def rep_j(x):
        return jnp.broadcast_to(x[:, None, :], (SUB, SUB, GROUP_W)).reshape(SUB * SUB, GROUP_W)

    def body(c, carry):
        r0 = pl.multiple_of(c * SUB, SUB)
        qs = q_scr[pl.ds(r0, SUB), :]
        ks = k_scr[pl.ds(r0, SUB), :]
        vs = i_ref[pl.ds(r0, SUB), :]
        bs = b_scr[pl.ds(r0, SUB), :]
        st = st_scr[...]
        o_inter = _mm1(qs * jnp.exp(bs), st, NT)
        wgt = rep_t(qs) * jnp.exp(jnp.where(tmask, rep_t(bs) - rep_j(bs), NEG)) * rep_j(ks)
        a = _mm2(wgt, ones_bd)
        o_diag = jnp.sum((a * rep_j(vs)).reshape(SUB, SUB, GROUP_W), axis=0)
        o_ref[pl.ds(r0, SUB), :] = o_inter + o_diag
        b_last = bs[SUB - 1:SUB, :]
        kw = ks * jnp.exp(b_last - bs)
        st_scr[...] = st * jnp.exp(b_last) + jnp.where(same, _mm1(vs, kw, TN), 0.0)
        return carry

    lax.fori_loop(0, tb // SUB, body, 0)

    @pl.when(blk == pl.num_programs(1) - 1)
    def _():
        s_out_ref[...] = st_scr[...]


def _hgrn_prompt(p, lb, n_batch, seq):
    tb = min(256, seq)
    nb = seq // tb
    blk = lambda col: pl.BlockSpec((tb, GROUP_W), lambda b, i: (b * nb + i, col))
    o, st = pl.pallas_call(
        functools.partial(_hgrn_kernel, tb=tb),
        grid=(n_batch, nb),
        in_specs=[blk(7), blk(8), blk(9), pl.BlockSpec((1, GROUP_W), lambda b, i: (0, 0))],
        out_specs=[
            pl.BlockSpec((tb, GROUP_W), lambda b, i: (b * nb + i, 0)),
            pl.BlockSpec((None, GROUP_W, GROUP_W), lambda b, i: (b, 0, 0)),
        ],
        out_shape=[jax.ShapeDtypeStruct((n_batch * seq, GROUP_W), f32),
                   jax.ShapeDtypeStruct((n_batch, GROUP_W, GROUP_W), f32)],
        scratch_shapes=[pltpu.VMEM((GROUP_W, GROUP_W), f32)] + [pltpu.VMEM((tb, GROUP_W), f32)] * 3,
        compiler_params=_cparams("arbitrary", "arbitrary"),
        name="hgrn_prompt",
    )(p, p, p, lb.astype(f32).reshape(1, GROUP_W))
    return o, jnp.swapaxes(_bd_diag(st), -1, -2)


def _log_sigmoid(x):
    return jnp.minimum(x, 0.0) - jnp.log1p(jnp.exp(-jnp.abs(x)))


def _mlstm_kernel(q_ref, k_ref, v_ref, gc_ref, gr_ref, ibr_ref, fbr_ref, ibc_ref, fbc_ref,
                  o_ref, c_out_ref, n_out_ref, m_out_ref, c_scr, n_scr, m_scr, *, tb):
    blk = pl.program_id(1)

    @pl.when(blk == 0)
    def _():
        c_scr[...] = jnp.zeros(c_scr.shape, f32)
        n_scr[...] = jnp.zeros(n_scr.shape, f32)
        m_scr[...] = jnp.zeros(m_scr.shape, f32)

    q = q_ref[...]
    k = k_ref[...] * (HEAD_W ** -0.5)
    v = v_ref[...]
    gc = gc_ref[...]
    li_col = gc[:, 8:12] + ibr_ref[...]
    lf_col = _log_sigmoid(gc[:, 12:16] + fbr_ref[...])
    gr = gr_ref[...]
    li_row = gr[8:12, :] + ibc_ref[...]
    lf_row = _log_sigmoid(gr[12:16, :] + fbc_ref[...])
    b_row = _mm2(jnp.concatenate([lf_row, jnp.zeros_like(lf_row)], axis=0), _block_tri_t(tb, CHUNK))

    same, lower, _ = _bd_masks()
    tri = _tri(CHUNK, bf16)

    for ch in range(tb // CHUNK):
        lo = ch * CHUNK
        qc, kc, vc = q[lo:lo + CHUNK], k[lo:lo + CHUNK], v[lo:lo + CHUNK]
        b_col = _mm2l(tri, lf_col[lo:lo + CHUNK])
        b_stack = _stack_cols(b_col)
        d_mat = jnp.where(lower, b_stack - _cat_rows(b_row, lo) + _cat_rows(li_row, lo), NEG)
        m_row = m_scr[...]
        m_stack = jnp.concatenate(
            [jnp.broadcast_to(m_row[:, h:h + 1], (CHUNK, 1)) for h in range(N_HEADS)], axis=0)
        inter = b_stack + m_stack
        m_t = jnp.maximum(inter, jnp.max(d_mat, axis=-1, keepdims=True))
        w_inter = jnp.exp(inter - m_t)
        qsm = _head_stack(qc)
        ksm = _head_stack(kc)
        pmat = _mm1(qsm, ksm, NT) * jnp.exp(d_mat - m_t)
        c_bd = c_scr[...]
        n_row = n_scr[...]
        num = w_inter * _mm1(qsm, c_bd) + _mm1(pmat, _head_stack(vc))
        den = w_inter * jnp.sum(qsm * n_row, axis=-1, keepdims=True) + jnp.sum(pmat, axis=-1, keepdims=True)
        h_sm = num / jnp.maximum(jnp.abs(den), jnp.exp(-m_t))
        o_ref[lo:lo + CHUNK, :] = _fold_heads(h_sm)
        m_new = jnp.concatenate(
            [m_t[h * CHUNK + CHUNK - 1:h * CHUNK + CHUNK, :] for h in range(N_HEADS)], axis=1)
        b_last = b_col[CHUNK - 1:CHUNK, :]
        w_end = jnp.exp(b_last - b_col + li_col[lo:lo + CHUNK] - m_new)
        d0 = jnp.exp(b_last + m_row - m_new)
        kw = kc * _expand_cols(w_end)
        d0_stack = jnp.concatenate(
            [jnp.broadcast_to(d0[:, h:h + 1], (HEAD_W, 1)) for h in range(N_HEADS)], axis=0)
        c_scr[...] = d0_stack * c_bd + jnp.where(same, _mm1(kw, vc, TN), 0.0)
        n_scr[...] = _expand_cols(d0) * n_row + jnp.sum(kw, axis=0, keepdims=True)
        m_scr[...] = m_new

    @pl.when(blk == pl.num_programs(1) - 1)
    def _():
        c_out_ref[...] = c_scr[...]
        n_out_ref[...] = n_scr[...]
        m_out_ref[...] = m_scr[...]


def _mlstm_prompt(p, gt, i_bias, f_bias, n_batch, seq):
    tb = min(256, seq)
    nb = seq // tb
    blk = lambda col: pl.BlockSpec((tb, GROUP_W), lambda b, i: (b * nb + i, col))
    r14 = lambda z: z.astype(f32).reshape(1, N_HEADS)
    c41 = lambda z: z.astype(f32).reshape(N_HEADS, 1)
    small = lambda shape: pl.BlockSpec(shape, lambda b, i: (0, 0))
    o, c_bd, n_row, m_row = pl.pallas_call(
        functools.partial(_mlstm_kernel, tb=tb),
        grid=(n_batch, nb),
        in_specs=[blk(11), blk(12), blk(13),
                  pl.BlockSpec((tb, 128), lambda b, i: (b * nb + i, GATE_COL // 128)),
                  pl.BlockSpec((16, tb), lambda b, i: (0, b * nb + i)),
                  small((1, N_HEADS)), small((1, N_HEADS)), small((N_HEADS, 1)), small((N_HEADS, 1))],
        out_specs=[
            pl.BlockSpec((tb, GROUP_W), lambda b, i: (b * nb + i, 0)),
            pl.BlockSpec((None, GROUP_W, GROUP_W), lambda b, i: (b, 0, 0)),
            pl.BlockSpec((None, 1, GROUP_W), lambda b, i: (b, 0, 0)),
            pl.BlockSpec((None, 1, N_HEADS), lambda b, i: (b, 0, 0)),
        ],
        out_shape=[jax.ShapeDtypeStruct((n_batch * seq, GROUP_W), f32),
                   jax.ShapeDtypeStruct((n_batch, GROUP_W, GROUP_W), f32),
                   jax.ShapeDtypeStruct((n_batch, 1, GROUP_W), f32),
                   jax.ShapeDtypeStruct((n_batch, 1, N_HEADS), f32)],
        scratch_shapes=[pltpu.VMEM((GROUP_W, GROUP_W), f32), pltpu.VMEM((1, GROUP_W), f32),
                        pltpu.VMEM((1, N_HEADS), f32)],
        compiler_params=_cparams("arbitrary", "arbitrary"),
        name="mlstm_prompt",
    )(p, p, p, p, gt, r14(i_bias), r14(f_bias), c41(i_bias), c41(f_bias))
    return (o, _bd_diag(c_bd), n_row.reshape(n_batch, N_HEADS, HEAD_W), m_row.reshape(n_batch, N_HEADS))


def _gdn_dec_prep_kernel(u_ref, buf_ref, cw_ref, q_ref, k_ref, v_ref):
    w = cw_ref[...]
    conv = u_ref[...] * w[3:4]
    for jj in range(CONV_W - 1):
        conv = conv + buf_ref[jj] * w[jj:jj + 1]
    qkv = _silu(conv)

    def l2n(x):
        return x * lax.rsqrt(_group_sum(x * x, HEAD_W) + EPS)

    q_ref[...] = l2n(qkv[:, 0:GROUP_W]) * (HEAD_W ** -0.5)
    k_ref[...] = l2n(qkv[:, GROUP_W:2 * GROUP_W])
    v_ref[...] = qkv[:, 2 * GROUP_W:3 * GROUP_W]


def _gdn_dec_prep(p, conv_buf, conv_w):
    n_b = p.shape[0]
    out = jax.ShapeDtypeStruct((n_b, GROUP_W), f32)
    return pl.pallas_call(
        _gdn_dec_prep_kernel,
        grid=(1,),
        in_specs=[pl.BlockSpec((n_b, 3 * GROUP_W), lambda i: (0, 0)),
                  pl.BlockSpec((CONV_W - 1, n_b, 3 * GROUP_W), lambda i: (0, 0, 0)),
                  pl.BlockSpec((CONV_W, 3 * GROUP_W), lambda i: (0, 0))],
        out_specs=[pl.BlockSpec((n_b, GROUP_W), lambda i: (0, 0))] * 3,
        out_shape=[out, out, out],
        compiler_params=_cparams("arbitrary"),
        name="gdn_dec_prep",
    )(p, jnp.swapaxes(conv_buf.astype(f32), 0, 1), conv_w.astype(f32))


def _rec_decode_kernel(gq_ref, gk_ref, gv_ref, ga_ref, gb_ref, al_ref, dtb_ref, sg_ref,
                       cq_ref, cf_ref, ci_ref, lbc_ref, sh_ref,
                       dq_ref, dk_ref, dv_ref, di_ref, df_ref, ib_ref, fb_ref, sc_ref, sn_ref, sm_ref,
                       oa_ref, sg_out, oc_ref, sh_out, od_ref, sc_out, sn_out, sm_out):
    q, k, v = gq_ref[...], gk_ref[...], gv_ref[...]
    s = sg_ref[...]
    g = -jnp.exp(al_ref[...]) * _softplus(ga_ref[...] + dtb_ref[...])
    eg = jnp.exp(g)
    beta = jax.nn.sigmoid(gb_ref[...])
    ks = jnp.sum(k * s, axis=1, keepdims=True)
    qs = jnp.sum(q * s, axis=1, keepdims=True)
    u = beta * (v - eg * ks)
    qk = jnp.sum(q * k, axis=1, keepdims=True)
    oa_ref[...] = eg * qs + qk * u
    sg_out[...] = eg * s + k * u

    lb = lbc_ref[...]
    z = cf_ref[...]
    logf = jnp.log(lb + (1.0 - lb) * jax.nn.sigmoid(z))
    kc = (1.0 - lb) * jax.nn.sigmoid(-z)
    qc = _silu(cq_ref[...])
    vc = ci_ref[...]
    sh = sh_ref[...]
    ef = jnp.exp(logf)
    oc_ref[...] = jnp.sum((qc * ef) * sh, axis=1, keepdims=True) + jnp.sum(qc * kc, axis=1, keepdims=True) * vc
    sh_out[...] = ef * sh + kc * vc

    qd = dq_ref[...]
    kd = dk_ref[...] * (HEAD_W ** -0.5)
    vd = dv_ref[...]
    li = di_ref[...] + ib_ref[...]
    lf = _log_sigmoid(df_ref[...] + fb_ref[...])
    m0 = sm_ref[...]
    cs = sc_ref[...]
    n0 = sn_ref[...]
    inter = lf + m0
    m_t = jnp.maximum(inter, li)
    w_inter = jnp.exp(inter - m_t)
    qkd = jnp.sum(qd * kd, axis=1, keepdims=True) * jnp.exp(li - m_t)
    num = w_inter * jnp.sum(qd * cs, axis=1, keepdims=True) + qkd * vd
    den = w_inter * jnp.sum(qd * n0, axis=1, keepdims=True) + qkd
    od_ref[...] = num / jnp.maximum(jnp.abs(den), jnp.exp(-m_t))
    w_end = jnp.exp(li - m_t)
    d0 = jnp.exp(lf + m0 - m_t)
    sc_out[...] = d0 * cs + (w_end * kd) * vd
    sn_out[...] = d0 * n0 + w_end * kd
    sm_out[...] = m_t


def _rec_decode(p, gq, gk, gv, a_log, dt_bias, lb, i_bias, f_bias, s_gdn, s_hgrn, s_c, s_n, s_m):
    n_b = p.shape[0]
    rows = n_b * N_HEADS
    rb = min(16, rows)
    col = lambda z: z.reshape(rows, HEAD_W, 1)
    vrow = lambda z: z.reshape(rows, 1, HEAD_W)
    sca = lambda z: z.reshape(rows, 1, 1)
    per_head = lambda z: jnp.tile(z.astype(f32), n_b).reshape(rows, 1, 1)
    blockp = lambda b: p[:, b * GROUP_W:(b + 1) * GROUP_W]
    gates = p[:, GATE_COL:GATE_COL + 16]
    lb_col = jnp.tile(lb.astype(f32).reshape(N_HEADS, HEAD_W), (n_b, 1)).reshape(rows, HEAD_W, 1)
    st = lambda z: z.astype(f32).reshape(rows, HEAD_W, HEAD_W)
    args = [col(gq), col(gk), vrow(gv), sca(gates[:, 0:4]), sca(gates[:, 4:8]), per_head(a_log), per_head(dt_bias), st(s_gdn),
            col(blockp(7)), col(blockp(8)), vrow(blockp(9)), lb_col, st(s_hgrn),
            col(blockp(11)), col(blockp(12)), vrow(blockp(13)), sca(gates[:, 8:12]), sca(gates[:, 12:16]),
            per_head(i_bias), per_head(f_bias), st(s_c), col(s_n.astype(f32)), sca(s_m.astype(f32))]

    def spec(a):
        return pl.BlockSpec((rb,) + a.shape[1:], lambda i: (i, 0, 0))

    o_vrow = jax.ShapeDtypeStruct((rows, 1, HEAD_W), f32)
    o_st = jax.ShapeDtypeStruct((rows, HEAD_W, HEAD_W), f32)
    o_col = jax.ShapeDtypeStruct((rows, HEAD_W, 1), f32)
    o_sca = jax.ShapeDtypeStruct((rows, 1, 1), f32)
    outs = [o_vrow, o_st, o_vrow, o_st, o_vrow, o_st, o_col, o_sca]
    res = pl.pallas_call(
        _rec_decode_kernel,
        grid=(rows // rb,),
        in_specs=[spec(a) for a in args],
        out_specs=[spec(a) for a in outs],
        out_shape=outs,
        compiler_params=_cparams("arbitrary"),
        name="rec_decode",
    )(*args)
    oa, sg, oc, sh, od, sc, sn, sm = res
    s4 = lambda z: z.reshape(n_b, N_HEADS, HEAD_W, HEAD_W)
    o2 = lambda z: z.reshape(n_b, GROUP_W)
    return (o2(oa), s4(sg), o2(oc), s4(sh), o2(od), s4(sc),
            sn.reshape(n_b, N_HEADS, HEAD_W), sm.reshape(n_b, N_HEADS))


def _permute_w_in(w):
    d_in = w.shape[1]
    a_gate0 = 3 * GROUP_W
    d_gate0 = d_in - GROUP_W - 8
    main = jnp.concatenate([w[:, 0:a_gate0], w[:, a_gate0 + 8:d_gate0], w[:, d_gate0 + 8:]], axis=1)
    gates = jnp.concatenate([w[:, a_gate0:a_gate0 + 8], w[:, d_gate0:d_gate0 + 8]], axis=1)
    pad = jnp.zeros((w.shape[0], P_COLS - main.shape[1] - 16), w.dtype)
    return jnp.concatenate([main, gates, pad], axis=1).astype(bf16), gates.T.astype(bf16)


def kernel(x_prompt, x_sample, page_table, cache_k, cache_v, state_gdn_conv, state_gdn, state_hgrn, state_mlstm_C, state_mlstm_n, state_mlstm_m, attn_norm_g, w_in, gdn_conv_w, gdn_a_log, gdn_dt_bias, gdn_norm_g, diff_qk_norm_g, diff_lambda, diff_subln_g, rel_bias, hgrn_lb_logits, hgrn_norm_g, mlstm_i_bias, mlstm_f_bias, mlstm_norm_g, w_out, ffn_norm_g, ffn_w_gate, ffn_w_up, ffn_w_down, moe_router, moe_w_gate, moe_w_up, moe_w_down):
    depth = w_in.shape[0]
    n_bp, seq, _ = x_prompt.shape
    n_bs = x_sample.shape[0]
    n_pool, page = cache_k.shape[1], cache_k.shape[2]
    dt = x_prompt.dtype

    lb_p = jax.nn.softmax(hgrn_lb_logits.astype(f32), axis=0)
    lb_cum = jnp.cumsum(lb_p, axis=0)
    hgrn_lb = lb_cum - lb_cum[0:1]
    cache_k4 = jnp.transpose(cache_k, (0, 1, 3, 4, 5, 2)).reshape(depth, n_pool, GROUP_W, page)
    cache_v4 = jnp.transpose(cache_v, (0, 1, 3, 4, 2)).reshape(depth, n_pool, GROUP_W, page)

    xp = x_prompt.reshape(n_bp * seq, D_MODEL)
    xs = x_sample.reshape(n_bs, D_MODEL)
    outs_p, outs_s = [], []
    for l in range(depth):
        w_perm, w_gate_t = _permute_w_in(w_in[l])
        w_out_b = w_out[l].astype(bf16)
        gains = jnp.stack([jnp.tile(g.astype(f32), N_HEADS) for g in
                           (gdn_norm_g[l], diff_subln_g[l], hgrn_norm_g[l], mlstm_norm_g[l])])
        lam_init = 0.8 - 0.6 * math.exp(-0.3 * l)
        lam32 = diff_lambda[l].astype(f32)
        lam = jnp.exp(jnp.sum(lam32[0] * lam32[1])) - jnp.exp(jnp.sum(lam32[2] * lam32[3])) + lam_init
        if l % 2 == 0:
            ffn_w = (ffn_w_gate[l // 2].astype(bf16), ffn_w_up[l // 2].astype(bf16), ffn_w_down[l // 2].astype(bf16))
        else:
            router_pad = jnp.pad(moe_router[l // 2].astype(bf16), ((0, 0), (0, 128 - N_EXPERTS)))
            moe_w = (moe_w_gate[l // 2].astype(bf16), moe_w_up[l // 2].astype(bf16), moe_w_down[l // 2].astype(bf16))

        def channel_mix(x):
            if l % 2 == 0:
                return _ffn(x, ffn_norm_g[l], *ffn_w)
            h, comb = _router(x, ffn_norm_g[l], router_pad)
            return _moe(x, h, comb, *moe_w)

        p, gt = _inproj(xp, attn_norm_g[l], w_perm, w_gate_t)
        qnt, kn, vt = _bprep(p, diff_qk_norm_g[l], True)
        ob = _attn_prompt(qnt, kn, vt, lam, rel_bias, n_bp, seq)
        oa, s_gdn = _gdn_prompt(p, gt, gdn_conv_w[l], gdn_a_log[l], gdn_dt_bias[l], n_bp, seq)
        oc, s_hgrn = _hgrn_prompt(p, hgrn_lb[l], n_bp, seq)
        od, s_c, s_n, s_m = _mlstm_prompt(p, gt, mlstm_i_bias[l], mlstm_f_bias[l], n_bp, seq)
        xp = _outproj(oa, ob, oc, od, p, xp, gains, w_out_b, 1.0 - lam_init)
        xp = channel_mix(xp)
        p3 = p.reshape(n_bp, seq, P_COLS)
        outs_p.append((
            kn.reshape(n_bp, seq, N_HEADS, 2, DKB).astype(dt),
            p3[:, :, 6 * GROUP_W:7 * GROUP_W].reshape(n_bp, seq, N_HEADS, HEAD_W).astype(dt),
            p3[:, seq - (CONV_W - 1):, 0:3 * GROUP_W].astype(dt),
            s_gdn.astype(dt), s_hgrn.astype(dt), s_c.astype(dt), s_n.astype(dt), s_m.astype(dt)))

        p, gt = _inproj(xs, attn_norm_g[l], w_perm, w_gate_t)
        qn, kn = _bprep(p, diff_qk_norm_g[l], False)
        vn = p[:, 6 * GROUP_W:7 * GROUP_W]
        ob = _attn_decode(qn, kn, vn, page_table, cache_k4, cache_v4, l, lam, rel_bias)
        u = p[:, 0:3 * GROUP_W]
        gq, gk, gv = _gdn_dec_prep(u, state_gdn_conv[l], gdn_conv_w[l])
        oa, s_gdn, oc, s_hgrn, od, s_c, s_n, s_m = _rec_decode(
            p, gq, gk, gv, gdn_a_log[l], gdn_dt_bias[l], hgrn_lb[l], mlstm_i_bias[l], mlstm_f_bias[l],
            state_gdn[l], state_hgrn[l], state_mlstm_C[l], state_mlstm_n[l], state_mlstm_m[l])
        xs = _outproj(oa, ob, oc, od, p, xs, gains, w_out_b, 1.0 - lam_init)
        xs = channel_mix(xs)
        conv_new = jnp.concatenate([state_gdn_conv[l][:, 1:].astype(dt), u[:, None, :].astype(dt)], axis=1)
        outs_s.append((
            kn.reshape(n_bs, 1, N_HEADS, 2, DKB).astype(dt),
            vn.reshape(n_bs, 1, N_HEADS, HEAD_W).astype(dt),
            conv_new, s_gdn.astype(dt), s_hgrn.astype(dt), s_c.astype(dt), s_n.astype(dt), s_m.astype(dt)))

    kp, vp, convp, gdnp, hgrnp, mcp, mnp_, mmp = [jnp.stack(z) for z in zip(*outs_p)]
    ks_, vs_, convs, gdns, hgrns, mcs, mns, mms = [jnp.stack(z) for z in zip(*outs_s)]
    return (xp.reshape(n_bp, seq, D_MODEL), xs.reshape(n_bs, 1, D_MODEL), kp, vp, ks_, vs_, convp, convs,
            gdnp, gdns, hgrnp, hgrns, mcp, mcs, mnp_, mns, mmp, mms)
```

```python
import functools
import math

import numpy as np
import jax
import jax.numpy as jnp
from jax import lax
from jax.experimental import pallas as pl
from jax.experimental.pallas import tpu as pltpu

f32 = jnp.float32
bf16 = jnp.bfloat16

D_MODEL = 1024
N_HEADS = 4
HEAD_W = 64
GROUP_W = N_HEADS * HEAD_W
DKB = 32
CONV_W = 4
CHUNK = 64
SUB = 16
NUM_BUCKETS = 32
MAX_DISTANCE = 128
N_EXPERTS = 8
EPS = 1e-6
NEG = -1e30
P_COLS = 4096
GATE_COL = 3840
VMEM_LIMIT = 56 * 1024 * 1024

NN = ((1,), (0,))
NT = ((1,), (1,))
TN = ((0,), (0,))


def _dg(a, b, dims=NN):
    return lax.dot_general(a, b, (dims, ((), ())), preferred_element_type=f32)


def _split(a):
    hi = a.astype(bf16)
    lo = (a - hi.astype(f32)).astype(bf16)
    return hi, lo


def _mm3(a, b, dims=NN):
    ah, al = _split(a)
    bh, bl = _split(b)
    return _dg(ah, bh, dims) + (_dg(ah, bl, dims) + _dg(al, bh, dims))


def _mm2(a, b01, dims=NN):
    ah, al = _split(a)
    return _dg(ah, b01, dims) + _dg(al, b01, dims)


def _mm2l(a01, b, dims=NN):
    bh, bl = _split(b)
    return _dg(a01, bh, dims) + _dg(a01, bl, dims)


def _mm1(a, b, dims=NN):
    return _dg(a.astype(bf16), b.astype(bf16), dims)


def _iota(shape, dim):
    return lax.broadcasted_iota(jnp.int32, shape, dim)


def _group_ones(width, group):
    r = _iota((width, width), 0) // group
    c = _iota((width, width), 1) // group
    return (r == c).astype(bf16)


def _group_sum(x, group):
    ones = _group_ones(x.shape[-1], group)
    hi = x.astype(bf16)
    r1 = x - hi.astype(f32)
    mid = r1.astype(bf16)
    lo = (r1 - mid.astype(f32)).astype(bf16)
    return _dg(hi, ones) + (_dg(mid, ones) + _dg(lo, ones))


def _recip(x):
    r = 1.0 / x
    return r * (2.0 - x * r)


def _silu(x):
    return x * jax.nn.sigmoid(x)


def _softplus(x):
    return jnp.maximum(x, 0.0) + jnp.log1p(jnp.exp(-jnp.abs(x)))


def _stack_cols(xc, n=N_HEADS, rows=HEAD_W):
    return jnp.concatenate([xc[:, h:h + 1] for h in range(n)], axis=0)


def _expand_cols(xc, n=N_HEADS, width=HEAD_W):
    r = xc.shape[0]
    return jnp.concatenate([jnp.broadcast_to(xc[:, h:h + 1], (r, width)) for h in range(n)], axis=1)


def _cat_rows(xr, lo, n=N_HEADS, width=HEAD_W):
    return jnp.concatenate([xr[h:h + 1, lo:lo + width] for h in range(n)], axis=1)


def _head_stack(x, n=N_HEADS, width=HEAD_W):
    lane_head = _iota(x.shape, 1) // width
    return jnp.concatenate([jnp.where(lane_head == h, x, 0.0) for h in range(n)], axis=0)


def _fold_heads(x_sm, n=N_HEADS):
    r = x_sm.shape[0] // n
    out = x_sm[0:r]
    for h in range(1, n):
        out = out + x_sm[h * r:(h + 1) * r]
    return out


def _bd_masks(n=GROUP_W, blk=CHUNK):
    r = _iota((n, n), 0)
    c = _iota((n, n), 1)
    same = (r // blk) == (c // blk)
    lower = same & ((r % blk) >= (c % blk))
    strict = same & ((r % blk) > (c % blk))
    return same, lower, strict


def _cparams(*sem):
    return pltpu.CompilerParams(dimension_semantics=sem, vmem_limit_bytes=VMEM_LIMIT)


def _inproj_kernel(x_ref, g_ref, w_ref, wgt_ref, p_ref, gt_ref, h_scr):
    @pl.when(pl.program_id(1) == 0)
    def _():
        x = x_ref[...]
        ms = jnp.mean(x * x, axis=-1, keepdims=True)
        h = ((x * lax.rsqrt(ms + EPS)) * g_ref[...]).astype(bf16)
        h_scr[...] = h
        gt_ref[...] = _dg(wgt_ref[...], h, NT)
    p_ref[...] = _dg(h_scr[...], w_ref[...], NN)


def _inproj(x, g, w_perm, w_gate_t):
    t = x.shape[0]
    tm = min(512, t)
    tn = 1024
    return pl.pallas_call(
        _inproj_kernel,
        grid=(t // tm, P_COLS // tn),
        in_specs=[
            pl.BlockSpec((tm, D_MODEL), lambda i, j: (i, 0)),
            pl.BlockSpec((1, D_MODEL), lambda i, j: (0, 0)),
            pl.BlockSpec((D_MODEL, tn), lambda i, j: (0, j)),
            pl.BlockSpec((16, D_MODEL), lambda i, j: (0, 0)),
        ],
        out_specs=[
            pl.BlockSpec((tm, tn), lambda i, j: (i, j)),
            pl.BlockSpec((16, tm), lambda i, j: (0, i)),
        ],
        out_shape=[jax.ShapeDtypeStruct((t, P_COLS), f32), jax.ShapeDtypeStruct((16, t), f32)],
        scratch_shapes=[pltpu.VMEM((tm, D_MODEL), bf16)],
        compiler_params=_cparams("arbitrary", "arbitrary"),
        name="inproj",
    )(x, g.reshape(1, D_MODEL), w_perm, w_gate_t)


def _qk_gnorm(x, g):
    ms = _group_sum(x * x, DKB) * (1.0 / DKB)
    return (x * lax.rsqrt(ms + EPS)) * g


def _bprep_kernel(q_ref, k_ref, gq_ref, gk_ref, qn_ref, kn_ref):
    qn_ref[...] = _qk_gnorm(q_ref[...], gq_ref[...])
    kn_ref[...] = _qk_gnorm(k_ref[...], gk_ref[...])


def _bprep_t_kernel(q_ref, k_ref, v_ref, gq_ref, gk_ref, qnt_ref, kn_ref, vt_ref):
    qnt_ref[...] = _qk_gnorm(q_ref[...], gq_ref[...]).T
    kn_ref[...] = _qk_gnorm(k_ref[...], gk_ref[...])
    vt_ref[...] = v_ref[...].T


def _bprep(p, qk_norm_g, transposed):
    t = p.shape[0]
    tm = min(512, t)
    gq = jnp.tile(qk_norm_g[0], GROUP_W // DKB).reshape(1, GROUP_W)
    gk = jnp.tile(qk_norm_g[1], GROUP_W // DKB).reshape(1, GROUP_W)
    col = lambda c: pl.BlockSpec((tm, GROUP_W), lambda i: (i, c))
    gain = pl.BlockSpec((1, GROUP_W), lambda i: (0, 0))
    rows = pl.BlockSpec((tm, GROUP_W), lambda i: (i, 0))
    rows_t = pl.BlockSpec((GROUP_W, tm), lambda i: (0, i))
    if transposed:
        return pl.pallas_call(
            _bprep_t_kernel,
            grid=(t // tm,),
            in_specs=[col(4), col(5), col(6), gain, gain],
            out_specs=[rows_t, rows, rows_t],
            out_shape=[jax.ShapeDtypeStruct((GROUP_W, t), f32), jax.ShapeDtypeStruct((t, GROUP_W), f32),
                       jax.ShapeDtypeStruct((GROUP_W, t), f32)],
            compiler_params=_cparams("arbitrary"),
            name="bprep_t",
        )(p, p, p, gq, gk)
    return pl.pallas_call(
        _bprep_kernel,
        grid=(t // tm,),
        in_specs=[col(4), col(5), gain, gain],
        out_specs=[rows, rows],
        out_shape=[jax.ShapeDtypeStruct((t, GROUP_W), f32)] * 2,
        compiler_params=_cparams("arbitrary"),
        name="bprep",
    )(p, p, gq, gk)


def _outproj_kernel(oa_ref, ob_ref, oc_ref, od_ref, ag_ref, cg_ref, dg_ref, x_ref, g_ref, w_ref, y_ref, *, b_scale):
    def gnorm(x, g):
        ms = _group_sum(x * x, HEAD_W) * (1.0 / HEAD_W)
        return (x * lax.rsqrt(ms + EPS)) * g
    g = g_ref[...]
    mixes = (
        gnorm(oa_ref[...], g[0:1]) * _silu(ag_ref[...]),
        gnorm(ob_ref[...], g[1:2]) * b_scale,
        gnorm(oc_ref[...], g[2:3]) * jax.nn.sigmoid(cg_ref[...]),
        gnorm(od_ref[...], g[3:4]) * jax.nn.sigmoid(dg_ref[...]),
    )
    y = x_ref[...]
    for i, m in enumerate(mixes):
        y = y + _dg(m.astype(bf16), w_ref[i * GROUP_W:(i + 1) * GROUP_W, :], NN)
    y_ref[...] = y


def _outproj(oa, ob, oc, od, p, x, gains, w_out, b_scale):
    t = x.shape[0]
    tm = min(512, t)
    row = lambda i: (i, 0)
    return pl.pallas_call(
        functools.partial(_outproj_kernel, b_scale=b_scale),
        grid=(t // tm,),
        in_specs=[
            pl.BlockSpec((tm, GROUP_W), row), pl.BlockSpec((tm, GROUP_W), row),
            pl.BlockSpec((tm, GROUP_W), row), pl.BlockSpec((tm, GROUP_W), row),
            pl.BlockSpec((tm, GROUP_W), lambda i: (i, 3)),
            pl.BlockSpec((tm, GROUP_W), lambda i: (i, 10)),
            pl.BlockSpec((tm, GROUP_W), lambda i: (i, 14)),
            pl.BlockSpec((tm, D_MODEL), row),
            pl.BlockSpec((4, GROUP_W), lambda i: (0, 0)),
            pl.BlockSpec((D_MODEL, D_MODEL), lambda i: (0, 0)),
        ],
        out_specs=pl.BlockSpec((tm, D_MODEL), row),
        out_shape=jax.ShapeDtypeStruct((t, D_MODEL), f32),
        compiler_params=_cparams("arbitrary"),
        name="outproj",
    )(oa, ob, oc, od, p, p, p, x, gains, w_out)


def _ffn_kernel(x_ref, g_ref, wg_ref, wu_ref, wd_ref, y_ref, h_scr, acc_scr):
    f = pl.program_id(1)

    @pl.when(f == 0)
    def _():
        x = x_ref[...]
        ms = jnp.mean(x * x, axis=-1, keepdims=True)
        h_scr[...] = ((x * lax.rsqrt(ms + EPS)) * g_ref[...]).astype(bf16)
        acc_scr[...] = x

    h = h_scr[...]
    a = _silu(_dg(h, wg_ref[...])) * _dg(h, wu_ref[...])
    acc_scr[...] += _dg(a.astype(bf16), wd_ref[...])

    @pl.when(f == pl.num_programs(1) - 1)
    def _():
        y_ref[...] = acc_scr[...]


def _ffn(x, g, wg, wu, wd):
    t = x.shape[0]
    d_ff = wg.shape[1]
    tm = min(512, t)
    tf = d_ff // 2
    return pl.pallas_call(
        _ffn_kernel,
        grid=(t // tm, d_ff // tf),
        in_specs=[
            pl.BlockSpec((tm, D_MODEL), lambda i, f: (i, 0)),
            pl.BlockSpec((1, D_MODEL), lambda i, f: (0, 0)),
            pl.BlockSpec((D_MODEL, tf), lambda i, f: (0, f)),
            pl.BlockSpec((D_MODEL, tf), lambda i, f: (0, f)),
            pl.BlockSpec((tf, D_MODEL), lambda i, f: (f, 0)),
        ],
        out_specs=pl.BlockSpec((tm, D_MODEL), lambda i, f: (i, 0)),
        out_shape=jax.ShapeDtypeStruct((t, D_MODEL), f32),
        scratch_shapes=[pltpu.VMEM((tm, D_MODEL), bf16), pltpu.VMEM((tm, D_MODEL), f32)],
        compiler_params=_cparams("arbitrary", "arbitrary"),
        name="ffn",
    )(x, g.reshape(1, D_MODEL), wg, wu, wd)


LANE = 128
SUBL = D_MODEL // LANE
ROW_TILE = 256
SPARSE_MIN_TOKENS = 4096


def _tile_rows(x):
    return [x[:, j * LANE:(j + 1) * LANE] for j in range(SUBL)]


def _router_kernel(x_ref, g_ref, r_ref, h3_ref, meta_ref, cnt_ref, carry_scr):
    @pl.when(pl.program_id(0) == 0)
    def _():
        carry_scr[...] = jnp.zeros(carry_scr.shape, f32)

    x = x_ref[...]
    tm = x.shape[0]
    ms = jnp.mean(x * x, axis=-1, keepdims=True)
    h = (x * lax.rsqrt(ms + EPS)) * g_ref[...]
    for j, blk in enumerate(_tile_rows(h)):
        h3_ref[:, j, :] = blk
    logits = _dg(h.astype(bf16), r_ref[...])
    lane = _iota(logits.shape, 1)
    logits = jnp.where(lane < N_EXPERTS, logits, -jnp.inf)
    v1 = jnp.max(logits, axis=-1, keepdims=True)
    i1 = jnp.min(jnp.where(logits == v1, lane, LANE), axis=-1, keepdims=True)
    rest = jnp.where(lane == i1, -jnp.inf, logits)
    v2 = jnp.max(rest, axis=-1, keepdims=True)
    i2 = jnp.min(jnp.where(rest == v2, lane, LANE), axis=-1, keepdims=True)
    e2 = jnp.exp(v2 - v1)
    den = 1.0 + e2
    hit = ((lane == i1) | (lane == i2)).astype(f32)
    strict = (_iota((tm, tm), 0) > _iota((tm, tm), 1)).astype(bf16)
    before = _dg(strict, hit.astype(bf16)) + carry_scr[...]
    pos1 = jnp.sum(jnp.where(lane == i1, before, 0.0), axis=-1, keepdims=True)
    pos2 = jnp.sum(jnp.where(lane == i2, before, 0.0), axis=-1, keepdims=True)
    carry_scr[...] += jnp.sum(hit, axis=0, keepdims=True)
    meta = jnp.zeros(logits.shape, f32)
    for c, val in enumerate((i1.astype(f32), i2.astype(f32), 1.0 / den, e2 / den, pos1, pos2)):
        meta = jnp.where(lane == c, val, meta)
    meta_ref[...] = meta
    cnt_ref[...] = carry_scr[...]


def _router(x, g, router_pad):
    t = x.shape[0]
    tm = min(512, t)
    return pl.pallas_call(
        _router_kernel,
        grid=(t // tm,),
        in_specs=[
            pl.BlockSpec((tm, D_MODEL), lambda i: (i, 0)),
            pl.BlockSpec((1, D_MODEL), lambda i: (0, 0)),
            pl.BlockSpec((D_MODEL, LANE), lambda i: (0, 0)),
        ],
        out_specs=[pl.BlockSpec((tm, SUBL, LANE), lambda i: (i, 0, 0)),
                   pl.BlockSpec((tm, LANE), lambda i: (i, 0)),
                   pl.BlockSpec((1, LANE), lambda i: (0, 0))],
        out_shape=[jax.ShapeDtypeStruct((t, SUBL, LANE), f32), jax.ShapeDtypeStruct((t, LANE), f32),
                   jax.ShapeDtypeStruct((1, LANE), f32)],
        scratch_shapes=[pltpu.VMEM((1, LANE), f32)],
        compiler_params=_cparams("arbitrary"),
        name="router",
    )(x, g.reshape(1, D_MODEL), router_pad)


def _swiglu_bf16(x, wg, wu, wd):
    a = _silu(_dg(x, wg)) * _dg(x, wu)
    return _dg(a.astype(bf16), wd)


def _moe_dense_kernel(x_ref, h3_ref, meta_ref, wg_ref, wu_ref, wd_ref, y_ref, acc_scr):
    e = pl.program_id(1)
    f = pl.program_id(2)

    @pl.when((e == 0) & (f == 0))
    def _():
        acc_scr[...] = x_ref[...]

    meta = meta_ref[...]
    ef = e.astype(f32)
    cw = jnp.where(meta[:, 0:1] == ef, meta[:, 2:3], 0.0) + jnp.where(meta[:, 1:2] == ef, meta[:, 3:4], 0.0)
    h = jnp.concatenate([h3_ref[:, j, :] for j in range(SUBL)], axis=1).astype(bf16)
    acc_scr[...] += cw * _swiglu_bf16(h, wg_ref[...], wu_ref[...], wd_ref[...])

    @pl.when((e == pl.num_programs(1) - 1) & (f == pl.num_programs(2) - 1))
    def _():
        y_ref[...] = acc_scr[...]


def _moe_dense(x, h3, meta, wg, wu, wd):
    t = x.shape[0]
    n_e, _, d_ff = wg.shape
    tm = min(512, t)
    tf = d_ff // 2
    return pl.pallas_call(
        _moe_dense_kernel,
        grid=(t // tm, n_e, d_ff // tf),
        in_specs=[
            pl.BlockSpec((tm, D_MODEL), lambda i, e, f: (i, 0)),
            pl.BlockSpec((tm, SUBL, LANE), lambda i, e, f: (i, 0, 0)),
            pl.BlockSpec((tm, LANE), lambda i, e, f: (i, 0)),
            pl.BlockSpec((None, D_MODEL, tf), lambda i, e, f: (e, 0, f)),
            pl.BlockSpec((None, D_MODEL, tf), lambda i, e, f: (e, 0, f)),
            pl.BlockSpec((None, tf, D_MODEL), lambda i, e, f: (e, f, 0)),
        ],
        out_specs=pl.BlockSpec((tm, D_MODEL), lambda i, e, f: (i, 0)),
        out_shape=jax.ShapeDtypeStruct((t, D_MODEL), f32),
        scratch_shapes=[pltpu.VMEM((tm, D_MODEL), f32)],
        compiler_params=_cparams("arbitrary", "arbitrary", "arbitrary"),
        name="moe_dense",
    )(x, h3, meta, wg, wu, wd)


def _route_plan(meta, counts, t):
    cnt = counts[0, :N_EXPERTS].astype(jnp.int32)
    padded = ((cnt + ROW_TILE - 1) // ROW_TILE) * ROW_TILE
    ends = jnp.cumsum(padded)
    offs = ends - padded
    experts = jnp.arange(N_EXPERTS, dtype=jnp.int32)

    def dest(expert_col, rank_col):
        e = meta[:, expert_col].astype(jnp.int32)
        off = jnp.sum(jnp.where(e[:, None] == experts[None, :], offs[None, :], 0), axis=1)
        return off + meta[:, rank_col].astype(jnp.int32)

    n_rows = 2 * t + N_EXPERTS * ROW_TILE
    starts = jnp.arange(n_rows // ROW_TILE, dtype=jnp.int32) * ROW_TILE
    tile_expert = jnp.minimum(jnp.sum((starts[:, None] >= ends[None, :]).astype(jnp.int32), axis=1), N_EXPERTS - 1)
    n_used = (ends[N_EXPERTS - 1] // ROW_TILE).reshape(1)
    return dest(0, 4), dest(1, 5), tile_expert, n_used, n_rows


def _dispatch_kernel(d1_ref, d2_ref, h3_hbm, zero_hbm, xs_hbm, sem, *, tm):
    del zero_hbm
    base = pl.program_id(0) * tm

    def issue(k, carry):
        t = base + k
        src = h3_hbm.at[pl.ds(t, 1)]
        pltpu.make_async_copy(src, xs_hbm.at[pl.ds(d1_ref[t], 1)], sem).start()
        pltpu.make_async_copy(src, xs_hbm.at[pl.ds(d2_ref[t], 1)], sem).start()
        return carry

    lax.fori_loop(0, tm, issue, 0)
    for _ in range(2):
        pltpu.make_async_copy(h3_hbm.at[pl.ds(0, tm)], xs_hbm.at[pl.ds(0, tm)], sem).wait()


def _dispatch(h3, dest1, dest2, n_rows):
    t = h3.shape[0]
    tm = min(512, t)
    grid_spec = pltpu.PrefetchScalarGridSpec(
        num_scalar_prefetch=2,
        grid=(t // tm,),
        in_specs=[pl.BlockSpec(memory_space=pl.ANY), pl.BlockSpec(memory_space=pl.ANY)],
        out_specs=pl.BlockSpec(memory_space=pl.ANY),
        scratch_shapes=[pltpu.SemaphoreType.DMA(())],
    )
    return pl.pallas_call(
        functools.partial(_dispatch_kernel, tm=tm),
        grid_spec=grid_spec,
        out_shape=jax.ShapeDtypeStruct((n_rows, SUBL, LANE), f32),
        input_output_aliases={3: 0},
        compiler_params=_cparams("arbitrary"),
        name="moe_dispatch",
    )(dest1, dest2, h3, jnp.zeros((n_rows, SUBL, LANE), f32))


def _experts_kernel(te_ref, nu_ref, xs_ref, wg_ref, wu_ref, wd_ref, ys_ref):
    del te_ref
    r = pl.program_id(0)

    @pl.when(r < nu_ref[0])
    def _():
        x = jnp.concatenate([xs_ref[:, j, :] for j in range(SUBL)], axis=1).astype(bf16)
        for j, blk in enumerate(_tile_rows(_swiglu_bf16(x, wg_ref[...], wu_ref[...], wd_ref[...]))):
            ys_ref[:, j, :] = blk

    @pl.when(r >= nu_ref[0])
    def _():
        ys_ref[...] = jnp.zeros(ys_ref.shape, f32)


def _experts(xs, tile_expert, n_used, wg, wu, wd):
    n_rows = xs.shape[0]
    d_ff = wg.shape[2]
    rows = pl.BlockSpec((ROW_TILE, SUBL, LANE), lambda r, te, nu: (r, 0, 0))
    w_in = pl.BlockSpec((None, D_MODEL, d_ff), lambda r, te, nu: (te[r], 0, 0), pipeline_mode=pl.Buffered(1))
    w_dn = pl.BlockSpec((None, d_ff, D_MODEL), lambda r, te, nu: (te[r], 0, 0), pipeline_mode=pl.Buffered(1))
    grid_spec = pltpu.PrefetchScalarGridSpec(
        num_scalar_prefetch=2,
        grid=(n_rows // ROW_TILE,),
        in_specs=[rows, w_in, w_in, w_dn],
        out_specs=rows,
    )
    return pl.pallas_call(
        _experts_kernel,
        grid_spec=grid_spec,
        out_shape=jax.ShapeDtypeStruct((n_rows, SUBL, LANE), f32),
        compiler_params=_cparams("arbitrary"),
        name="moe_experts",
    )(tile_expert, n_used, xs, wg, wu, wd)


def _combine_kernel(d1_ref, d2_ref, x_ref, meta_ref, ys_hbm, y_ref, buf, sem, *, tm):
    base = pl.program_id(0) * tm

    def issue(k, carry):
        t = base + k
        pltpu.make_async_copy(ys_hbm.at[pl.ds(d1_ref[t], 1)], buf.at[0, pl.ds(k, 1)], sem).start()
        pltpu.make_async_copy(ys_hbm.at[pl.ds(d2_ref[t], 1)], buf.at[1, pl.ds(k, 1)], sem).start()
        return carry

    lax.fori_loop(0, tm, issue, 0)
    for s in range(2):
        pltpu.make_async_copy(ys_hbm.at[pl.ds(0, tm)], buf.at[s], sem).wait()
    meta = meta_ref[...]
    g1 = meta[:, 2:3]
    g2 = meta[:, 3:4]
    for j in range(SUBL):
        sl = slice(j * LANE, (j + 1) * LANE)
        y_ref[:, sl] = x_ref[:, sl] + (g1 * buf[0, :, j, :] + g2 * buf[1, :, j, :])


def _combine(x, meta, ys, dest1, dest2):
    t = x.shape[0]
    tm = min(256, t)
    grid_spec = pltpu.PrefetchScalarGridSpec(
        num_scalar_prefetch=2,
        grid=(t // tm,),
        in_specs=[pl.BlockSpec((tm, D_MODEL), lambda i, d1, d2: (i, 0)),
                  pl.BlockSpec((tm, LANE), lambda i, d1, d2: (i, 0)),
                  pl.BlockSpec(memory_space=pl.ANY)],
        out_specs=pl.BlockSpec((tm, D_MODEL), lambda i, d1, d2: (i, 0)),
        scratch_shapes=[pltpu.VMEM((2, tm, SUBL, LANE), f32), pltpu.SemaphoreType.DMA(())],
    )
    return pl.pallas_call(
        functools.partial(_combine_kernel, tm=tm),
        grid_spec=grid_spec,
        out_shape=jax.ShapeDtypeStruct((t, D_MODEL), f32),
        compiler_params=_cparams("arbitrary"),
        name="moe_combine",
    )(dest1, dest2, x, meta, ys)


def _moe(x, g, router_pad, wg, wu, wd):
    t = x.shape[0]
    h3, meta, counts = _router(x, g, router_pad)
    if t < SPARSE_MIN_TOKENS:
        return _moe_dense(x, h3, meta, wg, wu, wd)
    dest1, dest2, tile_expert, n_used, n_rows = _route_plan(meta, counts, t)
    xs = _dispatch(h3, dest1, dest2, n_rows)
    ys = _experts(xs, tile_expert, n_used, wg, wu, wd)
    return _combine(x, meta, ys, dest1, dest2)


def _t5_bucket_np(n):
    n = np.maximum(n, 0)
    max_exact = NUM_BUCKETS // 2
    nf = np.maximum(n, 1).astype(np.float32)
    large = max_exact + (np.log(nf / np.float32(max_exact)) / np.float32(math.log(MAX_DISTANCE / max_exact))
                         * np.float32(NUM_BUCKETS - max_exact)).astype(np.int32)
    return np.where(n < max_exact, n, np.minimum(large, NUM_BUCKETS - 1))


def _shifted_bias(rel_bias):
    rb = rel_bias.astype(f32)
    return rb - rb[NUM_BUCKETS - 1:NUM_BUCKETS]


ACC_ROWS = HEAD_W + 8
LOG2E = math.log2(math.e)


def _attn_kernel(qi_ref, kj_ref, lam_ref, qt_ref, k_ref, vt_ref, toep_ref, o_ref, qs_scr, m_scr, acc_scr, *, tq):
    p = pl.program_id(1)
    i = qi_ref[p]
    j = kj_ref[p]
    n_hc = 2 * N_HEADS
    c2 = (DKB ** -0.5) * LOG2E

    @pl.when(j == 0)
    def _():
        qt = qt_ref[...]
        row_grp = _iota(qt.shape, 0) // DKB
        for hc in range(n_hc):
            qs_scr[:, hc * tq:(hc + 1) * tq] = jnp.where(row_grp == hc, qt, 0.0).astype(bf16)
        m_scr[...] = jnp.full(m_scr.shape, NEG, f32)
        acc_scr[...] = jnp.zeros(acc_scr.shape, f32)

    def step(near):
        tk = k_ref.shape[0]
        st_all = _dg(k_ref[...].astype(bf16), qs_scr[...], NN)
        vt = vt_ref[...]
        ones = jnp.ones((ACC_ROWS - HEAD_W, tk), f32)
        for h in range(N_HEADS):
            vh = jnp.concatenate([vt[h * HEAD_W:(h + 1) * HEAD_W, :], ones], axis=0).astype(bf16)
            for hc in (2 * h, 2 * h + 1):
                s = st_all[:, hc * tq:(hc + 1) * tq] * c2
                if near:
                    s = s + toep_ref[(i - j) * N_HEADS + h]
                m_old = m_scr[hc:hc + 1, :]
                m_new = jnp.maximum(m_old, jnp.max(s, axis=0, keepdims=True))
                pexp = jnp.exp2(s - m_new)
                acc_scr[hc] = jnp.exp2(m_old - m_new) * acc_scr[hc] + _dg(vh, pexp.astype(bf16), NN)
                m_scr[hc:hc + 1, :] = m_new

    @pl.when(i - j <= 1)
    def _():
        step(True)

    @pl.when(i - j > 1)
    def _():
        step(False)

    @pl.when(j == i)
    def _():
        lam = lam_ref[0]
        outs = []
        for h in range(N_HEADS):
            a0 = acc_scr[2 * h]
            a1 = acc_scr[2 * h + 1]
            outs.append(a0[0:HEAD_W] * _recip(a0[HEAD_W:HEAD_W + 1])
                        - lam * (a1[0:HEAD_W] * _recip(a1[HEAD_W:HEAD_W + 1])))
        o_ref[...] = jnp.concatenate(outs, axis=0).T


def _toeplitz_kernel(u_ref, o_ref):
    t = o_ref.shape[0]
    rows = jnp.broadcast_to(u_ref[...], (t, 2 * t))
    o_ref[...] = pltpu.roll(rows, 0, 1, stride=1, stride_axis=0)[:, t:2 * t]


def _toeplitz_bias_tiles(rel_bias, t):
    m = np.arange(2 * t)[None, :]
    dist = m - t + np.array([0, t])[:, None]
    tab = _shifted_bias(rel_bias)
    u = jnp.take(tab, jnp.asarray(_t5_bucket_np(dist)), axis=0)
    u = jnp.where(jnp.asarray(dist >= 0)[:, :, None], u * LOG2E, NEG)
    u = jnp.transpose(u, (0, 2, 1)).reshape(2 * N_HEADS, 1, 2 * t)
    return pl.pallas_call(
        _toeplitz_kernel,
        grid=(2 * N_HEADS,),
        in_specs=[pl.BlockSpec((None, 1, 2 * t), lambda i: (i, 0, 0))],
        out_specs=pl.BlockSpec((None, t, t), lambda i: (i, 0, 0)),
        out_shape=jax.ShapeDtypeStruct((2 * N_HEADS, t, t), f32),
        compiler_params=_cparams("arbitrary"),
        name="toeplitz_bias",
    )(u)


def _attn_prompt(qnt, kn, vt, lam, rel_bias, n_batch, seq):
    tq = min(512, seq)
    nq = seq // tq
    pairs =[(i, j) for i in range(nq) for j in range(i + 1)]
    qi = jnp.asarray(np.array([a for a, _ in pairs], np.int32))
    kj = jnp.asarray(np.array([b for _, b in pairs], np.int32))
    toep = _toeplitz_bias_tiles(rel_bias, tq)
    grid_spec = pltpu.PrefetchScalarGridSpec(
        num_scalar_prefetch=2,
        grid=(n_batch, len(pairs)),
        in_specs=[
            pl.BlockSpec(memory_space=pltpu.SMEM),
            pl.BlockSpec((GROUP_W, tq), lambda b_, p_, qi_, kj_: (0, b_ * nq + qi_[p_])),
            pl.BlockSpec((tq, GROUP_W), lambda b_, p_, qi_, kj_: (b_ * nq + kj_[p_], 0)),
            pl.BlockSpec((GROUP_W, tq), lambda b_, p_, qi_, kj_: (0, b_ * nq + kj_[p_])),
            pl.BlockSpec((2 * N_HEADS, tq, tq), lambda b_, p_, qi_, kj_: (0, 0, 0)),
        ],
        out_specs=pl.BlockSpec((tq, GROUP_W), lambda b_, p_, qi_, kj_: (b_ * nq + qi_[p_], 0)),
        scratch_shapes=[
            pltpu.VMEM((GROUP_W, 2 * N_HEADS * tq), bf16),
            pltpu.VMEM((2 * N_HEADS, tq), f32),
            pltpu.VMEM((2 * N_HEADS, ACC_ROWS, tq), f32),
        ],
    )
    return pl.pallas_call(
        functools.partial(_attn_kernel, tq=tq),
        grid_spec=grid_spec,
        out_shape=jax.ShapeDtypeStruct((n_batch * seq, GROUP_W), f32),
        compiler_params=_cparams("arbitrary", "arbitrary"),
        name="attn_prompt",
    )(qi, kj, lam.reshape(1), qnt, kn, vt, toep)


def _attn_decode_kernel(pt_ref, lam_ref, q_ref, kn_ref, vn_ref, blast_ref, bself_ref, *rest, pg, n_pages):
    k_refs = rest[:pg]
    v_refs = rest[pg:2 * pg]
    o_ref, qs_scr, s_scr, aself_scr, acc_scr = rest[2 * pg:]
    t = pl.program_id(1)
    n_steps = n_pages // pg
    n_hc = 2 * N_HEADS
    page = k_refs[0].shape[1]
    scale = DKB ** -0.5
    rnd = lambda z: z.astype(bf16).astype(f32)

    @pl.when(t == 0)
    def _():
        q = jnp.broadcast_to(q_ref[...], (n_hc, GROUP_W))
        keep = (_iota(q.shape, 1) // DKB) == _iota(q.shape, 0)
        qs_scr[...] = jnp.where(keep, q, 0.0)

    @pl.when(t < n_steps)
    def _():
        qs_b = qs_scr[...].astype(bf16)
        parts = []
        for g in range(pg):
            s = _dg(qs_b, k_refs[g][...].astype(bf16), NN) * scale
            is_last = (t * pg + g) == (n_pages - 1)
            parts.append(s + jnp.where(is_last, blast_ref[...], 0.0))
        s_scr[t] = jnp.concatenate(parts, axis=1)

    @pl.when(t == n_steps - 1)
    def _():
        s_all = s_scr[...]
        s_self = jnp.sum(rnd(qs_scr[...]) * rnd(kn_ref[...]), axis=-1, keepdims=True) * scale + bself_ref[...]
        m = jnp.maximum(jnp.max(jnp.max(s_all, axis=2, keepdims=True), axis=0), s_self)
        p = jnp.exp(s_all - m)
        p_self = jnp.exp(s_self - m)
        l = jnp.sum(jnp.sum(p, axis=2, keepdims=True), axis=0) + p_self
        inv_l = _recip(l)
        pn = p * inv_l
        pn_self = p_self * inv_l
        lam = lam_ref[0]
        rows = [pn[:, 2 * h:2 * h + 1, :] - lam * pn[:, 2 * h + 1:2 * h + 2, :] for h in range(N_HEADS)]
        s_scr[...] = jnp.concatenate(rows + [jnp.zeros_like(rows[0])] * N_HEADS, axis=1)
        rows_self = [pn_self[2 * h:2 * h + 1] - lam * pn_self[2 * h + 1:2 * h + 2] for h in range(N_HEADS)]
        aself_scr[...] = jnp.concatenate(rows_self + [jnp.zeros_like(rows_self[0])] * N_HEADS, axis=0)
        acc_scr[...] = jnp.zeros(acc_scr.shape, f32)

    @pl.when(t >= n_steps)
    def _():
        a = s_scr[t - n_steps].astype(bf16)
        acc = acc_scr[...]
        for g in range(pg):
            acc = acc + _dg(a[:, g * page:(g + 1) * page], v_refs[g][...].astype(bf16), NT)
        acc_scr[...] = acc

    @pl.when(t == 2 * n_steps - 1)
    def _():
        o = acc_scr[...] + rnd(aself_scr[...]) * rnd(vn_ref[...])
        lane_head = _iota((1, GROUP_W), 1) // HEAD_W
        out = jnp.zeros((1, GROUP_W), f32)
        for h in range(N_HEADS):
            out = jnp.where(lane_head == h, o[h:h + 1], out)
        o_ref[...] = out


def _attn_decode(qn, kn, vn, page_table, cache_k, cache_v, layer, lam, rel_bias):
    n_b, n_pages = page_table.shape
    page = cache_k.shape[3]
    pg = min(16, n_pages)
    n_steps = n_pages // pg
    past = n_pages * page
    tab = _shifted_bias(rel_bias)
    d_last = past - ((n_pages - 1) * page + np.arange(page))
    blast = jnp.repeat(jnp.take(tab, jnp.asarray(_t5_bucket_np(d_last)), axis=0).T, 2, axis=0)
    bself = jnp.repeat(tab[0].reshape(N_HEADS, 1), 2, axis=0)

    def k_spec(g):
        return pl.BlockSpec((None, None, GROUP_W, page),
                            lambda b_, t_, pt: (layer, pt[b_, jnp.minimum(t_, n_steps - 1) * pg + g], 0, 0))

    def v_spec(g):
        return pl.BlockSpec((None, None, GROUP_W, page),
                            lambda b_, t_, pt: (layer, pt[b_, jnp.maximum(t_ - n_steps, 0) * pg + g], 0, 0))

    row = pl.BlockSpec((None, 1, GROUP_W), lambda b_, t_, pt: (b_, 0, 0))
    grid_spec = pltpu.PrefetchScalarGridSpec(
        num_scalar_prefetch=1,
        grid=(n_b, 2 * n_steps),
        in_specs=[pl.BlockSpec(memory_space=pltpu.SMEM), row, row, row,
                  pl.BlockSpec((2 * N_HEADS, page), lambda b_, t_, pt: (0, 0)),
                  pl.BlockSpec((2 * N_HEADS, 1), lambda b_, t_, pt: (0, 0))]
                 + [k_spec(g) for g in range(pg)] + [v_spec(g) for g in range(pg)],
        out_specs=row,
        scratch_shapes=[
            pltpu.VMEM((2 * N_HEADS, GROUP_W), f32),
            pltpu.VMEM((n_steps, 2 * N_HEADS, pg * page), f32),
            pltpu.VMEM((2 * N_HEADS, 1), f32),
            pltpu.VMEM((2 * N_HEADS, GROUP_W), f32),
        ],
    )
    r3 = lambda z: z.reshape(n_b, 1, GROUP_W)
    out = pl.pallas_call(
        functools.partial(_attn_decode_kernel, pg=pg, n_pages=n_pages),
        grid_spec=grid_spec,
        out_shape=jax.ShapeDtypeStruct((n_b, 1, GROUP_W), f32),
        compiler_params=_cparams("arbitrary", "arbitrary"),
        name="attn_decode",
    )(page_table, lam.reshape(1), r3(qn), r3(kn), r3(vn), blast, bself,
      *([cache_k] * pg), *([cache_v] * pg))
    return out.reshape(n_b, GROUP_W)


def _tri(n, dtype=f32):
    return (_iota((n, n), 0) >= _iota((n, n), 1)).astype(dtype)


def _block_tri_t(n, blk):
    r = _iota((n, n), 0)
    c = _iota((n, n), 1)
    return (((r // blk) == (c // blk)) & (r <= c)).astype(bf16)


def _block_tri(n, blk):
    r = _iota((n, n), 0)
    c = _iota((n, n), 1)
    return (((r // blk) == (c // blk)) & (r >= c)).astype(bf16)


def _gdn_kernel(u_ref, gc_ref, gr_ref, cw_ref, alr_ref, dtr_ref, alc_ref, dtc_ref, o_ref, s_out_ref, ext_scr, s_scr, *, tb):
    i = pl.program_id(1)

    @pl.when(i == 0)
    def _():
        ext_scr[0:8, :] = jnp.zeros((8, 3 * GROUP_W), f32)
        s_scr[...] = jnp.zeros(s_scr.shape, f32)

    ext_scr[8:8 + tb, :] = u_ref[...]
    w = cw_ref[...]
    conv = ext_scr[8:8 + tb, :] * w[3:4]
    for jj in range(1, CONV_W):
        conv = conv + ext_scr[8 - jj:8 - jj + tb, :] * w[3 - jj:4 - jj]
    ext_scr[0:8, :] = ext_scr[tb:tb + 8, :]
    qkv = _silu(conv)

    def l2n(x):
        return x * lax.rsqrt(_group_sum(x * x, HEAD_W) + EPS)

    q = l2n(qkv[:, 0:GROUP_W]) * (HEAD_W ** -0.5)
    k = l2n(qkv[:, GROUP_W:2 * GROUP_W])
    v = qkv[:, 2 * GROUP_W:3 * GROUP_W]

    gc = gc_ref[...]
    g_col = -jnp.exp(alr_ref[...]) * _softplus(gc[:, 0:4] + dtr_ref[...])
    beta_col = jax.nn.sigmoid(gc[:, 4:8])
    gr = gr_ref[...]
    g_row = -jnp.exp(alc_ref[...]) * _softplus(gr[0:4, :] + dtc_ref[...])
    g_row8 = jnp.concatenate([g_row, jnp.zeros_like(g_row)], axis=0)
    gcum_row = _mm2(g_row8, _block_tri_t(tb, CHUNK))

    same, lower, strict = _bd_masks()
    tri = _tri(CHUNK, bf16)
    r = _iota((GROUP_W, GROUP_W), 0)
    c = _iota((GROUP_W, GROUP_W), 1)
    eye = (r == c).astype(f32)

    chunks = range(tb // CHUNK)
    gcums, qks, m_bds = [], [], []
    for ch in chunks:
        lo = ch * CHUNK
        gcum = _mm2l(tri, g_col[lo:lo + CHUNK])
        g_stack = _stack_cols(gcum)
        g_cat = _cat_rows(gcum_row, lo)
        decay = jnp.exp(jnp.where(lower, g_stack - g_cat, NEG))
        ksm = _head_stack(k[lo:lo + CHUNK])
        kk = _mm1(ksm, ksm, NT)
        qks.append(_mm1(_head_stack(q[lo:lo + CHUNK]), ksm, NT) * decay)
        m_bds.append(_stack_cols(beta_col[lo:lo + CHUNK]) * kk * jnp.where(strict, decay, 0.0))
        gcums.append(gcum)

    def sibling(lev):
        return ((r >> (lev + 1)) == (c >> (lev + 1))) & (((r >> lev) & 1) == 1) & (((c >> lev) & 1) == 0)

    xs = [eye - jnp.where(sibling(0), m, 0.0) for m in m_bds]
    for lev in range(1, 6):
        sel = sibling(lev)
        xs = [x - _mm3(_mm3(x, jnp.where(sel, m, 0.0)), x) for x, m in zip(xs, m_bds)]

    for ch in chunks:
        lo = ch * CHUNK
        qc, kc, vc = q[lo:lo + CHUNK], k[lo:lo + CHUNK], v[lo:lo + CHUNK]
        gcum, bcol = gcums[ch], beta_col[lo:lo + CHUNK]
        s_bd = s_scr[...]
        kq_s = _mm1(jnp.concatenate([kc, qc], axis=0), s_bd)
        ks, qs = kq_s[0:CHUNK], kq_s[CHUNK:2 * CHUNK]
        eg_all = _expand_cols(jnp.exp(gcum))
        rhs = _expand_cols(bcol) * (vc - eg_all * ks)
        u_sm = _mm3(xs[ch], _head_stack(rhs))
        o_sm = _mm1(qks[ch], u_sm)
        o_ref[lo:lo + CHUNK, :] = eg_all * qs + _fold_heads(o_sm)
        u_all = _fold_heads(u_sm)
        g_last = gcum[CHUNK - 1:CHUNK, :]
        kw = kc * _expand_cols(jnp.exp(g_last - gcum))
        d_stack = jnp.concatenate(
            [jnp.broadcast_to(jnp.exp(g_last[:, h:h + 1]), (HEAD_W, 1)) for h in range(N_HEADS)], axis=0)
        s_scr[...] = d_stack * s_bd + jnp.where(same, _mm1(kw, u_all, TN), 0.0)

    @pl.when(i == pl.num_programs(1) - 1)
    def _():
        s_out_ref[...] = s_scr[...]


def _gdn_prompt(p, gt, conv_w, a_log, dt_bias, n_batch, seq):
    tb = min(256, seq)
    nb = seq // tb
    r14 = lambda z: z.astype(f32).reshape(1, N_HEADS)
    c41 = lambda z: z.astype(f32).reshape(N_HEADS, 1)
    o, s_bd = pl.pallas_call(
        functools.partial(_gdn_kernel, tb=tb),
        grid=(n_batch, nb),
        in_specs=[
            pl.BlockSpec((tb, 3 * GROUP_W), lambda b, i: (b * nb + i, 0)),
            pl.BlockSpec((tb, 128), lambda b, i: (b * nb + i, GATE_COL // 128)),
            pl.BlockSpec((16, tb), lambda b, i: (0, b * nb + i)),
            pl.BlockSpec((CONV_W, 3 * GROUP_W), lambda b, i: (0, 0)),
            pl.BlockSpec((1, N_HEADS), lambda b, i: (0, 0)),
            pl.BlockSpec((1, N_HEADS), lambda b, i: (0, 0)),
            pl.BlockSpec((N_HEADS, 1), lambda b, i: (0, 0)),
            pl.BlockSpec((N_HEADS, 1), lambda b, i: (0, 0)),
        ],
        out_specs=[
            pl.BlockSpec((tb, GROUP_W), lambda b, i: (b * nb + i, 0)),
            pl.BlockSpec((None, GROUP_W, GROUP_W), lambda b, i: (b, 0, 0)),
        ],
        out_shape=[jax.ShapeDtypeStruct((n_batch * seq, GROUP_W), f32),
                   jax.ShapeDtypeStruct((n_batch, GROUP_W, GROUP_W), f32)],
        scratch_shapes=[pltpu.VMEM((tb + 8, 3 * GROUP_W), f32), pltpu.VMEM((GROUP_W, GROUP_W), f32)],
        compiler_params=_cparams("arbitrary", "arbitrary"),
        name="gdn_prompt",
    )(p, p, gt, conv_w.astype(f32), r14(a_log), r14(dt_bias), c41(a_log), c41(dt_bias))
    return o, _bd_diag(s_bd)


def _bd_diag(s_bd):
    n_b = s_bd.shape[0]
    s5 = s_bd.reshape(n_b, N_HEADS, HEAD_W, N_HEADS, HEAD_W)
    return jnp.stack([s5[:, h, :, h, :] for h in range(N_HEADS)], axis=1)


def _hgrn_kernel(q_ref, f_ref, i_ref, lb_ref, o_ref, s_out_ref, st_scr, q_scr, k_scr, b_scr, *, tb):
    blk = pl.program_id(1)

    @pl.when(blk == 0)
    def _():
        st_scr[...] = jnp.zeros(st_scr.shape, f32)

    lb = lb_ref[...]
    z = f_ref[...]
    logf = jnp.log(lb + (1.0 - lb) * jax.nn.sigmoid(z))
    q_scr[...] = _silu(q_ref[...])
    k_scr[...] = (1.0 - lb) * jax.nn.sigmoid(-z)
    b_scr[...] = _mm2l(_block_tri(tb, SUB), logf)

    same, _, _ = _bd_masks()
    ones_bd = _group_ones(GROUP_W, HEAD_W)
    row = _iota((SUB * SUB, GROUP_W), 0)
    tmask = (row % SUB) >= (row // SUB)

    def rep_t(x):
        return jnp.broadcast_to(x[None], (SUB, SUB, GROUP_W)).reshape(SUB * SUB, GROUP_W)

    def rep_j(x):
        return jnp.broadcast_to(x[:, None, :], (SUB, SUB, GROUP_W)).reshape(SUB * SUB, GROUP_W)

    def body(c, carry):
        r0 = pl.multiple_of(c * SUB, SUB)
        qs = q_scr[pl.ds(r0, SUB), :]
        ks = k_scr[pl.ds(r0, SUB), :]
        vs = i_ref[pl.ds(r0, SUB), :]
        bs = b_scr[pl.ds(r0, SUB), :]
        st = st_scr[...]
        o_inter = _mm1(qs * jnp.exp(bs), st, NT)
        wgt = rep_t(qs) * jnp.exp(jnp.where(tmask, rep_t(bs) - rep_j(bs), NEG)) * rep_j(ks)
        a = _mm2(wgt, ones_bd)
        o_diag = jnp.sum((a * rep_j(vs)).reshape(SUB, SUB, GROUP_W), axis=0)
        o_ref[pl.ds(r0, SUB), :] = o_inter + o_diag
        b_last = bs[SUB - 1:SUB, :]
        kw = ks * jnp.exp(b_last - bs)
        st_scr[...] = st * jnp.exp(b_last) + jnp.where(same, _mm1(vs, kw, TN), 0.0)
        return carry

    lax.fori_loop(0, tb // SUB, body, 0)

    @pl.when(blk == pl.num_programs(1) - 1)
    def _():
        s_out_ref[...] = st_scr[...]


def _hgrn_prompt(p, lb, n_batch, seq):
    tb = min(256, seq)
    nb = seq // tb
    blk = lambda col: pl.BlockSpec((tb, GROUP_W), lambda b, i: (b * nb + i, col))
    o, st = pl.pallas_call(
        functools.partial(_hgrn_kernel, tb=tb),
        grid=(n_batch, nb),
        in_specs=[blk(7), blk(8), blk(9), pl.BlockSpec((1, GROUP_W), lambda b, i: (0, 0))],
        out_specs=[
            pl.BlockSpec((tb, GROUP_W), lambda b, i: (b * nb + i, 0)),
            pl.BlockSpec((None, GROUP_W, GROUP_W), lambda b, i: (b, 0, 0)),
        ],
        out_shape=[jax.ShapeDtypeStruct((n_batch * seq, GROUP_W), f32),
                   jax.ShapeDtypeStruct((n_batch, GROUP_W, GROUP_W), f32)],
        scratch_shapes=[pltpu.VMEM((GROUP_W, GROUP_W), f32)] + [pltpu.VMEM((tb, GROUP_W), f32)] * 3,
        compiler_params=_cparams("arbitrary", "arbitrary"),
        name="hgrn_prompt",
    )(p, p, p, lb.astype(f32).reshape(1, GROUP_W))
    return o, jnp.swapaxes(_bd_diag(st), -1, -2)


def _log_sigmoid(x):
    return jnp.minimum(x, 0.0) - jnp.log1p(jnp.exp(-jnp.abs(x)))


def _mlstm_kernel(q_ref, k_ref, v_ref, gc_ref, gr_ref, ibr_ref, fbr_ref, ibc_ref, fbc_ref,
                  o_ref, c_out_ref, n_out_ref, m_out_ref, c_scr, n_scr, m_scr, *, tb):
    blk = pl.program_id(1)

    @pl.when(blk == 0)
    def _():
        c_scr[...] = jnp.zeros(c_scr.shape, f32)
        n_scr[...] = jnp.zeros(n_scr.shape, f32)
        m_scr[...] = jnp.zeros(m_scr.shape, f32)

    q = q_ref[...]
    k = k_ref[...] * (HEAD_W ** -0.5)
    v = v_ref[...]
    gc = gc_ref[...]
    li_col = gc[:, 8:12] + ibr_ref[...]
    lf_col = _log_sigmoid(gc[:, 12:16] + fbr_ref[...])
    gr = gr_ref[...]
    li_row = gr[8:12, :] + ibc_ref[...]
    lf_row = _log_sigmoid(gr[12:16, :] + fbc_ref[...])
    b_row = _mm2(jnp.concatenate([lf_row, jnp.zeros_like(lf_row)], axis=0), _block_tri_t(tb, CHUNK))

    same, lower, _ = _bd_masks()
    tri = _tri(CHUNK, bf16)

    for ch in range(tb // CHUNK):
        lo = ch * CHUNK
        qc, kc, vc = q[lo:lo + CHUNK], k[lo:lo + CHUNK], v[lo:lo + CHUNK]
        b_col = _mm2l(tri, lf_col[lo:lo + CHUNK])
        b_stack = _stack_cols(b_col)
        d_mat = jnp.where(lower, b_stack - _cat_rows(b_row, lo) + _cat_rows(li_row, lo), NEG)
        m_row = m_scr[...]
        m_stack = jnp.concatenate(
            [jnp.broadcast_to(m_row[:, h:h + 1], (CHUNK, 1)) for h in range(N_HEADS)], axis=0)
        inter = b_stack + m_stack
        m_t = jnp.maximum(inter, jnp.max(d_mat, axis=-1, keepdims=True))
        w_inter = jnp.exp(inter - m_t)
        qsm = _head_stack(qc)
        ksm = _head_stack(kc)
        pmat = _mm1(qsm, ksm, NT) * jnp.exp(d_mat - m_t)
        c_bd = c_scr[...]
        n_row = n_scr[...]
        num = w_inter * _mm1(qsm, c_bd) + _mm1(pmat, _head_stack(vc))
        den = w_inter * jnp.sum(qsm * n_row, axis=-1, keepdims=True) + jnp.sum(pmat, axis=-1, keepdims=True)
        h_sm = num / jnp.maximum(jnp.abs(den), jnp.exp(-m_t))
        o_ref[lo:lo + CHUNK, :] = _fold_heads(h_sm)
        m_new = jnp.concatenate(
            [m_t[h * CHUNK + CHUNK - 1:h * CHUNK + CHUNK, :] for h in range(N_HEADS)], axis=1)
        b_last = b_col[CHUNK - 1:CHUNK, :]
        w_end = jnp.exp(b_last - b_col + li_col[lo:lo + CHUNK] - m_new)
        d0 = jnp.exp(b_last + m_row - m_new)
        kw = kc * _expand_cols(w_end)
        d0_stack = jnp.concatenate(
            [jnp.broadcast_to(d0[:, h:h + 1], (HEAD_W, 1)) for h in range(N_HEADS)], axis=0)
        c_scr[...] = d0_stack * c_bd + jnp.where(same, _mm1(kw, vc, TN), 0.0)
        n_scr[...] = _expand_cols(d0) * n_row + jnp.sum(kw, axis=0, keepdims=True)
        m_scr[...] = m_new

    @pl.when(blk == pl.num_programs(1) - 1)
    def _():
        c_out_ref[...] = c_scr[...]
        n_out_ref[...] = n_scr[...]
        m_out_ref[...] = m_scr[...]


def _mlstm_prompt(p, gt, i_bias, f_bias, n_batch, seq):
    tb = min(256, seq)
    nb = seq // tb
    blk = lambda col: pl.BlockSpec((tb, GROUP_W), lambda b, i: (b * nb + i, col))
    r14 = lambda z: z.astype(f32).reshape(1, N_HEADS)
    c41 = lambda z: z.astype(f32).reshape(N_HEADS, 1)
    small = lambda shape: pl.BlockSpec(shape, lambda b, i: (0, 0))
    o, c_bd, n_row, m_row = pl.pallas_call(
        functools.partial(_mlstm_kernel, tb=tb),
        grid=(n_batch, nb),
        in_specs=[blk(11), blk(12), blk(13),
                  pl.BlockSpec((tb, 128), lambda b, i: (b * nb + i, GATE_COL // 128)),
                  pl.BlockSpec((16, tb), lambda b, i: (0, b * nb + i)),
                  small((1, N_HEADS)), small((1, N_HEADS)), small((N_HEADS, 1)), small((N_HEADS, 1))],
        out_specs=[
            pl.BlockSpec((tb, GROUP_W), lambda b, i: (b * nb + i, 0)),
            pl.BlockSpec((None, GROUP_W, GROUP_W), lambda b, i: (b, 0, 0)),
            pl.BlockSpec((None, 1, GROUP_W), lambda b, i: (b, 0, 0)),
            pl.BlockSpec((None, 1, N_HEADS), lambda b, i: (b, 0, 0)),
        ],
        out_shape=[jax.ShapeDtypeStruct((n_batch * seq, GROUP_W), f32),
                   jax.ShapeDtypeStruct((n_batch, GROUP_W, GROUP_W), f32),
                   jax.ShapeDtypeStruct((n_batch, 1, GROUP_W), f32),
                   jax.ShapeDtypeStruct((n_batch, 1, N_HEADS), f32)],
        scratch_shapes=[pltpu.VMEM((GROUP_W, GROUP_W), f32), pltpu.VMEM((1, GROUP_W), f32),
                        pltpu.VMEM((1, N_HEADS), f32)],
        compiler_params=_cparams("arbitrary", "arbitrary"),
        name="mlstm_prompt",
    )(p, p, p, p, gt, r14(i_bias), r14(f_bias), c41(i_bias), c41(f_bias))
    return (o, _bd_diag(c_bd), n_row.reshape(n_batch, N_HEADS, HEAD_W), m_row.reshape(n_batch, N_HEADS))


def _gdn_dec_prep_kernel(u_ref, buf_ref, cw_ref, q_ref, k_ref, v_ref):
    w = cw_ref[...]
    conv = u_ref[...] * w[3:4]
    for jj in range(CONV_W - 1):
        conv = conv + buf_ref[jj] * w[jj:jj + 1]
    qkv = _silu(conv)

    def l2n(x):
        return x * lax.rsqrt(_group_sum(x * x, HEAD_W) + EPS)

    q_ref[...] = l2n(qkv[:, 0:GROUP_W]) * (HEAD_W ** -0.5)
    k_ref[...] = l2n(qkv[:, GROUP_W:2 * GROUP_W])
    v_ref[...] = qkv[:, 2 * GROUP_W:3 * GROUP_W]


def _gdn_dec_prep(p, conv_buf, conv_w):
    n_b = p.shape[0]
    out = jax.ShapeDtypeStruct((n_b, GROUP_W), f32)
    return pl.pallas_call(
        _gdn_dec_prep_kernel,
        grid=(1,),
        in_specs=[pl.BlockSpec((n_b, 3 * GROUP_W), lambda i: (0, 0)),
                  pl.BlockSpec((CONV_W - 1, n_b, 3 * GROUP_W), lambda i: (0, 0, 0)),
                  pl.BlockSpec((CONV_W, 3 * GROUP_W), lambda i: (0, 0))],
        out_specs=[pl.BlockSpec((n_b, GROUP_W), lambda i: (0, 0))] * 3,
        out_shape=[out, out, out],
        compiler_params=_cparams("arbitrary"),
        name="gdn_dec_prep",
    )(p, jnp.swapaxes(conv_buf.astype(f32), 0, 1), conv_w.astype(f32))


def _rec_decode_kernel(gq_ref, gk_ref, gv_ref, ga_ref, gb_ref, al_ref, dtb_ref, sg_ref,
                       cq_ref, cf_ref, ci_ref, lbc_ref, sh_ref,
                       dq_ref, dk_ref, dv_ref, di_ref, df_ref, ib_ref, fb_ref, sc_ref, sn_ref, sm_ref,
                       oa_ref, sg_out, oc_ref, sh_out, od_ref, sc_out, sn_out, sm_out):
    q, k, v = gq_ref[...], gk_ref[...], gv_ref[...]
    s = sg_ref[...]
    g = -jnp.exp(al_ref[...]) * _softplus(ga_ref[...] + dtb_ref[...])
    eg = jnp.exp(g)
    beta = jax.nn.sigmoid(gb_ref[...])
    ks = jnp.sum(k * s, axis=1, keepdims=True)
    qs = jnp.sum(q * s, axis=1, keepdims=True)
    u = beta * (v - eg * ks)
    qk = jnp.sum(q * k, axis=1, keepdims=True)
    oa_ref[...] = eg * qs + qk * u
    sg_out[...] = eg * s + k * u

    lb = lbc_ref[...]
    z = cf_ref[...]
    logf = jnp.log(lb + (1.0 - lb) * jax.nn.sigmoid(z))
    kc = (1.0 - lb) * jax.nn.sigmoid(-z)
    qc = _silu(cq_ref[...])
    vc = ci_ref[...]
    sh = sh_ref[...]
    ef = jnp.exp(logf)
    oc_ref[...] = jnp.sum((qc * ef) * sh, axis=1, keepdims=True) + jnp.sum(qc * kc, axis=1, keepdims=True) * vc
    sh_out[...] = ef * sh + kc * vc

    qd = dq_ref[...]
    kd = dk_ref[...] * (HEAD_W ** -0.5)
    vd = dv_ref[...]
    li = di_ref[...] + ib_ref[...]
    lf = _log_sigmoid(df_ref[...] + fb_ref[...])
    m0 = sm_ref[...]
    cs = sc_ref[...]
    n0 = sn_ref[...]
    inter = lf + m0
    m_t = jnp.maximum(inter, li)
    w_inter = jnp.exp(inter - m_t)
    qkd = jnp.sum(qd * kd, axis=1, keepdims=True) * jnp.exp(li - m_t)
    num = w_inter * jnp.sum(qd * cs, axis=1, keepdims=True) + qkd * vd
    den = w_inter * jnp.sum(qd * n0, axis=1, keepdims=True) + qkd
    od_ref[...] = num / jnp.maximum(jnp.abs(den), jnp.exp(-m_t))
    w_end = jnp.exp(li - m_t)
    d0 = jnp.exp(lf + m0 - m_t)
    sc_out[...] = d0 * cs + (w_end * kd) * vd
    sn_out[...] = d0 * n0 + w_end * kd
    sm_out[...] = m_t


def _rec_decode(p, gq, gk, gv, a_log, dt_bias, lb, i_bias, f_bias, s_gdn, s_hgrn, s_c, s_n, s_m):
    n_b = p.shape[0]
    rows = n_b * N_HEADS
    rb = min(16, rows)
    col = lambda z: z.reshape(rows, HEAD_W, 1)
    vrow = lambda z: z.reshape(rows, 1, HEAD_W)
    sca = lambda z: z.reshape(rows, 1, 1)
    per_head = lambda z: jnp.tile(z.astype(f32), n_b).reshape(rows, 1, 1)
    blockp = lambda b: p[:, b * GROUP_W:(b + 1) * GROUP_W]
    gates = p[:, GATE_COL:GATE_COL + 16]
    lb_col = jnp.tile(lb.astype(f32).reshape(N_HEADS, HEAD_W), (n_b, 1)).reshape(rows, HEAD_W, 1)
    st = lambda z: z.astype(f32).reshape(rows, HEAD_W, HEAD_W)
    args = [col(gq), col(gk), vrow(gv), sca(gates[:, 0:4]), sca(gates[:, 4:8]), per_head(a_log), per_head(dt_bias), st(s_gdn),
            col(blockp(7)), col(blockp(8)), vrow(blockp(9)), lb_col, st(s_hgrn),
            col(blockp(11)), col(blockp(12)), vrow(blockp(13)), sca(gates[:, 8:12]), sca(gates[:, 12:16]),
            per_head(i_bias), per_head(f_bias), st(s_c), col(s_n.astype(f32)), sca(s_m.astype(f32))]

    def spec(a):
        return pl.BlockSpec((rb,) + a.shape[1:], lambda i: (i, 0, 0))

    o_vrow = jax.ShapeDtypeStruct((rows, 1, HEAD_W), f32)
    o_st = jax.ShapeDtypeStruct((rows, HEAD_W, HEAD_W), f32)
    o_col = jax.ShapeDtypeStruct((rows, HEAD_W, 1), f32)
    o_sca = jax.ShapeDtypeStruct((rows, 1, 1), f32)
    outs = [o_vrow, o_st, o_vrow, o_st, o_vrow, o_st, o_col, o_sca]
    res = pl.pallas_call(
        _rec_decode_kernel,
        grid=(rows // rb,),
        in_specs=[spec(a) for a in args],
        out_specs=[spec(a) for a in outs],
        out_shape=outs,
        compiler_params=_cparams("arbitrary"),
        name="rec_decode",
    )(*args)
    oa, sg, oc, sh, od, sc, sn, sm = res
    s4 = lambda z: z.reshape(n_b, N_HEADS, HEAD_W, HEAD_W)
    o2 = lambda z: z.reshape(n_b, GROUP_W)
    return (o2(oa), s4(sg), o2(oc), s4(sh), o2(od), s4(sc),
            sn.reshape(n_b, N_HEADS, HEAD_W), sm.reshape(n_b, N_HEADS))


def _permute_w_in(w):
    d_in = w.shape[1]
    a_gate0 = 3 * GROUP_W
    d_gate0 = d_in - GROUP_W - 8
    main = jnp.concatenate([w[:, 0:a_gate0], w[:, a_gate0 + 8:d_gate0], w[:, d_gate0 + 8:]], axis=1)
    gates = jnp.concatenate([w[:, a_gate0:a_gate0 + 8], w[:, d_gate0:d_gate0 + 8]], axis=1)
    pad = jnp.zeros((w.shape[0], P_COLS - main.shape[1] - 16), w.dtype)
    return jnp.concatenate([main, gates, pad], axis=1).astype(bf16), gates.T.astype(bf16)


def kernel(x_prompt, x_sample, page_table, cache_k, cache_v, state_gdn_conv, state_gdn, state_hgrn, state_mlstm_C, state_mlstm_n, state_mlstm_m, attn_norm_g, w_in, gdn_conv_w, gdn_a_log, gdn_dt_bias, gdn_norm_g, diff_qk_norm_g, diff_lambda, diff_subln_g, rel_bias, hgrn_lb_logits, hgrn_norm_g, mlstm_i_bias, mlstm_f_bias, mlstm_norm_g, w_out, ffn_norm_g, ffn_w_gate, ffn_w_up, ffn_w_down, moe_router, moe_w_gate, moe_w_up, moe_w_down):
    depth = w_in.shape[0]
    n_bp, seq, _ = x_prompt.shape
    n_bs = x_sample.shape[0]
    n_pool, page = cache_k.shape[1], cache_k.shape[2]
    dt = x_prompt.dtype

    lb_p = jax.nn.softmax(hgrn_lb_logits.astype(f32), axis=0)
    lb_cum = jnp.cumsum(lb_p, axis=0)
    hgrn_lb = lb_cum - lb_cum[0:1]
    cache_k4 = jnp.transpose(cache_k, (0, 1, 3, 4, 5, 2)).reshape(depth, n_pool, GROUP_W, page)
    cache_v4 = jnp.transpose(cache_v, (0, 1, 3, 4, 2)).reshape(depth, n_pool, GROUP_W, page)

    xp = x_prompt.reshape(n_bp * seq, D_MODEL)
    xs = x_sample.reshape(n_bs, D_MODEL)
    outs_p, outs_s = [], []
    for l in range(depth):
        w_perm, w_gate_t = _permute_w_in(w_in[l])
        w_out_b = w_out[l].astype(bf16)
        gains = jnp.stack([jnp.tile(g.astype(f32), N_HEADS) for g in
                           (gdn_norm_g[l], diff_subln_g[l], hgrn_norm_g[l], mlstm_norm_g[l])])
        lam_init = 0.8 - 0.6 * math.exp(-0.3 * l)
        lam32 = diff_lambda[l].astype(f32)
        lam = jnp.exp(jnp.sum(lam32[0] * lam32[1])) - jnp.exp(jnp.sum(lam32[2] * lam32[3])) + lam_init
        if l % 2 == 0:
            ffn_w = (ffn_w_gate[l // 2].astype(bf16), ffn_w_up[l // 2].astype(bf16), ffn_w_down[l // 2].astype(bf16))
        else:
            router_pad = jnp.pad(moe_router[l // 2].astype(bf16), ((0, 0), (0, 128 - N_EXPERTS)))
            moe_w = (moe_w_gate[l // 2].astype(bf16), moe_w_up[l // 2].astype(bf16), moe_w_down[l // 2].astype(bf16))

        def channel_mix(x):
            if l % 2 == 0:
                return _ffn(x, ffn_norm_g[l], *ffn_w)
            return _moe(x, ffn_norm_g[l], router_pad, *moe_w)

        p, gt = _inproj(xp, attn_norm_g[l], w_perm, w_gate_t)
        qnt, kn, vt = _bprep(p, diff_qk_norm_g[l], True)
        ob = _attn_prompt(qnt, kn, vt, lam, rel_bias, n_bp, seq)
        oa, s_gdn = _gdn_prompt(p, gt, gdn_conv_w[l], gdn_a_log[l], gdn_dt_bias[l], n_bp, seq)
        oc, s_hgrn = _hgrn_prompt(p, hgrn_lb[l], n_bp, seq)
        od, s_c, s_n, s_m = _mlstm_prompt(p, gt, mlstm_i_bias[l], mlstm_f_bias[l], n_bp, seq)
        xp = _outproj(oa, ob, oc, od, p, xp, gains, w_out_b, 1.0 - lam_init)
        xp = channel_mix(xp)
        p3 = p.reshape(n_bp, seq, P_COLS)
        outs_p.append((
            kn.reshape(n_bp, seq, N_HEADS, 2, DKB).astype(dt),
            p3[:, :, 6 * GROUP_W:7 * GROUP_W].reshape(n_bp, seq, N_HEADS, HEAD_W).astype(dt),
            p3[:, seq - (CONV_W - 1):, 0:3 * GROUP_W].astype(dt),
            s_gdn.astype(dt), s_hgrn.astype(dt), s_c.astype(dt), s_n.astype(dt), s_m.astype(dt)))

        p, gt = _inproj(xs, attn_norm_g[l], w_perm, w_gate_t)
        qn, kn = _bprep(p, diff_qk_norm_g[l], False)
        vn = p[:, 6 * GROUP_W:7 * GROUP_W]
        ob = _attn_decode(qn, kn, vn, page_table, cache_k4, cache_v4, l, lam, rel_bias)
        u = p[:, 0:3 * GROUP_W]
        gq, gk, gv = _gdn_dec_prep(u, state_gdn_conv[l], gdn_conv_w[l])
        oa, s_gdn, oc, s_hgrn, od, s_c, s_n, s_m = _rec_decode(
            p, gq, gk, gv, gdn_a_log[l], gdn_dt_bias[l], hgrn_lb[l], mlstm_i_bias[l], mlstm_f_bias[l],
            state_gdn[l], state_hgrn[l], state_mlstm_C[l], state_mlstm_n[l], state_mlstm_m[l])
        xs = _outproj(oa, ob, oc, od, p, xs, gains, w_out_b, 1.0 - lam_init)
        xs = channel_mix(xs)
        conv_new = jnp.concatenate([state_gdn_conv[l][:, 1:].astype(dt), u[:, None, :].astype(dt)], axis=1)
        outs_s.append((
            kn.reshape(n_bs, 1, N_HEADS, 2, DKB).astype(dt),
            vn.reshape(n_bs, 1, N_HEADS, HEAD_W).astype(dt),
            conv_new, s_gdn.astype(dt), s_hgrn.astype(dt), s_c.astype(dt), s_n.astype(dt), s_m.astype(dt)))

    kp, vp, convp, gdnp, hgrnp, mcp, mnp_, mmp = [jnp.stack(z) for z in zip(*outs_p)]
    ks_, vs_, convs, gdns, hgrns, mcs, mns, mms = [jnp.stack(z) for z in zip(*outs_s)]
    return (xp.reshape(n_bp, seq, D_MODEL), xs.reshape(n_bs, 1, D_MODEL), kp, vp, ks_, vs_, convp, convs,
            gdnp, gdns, hgrnp, hgrns, mcp, mcs, mnp_, mns, mmp, mms)
```

```python
import functools
import math

import numpy as np
import jax
import jax.numpy as jnp
from jax import lax
from jax.experimental import pallas as pl
from jax.experimental.pallas import tpu as pltpu

f32 = jnp.float32
bf16 = jnp.bfloat16

D_MODEL = 1024
N_HEADS = 4
HEAD_W = 64
GROUP_W = N_HEADS * HEAD_W
DKB = 32
CONV_W = 4
CHUNK = 64
SUB = 16
NUM_BUCKETS = 32
MAX_DISTANCE = 128
N_EXPERTS = 8
EPS = 1e-6
NEG = -1e30
P_COLS = 4096
GATE_COL = 3840
VMEM_LIMIT = 56 * 1024 * 1024

NN = ((1,), (0,))
NT = ((1,), (1,))
TN = ((0,), (0,))


def _dg(a, b, dims=NN):
    return lax.dot_general(a, b, (dims, ((), ())), preferred_element_type=f32)


def _split(a):
    hi = a.astype(bf16)
    lo = (a - hi.astype(f32)).astype(bf16)
    return hi, lo


def _mm3(a, b, dims=NN):
    ah, al = _split(a)
    bh, bl = _split(b)
    return _dg(ah, bh, dims) + (_dg(ah, bl, dims) + _dg(al, bh, dims))


def _mm2(a, b01, dims=NN):
    ah, al = _split(a)
    return _dg(ah, b01, dims) + _dg(al, b01, dims)


def _mm2l(a01, b, dims=NN):
    bh, bl = _split(b)
    return _dg(a01, bh, dims) + _dg(a01, bl, dims)


def _mm1(a, b, dims=NN):
    return _dg(a.astype(bf16), b.astype(bf16), dims)


def _iota(shape, dim):
    return lax.broadcasted_iota(jnp.int32, shape, dim)


def _group_ones(width, group):
    r = _iota((width, width), 0) // group
    c = _iota((width, width), 1) // group
    return (r == c).astype(bf16)


def _group_sum(x, group):
    ones = _group_ones(x.shape[-1], group)
    hi = x.astype(bf16)
    r1 = x - hi.astype(f32)
    mid = r1.astype(bf16)
    lo = (r1 - mid.astype(f32)).astype(bf16)
    return _dg(hi, ones) + (_dg(mid, ones) + _dg(lo, ones))


def _recip(x):
    r = 1.0 / x
    return r * (2.0 - x * r)


def _silu(x):
    return x * jax.nn.sigmoid(x)


def _softplus(x):
    return jnp.maximum(x, 0.0) + jnp.log1p(jnp.exp(-jnp.abs(x)))


def _stack_cols(xc, n=N_HEADS, rows=HEAD_W):
    return jnp.concatenate([xc[:, h:h + 1] for h in range(n)], axis=0)


def _expand_cols(xc, n=N_HEADS, width=HEAD_W):
    r = xc.shape[0]
    return jnp.concatenate([jnp.broadcast_to(xc[:, h:h + 1], (r, width)) for h in range(n)], axis=1)


def _cat_rows(xr, lo, n=N_HEADS, width=HEAD_W):
    return jnp.concatenate([xr[h:h + 1, lo:lo + width] for h in range(n)], axis=1)


def _head_stack(x, n=N_HEADS, width=HEAD_W):
    lane_head = _iota(x.shape, 1) // width
    return jnp.concatenate([jnp.where(lane_head == h, x, 0.0) for h in range(n)], axis=0)


def _fold_heads(x_sm, n=N_HEADS):
    r = x_sm.shape[0] // n
    out = x_sm[0:r]
    for h in range(1, n):
        out = out + x_sm[h * r:(h + 1) * r]
    return out


def _bd_masks(n=GROUP_W, blk=CHUNK):
    r = _iota((n, n), 0)
    c = _iota((n, n), 1)
    same = (r // blk) == (c // blk)
    lower = same & ((r % blk) >= (c % blk))
    strict = same & ((r % blk) > (c % blk))
    return same, lower, strict


def _cparams(*sem):
    return pltpu.CompilerParams(dimension_semantics=sem, vmem_limit_bytes=VMEM_LIMIT)


def _inproj_kernel(x_ref, g_ref, w_ref, wgt_ref, p_ref, gt_ref, h_scr):
    @pl.when(pl.program_id(1) == 0)
    def _():
        x = x_ref[...]
        ms = jnp.mean(x * x, axis=-1, keepdims=True)
        h = ((x * lax.rsqrt(ms + EPS)) * g_ref[...]).astype(bf16)
        h_scr[...] = h
        gt_ref[...] = _dg(wgt_ref[...], h, NT)
    p_ref[...] = _dg(h_scr[...], w_ref[...], NN)


def _inproj(x, g, w_perm, w_gate_t):
    t = x.shape[0]
    tm = min(512, t)
    tn = 1024
    return pl.pallas_call(
        _inproj_kernel,
        grid=(t // tm, P_COLS // tn),
        in_specs=[
            pl.BlockSpec((tm, D_MODEL), lambda i, j: (i, 0)),
            pl.BlockSpec((1, D_MODEL), lambda i, j: (0, 0)),
            pl.BlockSpec((D_MODEL, tn), lambda i, j: (0, j)),
            pl.BlockSpec((16, D_MODEL), lambda i, j: (0, 0)),
        ],
        out_specs=[
            pl.BlockSpec((tm, tn), lambda i, j: (i, j)),
            pl.BlockSpec((16, tm), lambda i, j: (0, i)),
        ],
        out_shape=[jax.ShapeDtypeStruct((t, P_COLS), f32), jax.ShapeDtypeStruct((16, t), f32)],
        scratch_shapes=[pltpu.VMEM((tm, D_MODEL), bf16)],
        compiler_params=_cparams("arbitrary", "arbitrary"),
        name="inproj",
    )(x, g.reshape(1, D_MODEL), w_perm, w_gate_t)


def _qk_gnorm(x, g):
    ms = _group_sum(x * x, DKB) * (1.0 / DKB)
    return (x * lax.rsqrt(ms + EPS)) * g


def _bprep_kernel(q_ref, k_ref, gq_ref, gk_ref, qn_ref, kn_ref):
    qn_ref[...] = _qk_gnorm(q_ref[...], gq_ref[...])
    kn_ref[...] = _qk_gnorm(k_ref[...], gk_ref[...])


def _bprep_t_kernel(q_ref, k_ref, v_ref, gq_ref, gk_ref, qnt_ref, kn_ref, vt_ref):
    qnt_ref[...] = _qk_gnorm(q_ref[...], gq_ref[...]).T
    kn_ref[...] = _qk_gnorm(k_ref[...], gk_ref[...])
    vt_ref[...] = v_ref[...].T


def _bprep(p, qk_norm_g, transposed):
    t = p.shape[0]
    tm = min(512, t)
    gq = jnp.tile(qk_norm_g[0], GROUP_W // DKB).reshape(1, GROUP_W)
    gk = jnp.tile(qk_norm_g[1], GROUP_W // DKB).reshape(1, GROUP_W)
    col = lambda c: pl.BlockSpec((tm, GROUP_W), lambda i: (i, c))
    gain = pl.BlockSpec((1, GROUP_W), lambda i: (0, 0))
    rows = pl.BlockSpec((tm, GROUP_W), lambda i: (i, 0))
    rows_t = pl.BlockSpec((GROUP_W, tm), lambda i: (0, i))
    if transposed:
        return pl.pallas_call(
            _bprep_t_kernel,
            grid=(t // tm,),
            in_specs=[col(4), col(5), col(6), gain, gain],
            out_specs=[rows_t, rows, rows_t],
            out_shape=[jax.ShapeDtypeStruct((GROUP_W, t), f32), jax.ShapeDtypeStruct((t, GROUP_W), f32),
                       jax.ShapeDtypeStruct((GROUP_W, t), f32)],
            compiler_params=_cparams("arbitrary"),
            name="bprep_t",
        )(p, p, p, gq, gk)
    return pl.pallas_call(
        _bprep_kernel,
        grid=(t // tm,),
        in_specs=[col(4), col(5), gain, gain],
        out_specs=[rows, rows],
        out_shape=[jax.ShapeDtypeStruct((t, GROUP_W), f32)] * 2,
        compiler_params=_cparams("arbitrary"),
        name="bprep",
    )(p, p, gq, gk)


def _outproj_kernel(oa_ref, ob_ref, oc_ref, od_ref, ag_ref, cg_ref, dg_ref, x_ref, g_ref, w_ref, y_ref, *, b_scale):
    def gnorm(x, g):
        ms = _group_sum(x * x, HEAD_W) * (1.0 / HEAD_W)
        return (x * lax.rsqrt(ms + EPS)) * g
    g = g_ref[...]
    mixes = (
        gnorm(oa_ref[...], g[0:1]) * _silu(ag_ref[...]),
        gnorm(ob_ref[...], g[1:2]) * b_scale,
        gnorm(oc_ref[...], g[2:3]) * jax.nn.sigmoid(cg_ref[...]),
        gnorm(od_ref[...], g[3:4]) * jax.nn.sigmoid(dg_ref[...]),
    )
    y = x_ref[...]
    for i, m in enumerate(mixes):
        y = y + _dg(m.astype(bf16), w_ref[i * GROUP_W:(i + 1) * GROUP_W, :], NN)
    y_ref[...] = y


def _outproj(oa, ob, oc, od, p, x, gains, w_out, b_scale):
    t = x.shape[0]
    tm = min(512, t)
    row = lambda i: (i, 0)
    return pl.pallas_call(
        functools.partial(_outproj_kernel, b_scale=b_scale),
        grid=(t // tm,),
        in_specs=[
            pl.BlockSpec((tm, GROUP_W), row), pl.BlockSpec((tm, GROUP_W), row),
            pl.BlockSpec((tm, GROUP_W), row), pl.BlockSpec((tm, GROUP_W), row),
            pl.BlockSpec((tm, GROUP_W), lambda i: (i, 3)),
            pl.BlockSpec((tm, GROUP_W), lambda i: (i, 10)),
            pl.BlockSpec((tm, GROUP_W), lambda i: (i, 14)),
            pl.BlockSpec((tm, D_MODEL), row),
            pl.BlockSpec((4, GROUP_W), lambda i: (0, 0)),
            pl.BlockSpec((D_MODEL, D_MODEL), lambda i: (0, 0)),
        ],
        out_specs=pl.BlockSpec((tm, D_MODEL), row),
        out_shape=jax.ShapeDtypeStruct((t, D_MODEL), f32),
        compiler_params=_cparams("arbitrary"),
        name="outproj",
    )(oa, ob, oc, od, p, p, p, x, gains, w_out)


def _ffn_kernel(x_ref, g_ref, wg_ref, wu_ref, wd_ref, y_ref, h_scr, acc_scr):
    f = pl.program_id(1)

    @pl.when(f == 0)
    def _():
        x = x_ref[...]
        ms = jnp.mean(x * x, axis=-1, keepdims=True)
        h_scr[...] = ((x * lax.rsqrt(ms + EPS)) * g_ref[...]).astype(bf16)
        acc_scr[...] = x

    h = h_scr[...]
    a = _silu(_dg(h, wg_ref[...])) * _dg(h, wu_ref[...])
    acc_scr[...] += _dg(a.astype(bf16), wd_ref[...])

    @pl.when(f == pl.num_programs(1) - 1)
    def _():
        y_ref[...] = acc_scr[...]


def _ffn(x, g, wg, wu, wd):
    t = x.shape[0]
    d_ff = wg.shape[1]
    tm = min(512, t)
    tf = d_ff // 2
    return pl.pallas_call(
        _ffn_kernel,
        grid=(t // tm, d_ff // tf),
        in_specs=[
            pl.BlockSpec((tm, D_MODEL), lambda i, f: (i, 0)),
            pl.BlockSpec((1, D_MODEL), lambda i, f: (0, 0)),
            pl.BlockSpec((D_MODEL, tf), lambda i, f: (0, f)),
            pl.BlockSpec((D_MODEL, tf), lambda i, f: (0, f)),
            pl.BlockSpec((tf, D_MODEL), lambda i, f: (f, 0)),
        ],
        out_specs=pl.BlockSpec((tm, D_MODEL), lambda i, f: (i, 0)),
        out_shape=jax.ShapeDtypeStruct((t, D_MODEL), f32),
        scratch_shapes=[pltpu.VMEM((tm, D_MODEL), bf16), pltpu.VMEM((tm, D_MODEL), f32)],
        compiler_params=_cparams("arbitrary", "arbitrary"),
        name="ffn",
    )(x, g.reshape(1, D_MODEL), wg, wu, wd)


LANE = 128
SUBL = D_MODEL // LANE
ROW_TILE = 256
SPARSE_MIN_TOKENS = 4096


def _tile_rows(x):
    return [x[:, j * LANE:(j + 1) * LANE] for j in range(SUBL)]


def _router_kernel(x_ref, g_ref, r_ref, h3_ref, meta_ref, cnt_ref, carry_scr):
    @pl.when(pl.program_id(0) == 0)
    def _():
        carry_scr[...] = jnp.zeros(carry_scr.shape, f32)

    x = x_ref[...]
    tm = x.shape[0]
    ms = jnp.mean(x * x, axis=-1, keepdims=True)
    h = (x * lax.rsqrt(ms + EPS)) * g_ref[...]
    for j, blk in enumerate(_tile_rows(h)):
        h3_ref[:, j, :] = blk
    logits = _dg(h.astype(bf16), r_ref[...])
    lane = _iota(logits.shape, 1)
    logits = jnp.where(lane < N_EXPERTS, logits, -jnp.inf)
    v1 = jnp.max(logits, axis=-1, keepdims=True)
    i1 = jnp.min(jnp.where(logits == v1, lane, LANE), axis=-1, keepdims=True)
    rest = jnp.where(lane == i1, -jnp.inf, logits)
    v2 = jnp.max(rest, axis=-1, keepdims=True)
    i2 = jnp.min(jnp.where(rest == v2, lane, LANE), axis=-1, keepdims=True)
    e2 = jnp.exp(v2 - v1)
    den = 1.0 + e2
    hit = ((lane == i1) | (lane == i2)).astype(f32)
    strict = (_iota((tm, tm), 0) > _iota((tm, tm), 1)).astype(bf16)
    before = _dg(strict, hit.astype(bf16)) + carry_scr[...]
    pos1 = jnp.sum(jnp.where(lane == i1, before, 0.0), axis=-1, keepdims=True)
    pos2 = jnp.sum(jnp.where(lane == i2, before, 0.0), axis=-1, keepdims=True)
    carry_scr[...] += jnp.sum(hit, axis=0, keepdims=True)
    meta = jnp.zeros(logits.shape, f32)
    for c, val in enumerate((i1.astype(f32), i2.astype(f32), 1.0 / den, e2 / den, pos1, pos2)):
        meta = jnp.where(lane == c, val, meta)
    meta_ref[...] = meta
    cnt_ref[...] = carry_scr[...]


def _router(x, g, router_pad):
    t = x.shape[0]
    tm = min(512, t)
    return pl.pallas_call(
        _router_kernel,
        grid=(t // tm,),
        in_specs=[
            pl.BlockSpec((tm, D_MODEL), lambda i: (i, 0)),
            pl.BlockSpec((1, D_MODEL), lambda i: (0, 0)),
            pl.BlockSpec((D_MODEL, LANE), lambda i: (0, 0)),
        ],
        out_specs=[pl.BlockSpec((tm, SUBL, LANE), lambda i: (i, 0, 0)),
                   pl.BlockSpec((tm, LANE), lambda i: (i, 0)),
                   pl.BlockSpec((1, LANE), lambda i: (0, 0))],
        out_shape=[jax.ShapeDtypeStruct((t, SUBL, LANE), f32), jax.ShapeDtypeStruct((t, LANE), f32),
                   jax.ShapeDtypeStruct((1, LANE), f32)],
        scratch_shapes=[pltpu.VMEM((1, LANE), f32)],
        compiler_params=_cparams("arbitrary"),
        name="router",
    )(x, g.reshape(1, D_MODEL), router_pad)


def _swiglu_bf16(x, wg, wu, wd):
    a = _silu(_dg(x, wg)) * _dg(x, wu)
    return _dg(a.astype(bf16), wd)


def _moe_dense_kernel(x_ref, h3_ref, meta_ref, wg_ref, wu_ref, wd_ref, y_ref, acc_scr):
    e = pl.program_id(1)
    f = pl.program_id(2)

    @pl.when((e == 0) & (f == 0))
    def _():
        acc_scr[...] = x_ref[...]

    meta = meta_ref[...]
    ef = e.astype(f32)
    cw = jnp.where(meta[:, 0:1] == ef, meta[:, 2:3], 0.0) + jnp.where(meta[:, 1:2] == ef, meta[:, 3:4], 0.0)
    h = jnp.concatenate([h3_ref[:, j, :] for j in range(SUBL)], axis=1).astype(bf16)
    acc_scr[...] += cw * _swiglu_bf16(h, wg_ref[...], wu_ref[...], wd_ref[...])

    @pl.when((e == pl.num_programs(1) - 1) & (f == pl.num_programs(2) - 1))
    def _():
        y_ref[...] = acc_scr[...]


def _moe_dense(x, h3, meta, wg, wu, wd):
    t = x.shape[0]
    n_e, _, d_ff = wg.shape
    tm = min(512, t)
    tf = d_ff // 2
    return pl.pallas_call(
        _moe_dense_kernel,
        grid=(t // tm, n_e, d_ff // tf),
        in_specs=[
            pl.BlockSpec((tm, D_MODEL), lambda i, e, f: (i, 0)),
            pl.BlockSpec((tm, SUBL, LANE), lambda i, e, f: (i, 0, 0)),
            pl.BlockSpec((tm, LANE), lambda i, e, f: (i, 0)),
            pl.BlockSpec((None, D_MODEL, tf), lambda i, e, f: (e, 0, f)),
            pl.BlockSpec((None, D_MODEL, tf), lambda i, e, f: (e, 0, f)),
            pl.BlockSpec((None, tf, D_MODEL), lambda i, e, f: (e, f, 0)),
        ],
        out_specs=pl.BlockSpec((tm, D_MODEL), lambda i, e, f: (i, 0)),
        out_shape=jax.ShapeDtypeStruct((t, D_MODEL), f32),
        scratch_shapes=[pltpu.VMEM((tm, D_MODEL), f32)],
        compiler_params=_cparams("arbitrary", "arbitrary", "arbitrary"),
        name="moe_dense",
    )(x, h3, meta, wg, wu, wd)


def _route_plan(meta, counts, t):
    cnt = counts[0, :N_EXPERTS].astype(jnp.int32)
    padded = ((cnt + ROW_TILE - 1) // ROW_TILE) * ROW_TILE
    ends = jnp.cumsum(padded)
    offs = ends - padded
    experts = jnp.arange(N_EXPERTS, dtype=jnp.int32)

    def dest(expert_col, rank_col):
        e = meta[:, expert_col].astype(jnp.int32)
        off = jnp.sum(jnp.where(e[:, None] == experts[None, :], offs[None, :], 0), axis=1)
        return off + meta[:, rank_col].astype(jnp.int32)

    n_rows = 2 * t + N_EXPERTS * ROW_TILE
    starts = jnp.arange(n_rows // ROW_TILE, dtype=jnp.int32) * ROW_TILE
    tile_expert = jnp.minimum(jnp.sum((starts[:, None] >= ends[None, :]).astype(jnp.int32), axis=1), N_EXPERTS - 1)
    n_used = (ends[N_EXPERTS - 1] // ROW_TILE).reshape(1)
    return dest(0, 4), dest(1, 5), tile_expert, n_used, n_rows


def _dispatch_kernel(d1_ref, d2_ref, h3_ref, zero_hbm, xs_hbm, sem, *, tm):
    del zero_hbm
    base = pl.program_id(0) * tm

    def issue(k, carry):
        src = h3_ref.at[pl.ds(k, 1)]
        pltpu.make_async_copy(src, xs_hbm.at[pl.ds(d1_ref[base + k], 1)], sem).start()
        pltpu.make_async_copy(src, xs_hbm.at[pl.ds(d2_ref[base + k], 1)], sem).start()
        return carry

    lax.fori_loop(0, tm, issue, 0)
    for _ in range(2):
        pltpu.make_async_copy(h3_ref, xs_hbm.at[pl.ds(0, tm)], sem).wait()


def _dispatch(h3, dest1, dest2, n_rows):
    t = h3.shape[0]
    tm = min(512, t)
    grid_spec = pltpu.PrefetchScalarGridSpec(
        num_scalar_prefetch=2,
        grid=(t // tm,),
        in_specs=[pl.BlockSpec((tm, SUBL, LANE), lambda i, d1, d2: (i, 0, 0)), pl.BlockSpec(memory_space=pl.ANY)],
        out_specs=pl.BlockSpec(memory_space=pl.ANY),
        scratch_shapes=[pltpu.SemaphoreType.DMA(())],
    )
    return pl.pallas_call(
        functools.partial(_dispatch_kernel, tm=tm),
        grid_spec=grid_spec,
        out_shape=jax.ShapeDtypeStruct((n_rows, SUBL, LANE), f32),
        input_output_aliases={3: 0},
        compiler_params=_cparams("arbitrary"),
        name="moe_dispatch",
    )(dest1, dest2, h3, jnp.zeros((n_rows, SUBL, LANE), f32))


def _experts_kernel(te_ref, nu_ref, xs_ref, wg_ref, wu_ref, wd_ref, ys_ref):
    del te_ref
    r = pl.program_id(0)

    @pl.when(r < nu_ref[0])
    def _():
        x = jnp.concatenate([xs_ref[:, j, :] for j in range(SUBL)], axis=1).astype(bf16)
        for j, blk in enumerate(_tile_rows(_swiglu_bf16(x, wg_ref[...], wu_ref[...], wd_ref[...]))):
            ys_ref[:, j, :] = blk

    @pl.when(r >= nu_ref[0])
    def _():
        ys_ref[...] = jnp.zeros(ys_ref.shape, f32)


def _experts(xs, tile_expert, n_used, wg, wu, wd):
    n_rows = xs.shape[0]
    d_ff = wg.shape[2]
    rows = pl.BlockSpec((ROW_TILE, SUBL, LANE), lambda r, te, nu: (r, 0, 0))
    w_in = pl.BlockSpec((None, D_MODEL, d_ff), lambda r, te, nu: (te[r], 0, 0), pipeline_mode=pl.Buffered(1))
    w_dn = pl.BlockSpec((None, d_ff, D_MODEL), lambda r, te, nu: (te[r], 0, 0), pipeline_mode=pl.Buffered(1))
    grid_spec = pltpu.PrefetchScalarGridSpec(
        num_scalar_prefetch=2,
        grid=(n_rows // ROW_TILE,),
        in_specs=[rows, w_in, w_in, w_dn],
        out_specs=rows,
    )
    return pl.pallas_call(
        _experts_kernel,
        grid_spec=grid_spec,
        out_shape=jax.ShapeDtypeStruct((n_rows, SUBL, LANE), f32),
        compiler_params=_cparams("arbitrary"),
        name="moe_experts",
    )(tile_expert, n_used, xs, wg, wu, wd)


def _combine_kernel(d1_ref, d2_ref, x_ref, meta_ref, ys_hbm, y_ref, buf, sem, *, tm):
    base = pl.program_id(0) * tm

    def issue(k, carry):
        t = base + k
        pltpu.make_async_copy(ys_hbm.at[pl.ds(d1_ref[t], 1)], buf.at[0, pl.ds(k, 1)], sem).start()
        pltpu.make_async_copy(ys_hbm.at[pl.ds(d2_ref[t], 1)], buf.at[1, pl.ds(k, 1)], sem).start()
        return carry

    lax.fori_loop(0, tm, issue, 0)
    for s in range(2):
        pltpu.make_async_copy(ys_hbm.at[pl.ds(0, tm)], buf.at[s], sem).wait()
    meta = meta_ref[...]
    g1 = meta[:, 2:3]
    g2 = meta[:, 3:4]
    for j in range(SUBL):
        sl = slice(j * LANE, (j + 1) * LANE)
        y_ref[:, sl] = x_ref[:, sl] + (g1 * buf[0, :, j, :] + g2 * buf[1, :, j, :])


def _combine(x, meta, ys, dest1, dest2):
    t = x.shape[0]
    tm = min(256, t)
    grid_spec = pltpu.PrefetchScalarGridSpec(
        num_scalar_prefetch=2,
        grid=(t // tm,),
        in_specs=[pl.BlockSpec((tm, D_MODEL), lambda i, d1, d2: (i, 0)),
                  pl.BlockSpec((tm, LANE), lambda i, d1, d2: (i, 0)),
                  pl.BlockSpec(memory_space=pl.ANY)],
        out_specs=pl.BlockSpec((tm, D_MODEL), lambda i, d1, d2: (i, 0)),
        scratch_shapes=[pltpu.VMEM((2, tm, SUBL, LANE), f32), pltpu.SemaphoreType.DMA(())],
    )
    return pl.pallas_call(
        functools.partial(_combine_kernel, tm=tm),
        grid_spec=grid_spec,
        out_shape=jax.ShapeDtypeStruct((t, D_MODEL), f32),
        compiler_params=_cparams("arbitrary"),
        name="moe_combine",
    )(dest1, dest2, x, meta, ys)


def _moe(x, g, router_pad, wg, wu, wd):
    t = x.shape[0]
    h3, meta, counts = _router(x, g, router_pad)
    if t < SPARSE_MIN_TOKENS:
        return _moe_dense(x, h3, meta, wg, wu, wd)
    dest1, dest2, tile_expert, n_used, n_rows = _route_plan(meta, counts, t)
    xs = _dispatch(h3, dest1, dest2, n_rows)
    ys = _experts(xs, tile_expert, n_used, wg, wu, wd)
    return _combine(x, meta, ys, dest1, dest2)


def _t5_bucket_np(n):
    n = np.maximum(n, 0)
    max_exact = NUM_BUCKETS // 2
    nf = np.maximum(n, 1).astype(np.float32)
    large = max_exact + (np.log(nf / np.float32(max_exact)) / np.float32(math.log(MAX_DISTANCE / max_exact))
                         * np.float32(NUM_BUCKETS - max_exact)).astype(np.int32)
    return np.where(n < max_exact, n, np.minimum(large, NUM_BUCKETS - 1))


def _shifted_bias(rel_bias):
    rb = rel_bias.astype(f32)
    return rb - rb[NUM_BUCKETS - 1:NUM_BUCKETS]


ACC_ROWS = HEAD_W + 8
LOG2E = math.log2(math.e)


def _attn_kernel(qi_ref, kj_ref, lam_ref, qt_ref, k_ref, vt_ref, toep_ref, o_ref, qs_scr, m_scr, acc_scr, *, tq):
    p = pl.program_id(1)
    i = qi_ref[p]
    j = kj_ref[p]
    n_hc = 2 * N_HEADS
    c2 = (DKB ** -0.5) * LOG2E

    @pl.when(j == 0)
    def _():
        qt = qt_ref[...]
        row_grp = _iota(qt.shape, 0) // DKB
        for hc in range(n_hc):
            qs_scr[:, hc * tq:(hc + 1) * tq] = jnp.where(row_grp == hc, qt, 0.0).astype(bf16)
        m_scr[...] = jnp.full(m_scr.shape, NEG, f32)
        acc_scr[...] = jnp.zeros(acc_scr.shape, f32)

    def step(near):
        tk = k_ref.shape[0]
        st_all = _dg(k_ref[...].astype(bf16), qs_scr[...], NN)
        vt = vt_ref[...]
        ones = jnp.ones((ACC_ROWS - HEAD_W, tk), f32)
        for h in range(N_HEADS):
            vh = jnp.concatenate([vt[h * HEAD_W:(h + 1) * HEAD_W, :], ones], axis=0).astype(bf16)
            for hc in (2 * h, 2 * h + 1):
                s = st_all[:, hc * tq:(hc + 1) * tq] * c2
                if near:
                    s = s + toep_ref[(i - j) * N_HEADS + h]
                m_old = m_scr[hc:hc + 1, :]
                m_new = jnp.maximum(m_old, jnp.max(s, axis=0, keepdims=True))
                pexp = jnp.exp2(s - m_new)
                acc_scr[hc] = jnp.exp2(m_old - m_new) * acc_scr[hc] + _dg(vh, pexp.astype(bf16), NN)
                m_scr[hc:hc + 1, :] = m_new

    @pl.when(i - j <= 1)
    def _():
        step(True)

    @pl.when(i - j > 1)
    def _():
        step(False)

    @pl.when(j == i)
    def _():
        lam = lam_ref[0]
        outs = []
        for h in range(N_HEADS):
            a0 = acc_scr[2 * h]
            a1 = acc_scr[2 * h + 1]
            outs.append(a0[0:HEAD_W] * _recip(a0[HEAD_W:HEAD_W + 1])
                        - lam * (a1[0:HEAD_W] * _recip(a1[HEAD_W:HEAD_W + 1])))
        o_ref[...] = jnp.concatenate(outs, axis=0).T


def _toeplitz_kernel(u_ref, o_ref):
    t = o_ref.shape[0]
    rows = jnp.broadcast_to(u_ref[...], (t, 2 * t))
    o_ref[...] = pltpu.roll(rows, 0, 1, stride=1, stride_axis=0)[:, t:2 * t]


def _toeplitz_bias_tiles(rel_bias, t):
    m = np.arange(2 * t)[None, :]
    dist = m - t + np.array([0, t])[:, None]
    tab = _shifted_bias(rel_bias)
    u = jnp.take(tab, jnp.asarray(_t5_bucket_np(dist)), axis=0)
    u = jnp.where(jnp.asarray(dist >= 0)[:, :, None], u * LOG2E, NEG)
    u = jnp.transpose(u, (0, 2, 1)).reshape(2 * N_HEADS, 1, 2 * t)
    return pl.pallas_call(
        _toeplitz_kernel,
        grid=(2 * N_HEADS,),
        in_specs=[pl.BlockSpec((None, 1, 2 * t), lambda i: (i, 0, 0))],
        out_specs=pl.BlockSpec((None, t, t), lambda i: (i, 0, 0)),
        out_shape=jax.ShapeDtypeStruct((2 * N_HEADS, t, t), f32),
        compiler_params=_cparams("arbitrary"),
        name="toeplitz_bias",
    )(u)


def _attn_prompt(qnt, kn, vt, lam, rel_bias, n_batch, seq):
    tq = min(512, seq)
    nq = seq // tq
    pairs =[(i, j) for i in range(nq) for j in range(i + 1)]
    qi = jnp.asarray(np.array([a for a, _ in pairs], np.int32))
    kj = jnp.asarray(np.array([b for _, b in pairs], np.int32))
    toep = _toeplitz_bias_tiles(rel_bias, tq)
    grid_spec = pltpu.PrefetchScalarGridSpec(
        num_scalar_prefetch=2,
        grid=(n_batch, len(pairs)),
        in_specs=[
            pl.BlockSpec(memory_space=pltpu.SMEM),
            pl.BlockSpec((GROUP_W, tq), lambda b_, p_, qi_, kj_: (0, b_ * nq + qi_[p_])),
            pl.BlockSpec((tq, GROUP_W), lambda b_, p_, qi_, kj_: (b_ * nq + kj_[p_], 0)),
            pl.BlockSpec((GROUP_W, tq), lambda b_, p_, qi_, kj_: (0, b_ * nq + kj_[p_])),
            pl.BlockSpec((2 * N_HEADS, tq, tq), lambda b_, p_, qi_, kj_: (0, 0, 0)),
        ],
        out_specs=pl.BlockSpec((tq, GROUP_W), lambda b_, p_, qi_, kj_: (b_ * nq + qi_[p_], 0)),
        scratch_shapes=[
            pltpu.VMEM((GROUP_W, 2 * N_HEADS * tq), bf16),
            pltpu.VMEM((2 * N_HEADS, tq), f32),
            pltpu.VMEM((2 * N_HEADS, ACC_ROWS, tq), f32),
        ],
    )
    return pl.pallas_call(
        functools.partial(_attn_kernel, tq=tq),
        grid_spec=grid_spec,
        out_shape=jax.ShapeDtypeStruct((n_batch * seq, GROUP_W), f32),
        compiler_params=_cparams("arbitrary", "arbitrary"),
        name="attn_prompt",
    )(qi, kj, lam.reshape(1), qnt, kn, vt, toep)


def _attn_decode_kernel(pt_ref, lam_ref, q_ref, kn_ref, vn_ref, blast_ref, bself_ref, *rest, pg, n_pages):
    k_refs = rest[:pg]
    v_refs = rest[pg:2 * pg]
    o_ref, qs_scr, s_scr, aself_scr, acc_scr = rest[2 * pg:]
    t = pl.program_id(1)
    n_steps = n_pages // pg
    n_hc = 2 * N_HEADS
    page = k_refs[0].shape[1]
    scale = DKB ** -0.5
    rnd = lambda z: z.astype(bf16).astype(f32)

    @pl.when(t == 0)
    def _():
        q = jnp.broadcast_to(q_ref[...], (n_hc, GROUP_W))
        keep = (_iota(q.shape, 1) // DKB) == _iota(q.shape, 0)
        qs_scr[...] = jnp.where(keep, q, 0.0)

    @pl.when(t < n_steps)
    def _():
        qs_b = qs_scr[...].astype(bf16)
        parts = []
        for g in range(pg):
            s = _dg(qs_b, k_refs[g][...].astype(bf16), NN) * scale
            is_last = (t * pg + g) == (n_pages - 1)
            parts.append(s + jnp.where(is_last, blast_ref[...], 0.0))
        s_scr[t] = jnp.concatenate(parts, axis=1)

    @pl.when(t == n_steps - 1)
    def _():
        s_all = s_scr[...]
        s_self = jnp.sum(rnd(qs_scr[...]) * rnd(kn_ref[...]), axis=-1, keepdims=True) * scale + bself_ref[...]
        m = jnp.maximum(jnp.max(jnp.max(s_all, axis=2, keepdims=True), axis=0), s_self)
        p = jnp.exp(s_all - m)
        p_self = jnp.exp(s_self - m)
        l = jnp.sum(jnp.sum(p, axis=2, keepdims=True), axis=0) + p_self
        inv_l = _recip(l)
        pn = p * inv_l
        pn_self = p_self * inv_l
        lam = lam_ref[0]
        rows = [pn[:, 2 * h:2 * h + 1, :] - lam * pn[:, 2 * h + 1:2 * h + 2, :] for h in range(N_HEADS)]
        s_scr[...] = jnp.concatenate(rows + [jnp.zeros_like(rows[0])] * N_HEADS, axis=1)
        rows_self = [pn_self[2 * h:2 * h + 1] - lam * pn_self[2 * h + 1:2 * h + 2] for h in range(N_HEADS)]
        aself_scr[...] = jnp.concatenate(rows_self + [jnp.zeros_like(rows_self[0])] * N_HEADS, axis=0)
        acc_scr[...] = jnp.zeros(acc_scr.shape, f32)

    @pl.when(t >= n_steps)
    def _():
        a = s_scr[t - n_steps].astype(bf16)
        acc = acc_scr[...]
        for g in range(pg):
            acc = acc + _dg(a[:, g * page:(g + 1) * page], v_refs[g][...].astype(bf16), NT)
        acc_scr[...] = acc

    @pl.when(t == 2 * n_steps - 1)
    def _():
        o = acc_scr[...] + rnd(aself_scr[...]) * rnd(vn_ref[...])
        lane_head = _iota((1, GROUP_W), 1) // HEAD_W
        out = jnp.zeros((1, GROUP_W), f32)
        for h in range(N_HEADS):
            out = jnp.where(lane_head == h, o[h:h + 1], out)
        o_ref[...] = out


def _attn_decode(qn, kn, vn, page_table, cache_k, cache_v, layer, lam, rel_bias):
    n_b, n_pages = page_table.shape
    page = cache_k.shape[3]
    pg = min(16, n_pages)
    n_steps = n_pages // pg
    past = n_pages * page
    tab = _shifted_bias(rel_bias)
    d_last = past - ((n_pages - 1) * page + np.arange(page))
    blast = jnp.repeat(jnp.take(tab, jnp.asarray(_t5_bucket_np(d_last)), axis=0).T, 2, axis=0)
    bself = jnp.repeat(tab[0].reshape(N_HEADS, 1), 2, axis=0)

    def k_spec(g):
        return pl.BlockSpec((None, None, GROUP_W, page),
                            lambda b_, t_, pt: (layer, pt[b_, jnp.minimum(t_, n_steps - 1) * pg + g], 0, 0))

    def v_spec(g):
        return pl.BlockSpec((None, None, GROUP_W, page),
                            lambda b_, t_, pt: (layer, pt[b_, jnp.maximum(t_ - n_steps, 0) * pg + g], 0, 0))

    row = pl.BlockSpec((None, 1, GROUP_W), lambda b_, t_, pt: (b_, 0, 0))
    grid_spec = pltpu.PrefetchScalarGridSpec(
        num_scalar_prefetch=1,
        grid=(n_b, 2 * n_steps),
        in_specs=[pl.BlockSpec(memory_space=pltpu.SMEM), row, row, row,
                  pl.BlockSpec((2 * N_HEADS, page), lambda b_, t_, pt: (0, 0)),
                  pl.BlockSpec((2 * N_HEADS, 1), lambda b_, t_, pt: (0, 0))]
                 + [k_spec(g) for g in range(pg)] + [v_spec(g) for g in range(pg)],
        out_specs=row,
        scratch_shapes=[
            pltpu.VMEM((2 * N_HEADS, GROUP_W), f32),
            pltpu.VMEM((n_steps, 2 * N_HEADS, pg * page), f32),
            pltpu.VMEM((2 * N_HEADS, 1), f32),
            pltpu.VMEM((2 * N_HEADS, GROUP_W), f32),
        ],
    )
    r3 = lambda z: z.reshape(n_b, 1, GROUP_W)
    out = pl.pallas_call(
        functools.partial(_attn_decode_kernel, pg=pg, n_pages=n_pages),
        grid_spec=grid_spec,
        out_shape=jax.ShapeDtypeStruct((n_b, 1, GROUP_W), f32),
        compiler_params=_cparams("arbitrary", "arbitrary"),
        name="attn_decode",
    )(page_table, lam.reshape(1), r3(qn), r3(kn), r3(vn), blast, bself,
      *([cache_k] * pg), *([cache_v] * pg))
    return out.reshape(n_b, GROUP_W)


def _tri(n, dtype=f32):
    return (_iota((n, n), 0) >= _iota((n, n), 1)).astype(dtype)


def _block_tri_t(n, blk):
    r = _iota((n, n), 0)
    c = _iota((n, n), 1)
    return (((r // blk) == (c // blk)) & (r <= c)).astype(bf16)


def _block_tri(n, blk):
    r = _iota((n, n), 0)
    c = _iota((n, n), 1)
    return (((r // blk) == (c // blk)) & (r >= c)).astype(bf16)


def _gdn_kernel(u_ref, gc_ref, gr_ref, cw_ref, alr_ref, dtr_ref, alc_ref, dtc_ref, o_ref, s_out_ref, ext_scr, s_scr, *, tb):
    i = pl.program_id(1)

    @pl.when(i == 0)
    def _():
        ext_scr[0:8, :] = jnp.zeros((8, 3 * GROUP_W), f32)
        s_scr[...] = jnp.zeros(s_scr.shape, f32)

    ext_scr[8:8 + tb, :] = u_ref[...]
    w = cw_ref[...]
    conv = ext_scr[8:8 + tb, :] * w[3:4]
    for jj in range(1, CONV_W):
        conv = conv + ext_scr[8 - jj:8 - jj + tb, :] * w[3 - jj:4 - jj]
    ext_scr[0:8, :] = ext_scr[tb:tb + 8, :]
    qkv = _silu(conv)

    def l2n(x):
        return x * lax.rsqrt(_group_sum(x * x, HEAD_W) + EPS)

    q = l2n(qkv[:, 0:GROUP_W]) * (HEAD_W ** -0.5)
    k = l2n(qkv[:, GROUP_W:2 * GROUP_W])
    v = qkv[:, 2 * GROUP_W:3 * GROUP_W]

    gc = gc_ref[...]
    g_col = -jnp.exp(alr_ref[...]) * _softplus(gc[:, 0:4] + dtr_ref[...])
    beta_col = jax.nn.sigmoid(gc[:, 4:8])
    gr = gr_ref[...]
    g_row = -jnp.exp(alc_ref[...]) * _softplus(gr[0:4, :] + dtc_ref[...])
    g_row8 = jnp.concatenate([g_row, jnp.zeros_like(g_row)], axis=0)
    gcum_row = _mm2(g_row8, _block_tri_t(tb, CHUNK))

    same, lower, strict = _bd_masks()
    tri = _tri(CHUNK, bf16)
    r = _iota((GROUP_W, GROUP_W), 0)
    c = _iota((GROUP_W, GROUP_W), 1)
    eye = (r == c).astype(f32)

    chunks = range(tb // CHUNK)
    gcums, qks, m_bds = [], [], []
    for ch in chunks:
        lo = ch * CHUNK
        gcum = _mm2l(tri, g_col[lo:lo + CHUNK])
        g_stack = _stack_cols(gcum)
        g_cat = _cat_rows(gcum_row, lo)
        decay = jnp.exp(jnp.where(lower, g_stack - g_cat, NEG))
        ksm = _head_stack(k[lo:lo + CHUNK])
        kk = _mm1(ksm, ksm, NT)
        qks.append(_mm1(_head_stack(q[lo:lo + CHUNK]), ksm, NT) * decay)
        m_bds.append(_stack_cols(beta_col[lo:lo + CHUNK]) * kk * jnp.where(strict, decay, 0.0))
        gcums.append(gcum)

    def sibling(lev):
        return ((r >> (lev + 1)) == (c >> (lev + 1))) & (((r >> lev) & 1) == 1) & (((c >> lev) & 1) == 0)

    xs = [eye - jnp.where(sibling(0), m, 0.0) for m in m_bds]
    for lev in range(1, 6):
        sel = sibling(lev)
        xs = [x - _mm3(_mm3(x, jnp.where(sel, m, 0.0)), x) for x, m in zip(xs, m_bds)]

    for ch in chunks:
        lo = ch * CHUNK
        qc, kc, vc = q[lo:lo + CHUNK], k[lo:lo + CHUNK], v[lo:lo + CHUNK]
        gcum, bcol = gcums[ch], beta_col[lo:lo + CHUNK]
        s_bd = s_scr[...]
        kq_s = _mm1(jnp.concatenate([kc, qc], axis=0), s_bd)
        ks, qs = kq_s[0:CHUNK], kq_s[CHUNK:2 * CHUNK]
        eg_all = _expand_cols(jnp.exp(gcum))
        rhs = _expand_cols(bcol) * (vc - eg_all * ks)
        u_sm = _mm3(xs[ch], _head_stack(rhs))
        o_sm = _mm1(qks[ch], u_sm)
        o_ref[lo:lo + CHUNK, :] = eg_all * qs + _fold_heads(o_sm)
        u_all = _fold_heads(u_sm)
        g_last = gcum[CHUNK - 1:CHUNK, :]
        kw = kc * _expand_cols(jnp.exp(g_last - gcum))
        d_stack = jnp.concatenate(
            [jnp.broadcast_to(jnp.exp(g_last[:, h:h + 1]), (HEAD_W, 1)) for h in range(N_HEADS)], axis=0)
        s_scr[...] = d_stack * s_bd + jnp.where(same, _mm1(kw, u_all, TN), 0.0)

    @pl.when(i == pl.num_programs(1) - 1)
    def _():
        s_out_ref[...] = s_scr[...]


def _gdn_prompt(p, gt, conv_w, a_log, dt_bias, n_batch, seq):
    tb = min(256, seq)
    nb = seq // tb
    r14 = lambda z: z.astype(f32).reshape(1, N_HEADS)
    c41 = lambda z: z.astype(f32).reshape(N_HEADS, 1)
    o, s_bd = pl.pallas_call(
        functools.partial(_gdn_kernel, tb=tb),
        grid=(n_batch, nb),
        in_specs=[
            pl.BlockSpec((tb, 3 * GROUP_W), lambda b, i: (b * nb + i, 0)),
            pl.BlockSpec((tb, 128), lambda b, i: (b * nb + i, GATE_COL // 128)),
            pl.BlockSpec((16, tb), lambda b, i: (0, b * nb + i)),
            pl.BlockSpec((CONV_W, 3 * GROUP_W), lambda b, i: (0, 0)),
            pl.BlockSpec((1, N_HEADS), lambda b, i: (0, 0)),
            pl.BlockSpec((1, N_HEADS), lambda b, i: (0, 0)),
            pl.BlockSpec((N_HEADS, 1), lambda b, i: (0, 0)),
            pl.BlockSpec((N_HEADS, 1), lambda b, i: (0, 0)),
        ],
        out_specs=[
            pl.BlockSpec((tb, GROUP_W), lambda b, i: (b * nb + i, 0)),
            pl.BlockSpec((None, GROUP_W, GROUP_W), lambda b, i: (b, 0, 0)),
        ],
        out_shape=[jax.ShapeDtypeStruct((n_batch * seq, GROUP_W), f32),
                   jax.ShapeDtypeStruct((n_batch, GROUP_W, GROUP_W), f32)],
        scratch_shapes=[pltpu.VMEM((tb + 8, 3 * GROUP_W), f32), pltpu.VMEM((GROUP_W, GROUP_W), f32)],
        compiler_params=_cparams("arbitrary", "arbitrary"),
        name="gdn_prompt",
    )(p, p, gt, conv_w.astype(f32), r14(a_log), r14(dt_bias), c41(a_log), c41(dt_bias))
    return o, _bd_diag(s_bd)


def _bd_diag(s_bd):
    n_b = s_bd.shape[0]
    s5 = s_bd.reshape(n_b, N_HEADS, HEAD_W, N_HEADS, HEAD_W)
    return jnp.stack([s5[:, h, :, h, :] for h in range(N_HEADS)], axis=1)


def _hgrn_kernel(q_ref, f_ref, i_ref, lb_ref, o_ref, s_out_ref, st_scr, q_scr, k_scr, b_scr, *, tb):
    blk = pl.program_id(1)

    @pl.when(blk == 0)
    def _():
        st_scr[...] = jnp.zeros(st_scr.shape, f32)

    lb = lb_ref[...]
    z = f_ref[...]
    logf = jnp.log(lb + (1.0 - lb) * jax.nn.sigmoid(z))
    q_scr[...] = _silu(q_ref[...])
    k_scr[...] = (1.0 - lb) * jax.nn.sigmoid(-z)
    b_scr[...] = _mm2l(_block_tri(tb, SUB), logf)

    same, _, _ = _bd_masks()
    ones_bd = _group_ones(GROUP_W, HEAD_W)
    row = _iota((SUB * SUB, GROUP_W), 0)
    tmask = (row % SUB) >= (row // SUB)

    def rep_t(x):
        return jnp.broadcast_to(x[None], (SUB, SUB, GROUP_W)).reshape(SUB * SUB, GROUP_W)

    def rep_j(x):
        return jnp.broadcast_to(x[:, None, :], (SUB, SUB, GROUP_W)).reshape(SUB * SUB, GROUP_W)

    def body(c, carry):
        r0 = pl.multiple_of(c * SUB, SUB)
        qs = q_scr[pl.ds(r0, SUB), :]
        ks = k_scr[pl.ds(r0, SUB), :]
        vs = i_ref[pl.ds(r0, SUB), :]
        bs = b_scr[pl.ds(r0, SUB), :]
        st = st_scr[...]
        o_inter = _mm1(qs * jnp.exp(bs), st, NT)
        wgt = rep_t(qs) * jnp.exp(jnp.where(tmask, rep_t(bs) - rep_j(bs), NEG)) * rep_j(ks)
        a = _mm2(wgt, ones_bd)
        o_diag = jnp.sum((a * rep_j(vs)).reshape(SUB, SUB, GROUP_W), axis=0)
        o_ref[pl.ds(r0, SUB), :] = o_inter + o_diag
        b_last = bs[SUB - 1:SUB, :]
        kw = ks * jnp.exp(b_last - bs)
        st_scr[...] = st * jnp.exp(b_last) + jnp.where(same, _mm1(vs, kw, TN), 0.0)
        return carry

    lax.fori_loop(0, tb // SUB, body, 0)

    @pl.when(blk == pl.num_programs(1) - 1)
    def _():
        s_out_ref[...] = st_scr[...]


def _hgrn_prompt(p, lb, n_batch, seq):
    tb = min(256, seq)
    nb = seq // tb
    blk = lambda col: pl.BlockSpec((tb, GROUP_W), lambda b, i: (b * nb + i, col))
    o, st = pl.pallas_call(
        functools.partial(_hgrn_kernel, tb=tb),
        grid=(n_batch, nb),
        in_specs=[blk(7), blk(8), blk(9), pl.BlockSpec((1, GROUP_W), lambda b, i: (0, 0))],
        out_specs=[
            pl.BlockSpec((tb, GROUP_W), lambda b, i: (b * nb + i, 0)),
            pl.BlockSpec((None, GROUP_W, GROUP_W), lambda b, i: (b, 0, 0)),
        ],
        out_shape=[jax.ShapeDtypeStruct((n_batch * seq, GROUP_W), f32),
                   jax.ShapeDtypeStruct((n_batch, GROUP_W, GROUP_W), f32)],
        scratch_shapes=[pltpu.VMEM((GROUP_W, GROUP_W), f32)] + [pltpu.VMEM((tb, GROUP_W), f32)] * 3,
        compiler_params=_cparams("arbitrary", "arbitrary"),
        name="hgrn_prompt",
    )(p, p, p, lb.astype(f32).reshape(1, GROUP_W))
    return o, jnp.swapaxes(_bd_diag(st), -1, -2)


def _log_sigmoid(x):
    return jnp.minimum(x, 0.0) - jnp.log1p(jnp.exp(-jnp.abs(x)))


def _mlstm_kernel(q_ref, k_ref, v_ref, gc_ref, gr_ref, ibr_ref, fbr_ref, ibc_ref, fbc_ref,
                  o_ref, c_out_ref, n_out_ref, m_out_ref, c_scr, n_scr, m_scr, *, tb):
    blk = pl.program_id(1)

    @pl.when(blk == 0)
    def _():
        c_scr[...] = jnp.zeros(c_scr.shape, f32)
        n_scr[...] = jnp.zeros(n_scr.shape, f32)
        m_scr[...] = jnp.zeros(m_scr.shape, f32)

    q = q_ref[...]
    k = k_ref[...] * (HEAD_W ** -0.5)
    v = v_ref[...]
    gc = gc_ref[...]
    li_col = gc[:, 8:12] + ibr_ref[...]
    lf_col = _log_sigmoid(gc[:, 12:16] + fbr_ref[...])
    gr = gr_ref[...]
    li_row = gr[8:12, :] + ibc_ref[...]
    lf_row = _log_sigmoid(gr[12:16, :] + fbc_ref[...])
    b_row = _mm2(jnp.concatenate([lf_row, jnp.zeros_like(lf_row)], axis=0), _block_tri_t(tb, CHUNK))

    same, lower, _ = _bd_masks()
    tri = _tri(CHUNK, bf16)

    for ch in range(tb // CHUNK):
        lo = ch * CHUNK
        qc, kc, vc = q[lo:lo + CHUNK], k[lo:lo + CHUNK], v[lo:lo + CHUNK]
        b_col = _mm2l(tri, lf_col[lo:lo + CHUNK])
        b_stack = _stack_cols(b_col)
        d_mat = jnp.where(lower, b_stack - _cat_rows(b_row, lo) + _cat_rows(li_row, lo), NEG)
        m_row = m_scr[...]
        m_stack = jnp.concatenate(
            [jnp.broadcast_to(m_row[:, h:h + 1], (CHUNK, 1)) for h in range(N_HEADS)], axis=0)
        inter = b_stack + m_stack
        m_t = jnp.maximum(inter, jnp.max(d_mat, axis=-1, keepdims=True))
        w_inter = jnp.exp(inter - m_t)
        qsm = _head_stack(qc)
        ksm = _head_stack(kc)
        pmat = _mm1(qsm, ksm, NT) * jnp.exp(d_mat - m_t)
        c_bd = c_scr[...]
        n_row = n_scr[...]
        num = w_inter * _mm1(qsm, c_bd) + _mm1(pmat, _head_stack(vc))
        den = w_inter * jnp.sum(qsm * n_row, axis=-1, keepdims=True) + jnp.sum(pmat, axis=-1, keepdims=True)
        h_sm = num / jnp.maximum(jnp.abs(den), jnp.exp(-m_t))
        o_ref[lo:lo + CHUNK, :] = _fold_heads(h_sm)
        m_new = jnp.concatenate(
            [m_t[h * CHUNK + CHUNK - 1:h * CHUNK + CHUNK, :] for h in range(N_HEADS)], axis=1)
        b_last = b_col[CHUNK - 1:CHUNK, :]
        w_end = jnp.exp(b_last - b_col + li_col[lo:lo + CHUNK] - m_new)
        d0 = jnp.exp(b_last + m_row - m_new)
        kw = kc * _expand_cols(w_end)
        d0_stack = jnp.concatenate(
            [jnp.broadcast_to(d0[:, h:h + 1], (HEAD_W, 1)) for h in range(N_HEADS)], axis=0)
        c_scr[...] = d0_stack * c_bd + jnp.where(same, _mm1(kw, vc, TN), 0.0)
        n_scr[...] = _expand_cols(d0) * n_row + jnp.sum(kw, axis=0, keepdims=True)
        m_scr[...] = m_new

    @pl.when(blk == pl.num_programs(1) - 1)
    def _():
        c_out_ref[...] = c_scr[...]
        n_out_ref[...] = n_scr[...]
        m_out_ref[...] = m_scr[...]


def _mlstm_prompt(p, gt, i_bias, f_bias, n_batch, seq):
    tb = min(256, seq)
    nb = seq // tb
    blk = lambda col: pl.BlockSpec((tb, GROUP_W), lambda b, i: (b * nb + i, col))
    r14 = lambda z: z.astype(f32).reshape(1, N_HEADS)
    c41 = lambda z: z.astype(f32).reshape(N_HEADS, 1)
    small = lambda shape: pl.BlockSpec(shape, lambda b, i: (0, 0))
    o, c_bd, n_row, m_row = pl.pallas_call(
        functools.partial(_mlstm_kernel, tb=tb),
        grid=(n_batch, nb),
        in_specs=[blk(11), blk(12), blk(13),
                  pl.BlockSpec((tb, 128), lambda b, i: (b * nb + i, GATE_COL // 128)),
                  pl.BlockSpec((16, tb), lambda b, i: (0, b * nb + i)),
                  small((1, N_HEADS)), small((1, N_HEADS)), small((N_HEADS, 1)), small((N_HEADS, 1))],
        out_specs=[
            pl.BlockSpec((tb, GROUP_W), lambda b, i: (b * nb + i, 0)),
            pl.BlockSpec((None, GROUP_W, GROUP_W), lambda b, i: (b, 0, 0)),
            pl.BlockSpec((None, 1, GROUP_W), lambda b, i: (b, 0, 0)),
            pl.BlockSpec((None, 1, N_HEADS), lambda b, i: (b, 0, 0)),
        ],
        out_shape=[jax.ShapeDtypeStruct((n_batch * seq, GROUP_W), f32),
                   jax.ShapeDtypeStruct((n_batch, GROUP_W, GROUP_W), f32),
                   jax.ShapeDtypeStruct((n_batch, 1, GROUP_W), f32),
                   jax.ShapeDtypeStruct((n_batch, 1, N_HEADS), f32)],
        scratch_shapes=[pltpu.VMEM((GROUP_W, GROUP_W), f32), pltpu.VMEM((1, GROUP_W), f32),
                        pltpu.VMEM((1, N_HEADS), f32)],
        compiler_params=_cparams("arbitrary", "arbitrary"),
        name="mlstm_prompt",
    )(p, p, p, p, gt, r14(i_bias), r14(f_bias), c41(i_bias), c41(f_bias))
    return (o, _bd_diag(c_bd), n_row.reshape(n_batch, N_HEADS, HEAD_W), m_row.reshape(n_batch, N_HEADS))


def _gdn_dec_prep_kernel(u_ref, buf_ref, cw_ref, q_ref, k_ref, v_ref):
    w = cw_ref[...]
    conv = u_ref[...] * w[3:4]
    for jj in range(CONV_W - 1):
        conv = conv + buf_ref[jj] * w[jj:jj + 1]
    qkv = _silu(conv)

    def l2n(x):
        return x * lax.rsqrt(_group_sum(x * x, HEAD_W) + EPS)

    q_ref[...] = l2n(qkv[:, 0:GROUP_W]) * (HEAD_W ** -0.5)
    k_ref[...] = l2n(qkv[:, GROUP_W:2 * GROUP_W])
    v_ref[...] = qkv[:, 2 * GROUP_W:3 * GROUP_W]


def _gdn_dec_prep(p, conv_buf, conv_w):
    n_b = p.shape[0]
    out = jax.ShapeDtypeStruct((n_b, GROUP_W), f32)
    return pl.pallas_call(
        _gdn_dec_prep_kernel,
        grid=(1,),
        in_specs=[pl.BlockSpec((n_b, 3 * GROUP_W), lambda i: (0, 0)),
                  pl.BlockSpec((CONV_W - 1, n_b, 3 * GROUP_W), lambda i: (0, 0, 0)),
                  pl.BlockSpec((CONV_W, 3 * GROUP_W), lambda i: (0, 0))],
        out_specs=[pl.BlockSpec((n_b, GROUP_W), lambda i: (0, 0))] * 3,
        out_shape=[out, out, out],
        compiler_params=_cparams("arbitrary"),
        name="gdn_dec_prep",
    )(p, jnp.swapaxes(conv_buf.astype(f32), 0, 1), conv_w.astype(f32))


def _rec_decode_kernel(gq_ref, gk_ref, gv_ref, ga_ref, gb_ref, al_ref, dtb_ref, sg_ref,
                       cq_ref, cf_ref, ci_ref, lbc_ref, sh_ref,
                       dq_ref, dk_ref, dv_ref, di_ref, df_ref, ib_ref, fb_ref, sc_ref, sn_ref, sm_ref,
                       oa_ref, sg_out, oc_ref, sh_out, od_ref, sc_out, sn_out, sm_out):
    q, k, v = gq_ref[...], gk_ref[...], gv_ref[...]
    s = sg_ref[...]
    g = -jnp.exp(al_ref[...]) * _softplus(ga_ref[...] + dtb_ref[...])
    eg = jnp.exp(g)
    beta = jax.nn.sigmoid(gb_ref[...])
    ks = jnp.sum(k * s, axis=1, keepdims=True)
    qs = jnp.sum(q * s, axis=1, keepdims=True)
    u = beta * (v - eg * ks)
    qk = jnp.sum(q * k, axis=1, keepdims=True)
    oa_ref[...] = eg * qs + qk * u
    sg_out[...] = eg * s + k * u

    lb = lbc_ref[...]
    z = cf_ref[...]
    logf = jnp.log(lb + (1.0 - lb) * jax.nn.sigmoid(z))
    kc = (1.0 - lb) * jax.nn.sigmoid(-z)
    qc = _silu(cq_ref[...])
    vc = ci_ref[...]
    sh = sh_ref[...]
    ef = jnp.exp(logf)
    oc_ref[...] = jnp.sum((qc * ef) * sh, axis=1, keepdims=True) + jnp.sum(qc * kc, axis=1, keepdims=True) * vc
    sh_out[...] = ef * sh + kc * vc

    qd = dq_ref[...]
    kd = dk_ref[...] * (HEAD_W ** -0.5)
    vd = dv_ref[...]
    li = di_ref[...] + ib_ref[...]
    lf = _log_sigmoid(df_ref[...] + fb_ref[...])
    m0 = sm_ref[...]
    cs = sc_ref[...]
    n0 = sn_ref[...]
    inter = lf + m0
    m_t = jnp.maximum(inter, li)
    w_inter = jnp.exp(inter - m_t)
    qkd = jnp.sum(qd * kd, axis=1, keepdims=True) * jnp.exp(li - m_t)
    num = w_inter * jnp.sum(qd * cs, axis=1, keepdims=True) + qkd * vd
    den = w_inter * jnp.sum(qd * n0, axis=1, keepdims=True) + qkd
    od_ref[...] = num / jnp.maximum(jnp.abs(den), jnp.exp(-m_t))
    w_end = jnp.exp(li - m_t)
    d0 = jnp.exp(lf + m0 - m_t)
    sc_out[...] = d0 * cs + (w_end * kd) * vd
    sn_out[...] = d0 * n0 + w_end * kd
    sm_out[...] = m_t


def _rec_decode(p, gq, gk, gv, a_log, dt_bias, lb, i_bias, f_bias, s_gdn, s_hgrn, s_c, s_n, s_m):
    n_b = p.shape[0]
    rows = n_b * N_HEADS
    rb = min(16, rows)
    col = lambda z: z.reshape(rows, HEAD_W, 1)
    vrow = lambda z: z.reshape(rows, 1, HEAD_W)
    sca = lambda z: z.reshape(rows, 1, 1)
    per_head = lambda z: jnp.tile(z.astype(f32), n_b).reshape(rows, 1, 1)
    blockp = lambda b: p[:, b * GROUP_W:(b + 1) * GROUP_W]
    gates = p[:, GATE_COL:GATE_COL + 16]
    lb_col = jnp.tile(lb.astype(f32).reshape(N_HEADS, HEAD_W), (n_b, 1)).reshape(rows, HEAD_W, 1)
    st = lambda z: z.astype(f32).reshape(rows, HEAD_W, HEAD_W)
    args = [col(gq), col(gk), vrow(gv), sca(gates[:, 0:4]), sca(gates[:, 4:8]), per_head(a_log), per_head(dt_bias), st(s_gdn),
            col(blockp(7)), col(blockp(8)), vrow(blockp(9)), lb_col, st(s_hgrn),
            col(blockp(11)), col(blockp(12)), vrow(blockp(13)), sca(gates[:, 8:12]), sca(gates[:, 12:16]),
            per_head(i_bias), per_head(f_bias), st(s_c), col(s_n.astype(f32)), sca(s_m.astype(f32))]

    def spec(a):
        return pl.BlockSpec((rb,) + a.shape[1:], lambda i: (i, 0, 0))

    o_vrow = jax.ShapeDtypeStruct((rows, 1, HEAD_W), f32)
    o_st = jax.ShapeDtypeStruct((rows, HEAD_W, HEAD_W), f32)
    o_col = jax.ShapeDtypeStruct((rows, HEAD_W, 1), f32)
    o_sca = jax.ShapeDtypeStruct((rows, 1, 1), f32)
    outs = [o_vrow, o_st, o_vrow, o_st, o_vrow, o_st, o_col, o_sca]
    res = pl.pallas_call(
        _rec_decode_kernel,
        grid=(rows // rb,),
        in_specs=[spec(a) for a in args],
        out_specs=[spec(a) for a in outs],
        out_shape=outs,
        compiler_params=_cparams("arbitrary"),
        name="rec_decode",
    )(*args)
    oa, sg, oc, sh, od, sc, sn, sm = res
    s4 = lambda z: z.reshape(n_b, N_HEADS, HEAD_W, HEAD_W)
    o2 = lambda z: z.reshape(n_b, GROUP_W)
    return (o2(oa), s4(sg), o2(oc), s4(sh), o2(od), s4(sc),
            sn.reshape(n_b, N_HEADS, HEAD_W), sm.reshape(n_b, N_HEADS))


def _permute_w_in(w):
    d_in = w.shape[1]
    a_gate0 = 3 * GROUP_W
    d_gate0 = d_in - GROUP_W - 8
    main = jnp.concatenate([w[:, 0:a_gate0], w[:, a_gate0 + 8:d_gate0], w[:, d_gate0 + 8:]], axis=1)
    gates = jnp.concatenate([w[:, a_gate0:a_gate0 + 8], w[:, d_gate0:d_gate0 + 8]], axis=1)
    pad = jnp.zeros((w.shape[0], P_COLS - main.shape[1] - 16), w.dtype)
    return jnp.concatenate([main, gates, pad], axis=1).astype(bf16), gates.T.astype(bf16)


def kernel(x_prompt, x_sample, page_table, cache_k, cache_v, state_gdn_conv, state_gdn, state_hgrn, state_mlstm_C, state_mlstm_n, state_mlstm_m, attn_norm_g, w_in, gdn_conv_w, gdn_a_log, gdn_dt_bias, gdn_norm_g, diff_qk_norm_g, diff_lambda, diff_subln_g, rel_bias, hgrn_lb_logits, hgrn_norm_g, mlstm_i_bias, mlstm_f_bias, mlstm_norm_g, w_out, ffn_norm_g, ffn_w_gate, ffn_w_up, ffn_w_down, moe_router, moe_w_gate, moe_w_up, moe_w_down):
    depth = w_in.shape[0]
    n_bp, seq, _ = x_prompt.shape
    n_bs = x_sample.shape[0]
    n_pool, page = cache_k.shape[1], cache_k.shape[2]
    dt = x_prompt.dtype

    lb_p = jax.nn.softmax(hgrn_lb_logits.astype(f32), axis=0)
    lb_cum = jnp.cumsum(lb_p, axis=0)
    hgrn_lb = lb_cum - lb_cum[0:1]
    cache_k4 = jnp.transpose(cache_k, (0, 1, 3, 4, 5, 2)).reshape(depth, n_pool, GROUP_W, page)
    cache_v4 = jnp.transpose(cache_v, (0, 1, 3, 4, 2)).reshape(depth, n_pool, GROUP_W, page)

    xp = x_prompt.reshape(n_bp * seq, D_MODEL)
    xs = x_sample.reshape(n_bs, D_MODEL)
    outs_p, outs_s = [], []
    for l in range(depth):
        w_perm, w_gate_t = _permute_w_in(w_in[l])
        w_out_b = w_out[l].astype(bf16)
        gains = jnp.stack([jnp.tile(g.astype(f32), N_HEADS) for g in
                           (gdn_norm_g[l], diff_subln_g[l], hgrn_norm_g[l], mlstm_norm_g[l])])
        lam_init = 0.8 - 0.6 * math.exp(-0.3 * l)
        lam32 = diff_lambda[l].astype(f32)
        lam = jnp.exp(jnp.sum(lam32[0] * lam32[1])) - jnp.exp(jnp.sum(lam32[2] * lam32[3])) + lam_init
        if l % 2 == 0:
            ffn_w = (ffn_w_gate[l // 2].astype(bf16), ffn_w_up[l // 2].astype(bf16), ffn_w_down[l // 2].astype(bf16))
        else:
            router_pad = jnp.pad(moe_router[l // 2].astype(bf16), ((0, 0), (0, 128 - N_EXPERTS)))
            moe_w = (moe_w_gate[l // 2].astype(bf16), moe_w_up[l // 2].astype(bf16), moe_w_down[l // 2].astype(bf16))

        def channel_mix(x):
            if l % 2 == 0:
                return _ffn(x, ffn_norm_g[l], *ffn_w)
            return _moe(x, ffn_norm_g[l], router_pad, *moe_w)

        p, gt = _inproj(xp, attn_norm_g[l], w_perm, w_gate_t)
        qnt, kn, vt = _bprep(p, diff_qk_norm_g[l], True)
        ob = _attn_prompt(qnt, kn, vt, lam, rel_bias, n_bp, seq)
        oa, s_gdn = _gdn_prompt(p, gt, gdn_conv_w[l], gdn_a_log[l], gdn_dt_bias[l], n_bp, seq)
        oc, s_hgrn = _hgrn_prompt(p, hgrn_lb[l], n_bp, seq)
        od, s_c, s_n, s_m = _mlstm_prompt(p, gt, mlstm_i_bias[l], mlstm_f_bias[l], n_bp, seq)
        xp = _outproj(oa, ob, oc, od, p, xp, gains, w_out_b, 1.0 - lam_init)
        xp = channel_mix(xp)
        p3 = p.reshape(n_bp, seq, P_COLS)
        outs_p.append((
            kn.reshape(n_bp, seq, N_HEADS, 2, DKB).astype(dt),
            p3[:, :, 6 * GROUP_W:7 * GROUP_W].reshape(n_bp, seq, N_HEADS, HEAD_W).astype(dt),
            p3[:, seq - (CONV_W - 1):, 0:3 * GROUP_W].astype(dt),
            s_gdn.astype(dt), s_hgrn.astype(dt), s_c.astype(dt), s_n.astype(dt), s_m.astype(dt)))

        p, gt = _inproj(xs, attn_norm_g[l], w_perm, w_gate_t)
        qn, kn = _bprep(p, diff_qk_norm_g[l], False)
        vn = p[:, 6 * GROUP_W:7 * GROUP_W]
        ob = _attn_decode(qn, kn, vn, page_table, cache_k4, cache_v4, l, lam, rel_bias)
        u = p[:, 0:3 * GROUP_W]
        gq, gk, gv = _gdn_dec_prep(u, state_gdn_conv[l], gdn_conv_w[l])
        oa, s_gdn, oc, s_hgrn, od, s_c, s_n, s_m = _rec_decode(
            p, gq, gk, gv, gdn_a_log[l], gdn_dt_bias[l], hgrn_lb[l], mlstm_i_bias[l], mlstm_f_bias[l],
            state_gdn[l], state_hgrn[l], state_mlstm_C[l], state_mlstm_n[l], state_mlstm_m[l])
        xs = _outproj(oa, ob, oc, od, p, xs, gains, w_out_b, 1.0 - lam_init)
        xs = channel_mix(xs)
        conv_new = jnp.concatenate([state_gdn_conv[l][:, 1:].astype(dt), u[:, None, :].astype(dt)], axis=1)
        outs_s.append((
            kn.reshape(n_bs, 1, N_HEADS, 2, DKB).astype(dt),
            vn.reshape(n_bs, 1, N_HEADS, HEAD_W).astype(dt),
            conv_new, s_gdn.astype(dt), s_hgrn.astype(dt), s_c.astype(dt), s_n.astype(dt), s_m.astype(dt)))

    kp, vp, convp, gdnp, hgrnp, mcp, mnp_, mmp = [jnp.stack(z) for z in zip(*outs_p)]
    ks_, vs_, convs, gdns, hgrns, mcs, mns, mms = [jnp.stack(z) for z in zip(*outs_s)]
    return (xp.reshape(n_bp, seq, D_MODEL), xs.reshape(n_bs, 1, D_MODEL), kp, vp, ks_, vs_, convp, convs,
            gdnp, gdns, hgrnp, hgrns, mcp, mcs, mnp_, mns, mmp, mms)
```

```python
import functools
import math

import numpy as np
import jax
import jax.numpy as jnp
from jax import lax
from jax.experimental import pallas as pl
from jax.experimental.pallas import tpu as pltpu

f32 = jnp.float32
bf16 = jnp.bfloat16

D_MODEL = 1024
N_HEADS = 4
HEAD_W = 64
GROUP_W = N_HEADS * HEAD_W
DKB = 32
CONV_W = 4
CHUNK = 64
SUB = 16
NUM_BUCKETS = 32
MAX_DISTANCE = 128
N_EXPERTS = 8
EPS = 1e-6
NEG = -1e30
P_COLS = 4096
GATE_COL = 3840
VMEM_LIMIT = 56 * 1024 * 1024

NN = ((1,), (0,))
NT = ((1,), (1,))
TN = ((0,), (0,))


def _dg(a, b, dims=NN):
    return lax.dot_general(a, b, (dims, ((), ())), preferred_element_type=f32)


def _split(a):
    hi = a.astype(bf16)
    lo = (a - hi.astype(f32)).astype(bf16)
    return hi, lo


def _mm3(a, b, dims=NN):
    ah, al = _split(a)
    bh, bl = _split(b)
    return _dg(ah, bh, dims) + (_dg(ah, bl, dims) + _dg(al, bh, dims))


def _mm2(a, b01, dims=NN):
    ah, al = _split(a)
    return _dg(ah, b01, dims) + _dg(al, b01, dims)


def _mm2l(a01, b, dims=NN):
    bh, bl = _split(b)
    return _dg(a01, bh, dims) + _dg(a01, bl, dims)


def _mm1(a, b, dims=NN):
    return _dg(a.astype(bf16), b.astype(bf16), dims)


def _iota(shape, dim):
    return lax.broadcasted_iota(jnp.int32, shape, dim)


def _group_ones(width, group):
    r = _iota((width, width), 0) // group
    c = _iota((width, width), 1) // group
    return (r == c).astype(bf16)


def _group_sum(x, group):
    ones = _group_ones(x.shape[-1], group)
    hi = x.astype(bf16)
    r1 = x - hi.astype(f32)
    mid = r1.astype(bf16)
    lo = (r1 - mid.astype(f32)).astype(bf16)
    return _dg(hi, ones) + (_dg(mid, ones) + _dg(lo, ones))


def _recip(x):
    r = 1.0 / x
    return r * (2.0 - x * r)


def _silu(x):
    return x * jax.nn.sigmoid(x)


def _softplus(x):
    return jnp.maximum(x, 0.0) + jnp.log1p(jnp.exp(-jnp.abs(x)))


def _stack_cols(xc, n=N_HEADS, rows=HEAD_W):
    return jnp.concatenate([xc[:, h:h + 1] for h in range(n)], axis=0)


def _expand_cols(xc, n=N_HEADS, width=HEAD_W):
    r = xc.shape[0]
    return jnp.concatenate([jnp.broadcast_to(xc[:, h:h + 1], (r, width)) for h in range(n)], axis=1)


def _cat_rows(xr, lo, n=N_HEADS, width=HEAD_W):
    return jnp.concatenate([xr[h:h + 1, lo:lo + width] for h in range(n)], axis=1)


def _head_stack(x, n=N_HEADS, width=HEAD_W):
    lane_head = _iota(x.shape, 1) // width
    return jnp.concatenate([jnp.where(lane_head == h, x, 0.0) for h in range(n)], axis=0)


def _fold_heads(x_sm, n=N_HEADS):
    r = x_sm.shape[0] // n
    out = x_sm[0:r]
    for h in range(1, n):
        out = out + x_sm[h * r:(h + 1) * r]
    return out


def _bd_masks(n=GROUP_W, blk=CHUNK):
    r = _iota((n, n), 0)
    c = _iota((n, n), 1)
    same = (r // blk) == (c // blk)
    lower = same & ((r % blk) >= (c % blk))
    strict = same & ((r % blk) > (c % blk))
    return same, lower, strict


def _cparams(*sem):
    return pltpu.CompilerParams(dimension_semantics=sem, vmem_limit_bytes=VMEM_LIMIT)


def _inproj_kernel(x_ref, g_ref, w_ref, wgt_ref, p_ref, gt_ref, h_scr):
    @pl.when(pl.program_id(1) == 0)
    def _():
        x = x_ref[...]
        ms = jnp.mean(x * x, axis=-1, keepdims=True)
        h = ((x * lax.rsqrt(ms + EPS)) * g_ref[...]).astype(bf16)
        h_scr[...] = h
        gt_ref[...] = _dg(wgt_ref[...], h, NT)
    p_ref[...] = _dg(h_scr[...], w_ref[...], NN)


def _inproj(x, g, w_perm, w_gate_t):
    t = x.shape[0]
    tm = min(1024, t)
    tn = 1024
    return pl.pallas_call(
        _inproj_kernel,
        grid=(t // tm, P_COLS // tn),
        in_specs=[
            pl.BlockSpec((tm, D_MODEL), lambda i, j: (i, 0)),
            pl.BlockSpec((1, D_MODEL), lambda i, j: (0, 0)),
            pl.BlockSpec((D_MODEL, tn), lambda i, j: (0, j)),
            pl.BlockSpec((16, D_MODEL), lambda i, j: (0, 0)),
        ],
        out_specs=[
            pl.BlockSpec((tm, tn), lambda i, j: (i, j)),
            pl.BlockSpec((16, tm), lambda i, j: (0, i)),
        ],
        out_shape=[jax.ShapeDtypeStruct((t, P_COLS), f32), jax.ShapeDtypeStruct((16, t), f32)],
        scratch_shapes=[pltpu.VMEM((tm, D_MODEL), bf16)],
        compiler_params=_cparams("arbitrary", "arbitrary"),
        name="inproj",
    )(x, g.reshape(1, D_MODEL), w_perm, w_gate_t)


def _qk_gnorm(x, g):
    ms = _group_sum(x * x, DKB) * (1.0 / DKB)
    return (x * lax.rsqrt(ms + EPS)) * g


def _bprep_kernel(q_ref, k_ref, gq_ref, gk_ref, qn_ref, kn_ref):
    qn_ref[...] = _qk_gnorm(q_ref[...], gq_ref[...])
    kn_ref[...] = _qk_gnorm(k_ref[...], gk_ref[...])


def _bprep_t_kernel(q_ref, k_ref, v_ref, gq_ref, gk_ref, qnt_ref, kn_ref, vt_ref):
    qnt_ref[...] = _qk_gnorm(q_ref[...], gq_ref[...]).T
    kn_ref[...] = _qk_gnorm(k_ref[...], gk_ref[...])
    vt_ref[...] = v_ref[...].T


def _bprep(p, qk_norm_g, transposed):
    t = p.shape[0]
    tm = min(512, t)
    gq = jnp.tile(qk_norm_g[0], GROUP_W // DKB).reshape(1, GROUP_W)
    gk = jnp.tile(qk_norm_g[1], GROUP_W // DKB).reshape(1, GROUP_W)
    col = lambda c: pl.BlockSpec((tm, GROUP_W), lambda i: (i, c))
    gain = pl.BlockSpec((1, GROUP_W), lambda i: (0, 0))
    rows = pl.BlockSpec((tm, GROUP_W), lambda i: (i, 0))
    rows_t = pl.BlockSpec((GROUP_W, tm), lambda i: (0, i))
    if transposed:
        return pl.pallas_call(
            _bprep_t_kernel,
            grid=(t // tm,),
            in_specs=[col(4), col(5), col(6), gain, gain],
            out_specs=[rows_t, rows, rows_t],
            out_shape=[jax.ShapeDtypeStruct((GROUP_W, t), f32), jax.ShapeDtypeStruct((t, GROUP_W), f32),
                       jax.ShapeDtypeStruct((GROUP_W, t), f32)],
            compiler_params=_cparams("arbitrary"),
            name="bprep_t",
        )(p, p, p, gq, gk)
    return pl.pallas_call(
        _bprep_kernel,
        grid=(t // tm,),
        in_specs=[col(4), col(5), gain, gain],
        out_specs=[rows, rows],
        out_shape=[jax.ShapeDtypeStruct((t, GROUP_W), f32)] * 2,
        compiler_params=_cparams("arbitrary"),
        name="bprep",
    )(p, p, gq, gk)


def _outproj_kernel(oa_ref, ob_ref, oc_ref, od_ref, ag_ref, cg_ref, dg_ref, x_ref, g_ref, w_ref, y_ref, *, b_scale):
    def gnorm(x, g):
        ms = _group_sum(x * x, HEAD_W) * (1.0 / HEAD_W)
        return (x * lax.rsqrt(ms + EPS)) * g
    g = g_ref[...]
    mixes = (
        gnorm(oa_ref[...], g[0:1]) * _silu(ag_ref[...]),
        gnorm(ob_ref[...], g[1:2]) * b_scale,
        gnorm(oc_ref[...], g[2:3]) * jax.nn.sigmoid(cg_ref[...]),
        gnorm(od_ref[...], g[3:4]) * jax.nn.sigmoid(dg_ref[...]),
    )
    y = x_ref[...]
    for i, m in enumerate(mixes):
        y = y + _dg(m.astype(bf16), w_ref[i * GROUP_W:(i + 1) * GROUP_W, :], NN)
    y_ref[...] = y


def _outproj(oa, ob, oc, od, p, x, gains, w_out, b_scale):
    t = x.shape[0]
    tm = min(512, t)
    row = lambda i: (i, 0)
    return pl.pallas_call(
        functools.partial(_outproj_kernel, b_scale=b_scale),
        grid=(t // tm,),
        in_specs=[
            pl.BlockSpec((tm, GROUP_W), row), pl.BlockSpec((tm, GROUP_W), row),
            pl.BlockSpec((tm, GROUP_W), row), pl.BlockSpec((tm, GROUP_W), row),
            pl.BlockSpec((tm, GROUP_W), lambda i: (i, 3)),
            pl.BlockSpec((tm, GROUP_W), lambda i: (i, 10)),
            pl.BlockSpec((tm, GROUP_W), lambda i: (i, 14)),
            pl.BlockSpec((tm, D_MODEL), row),
            pl.BlockSpec((4, GROUP_W), lambda i: (0, 0)),
            pl.BlockSpec((D_MODEL, D_MODEL), lambda i: (0, 0)),
        ],
        out_specs=pl.BlockSpec((tm, D_MODEL), row),
        out_shape=jax.ShapeDtypeStruct((t, D_MODEL), f32),
        compiler_params=_cparams("arbitrary"),
        name="outproj",
    )(oa, ob, oc, od, p, p, p, x, gains, w_out)


def _ffn_kernel(x_ref, g_ref, wg_ref, wu_ref, wd_ref, y_ref, h_scr, acc_scr):
    f = pl.program_id(1)

    @pl.when(f == 0)
    def _():
        x = x_ref[...]
        ms = jnp.mean(x * x, axis=-1, keepdims=True)
        h_scr[...] = ((x * lax.rsqrt(ms + EPS)) * g_ref[...]).astype(bf16)
        acc_scr[...] = x

    h = h_scr[...]
    a = _silu(_dg(h, wg_ref[...])) * _dg(h, wu_ref[...])
    acc_scr[...] += _dg(a.astype(bf16), wd_ref[...])

    @pl.when(f == pl.num_programs(1) - 1)
    def _():
        y_ref[...] = acc_scr[...]


def _ffn(x, g, wg, wu, wd):
    t = x.shape[0]
    d_ff = wg.shape[1]
    tm = min(512, t)
    tf = d_ff // 2
    return pl.pallas_call(
        _ffn_kernel,
        grid=(t // tm, d_ff // tf),
        in_specs=[
            pl.BlockSpec((tm, D_MODEL), lambda i, f: (i, 0)),
            pl.BlockSpec((1, D_MODEL), lambda i, f: (0, 0)),
            pl.BlockSpec((D_MODEL, tf), lambda i, f: (0, f)),
            pl.BlockSpec((D_MODEL, tf), lambda i, f: (0, f)),
            pl.BlockSpec((tf, D_MODEL), lambda i, f: (f, 0)),
        ],
        out_specs=pl.BlockSpec((tm, D_MODEL), lambda i, f: (i, 0)),
        out_shape=jax.ShapeDtypeStruct((t, D_MODEL), f32),
        scratch_shapes=[pltpu.VMEM((tm, D_MODEL), bf16), pltpu.VMEM((tm, D_MODEL), f32)],
        compiler_params=_cparams("arbitrary", "arbitrary"),
        name="ffn",
    )(x, g.reshape(1, D_MODEL), wg, wu, wd)


LANE = 128
SUBL = D_MODEL // LANE
ROW_TILE = 256
SPARSE_MIN_TOKENS = 4096


def _tile_rows(x):
    return [x[:, j * LANE:(j + 1) * LANE] for j in range(SUBL)]


def _router_kernel(x_ref, g_ref, r_ref, h3_ref, meta_ref, cnt_ref, carry_scr):
    @pl.when(pl.program_id(0) == 0)
    def _():
        carry_scr[...] = jnp.zeros(carry_scr.shape, f32)

    x = x_ref[...]
    tm = x.shape[0]
    ms = jnp.mean(x * x, axis=-1, keepdims=True)
    h = (x * lax.rsqrt(ms + EPS)) * g_ref[...]
    for j, blk in enumerate(_tile_rows(h)):
        h3_ref[:, j, :] = blk
    logits = _dg(h.astype(bf16), r_ref[...])
    lane = _iota(logits.shape, 1)
    logits = jnp.where(lane < N_EXPERTS, logits, -jnp.inf)
    v1 = jnp.max(logits, axis=-1, keepdims=True)
    i1 = jnp.min(jnp.where(logits == v1, lane, LANE), axis=-1, keepdims=True)
    rest = jnp.where(lane == i1, -jnp.inf, logits)
    v2 = jnp.max(rest, axis=-1, keepdims=True)
    i2 = jnp.min(jnp.where(rest == v2, lane, LANE), axis=-1, keepdims=True)
    e2 = jnp.exp(v2 - v1)
    den = 1.0 + e2
    hit = ((lane == i1) | (lane == i2)).astype(f32)
    strict = (_iota((tm, tm), 0) > _iota((tm, tm), 1)).astype(bf16)
    before = _dg(strict, hit.astype(bf16)) + carry_scr[...]
    pos1 = jnp.sum(jnp.where(lane == i1, before, 0.0), axis=-1, keepdims=True)
    pos2 = jnp.sum(jnp.where(lane == i2, before, 0.0), axis=-1, keepdims=True)
    carry_scr[...] += jnp.sum(hit, axis=0, keepdims=True)
    meta = jnp.zeros(logits.shape, f32)
    for c, val in enumerate((i1.astype(f32), i2.astype(f32), 1.0 / den, e2 / den, pos1, pos2)):
        meta = jnp.where(lane == c, val, meta)
    meta_ref[...] = meta
    cnt_ref[...] = carry_scr[...]


def _router(x, g, router_pad):
    t = x.shape[0]
    tm = min(512, t)
    return pl.pallas_call(
        _router_kernel,
        grid=(t // tm,),
        in_specs=[
            pl.BlockSpec((tm, D_MODEL), lambda i: (i, 0)),
            pl.BlockSpec((1, D_MODEL), lambda i: (0, 0)),
            pl.BlockSpec((D_MODEL, LANE), lambda i: (0, 0)),
        ],
        out_specs=[pl.BlockSpec((tm, SUBL, LANE), lambda i: (i, 0, 0)),
                   pl.BlockSpec((tm, LANE), lambda i: (i, 0)),
                   pl.BlockSpec((1, LANE), lambda i: (0, 0))],
        out_shape=[jax.ShapeDtypeStruct((t, SUBL, LANE), f32), jax.ShapeDtypeStruct((t, LANE), f32),
                   jax.ShapeDtypeStruct((1, LANE), f32)],
        scratch_shapes=[pltpu.VMEM((1, LANE), f32)],
        compiler_params=_cparams("arbitrary"),
        name="router",
    )(x, g.reshape(1, D_MODEL), router_pad)


def _swiglu_bf16(x, wg, wu, wd):
    a = _silu(_dg(x, wg)) * _dg(x, wu)
    return _dg(a.astype(bf16), wd)


def _moe_dense_kernel(x_ref, h3_ref, meta_ref, wg_ref, wu_ref, wd_ref, y_ref, acc_scr):
    e = pl.program_id(1)
    f = pl.program_id(2)

    @pl.when((e == 0) & (f == 0))
    def _():
        acc_scr[...] = x_ref[...]

    meta = meta_ref[...]
    ef = e.astype(f32)
    cw = jnp.where(meta[:, 0:1] == ef, meta[:, 2:3], 0.0) + jnp.where(meta[:, 1:2] == ef, meta[:, 3:4], 0.0)
    h = jnp.concatenate([h3_ref[:, j, :] for j in range(SUBL)], axis=1).astype(bf16)
    acc_scr[...] += cw * _swiglu_bf16(h, wg_ref[...], wu_ref[...], wd_ref[...])

    @pl.when((e == pl.num_programs(1) - 1) & (f == pl.num_programs(2) - 1))
    def _():
        y_ref[...] = acc_scr[...]


def _moe_dense(x, h3, meta, wg, wu, wd):
    t = x.shape[0]
    n_e, _, d_ff = wg.shape
    tm = min(512, t)
    tf = d_ff // 2
    return pl.pallas_call(
        _moe_dense_kernel,
        grid=(t // tm, n_e, d_ff // tf),
        in_specs=[
            pl.BlockSpec((tm, D_MODEL), lambda i, e, f: (i, 0)),
            pl.BlockSpec((tm, SUBL, LANE), lambda i, e, f: (i, 0, 0)),
            pl.BlockSpec((tm, LANE), lambda i, e, f: (i, 0)),
            pl.BlockSpec((None, D_MODEL, tf), lambda i, e, f: (e, 0, f)),
            pl.BlockSpec((None, D_MODEL, tf), lambda i, e, f: (e, 0, f)),
            pl.BlockSpec((None, tf, D_MODEL), lambda i, e, f: (e, f, 0)),
        ],
        out_specs=pl.BlockSpec((tm, D_MODEL), lambda i, e, f: (i, 0)),
        out_shape=jax.ShapeDtypeStruct((t, D_MODEL), f32),
        scratch_shapes=[pltpu.VMEM((tm, D_MODEL), f32)],
        compiler_params=_cparams("arbitrary", "arbitrary", "arbitrary"),
        name="moe_dense",
    )(x, h3, meta, wg, wu, wd)


def _route_plan(meta, counts, t):
    cnt = counts[0, :N_EXPERTS].astype(jnp.int32)
    padded = ((cnt + ROW_TILE - 1) // ROW_TILE) * ROW_TILE
    ends = jnp.cumsum(padded)
    offs = ends - padded
    experts = jnp.arange(N_EXPERTS, dtype=jnp.int32)

    def dest(expert_col, rank_col):
        e = meta[:, expert_col].astype(jnp.int32)
        off = jnp.sum(jnp.where(e[:, None] == experts[None, :], offs[None, :], 0), axis=1)
        return off + meta[:, rank_col].astype(jnp.int32)

    n_rows = 2 * t + N_EXPERTS * ROW_TILE
    starts = jnp.arange(n_rows // ROW_TILE, dtype=jnp.int32) * ROW_TILE
    tile_expert = jnp.minimum(jnp.sum((starts[:, None] >= ends[None, :]).astype(jnp.int32), axis=1), N_EXPERTS - 1)
    n_used = (ends[N_EXPERTS - 1] // ROW_TILE).reshape(1)
    return dest(0, 4), dest(1, 5), tile_expert, n_used, n_rows


def _dispatch_kernel(d1_ref, d2_ref, h3_ref, zero_hbm, xs_hbm, sem, *, tm):
    del zero_hbm
    base = pl.program_id(0) * tm

    def issue(k, carry):
        src = h3_ref.at[pl.ds(k, 1)]
        pltpu.make_async_copy(src, xs_hbm.at[pl.ds(d1_ref[base + k], 1)], sem).start()
        pltpu.make_async_copy(src, xs_hbm.at[pl.ds(d2_ref[base + k], 1)], sem).start()
        return carry

    lax.fori_loop(0, tm, issue, 0)
    for _ in range(2):
        pltpu.make_async_copy(h3_ref, xs_hbm.at[pl.ds(0, tm)], sem).wait()


def _dispatch(h3, dest1, dest2, n_rows):
    t = h3.shape[0]
    tm = min(512, t)
    grid_spec = pltpu.PrefetchScalarGridSpec(
        num_scalar_prefetch=2,
        grid=(t // tm,),
        in_specs=[pl.BlockSpec((tm, SUBL, LANE), lambda i, d1, d2: (i, 0, 0)), pl.BlockSpec(memory_space=pl.ANY)],
        out_specs=pl.BlockSpec(memory_space=pl.ANY),
        scratch_shapes=[pltpu.SemaphoreType.DMA(())],
    )
    return pl.pallas_call(
        functools.partial(_dispatch_kernel, tm=tm),
        grid_spec=grid_spec,
        out_shape=jax.ShapeDtypeStruct((n_rows, SUBL, LANE), f32),
        input_output_aliases={3: 0},
        compiler_params=_cparams("arbitrary"),
        name="moe_dispatch",
    )(dest1, dest2, h3, jnp.zeros((n_rows, SUBL, LANE), f32))


def _experts_kernel(te_ref, nu_ref, xs_ref, wg_ref, wu_ref, wd_ref, ys_ref):
    del te_ref
    r = pl.program_id(0)

    @pl.when(r < nu_ref[0])
    def _():
        x = jnp.concatenate([xs_ref[:, j, :] for j in range(SUBL)], axis=1).astype(bf16)
        for j, blk in enumerate(_tile_rows(_swiglu_bf16(x, wg_ref[...], wu_ref[...], wd_ref[...]))):
            ys_ref[:, j, :] = blk

    @pl.when(r >= nu_ref[0])
    def _():
        ys_ref[...] = jnp.zeros(ys_ref.shape, f32)


def _experts(xs, tile_expert, n_used, wg, wu, wd):
    n_rows = xs.shape[0]
    d_ff = wg.shape[2]
    rows = pl.BlockSpec((ROW_TILE, SUBL, LANE), lambda r, te, nu: (r, 0, 0))
    w_in = pl.BlockSpec((None, D_MODEL, d_ff), lambda r, te, nu: (te[r], 0, 0), pipeline_mode=pl.Buffered(1))
    w_dn = pl.BlockSpec((None, d_ff, D_MODEL), lambda r, te, nu: (te[r], 0, 0), pipeline_mode=pl.Buffered(1))
    grid_spec = pltpu.PrefetchScalarGridSpec(
        num_scalar_prefetch=2,
        grid=(n_rows // ROW_TILE,),
        in_specs=[rows, w_in, w_in, w_dn],
        out_specs=rows,
    )
    return pl.pallas_call(
        _experts_kernel,
        grid_spec=grid_spec,
        out_shape=jax.ShapeDtypeStruct((n_rows, SUBL, LANE), f32),
        compiler_params=_cparams("arbitrary"),
        name="moe_experts",
    )(tile_expert, n_used, xs, wg, wu, wd)


def _combine_kernel(d1_ref, d2_ref, x_ref, meta_ref, ys_hbm, y_ref, buf, sem, *, tm):
    base = pl.program_id(0) * tm

    def issue(k, carry):
        t = base + k
        pltpu.make_async_copy(ys_hbm.at[pl.ds(d1_ref[t], 1)], buf.at[0, pl.ds(k, 1)], sem).start()
        pltpu.make_async_copy(ys_hbm.at[pl.ds(d2_ref[t], 1)], buf.at[1, pl.ds(k, 1)], sem).start()
        return carry

    lax.fori_loop(0, tm, issue, 0)
    for s in range(2):
        pltpu.make_async_copy(ys_hbm.at[pl.ds(0, tm)], buf.at[s], sem).wait()
    meta = meta_ref[...]
    g1 = meta[:, 2:3]
    g2 = meta[:, 3:4]
    for j in range(SUBL):
        sl = slice(j * LANE, (j + 1) * LANE)
        y_ref[:, sl] = x_ref[:, sl] + (g1 * buf[0, :, j, :] + g2 * buf[1, :, j, :])


def _combine(x, meta, ys, dest1, dest2):
    t = x.shape[0]
    tm = min(256, t)
    grid_spec = pltpu.PrefetchScalarGridSpec(
        num_scalar_prefetch=2,
        grid=(t // tm,),
        in_specs=[pl.BlockSpec((tm, D_MODEL), lambda i, d1, d2: (i, 0)),
                  pl.BlockSpec((tm, LANE), lambda i, d1, d2: (i, 0)),
                  pl.BlockSpec(memory_space=pl.ANY)],
        out_specs=pl.BlockSpec((tm, D_MODEL), lambda i, d1, d2: (i, 0)),
        scratch_shapes=[pltpu.VMEM((2, tm, SUBL, LANE), f32), pltpu.SemaphoreType.DMA(())],
    )
    return pl.pallas_call(
        functools.partial(_combine_kernel, tm=tm),
        grid_spec=grid_spec,
        out_shape=jax.ShapeDtypeStruct((t, D_MODEL), f32),
        compiler_params=_cparams("arbitrary"),
        name="moe_combine",
    )(dest1, dest2, x, meta, ys)


def _moe(x, g, router_pad, wg, wu, wd):
    t = x.shape[0]
    h3, meta, counts = _router(x, g, router_pad)
    if t < SPARSE_MIN_TOKENS:
        return _moe_dense(x, h3, meta, wg, wu, wd)
    dest1, dest2, tile_expert, n_used, n_rows = _route_plan(meta, counts, t)
    xs = _dispatch(h3, dest1, dest2, n_rows)
    ys = _experts(xs, tile_expert, n_used, wg, wu, wd)
    return _combine(x, meta, ys, dest1, dest2)


def _t5_bucket_np(n):
    n = np.maximum(n, 0)
    max_exact = NUM_BUCKETS // 2
    nf = np.maximum(n, 1).astype(np.float32)
    large = max_exact + (np.log(nf / np.float32(max_exact)) / np.float32(math.log(MAX_DISTANCE / max_exact))
                         * np.float32(NUM_BUCKETS - max_exact)).astype(np.int32)
    return np.where(n < max_exact, n, np.minimum(large, NUM_BUCKETS - 1))


def _shifted_bias(rel_bias):
    rb = rel_bias.astype(f32)
    return rb - rb[NUM_BUCKETS - 1:NUM_BUCKETS]


ACC_ROWS = HEAD_W + 8
LOG2E = math.log2(math.e)


def _attn_kernel(qi_ref, kj_ref, lam_ref, qt_ref, k_ref, vt_ref, toep_ref, o_ref, qs_scr, m_scr, acc_scr, *, tq):
    p = pl.program_id(1)
    i = qi_ref[p]
    j = kj_ref[p]
    n_hc = 2 * N_HEADS
    c2 = (DKB ** -0.5) * LOG2E

    @pl.when(j == 0)
    def _():
        qt = qt_ref[...]
        row_grp = _iota(qt.shape, 0) // DKB
        for hc in range(n_hc):
            qs_scr[:, hc * tq:(hc + 1) * tq] = jnp.where(row_grp == hc, qt, 0.0).astype(bf16)
        m_scr[...] = jnp.full(m_scr.shape, NEG, f32)
        acc_scr[...] = jnp.zeros(acc_scr.shape, f32)

    def step(near):
        tk = k_ref.shape[0]
        st_all = _dg(k_ref[...].astype(bf16), qs_scr[...], NN)
        vt = vt_ref[...]
        ones = jnp.ones((ACC_ROWS - HEAD_W, tk), f32)
        for h in range(N_HEADS):
            vh = jnp.concatenate([vt[h * HEAD_W:(h + 1) * HEAD_W, :], ones], axis=0).astype(bf16)
            for hc in (2 * h, 2 * h + 1):
                s = st_all[:, hc * tq:(hc + 1) * tq] * c2
                if near:
                    s = s + toep_ref[(i - j) * N_HEADS + h]
                m_old = m_scr[hc:hc + 1, :]
                m_new = jnp.maximum(m_old, jnp.max(s, axis=0, keepdims=True))
                pexp = jnp.exp2(s - m_new)
                acc_scr[hc] = jnp.exp2(m_old - m_new) * acc_scr[hc] + _dg(vh, pexp.astype(bf16), NN)
                m_scr[hc:hc + 1, :] = m_new

    @pl.when(i - j <= 1)
    def _():
        step(True)

    @pl.when(i - j > 1)
    def _():
        step(False)

    @pl.when(j == i)
    def _():
        lam = lam_ref[0]
        outs = []
        for h in range(N_HEADS):
            a0 = acc_scr[2 * h]
            a1 = acc_scr[2 * h + 1]
            outs.append(a0[0:HEAD_W] * _recip(a0[HEAD_W:HEAD_W + 1])
                        - lam * (a1[0:HEAD_W] * _recip(a1[HEAD_W:HEAD_W + 1])))
        o_ref[...] = jnp.concatenate(outs, axis=0).T


def _toeplitz_kernel(u_ref, o_ref):
    t = o_ref.shape[0]
    rows = jnp.broadcast_to(u_ref[...], (t, 2 * t))
    o_ref[...] = pltpu.roll(rows, 0, 1, stride=1, stride_axis=0)[:, t:2 * t]


def _toeplitz_bias_tiles(rel_bias, t):
    m = np.arange(2 * t)[None, :]
    dist = m - t + np.array([0, t])[:, None]
    tab = _shifted_bias(rel_bias)
    u = jnp.take(tab, jnp.asarray(_t5_bucket_np(dist)), axis=0)
    u = jnp.where(jnp.asarray(dist >= 0)[:, :, None], u * LOG2E, NEG)
    u = jnp.transpose(u, (0, 2, 1)).reshape(2 * N_HEADS, 1, 2 * t)
    return pl.pallas_call(
        _toeplitz_kernel,
        grid=(2 * N_HEADS,),
        in_specs=[pl.BlockSpec((None, 1, 2 * t), lambda i: (i, 0, 0))],
        out_specs=pl.BlockSpec((None, t, t), lambda i: (i, 0, 0)),
        out_shape=jax.ShapeDtypeStruct((2 * N_HEADS, t, t), f32),
        compiler_params=_cparams("arbitrary"),
        name="toeplitz_bias",
    )(u)


def _attn_prompt(qnt, kn, vt, lam, rel_bias, n_batch, seq):
    tq = min(512, seq)
    nq = seq // tq
    pairs =[(i, j) for i in range(nq) for j in range(i + 1)]
    qi = jnp.asarray(np.array([a for a, _ in pairs], np.int32))
    kj = jnp.asarray(np.array([b for _, b in pairs], np.int32))
    toep = _toeplitz_bias_tiles(rel_bias, tq)
    grid_spec = pltpu.PrefetchScalarGridSpec(
        num_scalar_prefetch=2,
        grid=(n_batch, len(pairs)),
        in_specs=[
            pl.BlockSpec(memory_space=pltpu.SMEM),
            pl.BlockSpec((GROUP_W, tq), lambda b_, p_, qi_, kj_: (0, b_ * nq + qi_[p_])),
            pl.BlockSpec((tq, GROUP_W), lambda b_, p_, qi_, kj_: (b_ * nq + kj_[p_], 0)),
            pl.BlockSpec((GROUP_W, tq), lambda b_, p_, qi_, kj_: (0, b_ * nq + kj_[p_])),
            pl.BlockSpec((2 * N_HEADS, tq, tq), lambda b_, p_, qi_, kj_: (0, 0, 0)),
        ],
        out_specs=pl.BlockSpec((tq, GROUP_W), lambda b_, p_, qi_, kj_: (b_ * nq + qi_[p_], 0)),
        scratch_shapes=[
            pltpu.VMEM((GROUP_W, 2 * N_HEADS * tq), bf16),
            pltpu.VMEM((2 * N_HEADS, tq), f32),
            pltpu.VMEM((2 * N_HEADS, ACC_ROWS, tq), f32),
        ],
    )
    return pl.pallas_call(
        functools.partial(_attn_kernel, tq=tq),
        grid_spec=grid_spec,
        out_shape=jax.ShapeDtypeStruct((n_batch * seq, GROUP_W), f32),
        compiler_params=_cparams("arbitrary", "arbitrary"),
        name="attn_prompt",
    )(qi, kj, lam.reshape(1), qnt, kn, vt, toep)


def _attn_decode_kernel(pt_ref, lam_ref, q_ref, kn_ref, vn_ref, blast_ref, bself_ref, *rest, pg, n_pages):
    k_refs = rest[:pg]
    v_refs = rest[pg:2 * pg]
    o_ref, qs_scr, s_scr, v_scr = rest[2 * pg:]
    t = pl.program_id(1)
    n_steps = n_pages // pg
    n_hc = 2 * N_HEADS
    page = k_refs[0].shape[1]
    scale = DKB ** -0.5
    rnd = lambda z: z.astype(bf16).astype(f32)

    @pl.when(t == 0)
    def _():
        q = jnp.broadcast_to(q_ref[...], (n_hc, GROUP_W))
        keep = (_iota(q.shape, 1) // DKB) == _iota(q.shape, 0)
        qs_scr[...] = jnp.where(keep, q, 0.0)

    qs_b = qs_scr[...].astype(bf16)
    parts = []
    for g in range(pg):
        s = _dg(qs_b, k_refs[g][...].astype(bf16), NN) * scale
        is_last = (t * pg + g) == (n_pages - 1)
        parts.append(s + jnp.where(is_last, blast_ref[...], 0.0))
        v_scr[t * pg + g] = v_refs[g][...].astype(bf16)
    s_scr[t] = jnp.concatenate(parts, axis=1)

    @pl.when(t == n_steps - 1)
    def _():
        s_all = s_scr[...]
        s_self = jnp.sum(rnd(qs_scr[...]) * rnd(kn_ref[...]), axis=-1, keepdims=True) * scale + bself_ref[...]
        m = jnp.maximum(jnp.max(jnp.max(s_all, axis=2, keepdims=True), axis=0), s_self)
        p = jnp.exp(s_all - m)
        p_self = jnp.exp(s_self - m)
        l = jnp.sum(jnp.sum(p, axis=2, keepdims=True), axis=0) + p_self
        inv_l = _recip(l)
        pn = p * inv_l
        pn_self = p_self * inv_l
        lam = lam_ref[0]
        rows = [pn[:, 2 * h:2 * h + 1, :] - lam * pn[:, 2 * h + 1:2 * h + 2, :] for h in range(N_HEADS)]
        s_scr[...] = jnp.concatenate(rows + [jnp.zeros_like(rows[0])] * N_HEADS, axis=1)
        rows_self = [pn_self[2 * h:2 * h + 1] - lam * pn_self[2 * h + 1:2 * h + 2] for h in range(N_HEADS)]
        a_self = jnp.concatenate(rows_self + [jnp.zeros_like(rows_self[0])] * N_HEADS, axis=0)

        def weighted_values(t2, acc):
            a = s_scr[t2].astype(bf16)
            for g in range(pg):
                acc = acc + _dg(a[:, g * page:(g + 1) * page], v_scr[t2 * pg + g], NT)
            return acc

        o = lax.fori_loop(0, n_steps, weighted_values, rnd(a_self) * rnd(vn_ref[...]))
        lane_head = _iota((1, GROUP_W), 1) // HEAD_W
        out = jnp.zeros((1, GROUP_W), f32)
        for h in range(N_HEADS):
            out = jnp.where(lane_head == h, o[h:h + 1], out)
        o_ref[...] = out


def _attn_decode(qn, kn, vn, page_table, cache_k, cache_v, layer, lam, rel_bias):
    n_b, n_pages = page_table.shape
    page = cache_k.shape[3]
    pg = min(16, n_pages)
    n_steps = n_pages // pg
    past = n_pages * page
    tab = _shifted_bias(rel_bias)
    d_last = past - ((n_pages - 1) * page + np.arange(page))
    blast = jnp.repeat(jnp.take(tab, jnp.asarray(_t5_bucket_np(d_last)), axis=0).T, 2, axis=0)
    bself = jnp.repeat(tab[0].reshape(N_HEADS, 1), 2, axis=0)

    def page_spec(g):
        return pl.BlockSpec((None, None, GROUP_W, page), lambda b_, t_, pt: (layer, pt[b_, t_ * pg + g], 0, 0))

    row = pl.BlockSpec((None, 1, GROUP_W), lambda b_, t_, pt: (b_, 0, 0))
    grid_spec = pltpu.PrefetchScalarGridSpec(
        num_scalar_prefetch=1,
        grid=(n_b, n_steps),
        in_specs=[pl.BlockSpec(memory_space=pltpu.SMEM), row, row, row,
                  pl.BlockSpec((2 * N_HEADS, page), lambda b_, t_, pt: (0, 0)),
                  pl.BlockSpec((2 * N_HEADS, 1), lambda b_, t_, pt: (0, 0))]
                 + [page_spec(g) for g in range(pg)] * 2,
        out_specs=row,
        scratch_shapes=[
            pltpu.VMEM((2 * N_HEADS, GROUP_W), f32),
            pltpu.VMEM((n_steps, 2 * N_HEADS, pg * page), f32),
            pltpu.VMEM((n_pages, GROUP_W, page), bf16),
        ],
    )
    r3 = lambda z: z.reshape(n_b, 1, GROUP_W)
    out = pl.pallas_call(
        functools.partial(_attn_decode_kernel, pg=pg, n_pages=n_pages),
        grid_spec=grid_spec,
        out_shape=jax.ShapeDtypeStruct((n_b, 1, GROUP_W), f32),
        compiler_params=_cparams("arbitrary", "arbitrary"),
        name="attn_decode",
    )(page_table, lam.reshape(1), r3(qn), r3(kn), r3(vn), blast, bself,
      *([cache_k] * pg), *([cache_v] * pg))
    return out.reshape(n_b, GROUP_W)


def _tri(n, dtype=f32):
    return (_iota((n, n), 0) >= _iota((n, n), 1)).astype(dtype)


def _block_tri_t(n, blk):
    r = _iota((n, n), 0)
    c = _iota((n, n), 1)
    return (((r // blk) == (c // blk)) & (r <= c)).astype(bf16)


def _block_tri(n, blk):
    r = _iota((n, n), 0)
    c = _iota((n, n), 1)
    return (((r // blk) == (c // blk)) & (r >= c)).astype(bf16)


def _gdn_kernel(u_ref, gc_ref, gr_ref, cw_ref, alr_ref, dtr_ref, alc_ref, dtc_ref, o_ref, s_out_ref, ext_scr, s_scr, *, tb):
    i = pl.program_id(1)

    @pl.when(i == 0)
    def _():
        ext_scr[0:8, :] = jnp.zeros((8, 3 * GROUP_W), f32)
        s_scr[...] = jnp.zeros(s_scr.shape, f32)

    ext_scr[8:8 + tb, :] = u_ref[...]
    w = cw_ref[...]
    conv = ext_scr[8:8 + tb, :] * w[3:4]
    for jj in range(1, CONV_W):
        conv = conv + ext_scr[8 - jj:8 - jj + tb, :] * w[3 - jj:4 - jj]
    ext_scr[0:8, :] = ext_scr[tb:tb + 8, :]
    qkv = _silu(conv)

    def l2n(x):
        return x * lax.rsqrt(_group_sum(x * x, HEAD_W) + EPS)

    q = l2n(qkv[:, 0:GROUP_W]) * (HEAD_W ** -0.5)
    k = l2n(qkv[:, GROUP_W:2 * GROUP_W])
    v = qkv[:, 2 * GROUP_W:3 * GROUP_W]

    gc = gc_ref[...]
    g_col = -jnp.exp(alr_ref[...]) * _softplus(gc[:, 0:4] + dtr_ref[...])
    beta_col = jax.nn.sigmoid(gc[:, 4:8])
    gr = gr_ref[...]
    g_row = -jnp.exp(alc_ref[...]) * _softplus(gr[0:4, :] + dtc_ref[...])
    g_row8 = jnp.concatenate([g_row, jnp.zeros_like(g_row)], axis=0)
    gcum_row = _mm2(g_row8, _block_tri_t(tb, CHUNK))

    same, lower, strict = _bd_masks()
    tri = _tri(CHUNK, bf16)
    r = _iota((GROUP_W, GROUP_W), 0)
    c = _iota((GROUP_W, GROUP_W), 1)
    eye = (r == c).astype(f32)

    chunks = range(tb // CHUNK)
    gcums, qks, m_bds = [], [], []
    for ch in chunks:
        lo = ch * CHUNK
        gcum = _mm2l(tri, g_col[lo:lo + CHUNK])
        g_stack = _stack_cols(gcum)
        g_cat = _cat_rows(gcum_row, lo)
        decay = jnp.exp(jnp.where(lower, g_stack - g_cat, NEG))
        ksm = _head_stack(k[lo:lo + CHUNK])
        kk = _mm1(ksm, ksm, NT)
        qks.append(_mm1(_head_stack(q[lo:lo + CHUNK]), ksm, NT) * decay)
        m_bds.append(_stack_cols(beta_col[lo:lo + CHUNK]) * kk * jnp.where(strict, decay, 0.0))
        gcums.append(gcum)

    def sibling(lev):
        return ((r >> (lev + 1)) == (c >> (lev + 1))) & (((r >> lev) & 1) == 1) & (((c >> lev) & 1) == 0)

    xs = [eye - jnp.where(sibling(0), m, 0.0) for m in m_bds]
    for lev in range(1, 6):
        sel = sibling(lev)
        xs = [x - _mm3(_mm3(x, jnp.where(sel, m, 0.0)), x) for x, m in zip(xs, m_bds)]

    for ch in chunks:
        lo = ch * CHUNK
        qc, kc, vc = q[lo:lo + CHUNK], k[lo:lo + CHUNK], v[lo:lo + CHUNK]
        gcum, bcol = gcums[ch], beta_col[lo:lo + CHUNK]
        s_bd = s_scr[...]
        kq_s = _mm1(jnp.concatenate([kc, qc], axis=0), s_bd)
        ks, qs = kq_s[0:CHUNK], kq_s[CHUNK:2 * CHUNK]
        eg_all = _expand_cols(jnp.exp(gcum))
        rhs = _expand_cols(bcol) * (vc - eg_all * ks)
        u_sm = _mm3(xs[ch], _head_stack(rhs))
        o_sm = _mm1(qks[ch], u_sm)
        o_ref[lo:lo + CHUNK, :] = eg_all * qs + _fold_heads(o_sm)
        u_all = _fold_heads(u_sm)
        g_last = gcum[CHUNK - 1:CHUNK, :]
        kw = kc * _expand_cols(jnp.exp(g_last - gcum))
        d_stack = jnp.concatenate(
            [jnp.broadcast_to(jnp.exp(g_last[:, h:h + 1]), (HEAD_W, 1)) for h in range(N_HEADS)], axis=0)
        s_scr[...] = d_stack * s_bd + jnp.where(same, _mm1(kw, u_all, TN), 0.0)

    @pl.when(i == pl.num_programs(1) - 1)
    def _():
        s_out_ref[...] = s_scr[...]


def _gdn_prompt(p, gt, conv_w, a_log, dt_bias, n_batch, seq):
    tb = min(256, seq)
    nb = seq // tb
    r14 = lambda z: z.astype(f32).reshape(1, N_HEADS)
    c41 = lambda z: z.astype(f32).reshape(N_HEADS, 1)
    o, s_bd = pl.pallas_call(
        functools.partial(_gdn_kernel, tb=tb),
        grid=(n_batch, nb),
        in_specs=[
            pl.BlockSpec((tb, 3 * GROUP_W), lambda b, i: (b * nb + i, 0)),
            pl.BlockSpec((tb, 128), lambda b, i: (b * nb + i, GATE_COL // 128)),
            pl.BlockSpec((16, tb), lambda b, i: (0, b * nb + i)),
            pl.BlockSpec((CONV_W, 3 * GROUP_W), lambda b, i: (0, 0)),
            pl.BlockSpec((1, N_HEADS), lambda b, i: (0, 0)),
            pl.BlockSpec((1, N_HEADS), lambda b, i: (0, 0)),
            pl.BlockSpec((N_HEADS, 1), lambda b, i: (0, 0)),
            pl.BlockSpec((N_HEADS, 1), lambda b, i: (0, 0)),
        ],
        out_specs=[
            pl.BlockSpec((tb, GROUP_W), lambda b, i: (b * nb + i, 0)),
            pl.BlockSpec((None, GROUP_W, GROUP_W), lambda b, i: (b, 0, 0)),
        ],
        out_shape=[jax.ShapeDtypeStruct((n_batch * seq, GROUP_W), f32),
                   jax.ShapeDtypeStruct((n_batch, GROUP_W, GROUP_W), f32)],
        scratch_shapes=[pltpu.VMEM((tb + 8, 3 * GROUP_W), f32), pltpu.VMEM((GROUP_W, GROUP_W), f32)],
        compiler_params=_cparams("arbitrary", "arbitrary"),
        name="gdn_prompt",
    )(p, p, gt, conv_w.astype(f32), r14(a_log), r14(dt_bias), c41(a_log), c41(dt_bias))
    return o, _bd_diag(s_bd)


def _bd_diag(s_bd):
    n_b = s_bd.shape[0]
    s5 = s_bd.reshape(n_b, N_HEADS, HEAD_W, N_HEADS, HEAD_W)
    return jnp.stack([s5[:, h, :, h, :] for h in range(N_HEADS)], axis=1)


def _hgrn_kernel(q_ref, f_ref, i_ref, lb_ref, o_ref, s_out_ref, st_scr, q_scr, k_scr, b_scr, *, tb):
    blk = pl.program_id(1)

    @pl.when(blk == 0)
    def _():
        st_scr[...] = jnp.zeros(st_scr.shape, f32)

    lb = lb_ref[...]
    z = f_ref[...]
    logf = jnp.log(lb + (1.0 - lb) * jax.nn.sigmoid(z))
    q_scr[...] = _silu(q_ref[...])
    k_scr[...] = (1.0 - lb) * jax.nn.sigmoid(-z)
    b_scr[...] = _mm2l(_block_tri(tb, SUB), logf)

    same, _, _ = _bd_masks()
    ones_bd = _group_ones(GROUP_W, HEAD_W)
    row = _iota((SUB * SUB, GROUP_W), 0)
    tmask = (row % SUB) >= (row // SUB)

    def rep_t(x):
        return jnp.broadcast_to(x[None], (SUB, SUB, GROUP_W)).reshape(SUB * SUB, GROUP_W)

    def rep_j(x):
        return jnp.broadcast_to(x[:, None, :], (SUB, SUB, GROUP_W)).reshape(SUB * SUB, GROUP_W)

    def body(c, carry):
        r0 = pl.multiple_of(c * SUB, SUB)
        qs = q_scr[pl.ds(r0, SUB), :]
        ks = k_scr[pl.ds(r0, SUB), :]
        vs = i_ref[pl.ds(r0, SUB), :]
        bs = b_scr[pl.ds(r0, SUB), :]
        st = st_scr[...]
        o_inter = _mm1(qs * jnp.exp(bs), st, NT)
        wgt = rep_t(qs) * jnp.exp(jnp.where(tmask, rep_t(bs) - rep_j(bs), NEG)) * rep_j(ks)
        a = _mm2(wgt, ones_bd)
        o_diag = jnp.sum((a * rep_j(vs)).reshape(SUB, SUB, GROUP_W), axis=0)
        o_ref[pl.ds(r0, SUB), :] = o_inter + o_diag
        b_last = bs[SUB - 1:SUB, :]
        kw = ks * jnp.exp(b_last - bs)
        st_scr[...] = st * jnp.exp(b_last) + jnp.where(same, _mm1(vs, kw, TN), 0.0)
        return carry

    lax.fori_loop(0, tb // SUB, body, 0, unroll=2)

    @pl.when(blk == pl.num_programs(1) - 1)
    def _():
        s_out_ref[...] = st_scr[...]


def _hgrn_prompt(p, lb, n_batch, seq):
    tb = min(256, seq)
    nb = seq // tb
    blk = lambda col: pl.BlockSpec((tb, GROUP_W), lambda b, i: (b * nb + i, col))
    o, st = pl.pallas_call(
        functools.partial(_hgrn_kernel, tb=tb),
        grid=(n_batch, nb),
        in_specs=[blk(7), blk(8), blk(9), pl.BlockSpec((1, GROUP_W), lambda b, i: (0, 0))],
        out_specs=[
            pl.BlockSpec((tb, GROUP_W), lambda b, i: (b * nb + i, 0)),
            pl.BlockSpec((None, GROUP_W, GROUP_W), lambda b, i: (b, 0, 0)),
        ],
        out_shape=[jax.ShapeDtypeStruct((n_batch * seq, GROUP_W), f32),
                   jax.ShapeDtypeStruct((n_batch, GROUP_W, GROUP_W), f32)],
        scratch_shapes=[pltpu.VMEM((GROUP_W, GROUP_W), f32)] + [pltpu.VMEM((tb, GROUP_W), f32)] * 3,
        compiler_params=_cparams("arbitrary", "arbitrary"),
        name="hgrn_prompt",
    )(p, p, p, lb.astype(f32).reshape(1, GROUP_W))
    return o, jnp.swapaxes(_bd_diag(st), -1, -2)


def _log_sigmoid(x):
    return jnp.minimum(x, 0.0) - jnp.log1p(jnp.exp(-jnp.abs(x)))


def _mlstm_kernel(q_ref, k_ref, v_ref, gc_ref, gr_ref, ibr_ref, fbr_ref, ibc_ref, fbc_ref,
                  o_ref, c_out_ref, n_out_ref, m_out_ref, c_scr, n_scr, m_scr, *, tb):
    blk = pl.program_id(1)

    @pl.when(blk == 0)
    def _():
        c_scr[...] = jnp.zeros(c_scr.shape, f32)
        n_scr[...] = jnp.zeros(n_scr.shape, f32)
        m_scr[...] = jnp.zeros(m_scr.shape, f32)

    q = q_ref[...]
    k = k_ref[...] * (HEAD_W ** -0.5)
    v = v_ref[...]
    gc = gc_ref[...]
    li_col = gc[:, 8:12] + ibr_ref[...]
    lf_col = _log_sigmoid(gc[:, 12:16] + fbr_ref[...])
    gr = gr_ref[...]
    li_row = gr[8:12, :] + ibc_ref[...]
    lf_row = _log_sigmoid(gr[12:16, :] + fbc_ref[...])
    b_row = _mm2(jnp.concatenate([lf_row, jnp.zeros_like(lf_row)], axis=0), _block_tri_t(tb, CHUNK))

    same, lower, _ = _bd_masks()
    tri = _tri(CHUNK, bf16)

    for ch in range(tb // CHUNK):
        lo = ch * CHUNK
        qc, kc, vc = q[lo:lo + CHUNK], k[lo:lo + CHUNK], v[lo:lo + CHUNK]
        b_col = _mm2l(tri, lf_col[lo:lo + CHUNK])
        b_stack = _stack_cols(b_col)
        d_mat = jnp.where(lower, b_stack - _cat_rows(b_row, lo) + _cat_rows(li_row, lo), NEG)
        m_row = m_scr[...]
        m_stack = jnp.concatenate(
            [jnp.broadcast_to(m_row[:, h:h + 1], (CHUNK, 1)) for h in range(N_HEADS)], axis=0)
        inter = b_stack + m_stack
        m_t = jnp.maximum(inter, jnp.max(d_mat, axis=-1, keepdims=True))
        w_inter = jnp.exp(inter - m_t)
        qsm = _head_stack(qc)
        ksm = _head_stack(kc)
        pmat = _mm1(qsm, ksm, NT) * jnp.exp(d_mat - m_t)
        c_bd = c_scr[...]
        n_row = n_scr[...]
        num = w_inter * _mm1(qsm, c_bd) + _mm1(pmat, _head_stack(vc))
        den = w_inter * jnp.sum(qsm * n_row, axis=-1, keepdims=True) + jnp.sum(pmat, axis=-1, keepdims=True)
        h_sm = num / jnp.maximum(jnp.abs(den), jnp.exp(-m_t))
        o_ref[lo:lo + CHUNK, :] = _fold_heads(h_sm)
        m_new = jnp.concatenate(
            [m_t[h * CHUNK + CHUNK - 1:h * CHUNK + CHUNK, :] for h in range(N_HEADS)], axis=1)
        b_last = b_col[CHUNK - 1:CHUNK, :]
        w_end = jnp.exp(b_last - b_col + li_col[lo:lo + CHUNK] - m_new)
        d0 = jnp.exp(b_last + m_row - m_new)
        kw = kc * _expand_cols(w_end)
        d0_stack = jnp.concatenate(
            [jnp.broadcast_to(d0[:, h:h + 1], (HEAD_W, 1)) for h in range(N_HEADS)], axis=0)
        c_scr[...] = d0_stack * c_bd + jnp.where(same, _mm1(kw, vc, TN), 0.0)
        n_scr[...] = _expand_cols(d0) * n_row + jnp.sum(kw, axis=0, keepdims=True)
        m_scr[...] = m_new

    @pl.when(blk == pl.num_programs(1) - 1)
    def _():
        c_out_ref[...] = c_scr[...]
        n_out_ref[...] = n_scr[...]
        m_out_ref[...] = m_scr[...]


def _mlstm_prompt(p, gt, i_bias, f_bias, n_batch, seq):
    tb = min(256, seq)
    nb = seq // tb
    blk = lambda col: pl.BlockSpec((tb, GROUP_W), lambda b, i: (b * nb + i, col))
    r14 = lambda z: z.astype(f32).reshape(1, N_HEADS)
    c41 = lambda z: z.astype(f32).reshape(N_HEADS, 1)
    small = lambda shape: pl.BlockSpec(shape, lambda b, i: (0, 0))
    o, c_bd, n_row, m_row = pl.pallas_call(
        functools.partial(_mlstm_kernel, tb=tb),
        grid=(n_batch, nb),
        in_specs=[blk(11), blk(12), blk(13),
                  pl.BlockSpec((tb, 128), lambda b, i: (b * nb + i, GATE_COL // 128)),
                  pl.BlockSpec((16, tb), lambda b, i: (0, b * nb + i)),
                  small((1, N_HEADS)), small((1, N_HEADS)), small((N_HEADS, 1)), small((N_HEADS, 1))],
        out_specs=[
            pl.BlockSpec((tb, GROUP_W), lambda b, i: (b * nb + i, 0)),
            pl.BlockSpec((None, GROUP_W, GROUP_W), lambda b, i: (b, 0, 0)),
            pl.BlockSpec((None, 1, GROUP_W), lambda b, i: (b, 0, 0)),
            pl.BlockSpec((None, 1, N_HEADS), lambda b, i: (b, 0, 0)),
        ],
        out_shape=[jax.ShapeDtypeStruct((n_batch * seq, GROUP_W), f32),
                   jax.ShapeDtypeStruct((n_batch, GROUP_W, GROUP_W), f32),
                   jax.ShapeDtypeStruct((n_batch, 1, GROUP_W), f32),
                   jax.ShapeDtypeStruct((n_batch, 1, N_HEADS), f32)],
        scratch_shapes=[pltpu.VMEM((GROUP_W, GROUP_W), f32), pltpu.VMEM((1, GROUP_W), f32),
                        pltpu.VMEM((1, N_HEADS), f32)],
        compiler_params=_cparams("arbitrary", "arbitrary"),
        name="mlstm_prompt",
    )(p, p, p, p, gt, r14(i_bias), r14(f_bias), c41(i_bias), c41(f_bias))
    return (o, _bd_diag(c_bd), n_row.reshape(n_batch, N_HEADS, HEAD_W), m_row.reshape(n_batch, N_HEADS))


def _gdn_dec_prep_kernel(u_ref, buf_ref, cw_ref, q_ref, k_ref, v_ref):
    w = cw_ref[...]
    conv = u_ref[...] * w[3:4]
    for jj in range(CONV_W - 1):
        conv = conv + buf_ref[jj] * w[jj:jj + 1]
    qkv = _silu(conv)

    def l2n(x):
        return x * lax.rsqrt(_group_sum(x * x, HEAD_W) + EPS)

    q_ref[...] = l2n(qkv[:, 0:GROUP_W]) * (HEAD_W ** -0.5)
    k_ref[...] = l2n(qkv[:, GROUP_W:2 * GROUP_W])
    v_ref[...] = qkv[:, 2 * GROUP_W:3 * GROUP_W]


def _gdn_dec_prep(p, conv_buf, conv_w):
    n_b = p.shape[0]
    out = jax.ShapeDtypeStruct((n_b, GROUP_W), f32)
    return pl.pallas_call(
        _gdn_dec_prep_kernel,
        grid=(1,),
        in_specs=[pl.BlockSpec((n_b, 3 * GROUP_W), lambda i: (0, 0)),
                  pl.BlockSpec((CONV_W - 1, n_b, 3 * GROUP_W), lambda i: (0, 0, 0)),
                  pl.BlockSpec((CONV_W, 3 * GROUP_W), lambda i: (0, 0))],
        out_specs=[pl.BlockSpec((n_b, GROUP_W), lambda i: (0, 0))] * 3,
        out_shape=[out, out, out],
        compiler_params=_cparams("arbitrary"),
        name="gdn_dec_prep",
    )(p, jnp.swapaxes(conv_buf.astype(f32), 0, 1), conv_w.astype(f32))


def _rec_decode_kernel(gq_ref, gk_ref, gv_ref, ga_ref, gb_ref, al_ref, dtb_ref, sg_ref,
                       cq_ref, cf_ref, ci_ref, lbc_ref, sh_ref,
                       dq_ref, dk_ref, dv_ref, di_ref, df_ref, ib_ref, fb_ref, sc_ref, sn_ref, sm_ref,
                       oa_ref, sg_out, oc_ref, sh_out, od_ref, sc_out, sn_out, sm_out):
    q, k, v = gq_ref[...], gk_ref[...], gv_ref[...]
    s = sg_ref[...]
    g = -jnp.exp(al_ref[...]) * _softplus(ga_ref[...] + dtb_ref[...])
    eg = jnp.exp(g)
    beta = jax.nn.sigmoid(gb_ref[...])
    ks = jnp.sum(k * s, axis=1, keepdims=True)
    qs = jnp.sum(q * s, axis=1, keepdims=True)
    u = beta * (v - eg * ks)
    qk = jnp.sum(q * k, axis=1, keepdims=True)
    oa_ref[...] = eg * qs + qk * u
    sg_out[...] = eg * s + k * u

    lb = lbc_ref[...]
    z = cf_ref[...]
    logf = jnp.log(lb + (1.0 - lb) * jax.nn.sigmoid(z))
    kc = (1.0 - lb) * jax.nn.sigmoid(-z)
    qc = _silu(cq_ref[...])
    vc = ci_ref[...]
    sh = sh_ref[...]
    ef = jnp.exp(logf)
    oc_ref[...] = jnp.sum((qc * ef) * sh, axis=1, keepdims=True) + jnp.sum(qc * kc, axis=1, keepdims=True) * vc
    sh_out[...] = ef * sh + kc * vc

    qd = dq_ref[...]
    kd = dk_ref[...] * (HEAD_W ** -0.5)
    vd = dv_ref[...]
    li = di_ref[...] + ib_ref[...]
    lf = _log_sigmoid(df_ref[...] + fb_ref[...])
    m0 = sm_ref[...]
    cs = sc_ref[...]
    n0 = sn_ref[...]
    inter = lf + m0
    m_t = jnp.maximum(inter, li)
    w_inter = jnp.exp(inter - m_t)
    qkd = jnp.sum(qd * kd, axis=1, keepdims=True) * jnp.exp(li - m_t)
    num = w_inter * jnp.sum(qd * cs, axis=1, keepdims=True) + qkd * vd
    den = w_inter * jnp.sum(qd * n0, axis=1, keepdims=True) + qkd
    od_ref[...] = num / jnp.maximum(jnp.abs(den), jnp.exp(-m_t))
    w_end = jnp.exp(li - m_t)
    d0 = jnp.exp(lf + m0 - m_t)
    sc_out[...] = d0 * cs + (w_end * kd) * vd
    sn_out[...] = d0 * n0 + w_end * kd
    sm_out[...] = m_t


def _rec_decode(p, gq, gk, gv, a_log, dt_bias, lb, i_bias, f_bias, s_gdn, s_hgrn, s_c, s_n, s_m):
    n_b = p.shape[0]
    rows = n_b * N_HEADS
    rb = min(16, rows)
    col = lambda z: z.reshape(rows, HEAD_W, 1)
    vrow = lambda z: z.reshape(rows, 1, HEAD_W)
    sca = lambda z: z.reshape(rows, 1, 1)
    per_head = lambda z: jnp.tile(z.astype(f32), n_b).reshape(rows, 1, 1)
    blockp = lambda b: p[:, b * GROUP_W:(b + 1) * GROUP_W]
    gates = p[:, GATE_COL:GATE_COL + 16]
    lb_col = jnp.tile(lb.astype(f32).reshape(N_HEADS, HEAD_W), (n_b, 1)).reshape(rows, HEAD_W, 1)
    st = lambda z: z.astype(f32).reshape(rows, HEAD_W, HEAD_W)
    args = [col(gq), col(gk), vrow(gv), sca(gates[:, 0:4]), sca(gates[:, 4:8]), per_head(a_log), per_head(dt_bias), st(s_gdn),
            col(blockp(7)), col(blockp(8)), vrow(blockp(9)), lb_col, st(s_hgrn),
            col(blockp(11)), col(blockp(12)), vrow(blockp(13)), sca(gates[:, 8:12]), sca(gates[:, 12:16]),
            per_head(i_bias), per_head(f_bias), st(s_c), col(s_n.astype(f32)), sca(s_m.astype(f32))]

    def spec(a):
        return pl.BlockSpec((rb,) + a.shape[1:], lambda i: (i, 0, 0))

    o_vrow = jax.ShapeDtypeStruct((rows, 1, HEAD_W), f32)
    o_st = jax.ShapeDtypeStruct((rows, HEAD_W, HEAD_W), f32)
    o_col = jax.ShapeDtypeStruct((rows, HEAD_W, 1), f32)
    o_sca = jax.ShapeDtypeStruct((rows, 1, 1), f32)
    outs = [o_vrow, o_st, o_vrow, o_st, o_vrow, o_st, o_col, o_sca]
    res = pl.pallas_call(
        _rec_decode_kernel,
        grid=(rows // rb,),
        in_specs=[spec(a) for a in args],
        out_specs=[spec(a) for a in outs],
        out_shape=outs,
        compiler_params=_cparams("arbitrary"),
        name="rec_decode",
    )(*args)
    oa, sg, oc, sh, od, sc, sn, sm = res
    s4 = lambda z: z.reshape(n_b, N_HEADS, HEAD_W, HEAD_W)
    o2 = lambda z: z.reshape(n_b, GROUP_W)
    return (o2(oa), s4(sg), o2(oc), s4(sh), o2(od), s4(sc),
            sn.reshape(n_b, N_HEADS, HEAD_W), sm.reshape(n_b, N_HEADS))


def _permute_w_in(w):
    d_in = w.shape[1]
    a_gate0 = 3 * GROUP_W
    d_gate0 = d_in - GROUP_W - 8
    main = jnp.concatenate([w[:, 0:a_gate0], w[:, a_gate0 + 8:d_gate0], w[:, d_gate0 + 8:]], axis=1)
    gates = jnp.concatenate([w[:, a_gate0:a_gate0 + 8], w[:, d_gate0:d_gate0 + 8]], axis=1)
    pad = jnp.zeros((w.shape[0], P_COLS - main.shape[1] - 16), w.dtype)
    return jnp.concatenate([main, gates, pad], axis=1).astype(bf16), gates.T.astype(bf16)


def kernel(x_prompt, x_sample, page_table, cache_k, cache_v, state_gdn_conv, state_gdn, state_hgrn, state_mlstm_C, state_mlstm_n, state_mlstm_m, attn_norm_g, w_in, gdn_conv_w, gdn_a_log, gdn_dt_bias, gdn_norm_g, diff_qk_norm_g, diff_lambda, diff_subln_g, rel_bias, hgrn_lb_logits, hgrn_norm_g, mlstm_i_bias, mlstm_f_bias, mlstm_norm_g, w_out, ffn_norm_g, ffn_w_gate, ffn_w_up, ffn_w_down, moe_router, moe_w_gate, moe_w_up, moe_w_down):
    depth = w_in.shape[0]
    n_bp, seq, _ = x_prompt.shape
    n_bs = x_sample.shape[0]
    n_pool, page = cache_k.shape[1], cache_k.shape[2]
    dt = x_prompt.dtype

    lb_p = jax.nn.softmax(hgrn_lb_logits.astype(f32), axis=0)
    lb_cum = jnp.cumsum(lb_p, axis=0)
    hgrn_lb = lb_cum - lb_cum[0:1]
    cache_k4 = jnp.transpose(cache_k, (0, 1, 3, 4, 5, 2)).reshape(depth, n_pool, GROUP_W, page)
    cache_v4 = jnp.transpose(cache_v, (0, 1, 3, 4, 2)).reshape(depth, n_pool, GROUP_W, page)

    xp = x_prompt.reshape(n_bp * seq, D_MODEL)
    xs = x_sample.reshape(n_bs, D_MODEL)
    outs_p, outs_s = [], []
    for l in range(depth):
        w_perm, w_gate_t = _permute_w_in(w_in[l])
        w_out_b = w_out[l].astype(bf16)
        gains = jnp.stack([jnp.tile(g.astype(f32), N_HEADS) for g in
                           (gdn_norm_g[l], diff_subln_g[l], hgrn_norm_g[l], mlstm_norm_g[l])])
        lam_init = 0.8 - 0.6 * math.exp(-0.3 * l)
        lam32 = diff_lambda[l].astype(f32)
        lam = jnp.exp(jnp.sum(lam32[0] * lam32[1])) - jnp.exp(jnp.sum(lam32[2] * lam32[3])) + lam_init
        if l % 2 == 0:
            ffn_w = (ffn_w_gate[l // 2].astype(bf16), ffn_w_up[l // 2].astype(bf16), ffn_w_down[l // 2].astype(bf16))
        else:
            router_pad = jnp.pad(moe_router[l // 2].astype(bf16), ((0, 0), (0, 128 - N_EXPERTS)))
            moe_w = (moe_w_gate[l // 2].astype(bf16), moe_w_up[l // 2].astype(bf16), moe_w_down[l // 2].astype(bf16))

        def channel_mix(x):
            if l % 2 == 0:
                return _ffn(x, ffn_norm_g[l], *ffn_w)
            return _moe(x, ffn_norm_g[l], router_pad, *moe_w)

        p, gt = _inproj(xp, attn_norm_g[l], w_perm, w_gate_t)
        qnt, kn, vt = _bprep(p, diff_qk_norm_g[l], True)
        ob = _attn_prompt(qnt, kn, vt, lam, rel_bias, n_bp, seq)
        oa, s_gdn = _gdn_prompt(p, gt, gdn_conv_w[l], gdn_a_log[l], gdn_dt_bias[l], n_bp, seq)
        oc, s_hgrn = _hgrn_prompt(p, hgrn_lb[l], n_bp, seq)
        od, s_c, s_n, s_m = _mlstm_prompt(p, gt, mlstm_i_bias[l], mlstm_f_bias[l], n_bp, seq)
        xp = _outproj(oa, ob, oc, od, p, xp, gains, w_out_b, 1.0 - lam_init)
        xp = channel_mix(xp)
        p3 = p.reshape(n_bp, seq, P_COLS)
        outs_p.append((
            kn.reshape(n_bp, seq, N_HEADS, 2, DKB).astype(dt),
            p3[:, :, 6 * GROUP_W:7 * GROUP_W].reshape(n_bp, seq, N_HEADS, HEAD_W).astype(dt),
            p3[:, seq - (CONV_W - 1):, 0:3 * GROUP_W].astype(dt),
            s_gdn.astype(dt), s_hgrn.astype(dt), s_c.astype(dt), s_n.astype(dt), s_m.astype(dt)))

        p, gt = _inproj(xs, attn_norm_g[l], w_perm, w_gate_t)
        qn, kn = _bprep(p, diff_qk_norm_g[l], False)
        vn = p[:, 6 * GROUP_W:7 * GROUP_W]
        ob = _attn_decode(qn, kn, vn, page_table, cache_k4, cache_v4, l, lam, rel_bias)
        u = p[:, 0:3 * GROUP_W]
        gq, gk, gv = _gdn_dec_prep(u, state_gdn_conv[l], gdn_conv_w[l])
        oa, s_gdn, oc, s_hgrn, od, s_c, s_n, s_m = _rec_decode(
            p, gq, gk, gv, gdn_a_log[l], gdn_dt_bias[l], hgrn_lb[l], mlstm_i_bias[l], mlstm_f_bias[l],
            state_gdn[l], state_hgrn[l], state_mlstm_C[l], state_mlstm_n[l], state_mlstm_m[l])
        xs = _outproj(oa, ob, oc, od, p, xs, gains, w_out_b, 1.0 - lam_init)
        xs = channel_mix(xs)
        conv_new = jnp.concatenate([state_gdn_conv[l][:, 1:].astype(dt), u[:, None, :].astype(dt)], axis=1)
        outs_s.append((
            kn.reshape(n_bs, 1, N_HEADS, 2, DKB).astype(dt),
            vn.reshape(n_bs, 1, N_HEADS, HEAD_W).astype(dt),
            conv_new, s_gdn.astype(dt), s_hgrn.astype(dt), s_c.astype(dt), s_n.astype(dt), s_m.astype(dt)))

    kp, vp, convp, gdnp, hgrnp, mcp, mnp_, mmp = [jnp.stack(z) for z in zip(*outs_p)]
    ks_, vs_, convs, gdns, hgrns, mcs, mns, mms = [jnp.stack(z) for z in zip(*outs_s)]
    return (xp.reshape(n_bp, seq, D_MODEL), xs.reshape(n_bs, 1, D_MODEL), kp, vp, ks_, vs_, convp, convs,
            gdnp, gdns, hgrnp, hgrns, mcp, mcs, mnp_, mns, mmp, mms)
```

```python
import functools
import math

import numpy as np
import jax
import jax.numpy as jnp
from jax import lax
from jax.experimental import pallas as pl
from jax.experimental.pallas import tpu as pltpu

f32 = jnp.float32
bf16 = jnp.bfloat16

D_MODEL = 1024
N_HEADS = 4
HEAD_W = 64
GROUP_W = N_HEADS * HEAD_W
DKB = 32
CONV_W = 4
CHUNK = 64
SUB = 16
NUM_BUCKETS = 32
MAX_DISTANCE = 128
N_EXPERTS = 8
EPS = 1e-6
NEG = -1e30
P_COLS = 4096
GATE_COL = 3840
VMEM_LIMIT = 56 * 1024 * 1024

NN = ((1,), (0,))
NT = ((1,), (1,))
TN = ((0,), (0,))


def _dg(a, b, dims=NN):
    return lax.dot_general(a, b, (dims, ((), ())), preferred_element_type=f32)


def _split(a):
    hi = a.astype(bf16)
    lo = (a - hi.astype(f32)).astype(bf16)
    return hi, lo


def _mm3(a, b, dims=NN):
    ah, al = _split(a)
    bh, bl = _split(b)
    return _dg(ah, bh, dims) + (_dg(ah, bl, dims) + _dg(al, bh, dims))


def _mm2(a, b01, dims=NN):
    ah, al = _split(a)
    return _dg(ah, b01, dims) + _dg(al, b01, dims)


def _mm2l(a01, b, dims=NN):
    bh, bl = _split(b)
    return _dg(a01, bh, dims) + _dg(a01, bl, dims)


def _mm1(a, b, dims=NN):
    return _dg(a.astype(bf16), b.astype(bf16), dims)


def _iota(shape, dim):
    return lax.broadcasted_iota(jnp.int32, shape, dim)


def _group_ones(width, group):
    r = _iota((width, width), 0) // group
    c = _iota((width, width), 1) // group
    return (r == c).astype(bf16)


def _group_sum(x, group):
    ones = _group_ones(x.shape[-1], group)
    hi = x.astype(bf16)
    r1 = x - hi.astype(f32)
    mid = r1.astype(bf16)
    lo = (r1 - mid.astype(f32)).astype(bf16)
    return _dg(hi, ones) + (_dg(mid, ones) + _dg(lo, ones))


def _recip(x):
    r = 1.0 / x
    return r * (2.0 - x * r)


def _silu(x):
    return x * jax.nn.sigmoid(x)


def _softplus(x):
    return jnp.maximum(x, 0.0) + jnp.log1p(jnp.exp(-jnp.abs(x)))


def _stack_cols(xc, n=N_HEADS, rows=HEAD_W):
    return jnp.concatenate([xc[:, h:h + 1] for h in range(n)], axis=0)


def _expand_cols(xc, n=N_HEADS, width=HEAD_W):
    r = xc.shape[0]
    return jnp.concatenate([jnp.broadcast_to(xc[:, h:h + 1], (r, width)) for h in range(n)], axis=1)


def _cat_rows(xr, lo, n=N_HEADS, width=HEAD_W):
    return jnp.concatenate([xr[h:h + 1, lo:lo + width] for h in range(n)], axis=1)


def _head_stack(x, n=N_HEADS, width=HEAD_W):
    lane_head = _iota(x.shape, 1) // width
    return jnp.concatenate([jnp.where(lane_head == h, x, 0.0) for h in range(n)], axis=0)


def _fold_heads(x_sm, n=N_HEADS):
    r = x_sm.shape[0] // n
    out = x_sm[0:r]
    for h in range(1, n):
        out = out + x_sm[h * r:(h + 1) * r]
    return out


def _bd_masks(n=GROUP_W, blk=CHUNK):
    r = _iota((n, n), 0)
    c = _iota((n, n), 1)
    same = (r // blk) == (c // blk)
    lower = same & ((r % blk) >= (c % blk))
    strict = same & ((r % blk) > (c % blk))
    return same, lower, strict


def _cparams(*sem):
    return pltpu.CompilerParams(dimension_semantics=sem, vmem_limit_bytes=VMEM_LIMIT)


def _inproj_kernel(x_ref, g_ref, w_ref, wgt_ref, p_ref, gt_ref, h_scr):
    @pl.when(pl.program_id(1) == 0)
    def _():
        x = x_ref[...]
        ms = jnp.mean(x * x, axis=-1, keepdims=True)
        h = ((x * lax.rsqrt(ms + EPS)) * g_ref[...]).astype(bf16)
        h_scr[...] = h
        gt_ref[...] = _dg(wgt_ref[...], h, NT)
    p_ref[...] = _dg(h_scr[...], w_ref[...], NN)


def _inproj(x, g, w_perm, w_gate_t):
    t = x.shape[0]
    tm = min(1024, t)
    tn = 1024
    return pl.pallas_call(
        _inproj_kernel,
        grid=(t // tm, P_COLS // tn),
        in_specs=[
            pl.BlockSpec((tm, D_MODEL), lambda i, j: (i, 0)),
            pl.BlockSpec((1, D_MODEL), lambda i, j: (0, 0)),
            pl.BlockSpec((D_MODEL, tn), lambda i, j: (0, j)),
            pl.BlockSpec((16, D_MODEL), lambda i, j: (0, 0)),
        ],
        out_specs=[
            pl.BlockSpec((tm, tn), lambda i, j: (i, j)),
            pl.BlockSpec((16, tm), lambda i, j: (0, i)),
        ],
        out_shape=[jax.ShapeDtypeStruct((t, P_COLS), f32), jax.ShapeDtypeStruct((16, t), f32)],
        scratch_shapes=[pltpu.VMEM((tm, D_MODEL), bf16)],
        compiler_params=_cparams("arbitrary", "arbitrary"),
        name="inproj",
    )(x, g.reshape(1, D_MODEL), w_perm, w_gate_t)


def _qk_gnorm(x, g):
    ms = _group_sum(x * x, DKB) * (1.0 / DKB)
    return (x * lax.rsqrt(ms + EPS)) * g


def _bprep_kernel(q_ref, k_ref, gq_ref, gk_ref, qn_ref, kn_ref):
    qn_ref[...] = _qk_gnorm(q_ref[...], gq_ref[...])
    kn_ref[...] = _qk_gnorm(k_ref[...], gk_ref[...])


def _bprep_t_kernel(q_ref, k_ref, v_ref, gq_ref, gk_ref, qnt_ref, kn_ref, vt_ref):
    qnt_ref[...] = _qk_gnorm(q_ref[...], gq_ref[...]).T
    kn_ref[...] = _qk_gnorm(k_ref[...], gk_ref[...])
    vt_ref[...] = v_ref[...].T


def _bprep(p, qk_norm_g, transposed):
    t = p.shape[0]
    tm = min(512, t)
    gq = jnp.tile(qk_norm_g[0], GROUP_W // DKB).reshape(1, GROUP_W)
    gk = jnp.tile(qk_norm_g[1], GROUP_W // DKB).reshape(1, GROUP_W)
    col = lambda c: pl.BlockSpec((tm, GROUP_W), lambda i: (i, c))
    gain = pl.BlockSpec((1, GROUP_W), lambda i: (0, 0))
    rows = pl.BlockSpec((tm, GROUP_W), lambda i: (i, 0))
    rows_t = pl.BlockSpec((GROUP_W, tm), lambda i: (0, i))
    if transposed:
        return pl.pallas_call(
            _bprep_t_kernel,
            grid=(t // tm,),
            in_specs=[col(4), col(5), col(6), gain, gain],
            out_specs=[rows_t, rows, rows_t],
            out_shape=[jax.ShapeDtypeStruct((GROUP_W, t), f32), jax.ShapeDtypeStruct((t, GROUP_W), f32),
                       jax.ShapeDtypeStruct((GROUP_W, t), f32)],
            compiler_params=_cparams("arbitrary"),
            name="bprep_t",
        )(p, p, p, gq, gk)
    return pl.pallas_call(
        _bprep_kernel,
        grid=(t // tm,),
        in_specs=[col(4), col(5), gain, gain],
        out_specs=[rows, rows],
        out_shape=[jax.ShapeDtypeStruct((t, GROUP_W), f32)] * 2,
        compiler_params=_cparams("arbitrary"),
        name="bprep",
    )(p, p, gq, gk)


def _outproj_kernel(oa_ref, ob_ref, oc_ref, od_ref, ag_ref, cg_ref, dg_ref, x_ref, g_ref, w_ref, y_ref, *, b_scale):
    def gnorm(x, g):
        ms = _group_sum(x * x, HEAD_W) * (1.0 / HEAD_W)
        return (x * lax.rsqrt(ms + EPS)) * g
    g = g_ref[...]
    mixes = (
        gnorm(oa_ref[...], g[0:1]) * _silu(ag_ref[...]),
        gnorm(ob_ref[...], g[1:2]) * b_scale,
        gnorm(oc_ref[...], g[2:3]) * jax.nn.sigmoid(cg_ref[...]),
        gnorm(od_ref[...], g[3:4]) * jax.nn.sigmoid(dg_ref[...]),
    )
    y = x_ref[...]
    for i, m in enumerate(mixes):
        y = y + _dg(m.astype(bf16), w_ref[i * GROUP_W:(i + 1) * GROUP_W, :], NN)
    y_ref[...] = y


def _outproj(oa, ob, oc, od, p, x, gains, w_out, b_scale):
    t = x.shape[0]
    tm = min(512, t)
    row = lambda i: (i, 0)
    return pl.pallas_call(
        functools.partial(_outproj_kernel, b_scale=b_scale),
        grid=(t // tm,),
        in_specs=[
            pl.BlockSpec((tm, GROUP_W), row), pl.BlockSpec((tm, GROUP_W), row),
            pl.BlockSpec((tm, GROUP_W), row), pl.BlockSpec((tm, GROUP_W), row),
            pl.BlockSpec((tm, GROUP_W), lambda i: (i, 3)),
            pl.BlockSpec((tm, GROUP_W), lambda i: (i, 10)),
            pl.BlockSpec((tm, GROUP_W), lambda i: (i, 14)),
            pl.BlockSpec((tm, D_MODEL), row),
            pl.BlockSpec((4, GROUP_W), lambda i: (0, 0)),
            pl.BlockSpec((D_MODEL, D_MODEL), lambda i: (0, 0)),
        ],
        out_specs=pl.BlockSpec((tm, D_MODEL), row),
        out_shape=jax.ShapeDtypeStruct((t, D_MODEL), f32),
        compiler_params=_cparams("arbitrary"),
        name="outproj",
    )(oa, ob, oc, od, p, p, p, x, gains, w_out)


def _ffn_kernel(x_ref, g_ref, wg_ref, wu_ref, wd_ref, y_ref, h_scr, acc_scr):
    f = pl.program_id(1)

    @pl.when(f == 0)
    def _():
        x = x_ref[...]
        ms = jnp.mean(x * x, axis=-1, keepdims=True)
        h_scr[...] = ((x * lax.rsqrt(ms + EPS)) * g_ref[...]).astype(bf16)
        acc_scr[...] = x

    h = h_scr[...]
    a = _silu(_dg(h, wg_ref[...])) * _dg(h, wu_ref[...])
    acc_scr[...] += _dg(a.astype(bf16), wd_ref[...])

    @pl.when(f == pl.num_programs(1) - 1)
    def _():
        y_ref[...] = acc_scr[...]


def _ffn(x, g, wg, wu, wd):
    t = x.shape[0]
    d_ff = wg.shape[1]
    tm = min(512, t)
    tf = d_ff // 2
    return pl.pallas_call(
        _ffn_kernel,
        grid=(t // tm, d_ff // tf),
        in_specs=[
            pl.BlockSpec((tm, D_MODEL), lambda i, f: (i, 0)),
            pl.BlockSpec((1, D_MODEL), lambda i, f: (0, 0)),
            pl.BlockSpec((D_MODEL, tf), lambda i, f: (0, f)),
            pl.BlockSpec((D_MODEL, tf), lambda i, f: (0, f)),
            pl.BlockSpec((tf, D_MODEL), lambda i, f: (f, 0)),
        ],
        out_specs=pl.BlockSpec((tm, D_MODEL), lambda i, f: (i, 0)),
        out_shape=jax.ShapeDtypeStruct((t, D_MODEL), f32),
        scratch_shapes=[pltpu.VMEM((tm, D_MODEL), bf16), pltpu.VMEM((tm, D_MODEL), f32)],
        compiler_params=_cparams("arbitrary", "arbitrary"),
        name="ffn",
    )(x, g.reshape(1, D_MODEL), wg, wu, wd)


LANE = 128
SUBL = D_MODEL // LANE
ROW_TILE = 256
SPARSE_MIN_TOKENS = 4096


def _tile_rows(x):
    return [x[:, j * LANE:(j + 1) * LANE] for j in range(SUBL)]


def _router_kernel(x_ref, g_ref, r_ref, h3_ref, meta_ref, cnt_ref, carry_scr):
    @pl.when(pl.program_id(0) == 0)
    def _():
        carry_scr[...] = jnp.zeros(carry_scr.shape, f32)

    x = x_ref[...]
    tm = x.shape[0]
    ms = jnp.mean(x * x, axis=-1, keepdims=True)
    h = (x * lax.rsqrt(ms + EPS)) * g_ref[...]
    for j, blk in enumerate(_tile_rows(h)):
        h3_ref[:, j, :] = blk
    logits = _dg(h.astype(bf16), r_ref[...])
    lane = _iota(logits.shape, 1)
    logits = jnp.where(lane < N_EXPERTS, logits, -jnp.inf)
    v1 = jnp.max(logits, axis=-1, keepdims=True)
    i1 = jnp.min(jnp.where(logits == v1, lane, LANE), axis=-1, keepdims=True)
    rest = jnp.where(lane == i1, -jnp.inf, logits)
    v2 = jnp.max(rest, axis=-1, keepdims=True)
    i2 = jnp.min(jnp.where(rest == v2, lane, LANE), axis=-1, keepdims=True)
    e2 = jnp.exp(v2 - v1)
    den = 1.0 + e2
    hit = ((lane == i1) | (lane == i2)).astype(f32)
    strict = (_iota((tm, tm), 0) > _iota((tm, tm), 1)).astype(bf16)
    before = _dg(strict, hit.astype(bf16)) + carry_scr[...]
    pos1 = jnp.sum(jnp.where(lane == i1, before, 0.0), axis=-1, keepdims=True)
    pos2 = jnp.sum(jnp.where(lane == i2, before, 0.0), axis=-1, keepdims=True)
    carry_scr[...] += jnp.sum(hit, axis=0, keepdims=True)
    meta = jnp.zeros(logits.shape, f32)
    for c, val in enumerate((i1.astype(f32), i2.astype(f32), 1.0 / den, e2 / den, pos1, pos2)):
        meta = jnp.where(lane == c, val, meta)
    meta_ref[...] = meta
    cnt_ref[...] = carry_scr[...]


def _router(x, g, router_pad):
    t = x.shape[0]
    tm = min(512, t)
    return pl.pallas_call(
        _router_kernel,
        grid=(t // tm,),
        in_specs=[
            pl.BlockSpec((tm, D_MODEL), lambda i: (i, 0)),
            pl.BlockSpec((1, D_MODEL), lambda i: (0, 0)),
            pl.BlockSpec((D_MODEL, LANE), lambda i: (0, 0)),
        ],
        out_specs=[pl.BlockSpec((tm, SUBL, LANE), lambda i: (i, 0, 0)),
                   pl.BlockSpec((tm, LANE), lambda i: (i, 0)),
                   pl.BlockSpec((1, LANE), lambda i: (0, 0))],
        out_shape=[jax.ShapeDtypeStruct((t, SUBL, LANE), f32), jax.ShapeDtypeStruct((t, LANE), f32),
                   jax.ShapeDtypeStruct((1, LANE), f32)],
        scratch_shapes=[pltpu.VMEM((1, LANE), f32)],
        compiler_params=_cparams("arbitrary"),
        name="router",
    )(x, g.reshape(1, D_MODEL), router_pad)


def _swiglu_bf16(x, wg, wu, wd):
    a = _silu(_dg(x, wg)) * _dg(x, wu)
    return _dg(a.astype(bf16), wd)


def _moe_dense_kernel(x_ref, h3_ref, meta_ref, wg_ref, wu_ref, wd_ref, y_ref, acc_scr):
    e = pl.program_id(1)
    f = pl.program_id(2)

    @pl.when((e == 0) & (f == 0))
    def _():
        acc_scr[...] = x_ref[...]

    meta = meta_ref[...]
    ef = e.astype(f32)
    cw = jnp.where(meta[:, 0:1] == ef, meta[:, 2:3], 0.0) + jnp.where(meta[:, 1:2] == ef, meta[:, 3:4], 0.0)
    h = jnp.concatenate([h3_ref[:, j, :] for j in range(SUBL)], axis=1).astype(bf16)
    acc_scr[...] += cw * _swiglu_bf16(h, wg_ref[...], wu_ref[...], wd_ref[...])

    @pl.when((e == pl.num_programs(1) - 1) & (f == pl.num_programs(2) - 1))
    def _():
        y_ref[...] = acc_scr[...]


def _moe_dense(x, h3, meta, wg, wu, wd):
    t = x.shape[0]
    n_e, _, d_ff = wg.shape
    tm = min(512, t)
    tf = d_ff // 2
    return pl.pallas_call(
        _moe_dense_kernel,
        grid=(t // tm, n_e, d_ff // tf),
        in_specs=[
            pl.BlockSpec((tm, D_MODEL), lambda i, e, f: (i, 0)),
            pl.BlockSpec((tm, SUBL, LANE), lambda i, e, f: (i, 0, 0)),
            pl.BlockSpec((tm, LANE), lambda i, e, f: (i, 0)),
            pl.BlockSpec((None, D_MODEL, tf), lambda i, e, f: (e, 0, f)),
            pl.BlockSpec((None, D_MODEL, tf), lambda i, e, f: (e, 0, f)),
            pl.BlockSpec((None, tf, D_MODEL), lambda i, e, f: (e, f, 0)),
        ],
        out_specs=pl.BlockSpec((tm, D_MODEL), lambda i, e, f: (i, 0)),
        out_shape=jax.ShapeDtypeStruct((t, D_MODEL), f32),
        scratch_shapes=[pltpu.VMEM((tm, D_MODEL), f32)],
        compiler_params=_cparams("arbitrary", "arbitrary", "arbitrary"),
        name="moe_dense",
    )(x, h3, meta, wg, wu, wd)


def _route_plan(meta, counts, t):
    cnt = counts[0, :N_EXPERTS].astype(jnp.int32)
    padded = ((cnt + ROW_TILE - 1) // ROW_TILE) * ROW_TILE
    ends = jnp.cumsum(padded)
    offs = ends - padded
    experts = jnp.arange(N_EXPERTS, dtype=jnp.int32)

    def dest(expert_col, rank_col):
        e = meta[:, expert_col].astype(jnp.int32)
        off = jnp.sum(jnp.where(e[:, None] == experts[None, :], offs[None, :], 0), axis=1)
        return off + meta[:, rank_col].astype(jnp.int32)

    n_rows = 2 * t + N_EXPERTS * ROW_TILE
    starts = jnp.arange(n_rows // ROW_TILE, dtype=jnp.int32) * ROW_TILE
    tile_expert = jnp.minimum(jnp.sum((starts[:, None] >= ends[None, :]).astype(jnp.int32), axis=1), N_EXPERTS - 1)
    n_used = (ends[N_EXPERTS - 1] // ROW_TILE).reshape(1)
    return dest(0, 4), dest(1, 5), tile_expert, n_used, n_rows


def _dispatch_kernel(d1_ref, d2_ref, h3_ref, zero_hbm, xs_hbm, sem, *, tm):
    del zero_hbm
    base = pl.program_id(0) * tm

    def issue(k, carry):
        src = h3_ref.at[pl.ds(k, 1)]
        pltpu.make_async_copy(src, xs_hbm.at[pl.ds(d1_ref[base + k], 1)], sem).start()
        pltpu.make_async_copy(src, xs_hbm.at[pl.ds(d2_ref[base + k], 1)], sem).start()
        return carry

    lax.fori_loop(0, tm, issue, 0)
    for _ in range(2):
        pltpu.make_async_copy(h3_ref, xs_hbm.at[pl.ds(0, tm)], sem).wait()


def _dispatch(h3, dest1, dest2, n_rows):
    t = h3.shape[0]
    tm = min(512, t)
    grid_spec = pltpu.PrefetchScalarGridSpec(
        num_scalar_prefetch=2,
        grid=(t // tm,),
        in_specs=[pl.BlockSpec((tm, SUBL, LANE), lambda i, d1, d2: (i, 0, 0)), pl.BlockSpec(memory_space=pl.ANY)],
        out_specs=pl.BlockSpec(memory_space=pl.ANY),
        scratch_shapes=[pltpu.SemaphoreType.DMA(())],
    )
    return pl.pallas_call(
        functools.partial(_dispatch_kernel, tm=tm),
        grid_spec=grid_spec,
        out_shape=jax.ShapeDtypeStruct((n_rows, SUBL, LANE), f32),
        input_output_aliases={3: 0},
        compiler_params=_cparams("arbitrary"),
        name="moe_dispatch",
    )(dest1, dest2, h3, jnp.zeros((n_rows, SUBL, LANE), f32))


def _experts_kernel(te_ref, nu_ref, xs_ref, wg_ref, wu_ref, wd_ref, ys_ref):
    del te_ref
    r = pl.program_id(0)

    @pl.when(r < nu_ref[0])
    def _():
        x = jnp.concatenate([xs_ref[:, j, :] for j in range(SUBL)], axis=1).astype(bf16)
        for j, blk in enumerate(_tile_rows(_swiglu_bf16(x, wg_ref[...], wu_ref[...], wd_ref[...]))):
            ys_ref[:, j, :] = blk

    @pl.when(r >= nu_ref[0])
    def _():
        ys_ref[...] = jnp.zeros(ys_ref.shape, f32)


def _experts(xs, tile_expert, n_used, wg, wu, wd):
    n_rows = xs.shape[0]
    d_ff = wg.shape[2]
    rows = pl.BlockSpec((ROW_TILE, SUBL, LANE), lambda r, te, nu: (r, 0, 0))
    w_in = pl.BlockSpec((None, D_MODEL, d_ff), lambda r, te, nu: (te[r], 0, 0), pipeline_mode=pl.Buffered(1))
    w_dn = pl.BlockSpec((None, d_ff, D_MODEL), lambda r, te, nu: (te[r], 0, 0), pipeline_mode=pl.Buffered(1))
    grid_spec = pltpu.PrefetchScalarGridSpec(
        num_scalar_prefetch=2,
        grid=(n_rows // ROW_TILE,),
        in_specs=[rows, w_in, w_in, w_dn],
        out_specs=rows,
    )
    return pl.pallas_call(
        _experts_kernel,
        grid_spec=grid_spec,
        out_shape=jax.ShapeDtypeStruct((n_rows, SUBL, LANE), f32),
        compiler_params=_cparams("arbitrary"),
        name="moe_experts",
    )(tile_expert, n_used, xs, wg, wu, wd)


def _combine_kernel(d1_ref, d2_ref, x_ref, meta_ref, ys_hbm, y_ref, buf, sem, *, tm):
    base = pl.program_id(0) * tm

    def issue(k, carry):
        t = base + k
        pltpu.make_async_copy(ys_hbm.at[pl.ds(d1_ref[t], 1)], buf.at[0, pl.ds(k, 1)], sem).start()
        pltpu.make_async_copy(ys_hbm.at[pl.ds(d2_ref[t], 1)], buf.at[1, pl.ds(k, 1)], sem).start()
        return carry

    lax.fori_loop(0, tm, issue, 0)
    for s in range(2):
        pltpu.make_async_copy(ys_hbm.at[pl.ds(0, tm)], buf.at[s], sem).wait()
    meta = meta_ref[...]
    g1 = meta[:, 2:3]
    g2 = meta[:, 3:4]
    for j in range(SUBL):
        sl = slice(j * LANE, (j + 1) * LANE)
        y_ref[:, sl] = x_ref[:, sl] + (g1 * buf[0, :, j, :] + g2 * buf[1, :, j, :])


def _combine(x, meta, ys, dest1, dest2):
    t = x.shape[0]
    tm = min(256, t)
    grid_spec = pltpu.PrefetchScalarGridSpec(
        num_scalar_prefetch=2,
        grid=(t // tm,),
        in_specs=[pl.BlockSpec((tm, D_MODEL), lambda i, d1, d2: (i, 0)),
                  pl.BlockSpec((tm, LANE), lambda i, d1, d2: (i, 0)),
                  pl.BlockSpec(memory_space=pl.ANY)],
        out_specs=pl.BlockSpec((tm, D_MODEL), lambda i, d1, d2: (i, 0)),
        scratch_shapes=[pltpu.VMEM((2, tm, SUBL, LANE), f32), pltpu.SemaphoreType.DMA(())],
    )
    return pl.pallas_call(
        functools.partial(_combine_kernel, tm=tm),
        grid_spec=grid_spec,
        out_shape=jax.ShapeDtypeStruct((t, D_MODEL), f32),
        compiler_params=_cparams("arbitrary"),
        name="moe_combine",
    )(dest1, dest2, x, meta, ys)


def _moe(x, g, router_pad, wg, wu, wd):
    t = x.shape[0]
    h3, meta, counts = _router(x, g, router_pad)
    if t < SPARSE_MIN_TOKENS:
        return _moe_dense(x, h3, meta, wg, wu, wd)
    dest1, dest2, tile_expert, n_used, n_rows = _route_plan(meta, counts, t)
    xs = _dispatch(h3, dest1, dest2, n_rows)
    ys = _experts(xs, tile_expert, n_used, wg, wu, wd)
    return _combine(x, meta, ys, dest1, dest2)


def _t5_bucket_np(n):
    n = np.maximum(n, 0)
    max_exact = NUM_BUCKETS // 2
    nf = np.maximum(n, 1).astype(np.float32)
    large = max_exact + (np.log(nf / np.float32(max_exact)) / np.float32(math.log(MAX_DISTANCE / max_exact))
                         * np.float32(NUM_BUCKETS - max_exact)).astype(np.int32)
    return np.where(n < max_exact, n, np.minimum(large, NUM_BUCKETS - 1))


def _shifted_bias(rel_bias):
    rb = rel_bias.astype(f32)
    return rb - rb[NUM_BUCKETS - 1:NUM_BUCKETS]


ACC_ROWS = HEAD_W + 8
LOG2E = math.log2(math.e)


def _attn_kernel(qi_ref, kj_ref, lam_ref, qt_ref, k_ref, vt_ref, toep_ref, o_ref, qs_scr, m_scr, acc_scr, *, tq):
    p = pl.program_id(1)
    i = qi_ref[p]
    j = kj_ref[p]
    n_hc = 2 * N_HEADS
    c2 = (DKB ** -0.5) * LOG2E

    @pl.when(j == 0)
    def _():
        qt = qt_ref[...] * c2
        row_grp = _iota(qt.shape, 0) // DKB
        for hc in range(n_hc):
            qs_scr[:, hc * tq:(hc + 1) * tq] = jnp.where(row_grp == hc, qt, 0.0).astype(bf16)
        m_scr[...] = jnp.full(m_scr.shape, NEG, f32)
        acc_scr[...] = jnp.zeros(acc_scr.shape, f32)

    def step(near):
        tk = k_ref.shape[0]
        st_all = _dg(k_ref[...].astype(bf16), qs_scr[...], NN)
        vt = vt_ref[...]
        ones = jnp.ones((ACC_ROWS - HEAD_W, tk), f32)
        for h in range(N_HEADS):
            vh = jnp.concatenate([vt[h * HEAD_W:(h + 1) * HEAD_W, :], ones], axis=0).astype(bf16)
            for hc in (2 * h, 2 * h + 1):
                s = st_all[:, hc * tq:(hc + 1) * tq]
                if near:
                    s = s + toep_ref[(i - j) * N_HEADS + h]
                m_old = m_scr[hc:hc + 1, :]
                m_new = jnp.maximum(m_old, jnp.max(s, axis=0, keepdims=True))
                pexp = jnp.exp2(s - m_new)
                acc_scr[hc] = jnp.exp2(m_old - m_new) * acc_scr[hc] + _dg(vh, pexp.astype(bf16), NN)
                m_scr[hc:hc + 1, :] = m_new

    @pl.when(i - j <= 1)
    def _():
        step(True)

    @pl.when(i - j > 1)
    def _():
        step(False)

    @pl.when(j == i)
    def _():
        lam = lam_ref[0]
        outs = []
        for h in range(N_HEADS):
            a0 = acc_scr[2 * h]
            a1 = acc_scr[2 * h + 1]
            outs.append(a0[0:HEAD_W] * _recip(a0[HEAD_W:HEAD_W + 1])
                        - lam * (a1[0:HEAD_W] * _recip(a1[HEAD_W:HEAD_W + 1])))
        o_ref[...] = jnp.concatenate(outs, axis=0).T


def _toeplitz_kernel(u_ref, o_ref):
    t = o_ref.shape[0]
    rows = jnp.broadcast_to(u_ref[...], (t, 2 * t))
    o_ref[...] = pltpu.roll(rows, 0, 1, stride=1, stride_axis=0)[:, t:2 * t]


def _toeplitz_bias_tiles(rel_bias, t):
    m = np.arange(2 * t)[None, :]
    dist = m - t + np.array([0, t])[:, None]
    tab = _shifted_bias(rel_bias)
    u = jnp.take(tab, jnp.asarray(_t5_bucket_np(dist)), axis=0)
    u = jnp.where(jnp.asarray(dist >= 0)[:, :, None], u * LOG2E, NEG)
    u = jnp.transpose(u, (0, 2, 1)).reshape(2 * N_HEADS, 1, 2 * t)
    return pl.pallas_call(
        _toeplitz_kernel,
        grid=(2 * N_HEADS,),
        in_specs=[pl.BlockSpec((None, 1, 2 * t), lambda i: (i, 0, 0))],
        out_specs=pl.BlockSpec((None, t, t), lambda i: (i, 0, 0)),
        out_shape=jax.ShapeDtypeStruct((2 * N_HEADS, t, t), f32),
        compiler_params=_cparams("arbitrary"),
        name="toeplitz_bias",
    )(u)


def _attn_prompt(qnt, kn, vt, lam, rel_bias, n_batch, seq):
    tq = min(512, seq)
    nq = seq // tq
    pairs =[(i, j) for i in range(nq) for j in range(i + 1)]
    qi = jnp.asarray(np.array([a for a, _ in pairs], np.int32))
    kj = jnp.asarray(np.array([b for _, b in pairs], np.int32))
    toep = _toeplitz_bias_tiles(rel_bias, tq)
    grid_spec = pltpu.PrefetchScalarGridSpec(
        num_scalar_prefetch=2,
        grid=(n_batch, len(pairs)),
        in_specs=[
            pl.BlockSpec(memory_space=pltpu.SMEM),
            pl.BlockSpec((GROUP_W, tq), lambda b_, p_, qi_, kj_: (0, b_ * nq + qi_[p_])),
            pl.BlockSpec((tq, GROUP_W), lambda b_, p_, qi_, kj_: (b_ * nq + kj_[p_], 0)),
            pl.BlockSpec((GROUP_W, tq), lambda b_, p_, qi_, kj_: (0, b_ * nq + kj_[p_])),
            pl.BlockSpec((2 * N_HEADS, tq, tq), lambda b_, p_, qi_, kj_: (0, 0, 0)),
        ],
        out_specs=pl.BlockSpec((tq, GROUP_W), lambda b_, p_, qi_, kj_: (b_ * nq + qi_[p_], 0)),
        scratch_shapes=[
            pltpu.VMEM((GROUP_W, 2 * N_HEADS * tq), bf16),
            pltpu.VMEM((2 * N_HEADS, tq), f32),
            pltpu.VMEM((2 * N_HEADS, ACC_ROWS, tq), f32),
        ],
    )
    return pl.pallas_call(
        functools.partial(_attn_kernel, tq=tq),
        grid_spec=grid_spec,
        out_shape=jax.ShapeDtypeStruct((n_batch * seq, GROUP_W), f32),
        compiler_params=_cparams("arbitrary", "arbitrary"),
        name="attn_prompt",
    )(qi, kj, lam.reshape(1), qnt, kn, vt, toep)


def _attn_decode_kernel(pt_ref, lam_ref, q_ref, kn_ref, vn_ref, blast_ref, bself_ref, *rest, pg, n_pages):
    k_refs = rest[:pg]
    v_refs = rest[pg:2 * pg]
    o_ref, qs_scr, s_scr, v_scr = rest[2 * pg:]
    t = pl.program_id(1)
    n_steps = n_pages // pg
    n_hc = 2 * N_HEADS
    page = k_refs[0].shape[1]
    scale = DKB ** -0.5
    rnd = lambda z: z.astype(bf16).astype(f32)

    @pl.when(t == 0)
    def _():
        q = jnp.broadcast_to(q_ref[...], (n_hc, GROUP_W))
        keep = (_iota(q.shape, 1) // DKB) == _iota(q.shape, 0)
        qs_scr[...] = jnp.where(keep, q, 0.0)

    qs_b = qs_scr[...].astype(bf16)
    parts = []
    for g in range(pg):
        s = _dg(qs_b, k_refs[g][...].astype(bf16), NN) * scale
        is_last = (t * pg + g) == (n_pages - 1)
        parts.append(s + jnp.where(is_last, blast_ref[...], 0.0))
        v_scr[t * pg + g] = v_refs[g][...].astype(bf16)
    s_scr[t] = jnp.concatenate(parts, axis=1)

    @pl.when(t == n_steps - 1)
    def _():
        s_all = s_scr[...]
        s_self = jnp.sum(rnd(qs_scr[...]) * rnd(kn_ref[...]), axis=-1, keepdims=True) * scale + bself_ref[...]
        m = jnp.maximum(jnp.max(jnp.max(s_all, axis=2, keepdims=True), axis=0), s_self)
        p = jnp.exp(s_all - m)
        p_self = jnp.exp(s_self - m)
        l = jnp.sum(jnp.sum(p, axis=2, keepdims=True), axis=0) + p_self
        inv_l = _recip(l)
        pn = p * inv_l
        pn_self = p_self * inv_l
        lam = lam_ref[0]
        rows = [pn[:, 2 * h:2 * h + 1, :] - lam * pn[:, 2 * h + 1:2 * h + 2, :] for h in range(N_HEADS)]
        s_scr[...] = jnp.concatenate(rows + [jnp.zeros_like(rows[0])] * N_HEADS, axis=1)
        rows_self = [pn_self[2 * h:2 * h + 1] - lam * pn_self[2 * h + 1:2 * h + 2] for h in range(N_HEADS)]
        a_self = jnp.concatenate(rows_self + [jnp.zeros_like(rows_self[0])] * N_HEADS, axis=0)

        def weighted_values(t2, acc):
            a = s_scr[t2].astype(bf16)
            for g in range(pg):
                acc = acc + _dg(a[:, g * page:(g + 1) * page], v_scr[t2 * pg + g], NT)
            return acc

        o = lax.fori_loop(0, n_steps, weighted_values, rnd(a_self) * rnd(vn_ref[...]))
        lane_head = _iota((1, GROUP_W), 1) // HEAD_W
        out = jnp.zeros((1, GROUP_W), f32)
        for h in range(N_HEADS):
            out = jnp.where(lane_head == h, o[h:h + 1], out)
        o_ref[...] = out


def _attn_decode(qn, kn, vn, page_table, cache_k, cache_v, layer, lam, rel_bias):
    n_b, n_pages = page_table.shape
    page = cache_k.shape[3]
    pg = min(16, n_pages)
    n_steps = n_pages // pg
    past = n_pages * page
    tab = _shifted_bias(rel_bias)
    d_last = past - ((n_pages - 1) * page + np.arange(page))
    blast = jnp.repeat(jnp.take(tab, jnp.asarray(_t5_bucket_np(d_last)), axis=0).T, 2, axis=0)
    bself = jnp.repeat(tab[0].reshape(N_HEADS, 1), 2, axis=0)

    def page_spec(g):
        return pl.BlockSpec((None, None, GROUP_W, page), lambda b_, t_, pt: (layer, pt[b_, t_ * pg + g], 0, 0))

    row = pl.BlockSpec((None, 1, GROUP_W), lambda b_, t_, pt: (b_, 0, 0))
    grid_spec = pltpu.PrefetchScalarGridSpec(
        num_scalar_prefetch=1,
        grid=(n_b, n_steps),
        in_specs=[pl.BlockSpec(memory_space=pltpu.SMEM), row, row, row,
                  pl.BlockSpec((2 * N_HEADS, page), lambda b_, t_, pt: (0, 0)),
                  pl.BlockSpec((2 * N_HEADS, 1), lambda b_, t_, pt: (0, 0))]
                 + [page_spec(g) for g in range(pg)] * 2,
        out_specs=row,
        scratch_shapes=[
            pltpu.VMEM((2 * N_HEADS, GROUP_W), f32),
            pltpu.VMEM((n_steps, 2 * N_HEADS, pg * page), f32),
            pltpu.VMEM((n_pages, GROUP_W, page), bf16),
        ],
    )
    r3 = lambda z: z.reshape(n_b, 1, GROUP_W)
    out = pl.pallas_call(
        functools.partial(_attn_decode_kernel, pg=pg, n_pages=n_pages),
        grid_spec=grid_spec,
        out_shape=jax.ShapeDtypeStruct((n_b, 1, GROUP_W), f32),
        compiler_params=_cparams("arbitrary", "arbitrary"),
        name="attn_decode",
    )(page_table, lam.reshape(1), r3(qn), r3(kn), r3(vn), blast, bself,
      *([cache_k] * pg), *([cache_v] * pg))
    return out.reshape(n_b, GROUP_W)


def _tri(n, dtype=f32):
    return (_iota((n, n), 0) >= _iota((n, n), 1)).astype(dtype)


def _block_tri_t(n, blk):
    r = _iota((n, n), 0)
    c = _iota((n, n), 1)
    return (((r // blk) == (c // blk)) & (r <= c)).astype(bf16)


def _block_tri(n, blk):
    r = _iota((n, n), 0)
    c = _iota((n, n), 1)
    return (((r // blk) == (c // blk)) & (r >= c)).astype(bf16)


def _gdn_kernel(u_ref, gc_ref, gr_ref, cw_ref, alr_ref, dtr_ref, alc_ref, dtc_ref, o_ref, s_out_ref, ext_scr, s_scr, *, tb):
    i = pl.program_id(1)

    @pl.when(i == 0)
    def _():
        ext_scr[0:8, :] = jnp.zeros((8, 3 * GROUP_W), f32)
        s_scr[...] = jnp.zeros(s_scr.shape, f32)

    ext_scr[8:8 + tb, :] = u_ref[...]
    w = cw_ref[...]
    conv = ext_scr[8:8 + tb, :] * w[3:4]
    for jj in range(1, CONV_W):
        conv = conv + ext_scr[8 - jj:8 - jj + tb, :] * w[3 - jj:4 - jj]
    ext_scr[0:8, :] = ext_scr[tb:tb + 8, :]
    qkv = _silu(conv)

    def l2n(x):
        return x * lax.rsqrt(_group_sum(x * x, HEAD_W) + EPS)

    q = l2n(qkv[:, 0:GROUP_W]) * (HEAD_W ** -0.5)
    k = l2n(qkv[:, GROUP_W:2 * GROUP_W])
    v = qkv[:, 2 * GROUP_W:3 * GROUP_W]

    gc = gc_ref[...]
    g_col = -jnp.exp(alr_ref[...]) * _softplus(gc[:, 0:4] + dtr_ref[...])
    beta_col = jax.nn.sigmoid(gc[:, 4:8])
    gr = gr_ref[...]
    g_row = -jnp.exp(alc_ref[...]) * _softplus(gr[0:4, :] + dtc_ref[...])
    g_row8 = jnp.concatenate([g_row, jnp.zeros_like(g_row)], axis=0)
    gcum_row = _mm2(g_row8, _block_tri_t(tb, CHUNK))

    same, lower, strict = _bd_masks()
    tri = _tri(CHUNK, bf16)
    r = _iota((GROUP_W, GROUP_W), 0)
    c = _iota((GROUP_W, GROUP_W), 1)
    eye = (r == c).astype(f32)

    chunks = range(tb // CHUNK)
    gcums, qks, m_bds = [], [], []
    for ch in chunks:
        lo = ch * CHUNK
        gcum = _mm2l(tri, g_col[lo:lo + CHUNK])
        g_stack = _stack_cols(gcum)
        g_cat = _cat_rows(gcum_row, lo)
        decay = jnp.exp(jnp.where(lower, g_stack - g_cat, NEG))
        ksm = _head_stack(k[lo:lo + CHUNK])
        kk = _mm1(ksm, ksm, NT)
        qks.append(_mm1(_head_stack(q[lo:lo + CHUNK]), ksm, NT) * decay)
        m_bds.append(_stack_cols(beta_col[lo:lo + CHUNK]) * kk * jnp.where(strict, decay, 0.0))
        gcums.append(gcum)

    def sibling(lev):
        return ((r >> (lev + 1)) == (c >> (lev + 1))) & (((r >> lev) & 1) == 1) & (((c >> lev) & 1) == 0)

    xs = [eye - jnp.where(sibling(0), m, 0.0) for m in m_bds]
    for lev in range(1, 6):
        sel = sibling(lev)
        xs = [x - _mm3(_mm2(x, jnp.where(sel, m, 0.0).astype(bf16)), x) for x, m in zip(xs, m_bds)]

    for ch in chunks:
        lo = ch * CHUNK
        qc, kc, vc = q[lo:lo + CHUNK], k[lo:lo + CHUNK], v[lo:lo + CHUNK]
        gcum, bcol = gcums[ch], beta_col[lo:lo + CHUNK]
        s_bd = s_scr[...]
        kq_s = _mm1(jnp.concatenate([kc, qc], axis=0), s_bd)
        ks, qs = kq_s[0:CHUNK], kq_s[CHUNK:2 * CHUNK]
        eg_all = _expand_cols(jnp.exp(gcum))
        rhs = _expand_cols(bcol) * (vc - eg_all * ks)
        u_sm = _mm3(xs[ch], _head_stack(rhs))
        o_sm = _mm1(qks[ch], u_sm)
        o_ref[lo:lo + CHUNK, :] = eg_all * qs + _fold_heads(o_sm)
        u_all = _fold_heads(u_sm)
        g_last = gcum[CHUNK - 1:CHUNK, :]
        kw = kc * _expand_cols(jnp.exp(g_last - gcum))
        d_stack = jnp.concatenate(
            [jnp.broadcast_to(jnp.exp(g_last[:, h:h + 1]), (HEAD_W, 1)) for h in range(N_HEADS)], axis=0)
        s_scr[...] = d_stack * s_bd + jnp.where(same, _mm1(kw, u_all, TN), 0.0)

    @pl.when(i == pl.num_programs(1) - 1)
    def _():
        s_out_ref[...] = s_scr[...]


def _gdn_prompt(p, gt, conv_w, a_log, dt_bias, n_batch, seq):
    tb = min(256, seq)
    nb = seq // tb
    r14 = lambda z: z.astype(f32).reshape(1, N_HEADS)
    c41 = lambda z: z.astype(f32).reshape(N_HEADS, 1)
    o, s_bd = pl.pallas_call(
        functools.partial(_gdn_kernel, tb=tb),
        grid=(n_batch, nb),
        in_specs=[
            pl.BlockSpec((tb, 3 * GROUP_W), lambda b, i: (b * nb + i, 0)),
            pl.BlockSpec((tb, 128), lambda b, i: (b * nb + i, GATE_COL // 128)),
            pl.BlockSpec((16, tb), lambda b, i: (0, b * nb + i)),
            pl.BlockSpec((CONV_W, 3 * GROUP_W), lambda b, i: (0, 0)),
            pl.BlockSpec((1, N_HEADS), lambda b, i: (0, 0)),
            pl.BlockSpec((1, N_HEADS), lambda b, i: (0, 0)),
            pl.BlockSpec((N_HEADS, 1), lambda b, i: (0, 0)),
            pl.BlockSpec((N_HEADS, 1), lambda b, i: (0, 0)),
        ],
        out_specs=[
            pl.BlockSpec((tb, GROUP_W), lambda b, i: (b * nb + i, 0)),
            pl.BlockSpec((None, GROUP_W, GROUP_W), lambda b, i: (b, 0, 0)),
        ],
        out_shape=[jax.ShapeDtypeStruct((n_batch * seq, GROUP_W), f32),
                   jax.ShapeDtypeStruct((n_batch, GROUP_W, GROUP_W), f32)],
        scratch_shapes=[pltpu.VMEM((tb + 8, 3 * GROUP_W), f32), pltpu.VMEM((GROUP_W, GROUP_W), f32)],
        compiler_params=_cparams("arbitrary", "arbitrary"),
        name="gdn_prompt",
    )(p, p, gt, conv_w.astype(f32), r14(a_log), r14(dt_bias), c41(a_log), c41(dt_bias))
    return o, _bd_diag(s_bd)


def _bd_diag(s_bd):
    n_b = s_bd.shape[0]
    s5 = s_bd.reshape(n_b, N_HEADS, HEAD_W, N_HEADS, HEAD_W)
    return jnp.stack([s5[:, h, :, h, :] for h in range(N_HEADS)], axis=1)


def _hgrn_kernel(q_ref, f_ref, i_ref, lb_ref, o_ref, s_out_ref, st_scr, q_scr, k_scr, b_scr, *, tb):
    blk = pl.program_id(1)

    @pl.when(blk == 0)
    def _():
        st_scr[...] = jnp.zeros(st_scr.shape, f32)

    lb = lb_ref[...]
    z = f_ref[...]
    logf = jnp.log(lb + (1.0 - lb) * jax.nn.sigmoid(z))
    q_scr[...] = _silu(q_ref[...])
    k_scr[...] = (1.0 - lb) * jax.nn.sigmoid(-z)
    b_scr[...] = _mm2l(_block_tri(tb, SUB), logf)

    same, _, _ = _bd_masks()
    ones_bd = _group_ones(GROUP_W, HEAD_W)
    row = _iota((SUB * SUB, GROUP_W), 0)
    tmask = (row % SUB) >= (row // SUB)

    def rep_t(x):
        return jnp.broadcast_to(x[None], (SUB, SUB, GROUP_W)).reshape(SUB * SUB, GROUP_W)

    def rep_j(x):
        return jnp.broadcast_to(x[:, None, :], (SUB, SUB, GROUP_W)).reshape(SUB * SUB, GROUP_W)

    def body(c, carry):
        r0 = pl.multiple_of(c * SUB, SUB)
        qs = q_scr[pl.ds(r0, SUB), :]
        ks = k_scr[pl.ds(r0, SUB), :]
        vs = i_ref[pl.ds(r0, SUB), :]
        bs = b_scr[pl.ds(r0, SUB), :]
        st = st_scr[...]
        o_inter = _mm1(qs * jnp.exp(bs), st, NT)
        wgt = rep_t(qs) * jnp.exp(jnp.where(tmask, rep_t(bs) - rep_j(bs), NEG)) * rep_j(ks)
        a = _mm2(wgt, ones_bd)
        o_diag = jnp.sum((a * rep_j(vs)).reshape(SUB, SUB, GROUP_W), axis=0)
        o_ref[pl.ds(r0, SUB), :] = o_inter + o_diag
        b_last = bs[SUB - 1:SUB, :]
        kw = ks * jnp.exp(b_last - bs)
        st_scr[...] = st * jnp.exp(b_last) + jnp.where(same, _mm1(vs, kw, TN), 0.0)
        return carry

    lax.fori_loop(0, tb // SUB, body, 0, unroll=2)

    @pl.when(blk == pl.num_programs(1) - 1)
    def _():
        s_out_ref[...] = st_scr[...]


def _hgrn_prompt(p, lb, n_batch, seq):
    tb = min(256, seq)
    nb = seq // tb
    blk = lambda col: pl.BlockSpec((tb, GROUP_W), lambda b, i: (b * nb + i, col))
    o, st = pl.pallas_call(
        functools.partial(_hgrn_kernel, tb=tb),
        grid=(n_batch, nb),
        in_specs=[blk(7), blk(8), blk(9), pl.BlockSpec((1, GROUP_W), lambda b, i: (0, 0))],
        out_specs=[
            pl.BlockSpec((tb, GROUP_W), lambda b, i: (b * nb + i, 0)),
            pl.BlockSpec((None, GROUP_W, GROUP_W), lambda b, i: (b, 0, 0)),
        ],
        out_shape=[jax.ShapeDtypeStruct((n_batch * seq, GROUP_W), f32),
                   jax.ShapeDtypeStruct((n_batch, GROUP_W, GROUP_W), f32)],
        scratch_shapes=[pltpu.VMEM((GROUP_W, GROUP_W), f32)] + [pltpu.VMEM((tb, GROUP_W), f32)] * 3,
        compiler_params=_cparams("arbitrary", "arbitrary"),
        name="hgrn_prompt",
    )(p, p, p, lb.astype(f32).reshape(1, GROUP_W))
    return o, jnp.swapaxes(_bd_diag(st), -1, -2)


def _log_sigmoid(x):
    return jnp.minimum(x, 0.0) - jnp.log1p(jnp.exp(-jnp.abs(x)))


def _mlstm_kernel(q_ref, k_ref, v_ref, gc_ref, gr_ref, ibr_ref, fbr_ref, ibc_ref, fbc_ref,
                  o_ref, c_out_ref, n_out_ref, m_out_ref, c_scr, n_scr, m_scr, *, tb):
    blk = pl.program_id(1)

    @pl.when(blk == 0)
    def _():
        c_scr[...] = jnp.zeros(c_scr.shape, f32)
        n_scr[...] = jnp.zeros(n_scr.shape, f32)
        m_scr[...] = jnp.zeros(m_scr.shape, f32)

    q = q_ref[...]
    k = k_ref[...] * (HEAD_W ** -0.5)
    v = v_ref[...]
    gc = gc_ref[...]
    li_col = gc[:, 8:12] + ibr_ref[...]
    lf_col = _log_sigmoid(gc[:, 12:16] + fbr_ref[...])
    gr = gr_ref[...]
    li_row = gr[8:12, :] + ibc_ref[...]
    lf_row = _log_sigmoid(gr[12:16, :] + fbc_ref[...])
    b_row = _mm2(jnp.concatenate([lf_row, jnp.zeros_like(lf_row)], axis=0), _block_tri_t(tb, CHUNK))

    same, lower, _ = _bd_masks()
    tri = _tri(CHUNK, bf16)

    for ch in range(tb // CHUNK):
        lo = ch * CHUNK
        qc, kc, vc = q[lo:lo + CHUNK], k[lo:lo + CHUNK], v[lo:lo + CHUNK]
        b_col = _mm2l(tri, lf_col[lo:lo + CHUNK])
        b_stack = _stack_cols(b_col)
        d_mat = jnp.where(lower, b_stack - _cat_rows(b_row, lo) + _cat_rows(li_row, lo), NEG)
        m_row = m_scr[...]
        m_stack = jnp.concatenate(
            [jnp.broadcast_to(m_row[:, h:h + 1], (CHUNK, 1)) for h in range(N_HEADS)], axis=0)
        inter = b_stack + m_stack
        m_t = jnp.maximum(inter, jnp.max(d_mat, axis=-1, keepdims=True))
        w_inter = jnp.exp(inter - m_t)
        qsm = _head_stack(qc)
        ksm = _head_stack(kc)
        pmat = _mm1(qsm, ksm, NT) * jnp.exp(d_mat - m_t)
        c_bd = c_scr[...]
        n_row = n_scr[...]
        num = w_inter * _mm1(qsm, c_bd) + _mm1(pmat, _head_stack(vc))
        den = w_inter * jnp.sum(qsm * n_row, axis=-1, keepdims=True) + jnp.sum(pmat, axis=-1, keepdims=True)
        h_sm = num / jnp.maximum(jnp.abs(den), jnp.exp(-m_t))
        o_ref[lo:lo + CHUNK, :] = _fold_heads(h_sm)
        m_new = jnp.concatenate(
            [m_t[h * CHUNK + CHUNK - 1:h * CHUNK + CHUNK, :] for h in range(N_HEADS)], axis=1)
        b_last = b_col[CHUNK - 1:CHUNK, :]
        w_end = jnp.exp(b_last - b_col + li_col[lo:lo + CHUNK] - m_new)
        d0 = jnp.exp(b_last + m_row - m_new)
        kw = kc * _expand_cols(w_end)
        d0_stack = jnp.concatenate(
            [jnp.broadcast_to(d0[:, h:h + 1], (HEAD_W, 1)) for h in range(N_HEADS)], axis=0)
        c_scr[...] = d0_stack * c_bd + jnp.where(same, _mm1(kw, vc, TN), 0.0)
        n_scr[...] = _expand_cols(d0) * n_row + jnp.sum(kw, axis=0, keepdims=True)
        m_scr[...] = m_new

    @pl.when(blk == pl.num_programs(1) - 1)
    def _():
        c_out_ref[...] = c_scr[...]
        n_out_ref[...] = n_scr[...]
        m_out_ref[...] = m_scr[...]


def _mlstm_prompt(p, gt, i_bias, f_bias, n_batch, seq):
    tb = min(256, seq)
    nb = seq // tb
    blk = lambda col: pl.BlockSpec((tb, GROUP_W), lambda b, i: (b * nb + i, col))
    r14 = lambda z: z.astype(f32).reshape(1, N_HEADS)
    c41 = lambda z: z.astype(f32).reshape(N_HEADS, 1)
    small = lambda shape: pl.BlockSpec(shape, lambda b, i: (0, 0))
    o, c_bd, n_row, m_row = pl.pallas_call(
        functools.partial(_mlstm_kernel, tb=tb),
        grid=(n_batch, nb),
        in_specs=[blk(11), blk(12), blk(13),
                  pl.BlockSpec((tb, 128), lambda b, i: (b * nb + i, GATE_COL // 128)),
                  pl.BlockSpec((16, tb), lambda b, i: (0, b * nb + i)),
                  small((1, N_HEADS)), small((1, N_HEADS)), small((N_HEADS, 1)), small((N_HEADS, 1))],
        out_specs=[
            pl.BlockSpec((tb, GROUP_W), lambda b, i: (b * nb + i, 0)),
            pl.BlockSpec((None, GROUP_W, GROUP_W), lambda b, i: (b, 0, 0)),
            pl.BlockSpec((None, 1, GROUP_W), lambda b, i: (b, 0, 0)),
            pl.BlockSpec((None, 1, N_HEADS), lambda b, i: (b, 0, 0)),
        ],
        out_shape=[jax.ShapeDtypeStruct((n_batch * seq, GROUP_W), f32),
                   jax.ShapeDtypeStruct((n_batch, GROUP_W, GROUP_W), f32),
                   jax.ShapeDtypeStruct((n_batch, 1, GROUP_W), f32),
                   jax.ShapeDtypeStruct((n_batch, 1, N_HEADS), f32)],
        scratch_shapes=[pltpu.VMEM((GROUP_W, GROUP_W), f32), pltpu.VMEM((1, GROUP_W), f32),
                        pltpu.VMEM((1, N_HEADS), f32)],
        compiler_params=_cparams("arbitrary", "arbitrary"),
        name="mlstm_prompt",
    )(p, p, p, p, gt, r14(i_bias), r14(f_bias), c41(i_bias), c41(f_bias))
    return (o, _bd_diag(c_bd), n_row.reshape(n_batch, N_HEADS, HEAD_W), m_row.reshape(n_batch, N_HEADS))


def _gdn_dec_prep_kernel(u_ref, buf_ref, cw_ref, q_ref, k_ref, v_ref):
    w = cw_ref[...]
    conv = u_ref[...] * w[3:4]
    for jj in range(CONV_W - 1):
        conv = conv + buf_ref[jj] * w[jj:jj + 1]
    qkv = _silu(conv)

    def l2n(x):
        return x * lax.rsqrt(_group_sum(x * x, HEAD_W) + EPS)

    q_ref[...] = l2n(qkv[:, 0:GROUP_W]) * (HEAD_W ** -0.5)
    k_ref[...] = l2n(qkv[:, GROUP_W:2 * GROUP_W])
    v_ref[...] = qkv[:, 2 * GROUP_W:3 * GROUP_W]


def _gdn_dec_prep(p, conv_buf, conv_w):
    n_b = p.shape[0]
    out = jax.ShapeDtypeStruct((n_b, GROUP_W), f32)
    return pl.pallas_call(
        _gdn_dec_prep_kernel,
        grid=(1,),
        in_specs=[pl.BlockSpec((n_b, 3 * GROUP_W), lambda i: (0, 0)),
                  pl.BlockSpec((CONV_W - 1, n_b, 3 * GROUP_W), lambda i: (0, 0, 0)),
                  pl.BlockSpec((CONV_W, 3 * GROUP_W), lambda i: (0, 0))],
        out_specs=[pl.BlockSpec((n_b, GROUP_W), lambda i: (0, 0))] * 3,
        out_shape=[out, out, out],
        compiler_params=_cparams("arbitrary"),
        name="gdn_dec_prep",
    )(p, jnp.swapaxes(conv_buf.astype(f32), 0, 1), conv_w.astype(f32))


def _rec_decode_kernel(cols_ref, vrows_ref, scal_ref, sg_ref, sh_ref, sc_ref,
                       o_ref, sg_out, sh_out, sc_out, sn_out, sm_out):
    cols = cols_ref[...]
    vrows = vrows_ref[...]
    scal = scal_ref[...]
    col = lambda k: cols[:, :, k:k + 1]
    vrow = lambda k: vrows[:, k:k + 1, :]
    sca = lambda k: scal[:, :, k:k + 1]

    q, k, v = col(0), col(1), vrow(0)
    s = sg_ref[...]
    g = -jnp.exp(sca(2)) * _softplus(sca(0) + sca(3))
    eg = jnp.exp(g)
    beta = jax.nn.sigmoid(sca(1))
    ks = jnp.sum(k * s, axis=1, keepdims=True)
    qs = jnp.sum(q * s, axis=1, keepdims=True)
    u = beta * (v - eg * ks)
    qk = jnp.sum(q * k, axis=1, keepdims=True)
    o_ref[:, 0:1, :] = eg * qs + qk * u
    sg_out[...] = eg * s + k * u

    lb = col(6)
    z = col(3)
    logf = jnp.log(lb + (1.0 - lb) * jax.nn.sigmoid(z))
    kc = (1.0 - lb) * jax.nn.sigmoid(-z)
    qc = _silu(col(2))
    vc = vrow(1)
    sh = sh_ref[...]
    ef = jnp.exp(logf)
    o_ref[:, 1:2, :] = (jnp.sum((qc * ef) * sh, axis=1, keepdims=True)
                        + jnp.sum(qc * kc, axis=1, keepdims=True) * vc)
    sh_out[...] = ef * sh + kc * vc

    qd = col(4)
    kd = col(5) * (HEAD_W ** -0.5)
    vd = vrow(2)
    li = sca(4) + sca(6)
    lf = _log_sigmoid(sca(5) + sca(7))
    m0 = sca(8)
    cs = sc_ref[...]
    n0 = col(7)
    inter = lf + m0
    m_t = jnp.maximum(inter, li)
    w_inter = jnp.exp(inter - m_t)
    qkd = jnp.sum(qd * kd, axis=1, keepdims=True) * jnp.exp(li - m_t)
    num = w_inter * jnp.sum(qd * cs, axis=1, keepdims=True) + qkd * vd
    den = w_inter * jnp.sum(qd * n0, axis=1, keepdims=True) + qkd
    o_ref[:, 2:3, :] = num / jnp.maximum(jnp.abs(den), jnp.exp(-m_t))
    w_end = jnp.exp(li - m_t)
    d0 = jnp.exp(lf + m0 - m_t)
    sc_out[...] = d0 * cs + (w_end * kd) * vd
    sn_out[...] = d0 * n0 + w_end * kd
    sm_out[...] = m_t


def _rec_decode(p, gq, gk, gv, a_log, dt_bias, lb, i_bias, f_bias, s_gdn, s_hgrn, s_c, s_n, s_m):
    n_b = p.shape[0]
    rows = n_b * N_HEADS
    rb = min(16, rows)
    rw = lambda z: z.astype(f32).reshape(rows, HEAD_W)
    blockp = lambda b: rw(p[:, b * GROUP_W:(b + 1) * GROUP_W])
    per_head = lambda z: jnp.tile(z.astype(f32), n_b)
    gate = lambda c: p[:, GATE_COL + c:GATE_COL + c + N_HEADS].reshape(rows)
    lb_rows = jnp.tile(lb.astype(f32).reshape(N_HEADS, HEAD_W), (n_b, 1))
    cols = jnp.stack([rw(gq), rw(gk), blockp(7), blockp(8), blockp(11), blockp(12), lb_rows, rw(s_n)], axis=-1)
    vrows = jnp.stack([rw(gv), blockp(9), blockp(13)], axis=1)
    scal = jnp.stack([gate(0), gate(4), per_head(a_log), per_head(dt_bias), gate(8), gate(12),
                      per_head(i_bias), per_head(f_bias), s_m.astype(f32).reshape(rows)], axis=-1).reshape(rows, 1, 9)
    st = lambda z: z.astype(f32).reshape(rows, HEAD_W, HEAD_W)
    args = [cols, vrows, scal, st(s_gdn), st(s_hgrn), st(s_c)]

    def spec(a):
        return pl.BlockSpec((rb,) + a.shape[1:], lambda i: (i, 0, 0))

    o_st = jax.ShapeDtypeStruct((rows, HEAD_W, HEAD_W), f32)
    outs = [jax.ShapeDtypeStruct((rows, 3, HEAD_W), f32), o_st, o_st, o_st,
            jax.ShapeDtypeStruct((rows, HEAD_W, 1), f32), jax.ShapeDtypeStruct((rows, 1, 1), f32)]
    o, sg, sh, sc, sn, sm = pl.pallas_call(
        _rec_decode_kernel,
        grid=(rows // rb,),
        in_specs=[spec(a) for a in args],
        out_specs=[spec(a) for a in outs],
        out_shape=outs,
        compiler_params=_cparams("arbitrary"),
        name="rec_decode",
    )(*args)
    s4 = lambda z: z.reshape(n_b, N_HEADS, HEAD_W, HEAD_W)
    o2 = lambda k: o[:, k, :].reshape(n_b, GROUP_W)
    return (o2(0), s4(sg), o2(1), s4(sh), o2(2), s4(sc),
            sn.reshape(n_b, N_HEADS, HEAD_W), sm.reshape(n_b, N_HEADS))


def _permute_w_in(w):
    d_in = w.shape[1]
    a_gate0 = 3 * GROUP_W
    d_gate0 = d_in - GROUP_W - 8
    main = jnp.concatenate([w[:, 0:a_gate0], w[:, a_gate0 + 8:d_gate0], w[:, d_gate0 + 8:]], axis=1)
    gates = jnp.concatenate([w[:, a_gate0:a_gate0 + 8], w[:, d_gate0:d_gate0 + 8]], axis=1)
    pad = jnp.zeros((w.shape[0], P_COLS - main.shape[1] - 16), w.dtype)
    return jnp.concatenate([main, gates, pad], axis=1).astype(bf16), gates.T.astype(bf16)


def kernel(x_prompt, x_sample, page_table, cache_k, cache_v, state_gdn_conv, state_gdn, state_hgrn, state_mlstm_C, state_mlstm_n, state_mlstm_m, attn_norm_g, w_in, gdn_conv_w, gdn_a_log, gdn_dt_bias, gdn_norm_g, diff_qk_norm_g, diff_lambda, diff_subln_g, rel_bias, hgrn_lb_logits, hgrn_norm_g, mlstm_i_bias, mlstm_f_bias, mlstm_norm_g, w_out, ffn_norm_g, ffn_w_gate, ffn_w_up, ffn_w_down, moe_router, moe_w_gate, moe_w_up, moe_w_down):
    depth = w_in.shape[0]
    n_bp, seq, _ = x_prompt.shape
    n_bs = x_sample.shape[0]
    n_pool, page = cache_k.shape[1], cache_k.shape[2]
    dt = x_prompt.dtype

    lb_p = jax.nn.softmax(hgrn_lb_logits.astype(f32), axis=0)
    lb_cum = jnp.cumsum(lb_p, axis=0)
    hgrn_lb = lb_cum - lb_cum[0:1]
    cache_k4 = jnp.transpose(cache_k, (0, 1, 3, 4, 5, 2)).reshape(depth, n_pool, GROUP_W, page)
    cache_v4 = jnp.transpose(cache_v, (0, 1, 3, 4, 2)).reshape(depth, n_pool, GROUP_W, page)

    xp = x_prompt.reshape(n_bp * seq, D_MODEL)
    xs = x_sample.reshape(n_bs, D_MODEL)
    outs_p, outs_s = [], []
    for l in range(depth):
        w_perm, w_gate_t = _permute_w_in(w_in[l])
        w_out_b = w_out[l].astype(bf16)
        gains = jnp.stack([jnp.tile(g.astype(f32), N_HEADS) for g in
                           (gdn_norm_g[l], diff_subln_g[l], hgrn_norm_g[l], mlstm_norm_g[l])])
        lam_init = 0.8 - 0.6 * math.exp(-0.3 * l)
        lam32 = diff_lambda[l].astype(f32)
        lam = jnp.exp(jnp.sum(lam32[0] * lam32[1])) - jnp.exp(jnp.sum(lam32[2] * lam32[3])) + lam_init
        if l % 2 == 0:
            ffn_w = (ffn_w_gate[l // 2].astype(bf16), ffn_w_up[l // 2].astype(bf16), ffn_w_down[l // 2].astype(bf16))
        else:
            router_pad = jnp.pad(moe_router[l // 2].astype(bf16), ((0, 0), (0, 128 - N_EXPERTS)))
            moe_w = (moe_w_gate[l // 2].astype(bf16), moe_w_up[l // 2].astype(bf16), moe_w_down[l // 2].astype(bf16))

        def channel_mix(x):
            if l % 2 == 0:
                return _ffn(x, ffn_norm_g[l], *ffn_w)
            return _moe(x, ffn_norm_g[l], router_pad, *moe_w)

        p, gt = _inproj(xp, attn_norm_g[l], w_perm, w_gate_t)
        qnt, kn, vt = _bprep(p, diff_qk_norm_g[l], True)
        ob = _attn_prompt(qnt, kn, vt, lam, rel_bias, n_bp, seq)
        oa, s_gdn = _gdn_prompt(p, gt, gdn_conv_w[l], gdn_a_log[l], gdn_dt_bias[l], n_bp, seq)
        oc, s_hgrn = _hgrn_prompt(p, hgrn_lb[l], n_bp, seq)
        od, s_c, s_n, s_m = _mlstm_prompt(p, gt, mlstm_i_bias[l], mlstm_f_bias[l], n_bp, seq)
        xp = _outproj(oa, ob, oc, od, p, xp, gains, w_out_b, 1.0 - lam_init)
        xp = channel_mix(xp)
        p3 = p.reshape(n_bp, seq, P_COLS)
        outs_p.append((
            kn.reshape(n_bp, seq, N_HEADS, 2, DKB).astype(dt),
            p3[:, :, 6 * GROUP_W:7 * GROUP_W].reshape(n_bp, seq, N_HEADS, HEAD_W).astype(dt),
            p3[:, seq - (CONV_W - 1):, 0:3 * GROUP_W].astype(dt),
            s_gdn.astype(dt), s_hgrn.astype(dt), s_c.astype(dt), s_n.astype(dt), s_m.astype(dt)))

        p, gt = _inproj(xs, attn_norm_g[l], w_perm, w_gate_t)
        qn, kn = _bprep(p, diff_qk_norm_g[l], False)
        vn = p[:, 6 * GROUP_W:7 * GROUP_W]
        ob = _attn_decode(qn, kn, vn, page_table, cache_k4, cache_v4, l, lam, rel_bias)
        u = p[:, 0:3 * GROUP_W]
        gq, gk, gv = _gdn_dec_prep(u, state_gdn_conv[l], gdn_conv_w[l])
        oa, s_gdn, oc, s_hgrn, od, s_c, s_n, s_m = _rec_decode(
            p, gq, gk, gv, gdn_a_log[l], gdn_dt_bias[l], hgrn_lb[l], mlstm_i_bias[l], mlstm_f_bias[l],
            state_gdn[l], state_hgrn[l], state_mlstm_C[l], state_mlstm_n[l], state_mlstm_m[l])
        xs = _outproj(oa, ob, oc, od, p, xs, gains, w_out_b, 1.0 - lam_init)
        xs = channel_mix(xs)
        conv_new = jnp.concatenate([state_gdn_conv[l][:, 1:].astype(dt), u[:, None, :].astype(dt)], axis=1)
        outs_s.append((
            kn.reshape(n_bs, 1, N_HEADS, 2, DKB).astype(dt),
            vn.reshape(n_bs, 1, N_HEADS, HEAD_W).astype(dt),
            conv_new, s_gdn.astype(dt), s_hgrn.astype(dt), s_c.astype(dt), s_n.astype(dt), s_m.astype(dt)))

    kp, vp, convp, gdnp, hgrnp, mcp, mnp_, mmp = [jnp.stack(z) for z in zip(*outs_p)]
    ks_, vs_, convs, gdns, hgrns, mcs, mns, mms = [jnp.stack(z) for z in zip(*outs_s)]
    return (xp.reshape(n_bp, seq, D_MODEL), xs.reshape(n_bs, 1, D_MODEL), kp, vp, ks_, vs_, convp, convs,
            gdnp, gdns, hgrnp, hgrns, mcp, mcs, mnp_, mns, mmp, mms)
```

```python
import functools
import math

import numpy as np
import jax
import jax.numpy as jnp
from jax import lax
from jax.experimental import pallas as pl
from jax.experimental.pallas import tpu as pltpu

f32 = jnp.float32
bf16 = jnp.bfloat16

D_MODEL = 1024
N_HEADS = 4
HEAD_W = 64
GROUP_W = N_HEADS * HEAD_W
DKB = 32
CONV_W = 4
CHUNK = 64
SUB = 16
NUM_BUCKETS = 32
MAX_DISTANCE = 128
N_EXPERTS = 8
EPS = 1e-6
NEG = -1e30
P_COLS = 4096
GATE_COL = 3840
VMEM_LIMIT = 56 * 1024 * 1024

NN = ((1,), (0,))
NT = ((1,), (1,))
TN = ((0,), (0,))


def _dg(a, b, dims=NN):
    return lax.dot_general(a, b, (dims, ((), ())), preferred_element_type=f32)


def _split(a):
    hi = a.astype(bf16)
    lo = (a - hi.astype(f32)).astype(bf16)
    return hi, lo


def _mm3(a, b, dims=NN):
    ah, al = _split(a)
    bh, bl = _split(b)
    return _dg(ah, bh, dims) + (_dg(ah, bl, dims) + _dg(al, bh, dims))


def _mm2(a, b01, dims=NN):
    ah, al = _split(a)
    return _dg(ah, b01, dims) + _dg(al, b01, dims)


def _mm2l(a01, b, dims=NN):
    bh, bl = _split(b)
    return _dg(a01, bh, dims) + _dg(a01, bl, dims)


def _mm1(a, b, dims=NN):
    return _dg(a.astype(bf16), b.astype(bf16), dims)


def _iota(shape, dim):
    return lax.broadcasted_iota(jnp.int32, shape, dim)


def _group_ones(width, group):
    r = _iota((width, width), 0) // group
    c = _iota((width, width), 1) // group
    return (r == c).astype(bf16)


def _group_sum(x, group):
    ones = _group_ones(x.shape[-1], group)
    hi = x.astype(bf16)
    r1 = x - hi.astype(f32)
    mid = r1.astype(bf16)
    lo = (r1 - mid.astype(f32)).astype(bf16)
    return _dg(hi, ones) + (_dg(mid, ones) + _dg(lo, ones))


def _recip(x):
    r = 1.0 / x
    return r * (2.0 - x * r)


def _silu(x):
    return x * jax.nn.sigmoid(x)


def _softplus(x):
    return jnp.maximum(x, 0.0) + jnp.log1p(jnp.exp(-jnp.abs(x)))


def _stack_cols(xc, n=N_HEADS, rows=HEAD_W):
    return jnp.concatenate([xc[:, h:h + 1] for h in range(n)], axis=0)


def _expand_cols(xc, n=N_HEADS, width=HEAD_W):
    r = xc.shape[0]
    return jnp.concatenate([jnp.broadcast_to(xc[:, h:h + 1], (r, width)) for h in range(n)], axis=1)


def _cat_rows(xr, lo, n=N_HEADS, width=HEAD_W):
    return jnp.concatenate([xr[h:h + 1, lo:lo + width] for h in range(n)], axis=1)


def _head_stack(x, n=N_HEADS, width=HEAD_W):
    lane_head = _iota(x.shape, 1) // width
    return jnp.concatenate([jnp.where(lane_head == h, x, 0.0) for h in range(n)], axis=0)


def _fold_heads(x_sm, n=N_HEADS):
    r = x_sm.shape[0] // n
    out = x_sm[0:r]
    for h in range(1, n):
        out = out + x_sm[h * r:(h + 1) * r]
    return out


def _bd_masks(n=GROUP_W, blk=CHUNK):
    r = _iota((n, n), 0)
    c = _iota((n, n), 1)
    same = (r // blk) == (c // blk)
    lower = same & ((r % blk) >= (c % blk))
    strict = same & ((r % blk) > (c % blk))
    return same, lower, strict


def _cparams(*sem):
    return pltpu.CompilerParams(dimension_semantics=sem, vmem_limit_bytes=VMEM_LIMIT)


def _inproj_kernel(x_ref, g_ref, w_ref, wgt_ref, p_ref, gt_ref, h_scr):
    @pl.when(pl.program_id(1) == 0)
    def _():
        x = x_ref[...]
        ms = jnp.mean(x * x, axis=-1, keepdims=True)
        h = ((x * lax.rsqrt(ms + EPS)) * g_ref[...]).astype(bf16)
        h_scr[...] = h
        gt_ref[...] = _dg(wgt_ref[...], h, NT)
    p_ref[...] = _dg(h_scr[...], w_ref[...], NN)


def _inproj(x, g, w_perm, w_gate_t):
    t = x.shape[0]
    tm = min(1024, t)
    tn = 1024
    return pl.pallas_call(
        _inproj_kernel,
        grid=(t // tm, P_COLS // tn),
        in_specs=[
            pl.BlockSpec((tm, D_MODEL), lambda i, j: (i, 0)),
            pl.BlockSpec((1, D_MODEL), lambda i, j: (0, 0)),
            pl.BlockSpec((D_MODEL, tn), lambda i, j: (0, j)),
            pl.BlockSpec((16, D_MODEL), lambda i, j: (0, 0)),
        ],
        out_specs=[
            pl.BlockSpec((tm, tn), lambda i, j: (i, j)),
            pl.BlockSpec((16, tm), lambda i, j: (0, i)),
        ],
        out_shape=[jax.ShapeDtypeStruct((t, P_COLS), f32), jax.ShapeDtypeStruct((16, t), f32)],
        scratch_shapes=[pltpu.VMEM((tm, D_MODEL), bf16)],
        compiler_params=_cparams("arbitrary", "arbitrary"),
        name="inproj",
    )(x, g.reshape(1, D_MODEL), w_perm, w_gate_t)


def _qk_gnorm(x, g):
    ms = _group_sum(x * x, DKB) * (1.0 / DKB)
    return (x * lax.rsqrt(ms + EPS)) * g


def _bprep_kernel(q_ref, k_ref, gq_ref, gk_ref, qn_ref, kn_ref):
    qn_ref[...] = _qk_gnorm(q_ref[...], gq_ref[...])
    kn_ref[...] = _qk_gnorm(k_ref[...], gk_ref[...])


def _bprep_t_kernel(q_ref, k_ref, v_ref, gq_ref, gk_ref, qnt_ref, kn_ref, vt_ref):
    qnt_ref[...] = _qk_gnorm(q_ref[...], gq_ref[...]).T
    kn_ref[...] = _qk_gnorm(k_ref[...], gk_ref[...])
    vt_ref[...] = v_ref[...].T


def _bprep(p, qk_norm_g, transposed):
    t = p.shape[0]
    tm = min(512, t)
    gq = jnp.tile(qk_norm_g[0], GROUP_W // DKB).reshape(1, GROUP_W)
    gk = jnp.tile(qk_norm_g[1], GROUP_W // DKB).reshape(1, GROUP_W)
    col = lambda c: pl.BlockSpec((tm, GROUP_W), lambda i: (i, c))
    gain = pl.BlockSpec((1, GROUP_W), lambda i: (0, 0))
    rows = pl.BlockSpec((tm, GROUP_W), lambda i: (i, 0))
    rows_t = pl.BlockSpec((GROUP_W, tm), lambda i: (0, i))
    if transposed:
        return pl.pallas_call(
            _bprep_t_kernel,
            grid=(t // tm,),
            in_specs=[col(4), col(5), col(6), gain, gain],
            out_specs=[rows_t, rows, rows_t],
            out_shape=[jax.ShapeDtypeStruct((GROUP_W, t), f32), jax.ShapeDtypeStruct((t, GROUP_W), f32),
                       jax.ShapeDtypeStruct((GROUP_W, t), f32)],
            compiler_params=_cparams("arbitrary"),
            name="bprep_t",
        )(p, p, p, gq, gk)
    return pl.pallas_call(
        _bprep_kernel,
        grid=(t // tm,),
        in_specs=[col(4), col(5), gain, gain],
        out_specs=[rows, rows],
        out_shape=[jax.ShapeDtypeStruct((t, GROUP_W), f32)] * 2,
        compiler_params=_cparams("arbitrary"),
        name="bprep",
    )(p, p, gq, gk)


def _outproj_kernel(oa_ref, ob_ref, oc_ref, od_ref, ag_ref, cg_ref, dg_ref, x_ref, g_ref, w_ref, y_ref, *, b_scale):
    def gnorm(x, g):
        ms = _group_sum(x * x, HEAD_W) * (1.0 / HEAD_W)
        return (x * lax.rsqrt(ms + EPS)) * g
    g = g_ref[...]
    mixes = (
        gnorm(oa_ref[...], g[0:1]) * _silu(ag_ref[...]),
        gnorm(ob_ref[...], g[1:2]) * b_scale,
        gnorm(oc_ref[...], g[2:3]) * jax.nn.sigmoid(cg_ref[...]),
        gnorm(od_ref[...], g[3:4]) * jax.nn.sigmoid(dg_ref[...]),
    )
    y = x_ref[...]
    for i, m in enumerate(mixes):
        y = y + _dg(m.astype(bf16), w_ref[i * GROUP_W:(i + 1) * GROUP_W, :], NN)
    y_ref[...] = y


def _outproj(oa, ob, oc, od, p, x, gains, w_out, b_scale):
    t = x.shape[0]
    tm = min(512, t)
    row = lambda i: (i, 0)
    return pl.pallas_call(
        functools.partial(_outproj_kernel, b_scale=b_scale),
        grid=(t // tm,),
        in_specs=[
            pl.BlockSpec((tm, GROUP_W), row), pl.BlockSpec((tm, GROUP_W), row),
            pl.BlockSpec((tm, GROUP_W), row), pl.BlockSpec((tm, GROUP_W), row),
            pl.BlockSpec((tm, GROUP_W), lambda i: (i, 3)),
            pl.BlockSpec((tm, GROUP_W), lambda i: (i, 10)),
            pl.BlockSpec((tm, GROUP_W), lambda i: (i, 14)),
            pl.BlockSpec((tm, D_MODEL), row),
            pl.BlockSpec((4, GROUP_W), lambda i: (0, 0)),
            pl.BlockSpec((D_MODEL, D_MODEL), lambda i: (0, 0)),
        ],
        out_specs=pl.BlockSpec((tm, D_MODEL), row),
        out_shape=jax.ShapeDtypeStruct((t, D_MODEL), f32),
        compiler_params=_cparams("arbitrary"),
        name="outproj",
    )(oa, ob, oc, od, p, p, p, x, gains, w_out)


def _ffn_kernel(x_ref, g_ref, wg_ref, wu_ref, wd_ref, y_ref, h_scr):
    @pl.when(pl.program_id(1) == 0)
    def _():
        x = x_ref[...]
        ms = jnp.mean(x * x, axis=-1, keepdims=True)
        h_scr[...] = ((x * lax.rsqrt(ms + EPS)) * g_ref[...]).astype(bf16)
        y_ref[...] = x

    h = h_scr[...]
    a = _silu(_dg(h, wg_ref[...])) * _dg(h, wu_ref[...])
    y_ref[...] += _dg(a.astype(bf16), wd_ref[...])


def _ffn(x, g, wg, wu, wd):
    t = x.shape[0]
    d_ff = wg.shape[1]
    tm = min(1024, t)
    tf = d_ff // 2
    return pl.pallas_call(
        _ffn_kernel,
        grid=(t // tm, d_ff // tf),
        in_specs=[
            pl.BlockSpec((tm, D_MODEL), lambda i, f: (i, 0)),
            pl.BlockSpec((1, D_MODEL), lambda i, f: (0, 0)),
            pl.BlockSpec((D_MODEL, tf), lambda i, f: (0, f)),
            pl.BlockSpec((D_MODEL, tf), lambda i, f: (0, f)),
            pl.BlockSpec((tf, D_MODEL), lambda i, f: (f, 0)),
        ],
        out_specs=pl.BlockSpec((tm, D_MODEL), lambda i, f: (i, 0)),
        out_shape=jax.ShapeDtypeStruct((t, D_MODEL), f32),
        scratch_shapes=[pltpu.VMEM((tm, D_MODEL), bf16)],
        compiler_params=_cparams("arbitrary", "arbitrary"),
        name="ffn",
    )(x, g.reshape(1, D_MODEL), wg, wu, wd)


LANE = 128
SUBL = D_MODEL // LANE
ROW_TILE = 256
SPARSE_MIN_TOKENS = 4096


def _tile_rows(x):
    return [x[:, j * LANE:(j + 1) * LANE] for j in range(SUBL)]


def _router_kernel(x_ref, g_ref, r_ref, h3_ref, meta_ref, cnt_ref, carry_scr):
    @pl.when(pl.program_id(0) == 0)
    def _():
        carry_scr[...] = jnp.zeros(carry_scr.shape, f32)

    x = x_ref[...]
    tm = x.shape[0]
    ms = jnp.mean(x * x, axis=-1, keepdims=True)
    h = (x * lax.rsqrt(ms + EPS)) * g_ref[...]
    for j, blk in enumerate(_tile_rows(h)):
        h3_ref[:, j, :] = blk
    logits = _dg(h.astype(bf16), r_ref[...])
    lane = _iota(logits.shape, 1)
    logits = jnp.where(lane < N_EXPERTS, logits, -jnp.inf)
    v1 = jnp.max(logits, axis=-1, keepdims=True)
    i1 = jnp.min(jnp.where(logits == v1, lane, LANE), axis=-1, keepdims=True)
    rest = jnp.where(lane == i1, -jnp.inf, logits)
    v2 = jnp.max(rest, axis=-1, keepdims=True)
    i2 = jnp.min(jnp.where(rest == v2, lane, LANE), axis=-1, keepdims=True)
    e2 = jnp.exp(v2 - v1)
    den = 1.0 + e2
    hit = ((lane == i1) | (lane == i2)).astype(f32)
    strict = (_iota((tm, tm), 0) > _iota((tm, tm), 1)).astype(bf16)
    before = _dg(strict, hit.astype(bf16)) + carry_scr[...]
    pos1 = jnp.sum(jnp.where(lane == i1, before, 0.0), axis=-1, keepdims=True)
    pos2 = jnp.sum(jnp.where(lane == i2, before, 0.0), axis=-1, keepdims=True)
    carry_scr[...] += jnp.sum(hit, axis=0, keepdims=True)
    meta = jnp.zeros(logits.shape, f32)
    for c, val in enumerate((i1.astype(f32), i2.astype(f32), 1.0 / den, e2 / den, pos1, pos2)):
        meta = jnp.where(lane == c, val, meta)
    meta_ref[...] = meta
    cnt_ref[...] = carry_scr[...]


def _router(x, g, router_pad):
    t = x.shape[0]
    tm = min(512, t)
    return pl.pallas_call(
        _router_kernel,
        grid=(t // tm,),
        in_specs=[
            pl.BlockSpec((tm, D_MODEL), lambda i: (i, 0)),
            pl.BlockSpec((1, D_MODEL), lambda i: (0, 0)),
            pl.BlockSpec((D_MODEL, LANE), lambda i: (0, 0)),
        ],
        out_specs=[pl.BlockSpec((tm, SUBL, LANE), lambda i: (i, 0, 0)),
                   pl.BlockSpec((tm, LANE), lambda i: (i, 0)),
                   pl.BlockSpec((1, LANE), lambda i: (0, 0))],
        out_shape=[jax.ShapeDtypeStruct((t, SUBL, LANE), f32), jax.ShapeDtypeStruct((t, LANE), f32),
                   jax.ShapeDtypeStruct((1, LANE), f32)],
        scratch_shapes=[pltpu.VMEM((1, LANE), f32)],
        compiler_params=_cparams("arbitrary"),
        name="router",
    )(x, g.reshape(1, D_MODEL), router_pad)


def _swiglu_bf16(x, wg, wu, wd):
    a = _silu(_dg(x, wg)) * _dg(x, wu)
    return _dg(a.astype(bf16), wd)


def _moe_dense_kernel(x_ref, h3_ref, meta_ref, wg_ref, wu_ref, wd_ref, y_ref, acc_scr):
    e = pl.program_id(1)
    f = pl.program_id(2)

    @pl.when((e == 0) & (f == 0))
    def _():
        acc_scr[...] = x_ref[...]

    meta = meta_ref[...]
    ef = e.astype(f32)
    cw = jnp.where(meta[:, 0:1] == ef, meta[:, 2:3], 0.0) + jnp.where(meta[:, 1:2] == ef, meta[:, 3:4], 0.0)
    h = jnp.concatenate([h3_ref[:, j, :] for j in range(SUBL)], axis=1).astype(bf16)
    acc_scr[...] += cw * _swiglu_bf16(h, wg_ref[...], wu_ref[...], wd_ref[...])

    @pl.when((e == pl.num_programs(1) - 1) & (f == pl.num_programs(2) - 1))
    def _():
        y_ref[...] = acc_scr[...]


def _moe_dense(x, h3, meta, wg, wu, wd):
    t = x.shape[0]
    n_e, _, d_ff = wg.shape
    tm = min(512, t)
    tf = d_ff // 2
    return pl.pallas_call(
        _moe_dense_kernel,
        grid=(t // tm, n_e, d_ff // tf),
        in_specs=[
            pl.BlockSpec((tm, D_MODEL), lambda i, e, f: (i, 0)),
            pl.BlockSpec((tm, SUBL, LANE), lambda i, e, f: (i, 0, 0)),
            pl.BlockSpec((tm, LANE), lambda i, e, f: (i, 0)),
            pl.BlockSpec((None, D_MODEL, tf), lambda i, e, f: (e, 0, f)),
            pl.BlockSpec((None, D_MODEL, tf), lambda i, e, f: (e, 0, f)),
            pl.BlockSpec((None, tf, D_MODEL), lambda i, e, f: (e, f, 0)),
        ],
        out_specs=pl.BlockSpec((tm, D_MODEL), lambda i, e, f: (i, 0)),
        out_shape=jax.ShapeDtypeStruct((t, D_MODEL), f32),
        scratch_shapes=[pltpu.VMEM((tm, D_MODEL), f32)],
        compiler_params=_cparams("arbitrary", "arbitrary", "arbitrary"),
        name="moe_dense",
    )(x, h3, meta, wg, wu, wd)


def _route_plan(meta, counts, t):
    cnt = counts[0, :N_EXPERTS].astype(jnp.int32)
    padded = ((cnt + ROW_TILE - 1) // ROW_TILE) * ROW_TILE
    ends = jnp.cumsum(padded)
    offs = ends - padded
    experts = jnp.arange(N_EXPERTS, dtype=jnp.int32)

    def dest(expert_col, rank_col):
        e = meta[:, expert_col].astype(jnp.int32)
        off = jnp.sum(jnp.where(e[:, None] == experts[None, :], offs[None, :], 0), axis=1)
        return off + meta[:, rank_col].astype(jnp.int32)

    n_rows = 2 * t + N_EXPERTS * ROW_TILE
    starts = jnp.arange(n_rows // ROW_TILE, dtype=jnp.int32) * ROW_TILE
    tile_expert = jnp.minimum(jnp.sum((starts[:, None] >= ends[None, :]).astype(jnp.int32), axis=1), N_EXPERTS - 1)
    n_used = (ends[N_EXPERTS - 1] // ROW_TILE).reshape(1)
    return dest(0, 4), dest(1, 5), tile_expert, n_used, n_rows


def _dispatch_kernel(d1_ref, d2_ref, h3_ref, zero_hbm, xs_hbm, sem, *, tm):
    del zero_hbm
    base = pl.program_id(0) * tm

    def issue(k, carry):
        src = h3_ref.at[pl.ds(k, 1)]
        pltpu.make_async_copy(src, xs_hbm.at[pl.ds(d1_ref[base + k], 1)], sem).start()
        pltpu.make_async_copy(src, xs_hbm.at[pl.ds(d2_ref[base + k], 1)], sem).start()
        return carry

    lax.fori_loop(0, tm, issue, 0)
    for _ in range(2):
        pltpu.make_async_copy(h3_ref, xs_hbm.at[pl.ds(0, tm)], sem).wait()


def _dispatch(h3, dest1, dest2, n_rows):
    t = h3.shape[0]
    tm = min(512, t)
    grid_spec = pltpu.PrefetchScalarGridSpec(
        num_scalar_prefetch=2,
        grid=(t // tm,),
        in_specs=[pl.BlockSpec((tm, SUBL, LANE), lambda i, d1, d2: (i, 0, 0)), pl.BlockSpec(memory_space=pl.ANY)],
        out_specs=pl.BlockSpec(memory_space=pl.ANY),
        scratch_shapes=[pltpu.SemaphoreType.DMA(())],
    )
    return pl.pallas_call(
        functools.partial(_dispatch_kernel, tm=tm),
        grid_spec=grid_spec,
        out_shape=jax.ShapeDtypeStruct((n_rows, SUBL, LANE), f32),
        input_output_aliases={3: 0},
        compiler_params=_cparams("arbitrary"),
        name="moe_dispatch",
    )(dest1, dest2, h3, jnp.zeros((n_rows, SUBL, LANE), f32))


def _experts_kernel(te_ref, nu_ref, xs_ref, wg_ref, wu_ref, wd_ref, ys_ref):
    del te_ref
    r = pl.program_id(0)

    @pl.when(r < nu_ref[0])
    def _():
        x = jnp.concatenate([xs_ref[:, j, :] for j in range(SUBL)], axis=1).astype(bf16)
        for j, blk in enumerate(_tile_rows(_swiglu_bf16(x, wg_ref[...], wu_ref[...], wd_ref[...]))):
            ys_ref[:, j, :] = blk

    @pl.when(r >= nu_ref[0])
    def _():
        ys_ref[...] = jnp.zeros(ys_ref.shape, f32)


def _experts(xs, tile_expert, n_used, wg, wu, wd):
    n_rows = xs.shape[0]
    d_ff = wg.shape[2]
    rows = pl.BlockSpec((ROW_TILE, SUBL, LANE), lambda r, te, nu: (r, 0, 0))
    w_in = pl.BlockSpec((None, D_MODEL, d_ff), lambda r, te, nu: (te[r], 0, 0), pipeline_mode=pl.Buffered(1))
    w_dn = pl.BlockSpec((None, d_ff, D_MODEL), lambda r, te, nu: (te[r], 0, 0), pipeline_mode=pl.Buffered(1))
    grid_spec = pltpu.PrefetchScalarGridSpec(
        num_scalar_prefetch=2,
        grid=(n_rows // ROW_TILE,),
        in_specs=[rows, w_in, w_in, w_dn],
        out_specs=rows,
    )
    return pl.pallas_call(
        _experts_kernel,
        grid_spec=grid_spec,
        out_shape=jax.ShapeDtypeStruct((n_rows, SUBL, LANE), f32),
        compiler_params=_cparams("arbitrary"),
        name="moe_experts",
    )(tile_expert, n_used, xs, wg, wu, wd)


def _combine_kernel(d1_ref, d2_ref, x_ref, meta_ref, ys_hbm, y_ref, buf, sem, *, tm):
    base = pl.program_id(0) * tm

    def issue(k, carry):
        t = base + k
        pltpu.make_async_copy(ys_hbm.at[pl.ds(d1_ref[t], 1)], buf.at[0, pl.ds(k, 1)], sem).start()
        pltpu.make_async_copy(ys_hbm.at[pl.ds(d2_ref[t], 1)], buf.at[1, pl.ds(k, 1)], sem).start()
        return carry

    lax.fori_loop(0, tm, issue, 0)
    for s in range(2):
        pltpu.make_async_copy(ys_hbm.at[pl.ds(0, tm)], buf.at[s], sem).wait()
    meta = meta_ref[...]
    g1 = meta[:, 2:3]
    g2 = meta[:, 3:4]
    for j in range(SUBL):
        sl = slice(j * LANE, (j + 1) * LANE)
        y_ref[:, sl] = x_ref[:, sl] + (g1 * buf[0, :, j, :] + g2 * buf[1, :, j, :])


def _combine(x, meta, ys, dest1, dest2):
    t = x.shape[0]
    tm = min(256, t)
    grid_spec = pltpu.PrefetchScalarGridSpec(
        num_scalar_prefetch=2,
        grid=(t // tm,),
        in_specs=[pl.BlockSpec((tm, D_MODEL), lambda i, d1, d2: (i, 0)),
                  pl.BlockSpec((tm, LANE), lambda i, d1, d2: (i, 0)),
                  pl.BlockSpec(memory_space=pl.ANY)],
        out_specs=pl.BlockSpec((tm, D_MODEL), lambda i, d1, d2: (i, 0)),
        scratch_shapes=[pltpu.VMEM((2, tm, SUBL, LANE), f32), pltpu.SemaphoreType.DMA(())],
    )
    return pl.pallas_call(
        functools.partial(_combine_kernel, tm=tm),
        grid_spec=grid_spec,
        out_shape=jax.ShapeDtypeStruct((t, D_MODEL), f32),
        compiler_params=_cparams("arbitrary"),
        name="moe_combine",
    )(dest1, dest2, x, meta, ys)


def _moe(x, g, router_pad, wg, wu, wd):
    t = x.shape[0]
    h3, meta, counts = _router(x, g, router_pad)
    if t < SPARSE_MIN_TOKENS:
        return _moe_dense(x, h3, meta, wg, wu, wd)
    dest1, dest2, tile_expert, n_used, n_rows = _route_plan(meta, counts, t)
    xs = _dispatch(h3, dest1, dest2, n_rows)
    ys = _experts(xs, tile_expert, n_used, wg, wu, wd)
    return _combine(x, meta, ys, dest1, dest2)


def _t5_bucket_np(n):
    n = np.maximum(n, 0)
    max_exact = NUM_BUCKETS // 2
    nf = np.maximum(n, 1).astype(np.float32)
    large = max_exact + (np.log(nf / np.float32(max_exact)) / np.float32(math.log(MAX_DISTANCE / max_exact))
                         * np.float32(NUM_BUCKETS - max_exact)).astype(np.int32)
    return np.where(n < max_exact, n, np.minimum(large, NUM_BUCKETS - 1))


def _shifted_bias(rel_bias):
    rb = rel_bias.astype(f32)
    return rb - rb[NUM_BUCKETS - 1:NUM_BUCKETS]


ACC_ROWS = HEAD_W + 8
LOG2E = math.log2(math.e)


def _attn_kernel(qi_ref, kj_ref, lam_ref, qt_ref, k_ref, vt_ref, toep_ref, o_ref, qs_scr, m_scr, acc_scr, *, tq):
    p = pl.program_id(1)
    i = qi_ref[p]
    j = kj_ref[p]
    n_hc = 2 * N_HEADS
    c2 = (DKB ** -0.5) * LOG2E

    @pl.when(j == 0)
    def _():
        qt = qt_ref[...] * c2
        row_grp = _iota(qt.shape, 0) // DKB
        for hc in range(n_hc):
            qs_scr[:, hc * tq:(hc + 1) * tq] = jnp.where(row_grp == hc, qt, 0.0).astype(bf16)
        m_scr[...] = jnp.full(m_scr.shape, NEG, f32)
        acc_scr[...] = jnp.zeros(acc_scr.shape, f32)

    def step(near):
        tk = k_ref.shape[0]
        st_all = _dg(k_ref[...].astype(bf16), qs_scr[...], NN)
        vt = vt_ref[...]
        ones = jnp.ones((ACC_ROWS - HEAD_W, tk), f32)
        for h in range(N_HEADS):
            vh = jnp.concatenate([vt[h * HEAD_W:(h + 1) * HEAD_W, :], ones], axis=0).astype(bf16)
            for hc in (2 * h, 2 * h + 1):
                s = st_all[:, hc * tq:(hc + 1) * tq]
                if near:
                    s = s + toep_ref[(i - j) * N_HEADS + h]
                m_old = m_scr[hc:hc + 1, :]
                m_new = jnp.maximum(m_old, jnp.max(s, axis=0, keepdims=True))
                pexp = jnp.exp2(s - m_new)
                acc_scr[hc] = jnp.exp2(m_old - m_new) * acc_scr[hc] + _dg(vh, pexp.astype(bf16), NN)
                m_scr[hc:hc + 1, :] = m_new

    @pl.when(i - j <= 1)
    def _():
        step(True)

    @pl.when(i - j > 1)
    def _():
        step(False)

    @pl.when(j == i)
    def _():
        lam = lam_ref[0]
        outs = []
        for h in range(N_HEADS):
            a0 = acc_scr[2 * h]
            a1 = acc_scr[2 * h + 1]
            outs.append(a0[0:HEAD_W] * _recip(a0[HEAD_W:HEAD_W + 1])
                        - lam * (a1[0:HEAD_W] * _recip(a1[HEAD_W:HEAD_W + 1])))
        o_ref[...] = jnp.concatenate(outs, axis=0).T


def _toeplitz_kernel(u_ref, o_ref):
    t = o_ref.shape[0]
    rows = jnp.broadcast_to(u_ref[...], (t, 2 * t))
    o_ref[...] = pltpu.roll(rows, 0, 1, stride=1, stride_axis=0)[:, t:2 * t]


def _toeplitz_bias_tiles(rel_bias, t):
    m = np.arange(2 * t)[None, :]
    dist = m - t + np.array([0, t])[:, None]
    tab = _shifted_bias(rel_bias)
    u = jnp.take(tab, jnp.asarray(_t5_bucket_np(dist)), axis=0)
    u = jnp.where(jnp.asarray(dist >= 0)[:, :, None], u * LOG2E, NEG)
    u = jnp.transpose(u, (0, 2, 1)).reshape(2 * N_HEADS, 1, 2 * t)
    return pl.pallas_call(
        _toeplitz_kernel,
        grid=(2 * N_HEADS,),
        in_specs=[pl.BlockSpec((None, 1, 2 * t), lambda i: (i, 0, 0))],
        out_specs=pl.BlockSpec((None, t, t), lambda i: (i, 0, 0)),
        out_shape=jax.ShapeDtypeStruct((2 * N_HEADS, t, t), f32),
        compiler_params=_cparams("arbitrary"),
        name="toeplitz_bias",
    )(u)


def _attn_prompt(qnt, kn, vt, lam, rel_bias, n_batch, seq):
    tq = min(512, seq)
    nq = seq // tq
    pairs =[(i, j) for i in range(nq) for j in range(i + 1)]
    qi = jnp.asarray(np.array([a for a, _ in pairs], np.int32))
    kj = jnp.asarray(np.array([b for _, b in pairs], np.int32))
    toep = _toeplitz_bias_tiles(rel_bias, tq)
    grid_spec = pltpu.PrefetchScalarGridSpec(
        num_scalar_prefetch=2,
        grid=(n_batch, len(pairs)),
        in_specs=[
            pl.BlockSpec(memory_space=pltpu.SMEM),
            pl.BlockSpec((GROUP_W, tq), lambda b_, p_, qi_, kj_: (0, b_ * nq + qi_[p_])),
            pl.BlockSpec((tq, GROUP_W), lambda b_, p_, qi_, kj_: (b_ * nq + kj_[p_], 0)),
            pl.BlockSpec((GROUP_W, tq), lambda b_, p_, qi_, kj_: (0, b_ * nq + kj_[p_])),
            pl.BlockSpec((2 * N_HEADS, tq, tq), lambda b_, p_, qi_, kj_: (0, 0, 0)),
        ],
        out_specs=pl.BlockSpec((tq, GROUP_W), lambda b_, p_, qi_, kj_: (b_ * nq + qi_[p_], 0)),
        scratch_shapes=[
            pltpu.VMEM((GROUP_W, 2 * N_HEADS * tq), bf16),
            pltpu.VMEM((2 * N_HEADS, tq), f32),
            pltpu.VMEM((2 * N_HEADS, ACC_ROWS, tq), f32),
        ],
    )
    return pl.pallas_call(
        functools.partial(_attn_kernel, tq=tq),
        grid_spec=grid_spec,
        out_shape=jax.ShapeDtypeStruct((n_batch * seq, GROUP_W), f32),
        compiler_params=_cparams("arbitrary", "arbitrary"),
        name="attn_prompt",
    )(qi, kj, lam.reshape(1), qnt, kn, vt, toep)


def _attn_decode_kernel(pt_ref, lam_ref, q_ref, kn_ref, vn_ref, blast_ref, bself_ref, *rest, pg, n_pages):
    k_refs = rest[:pg]
    v_refs = rest[pg:2 * pg]
    o_ref, qs_scr, s_scr, v_scr = rest[2 * pg:]
    t = pl.program_id(1)
    n_steps = n_pages // pg
    n_hc = 2 * N_HEADS
    page = k_refs[0].shape[1]
    scale = DKB ** -0.5
    rnd = lambda z: z.astype(bf16).astype(f32)

    @pl.when(t == 0)
    def _():
        q = jnp.broadcast_to(q_ref[...], (n_hc, GROUP_W))
        keep = (_iota(q.shape, 1) // DKB) == _iota(q.shape, 0)
        qs_scr[...] = jnp.where(keep, q, 0.0)

    qs_b = qs_scr[...].astype(bf16)
    parts = []
    for g in range(pg):
        s = _dg(qs_b, k_refs[g][...].astype(bf16), NN) * scale
        is_last = (t * pg + g) == (n_pages - 1)
        parts.append(s + jnp.where(is_last, blast_ref[...], 0.0))
        v_scr[t * pg + g] = v_refs[g][...].astype(bf16)
    s_scr[t] = jnp.concatenate(parts, axis=1)

    @pl.when(t == n_steps - 1)
    def _():
        s_all = s_scr[...]
        s_self = jnp.sum(rnd(qs_scr[...]) * rnd(kn_ref[...]), axis=-1, keepdims=True) * scale + bself_ref[...]
        m = jnp.maximum(jnp.max(jnp.max(s_all, axis=2, keepdims=True), axis=0), s_self)
        p = jnp.exp(s_all - m)
        p_self = jnp.exp(s_self - m)
        l = jnp.sum(jnp.sum(p, axis=2, keepdims=True), axis=0) + p_self
        inv_l = _recip(l)
        pn = p * inv_l
        pn_self = p_self * inv_l
        lam = lam_ref[0]
        rows = [pn[:, 2 * h:2 * h + 1, :] - lam * pn[:, 2 * h + 1:2 * h + 2, :] for h in range(N_HEADS)]
        s_scr[...] = jnp.concatenate(rows + [jnp.zeros_like(rows[0])] * N_HEADS, axis=1)
        rows_self = [pn_self[2 * h:2 * h + 1] - lam * pn_self[2 * h + 1:2 * h + 2] for h in range(N_HEADS)]
        a_self = jnp.concatenate(rows_self + [jnp.zeros_like(rows_self[0])] * N_HEADS, axis=0)

        def weighted_values(t2, acc):
            a = s_scr[t2].astype(bf16)
            for g in range(pg):
                acc = acc + _dg(a[:, g * page:(g + 1) * page], v_scr[t2 * pg + g], NT)
            return acc

        o = lax.fori_loop(0, n_steps, weighted_values, rnd(a_self) * rnd(vn_ref[...]))
        lane_head = _iota((1, GROUP_W), 1) // HEAD_W
        out = jnp.zeros((1, GROUP_W), f32)
        for h in range(N_HEADS):
            out = jnp.where(lane_head == h, o[h:h + 1], out)
        o_ref[...] = out


def _attn_decode(qn, kn, vn, page_table, cache_k, cache_v, layer, lam, rel_bias):
    n_b, n_pages = page_table.shape
    page = cache_k.shape[3]
    pg = min(16, n_pages)
    n_steps = n_pages // pg
    past = n_pages * page
    tab = _shifted_bias(rel_bias)
    d_last = past - ((n_pages - 1) * page + np.arange(page))
    blast = jnp.repeat(jnp.take(tab, jnp.asarray(_t5_bucket_np(d_last)), axis=0).T, 2, axis=0)
    bself = jnp.repeat(tab[0].reshape(N_HEADS, 1), 2, axis=0)

    def page_spec(g):
        return pl.BlockSpec((None, None, GROUP_W, page), lambda b_, t_, pt: (layer, pt[b_, t_ * pg + g], 0, 0))

    row = pl.BlockSpec((None, 1, GROUP_W), lambda b_, t_, pt: (b_, 0, 0))
    grid_spec = pltpu.PrefetchScalarGridSpec(
        num_scalar_prefetch=1,
        grid=(n_b, n_steps),
        in_specs=[pl.BlockSpec(memory_space=pltpu.SMEM), row, row, row,
                  pl.BlockSpec((2 * N_HEADS, page), lambda b_, t_, pt: (0, 0)),
                  pl.BlockSpec((2 * N_HEADS, 1), lambda b_, t_, pt: (0, 0))]
                 + [page_spec(g) for g in range(pg)] * 2,
        out_specs=row,
        scratch_shapes=[
            pltpu.VMEM((2 * N_HEADS, GROUP_W), f32),
            pltpu.VMEM((n_steps, 2 * N_HEADS, pg * page), f32),
            pltpu.VMEM((n_pages, GROUP_W, page), bf16),
        ],
    )
    r3 = lambda z: z.reshape(n_b, 1, GROUP_W)
    out = pl.pallas_call(
        functools.partial(_attn_decode_kernel, pg=pg, n_pages=n_pages),
        grid_spec=grid_spec,
        out_shape=jax.ShapeDtypeStruct((n_b, 1, GROUP_W), f32),
        compiler_params=_cparams("arbitrary", "arbitrary"),
        name="attn_decode",
    )(page_table, lam.reshape(1), r3(qn), r3(kn), r3(vn), blast, bself,
      *([cache_k] * pg), *([cache_v] * pg))
    return out.reshape(n_b, GROUP_W)


def _tri(n, dtype=f32):
    return (_iota((n, n), 0) >= _iota((n, n), 1)).astype(dtype)


def _block_tri_t(n, blk):
    r = _iota((n, n), 0)
    c = _iota((n, n), 1)
    return (((r // blk) == (c // blk)) & (r <= c)).astype(bf16)


def _block_tri(n, blk):
    r = _iota((n, n), 0)
    c = _iota((n, n), 1)
    return (((r // blk) == (c // blk)) & (r >= c)).astype(bf16)


def _gdn_kernel(u_ref, gc_ref, gr_ref, cw_ref, alr_ref, dtr_ref, alc_ref, dtc_ref, o_ref, s_out_ref, ext_scr, s_scr, *, tb):
    i = pl.program_id(1)

    @pl.when(i == 0)
    def _():
        ext_scr[0:8, :] = jnp.zeros((8, 3 * GROUP_W), f32)
        s_scr[...] = jnp.zeros(s_scr.shape, f32)

    ext_scr[8:8 + tb, :] = u_ref[...]
    w = cw_ref[...]
    conv = ext_scr[8:8 + tb, :] * w[3:4]
    for jj in range(1, CONV_W):
        conv = conv + ext_scr[8 - jj:8 - jj + tb, :] * w[3 - jj:4 - jj]
    ext_scr[0:8, :] = ext_scr[tb:tb + 8, :]
    qkv = _silu(conv)

    def l2n(x):
        return x * lax.rsqrt(_group_sum(x * x, HEAD_W) + EPS)

    q = l2n(qkv[:, 0:GROUP_W]) * (HEAD_W ** -0.5)
    k = l2n(qkv[:, GROUP_W:2 * GROUP_W])
    v = qkv[:, 2 * GROUP_W:3 * GROUP_W]

    gc = gc_ref[...]
    g_col = -jnp.exp(alr_ref[...]) * _softplus(gc[:, 0:4] + dtr_ref[...])
    beta_col = jax.nn.sigmoid(gc[:, 4:8])
    gr = gr_ref[...]
    g_row = -jnp.exp(alc_ref[...]) * _softplus(gr[0:4, :] + dtc_ref[...])
    g_row8 = jnp.concatenate([g_row, jnp.zeros_like(g_row)], axis=0)
    gcum_row = _mm2(g_row8, _block_tri_t(tb, CHUNK))

    same, lower, strict = _bd_masks()
    tri = _tri(CHUNK, bf16)
    r = _iota((GROUP_W, GROUP_W), 0)
    c = _iota((GROUP_W, GROUP_W), 1)
    eye = (r == c).astype(f32)

    chunks = range(tb // CHUNK)
    gcums, qks, m_bds = [], [], []
    for ch in chunks:
        lo = ch * CHUNK
        gcum = _mm2l(tri, g_col[lo:lo + CHUNK])
        g_stack = _stack_cols(gcum)
        g_cat = _cat_rows(gcum_row, lo)
        decay = jnp.exp(jnp.where(lower, g_stack - g_cat, NEG))
        ksm = _head_stack(k[lo:lo + CHUNK])
        kk = _mm1(ksm, ksm, NT)
        qks.append(_mm1(_head_stack(q[lo:lo + CHUNK]), ksm, NT) * decay)
        m_bds.append(_stack_cols(beta_col[lo:lo + CHUNK]) * kk * jnp.where(strict, decay, 0.0))
        gcums.append(gcum)

    def sibling(lev):
        return ((r >> (lev + 1)) == (c >> (lev + 1))) & (((r >> lev) & 1) == 1) & (((c >> lev) & 1) == 0)

    xs = [eye - jnp.where(sibling(0), m, 0.0) for m in m_bds]
    for lev in range(1, 6):
        sel = sibling(lev)
        xs = [x - _mm3(_mm2(x, jnp.where(sel, m, 0.0).astype(bf16)), x) for x, m in zip(xs, m_bds)]

    for ch in chunks:
        lo = ch * CHUNK
        qc, kc, vc = q[lo:lo + CHUNK], k[lo:lo + CHUNK], v[lo:lo + CHUNK]
        gcum, bcol = gcums[ch], beta_col[lo:lo + CHUNK]
        s_bd = s_scr[...]
        kq_s = _mm1(jnp.concatenate([kc, qc], axis=0), s_bd)
        ks, qs = kq_s[0:CHUNK], kq_s[CHUNK:2 * CHUNK]
        eg_all = _expand_cols(jnp.exp(gcum))
        rhs = _expand_cols(bcol) * (vc - eg_all * ks)
        u_sm = _mm3(xs[ch], _head_stack(rhs))
        o_sm = _mm1(qks[ch], u_sm)
        o_ref[lo:lo + CHUNK, :] = eg_all * qs + _fold_heads(o_sm)
        u_all = _fold_heads(u_sm)
        g_last = gcum[CHUNK - 1:CHUNK, :]
        kw = kc * _expand_cols(jnp.exp(g_last - gcum))
        d_stack = jnp.concatenate(
            [jnp.broadcast_to(jnp.exp(g_last[:, h:h + 1]), (HEAD_W, 1)) for h in range(N_HEADS)], axis=0)
        s_scr[...] = d_stack * s_bd + jnp.where(same, _mm1(kw, u_all, TN), 0.0)

    @pl.when(i == pl.num_programs(1) - 1)
    def _():
        s_out_ref[...] = s_scr[...]


def _gdn_prompt(p, gt, conv_w, a_log, dt_bias, n_batch, seq):
    tb = min(256, seq)
    nb = seq // tb
    r14 = lambda z: z.astype(f32).reshape(1, N_HEADS)
    c41 = lambda z: z.astype(f32).reshape(N_HEADS, 1)
    o, s_bd = pl.pallas_call(
        functools.partial(_gdn_kernel, tb=tb),
        grid=(n_batch, nb),
        in_specs=[
            pl.BlockSpec((tb, 3 * GROUP_W), lambda b, i: (b * nb + i, 0)),
            pl.BlockSpec((tb, 128), lambda b, i: (b * nb + i, GATE_COL // 128)),
            pl.BlockSpec((16, tb), lambda b, i: (0, b * nb + i)),
            pl.BlockSpec((CONV_W, 3 * GROUP_W), lambda b, i: (0, 0)),
            pl.BlockSpec((1, N_HEADS), lambda b, i: (0, 0)),
            pl.BlockSpec((1, N_HEADS), lambda b, i: (0, 0)),
            pl.BlockSpec((N_HEADS, 1), lambda b, i: (0, 0)),
            pl.BlockSpec((N_HEADS, 1), lambda b, i: (0, 0)),
        ],
        out_specs=[
            pl.BlockSpec((tb, GROUP_W), lambda b, i: (b * nb + i, 0)),
            pl.BlockSpec((None, GROUP_W, GROUP_W), lambda b, i: (b, 0, 0)),
        ],
        out_shape=[jax.ShapeDtypeStruct((n_batch * seq, GROUP_W), f32),
                   jax.ShapeDtypeStruct((n_batch, GROUP_W, GROUP_W), f32)],
        scratch_shapes=[pltpu.VMEM((tb + 8, 3 * GROUP_W), f32), pltpu.VMEM((GROUP_W, GROUP_W), f32)],
        compiler_params=_cparams("arbitrary", "arbitrary"),
        name="gdn_prompt",
    )(p, p, gt, conv_w.astype(f32), r14(a_log), r14(dt_bias), c41(a_log), c41(dt_bias))
    return o, _bd_diag(s_bd)


def _bd_diag(s_bd):
    n_b = s_bd.shape[0]
    s5 = s_bd.reshape(n_b, N_HEADS, HEAD_W, N_HEADS, HEAD_W)
    return jnp.stack([s5[:, h, :, h, :] for h in range(N_HEADS)], axis=1)


def _hgrn_kernel(q_ref, f_ref, i_ref, lb_ref, o_ref, s_out_ref, st_scr, q_scr, k_scr, b_scr, *, tb):
    blk = pl.program_id(1)

    @pl.when(blk == 0)
    def _():
        st_scr[...] = jnp.zeros(st_scr.shape, f32)

    lb = lb_ref[...]
    z = f_ref[...]
    logf = jnp.log(lb + (1.0 - lb) * jax.nn.sigmoid(z))
    q_scr[...] = _silu(q_ref[...])
    k_scr[...] = (1.0 - lb) * jax.nn.sigmoid(-z)
    b_scr[...] = _mm2l(_block_tri(tb, SUB), logf)

    same, _, _ = _bd_masks()
    ones_bd = _group_ones(GROUP_W, HEAD_W)
    row = _iota((SUB * SUB, GROUP_W), 0)
    tmask = (row % SUB) >= (row // SUB)

    def rep_t(x):
        return jnp.broadcast_to(x[None], (SUB, SUB, GROUP_W)).reshape(SUB * SUB, GROUP_W)

    def rep_j(x):
        return jnp.broadcast_to(x[:, None, :], (SUB, SUB, GROUP_W)).reshape(SUB * SUB, GROUP_W)

    def body(c, carry):
        r0 = pl.multiple_of(c * SUB, SUB)
        qs = q_scr[pl.ds(r0, SUB), :]
        ks = k_scr[pl.ds(r0, SUB), :]
        vs = i_ref[pl.ds(r0, SUB), :]
        bs = b_scr[pl.ds(r0, SUB), :]
        st = st_scr[...]
        o_inter = _mm1(qs * jnp.exp(bs), st, NT)
        wgt = rep_t(qs) * jnp.exp(jnp.where(tmask, rep_t(bs) - rep_j(bs), NEG)) * rep_j(ks)
        a = _mm2(wgt, ones_bd)
        o_diag = jnp.sum((a * rep_j(vs)).reshape(SUB, SUB, GROUP_W), axis=0)
        o_ref[pl.ds(r0, SUB), :] = o_inter + o_diag
        b_last = bs[SUB - 1:SUB, :]
        kw = ks * jnp.exp(b_last - bs)
        st_scr[...] = st * jnp.exp(b_last) + jnp.where(same, _mm1(vs, kw, TN), 0.0)
        return carry

    lax.fori_loop(0, tb // SUB, body, 0, unroll=4)

    @pl.when(blk == pl.num_programs(1) - 1)
    def _():
        s_out_ref[...] = st_scr[...]


def _hgrn_prompt(p, lb, n_batch, seq):
    tb = min(256, seq)
    nb = seq // tb
    blk = lambda col: pl.BlockSpec((tb, GROUP_W), lambda b, i: (b * nb + i, col))
    o, st = pl.pallas_call(
        functools.partial(_hgrn_kernel, tb=tb),
        grid=(n_batch, nb),
        in_specs=[blk(7), blk(8), blk(9), pl.BlockSpec((1, GROUP_W), lambda b, i: (0, 0))],
        out_specs=[
            pl.BlockSpec((tb, GROUP_W), lambda b, i: (b * nb + i, 0)),
            pl.BlockSpec((None, GROUP_W, GROUP_W), lambda b, i: (b, 0, 0)),
        ],
        out_shape=[jax.ShapeDtypeStruct((n_batch * seq, GROUP_W), f32),
                   jax.ShapeDtypeStruct((n_batch, GROUP_W, GROUP_W), f32)],
        scratch_shapes=[pltpu.VMEM((GROUP_W, GROUP_W), f32)] + [pltpu.VMEM((tb, GROUP_W), f32)] * 3,
        compiler_params=_cparams("arbitrary", "arbitrary"),
        name="hgrn_prompt",
    )(p, p, p, lb.astype(f32).reshape(1, GROUP_W))
    return o, jnp.swapaxes(_bd_diag(st), -1, -2)


def _log_sigmoid(x):
    return jnp.minimum(x, 0.0) - jnp.log1p(jnp.exp(-jnp.abs(x)))


def _mlstm_kernel(q_ref, k_ref, v_ref, gc_ref, gr_ref, ibr_ref, fbr_ref, ibc_ref, fbc_ref,
                  o_ref, c_out_ref, n_out_ref, m_out_ref, c_scr, n_scr, m_scr, *, tb):
    blk = pl.program_id(1)

    @pl.when(blk == 0)
    def _():
        c_scr[...] = jnp.zeros(c_scr.shape, f32)
        n_scr[...] = jnp.zeros(n_scr.shape, f32)
        m_scr[...] = jnp.zeros(m_scr.shape, f32)

    q = q_ref[...]
    k = k_ref[...] * (HEAD_W ** -0.5)
    v = v_ref[...]
    gc = gc_ref[...]
    li_col = gc[:, 8:12] + ibr_ref[...]
    lf_col = _log_sigmoid(gc[:, 12:16] + fbr_ref[...])
    gr = gr_ref[...]
    li_row = gr[8:12, :] + ibc_ref[...]
    lf_row = _log_sigmoid(gr[12:16, :] + fbc_ref[...])
    b_row = _mm2(jnp.concatenate([lf_row, jnp.zeros_like(lf_row)], axis=0), _block_tri_t(tb, CHUNK))

    same, lower, _ = _bd_masks()
    tri = _tri(CHUNK, bf16)

    for ch in range(tb // CHUNK):
        lo = ch * CHUNK
        qc, kc, vc = q[lo:lo + CHUNK], k[lo:lo + CHUNK], v[lo:lo + CHUNK]
        b_col = _mm2l(tri, lf_col[lo:lo + CHUNK])
        b_stack = _stack_cols(b_col)
        d_mat = jnp.where(lower, b_stack - _cat_rows(b_row, lo) + _cat_rows(li_row, lo), NEG)
        m_row = m_scr[...]
        m_stack = jnp.concatenate(
            [jnp.broadcast_to(m_row[:, h:h + 1], (CHUNK, 1)) for h in range(N_HEADS)], axis=0)
        inter = b_stack + m_stack
        m_t = jnp.maximum(inter, jnp.max(d_mat, axis=-1, keepdims=True))
        w_inter = jnp.exp(inter - m_t)
        qsm = _head_stack(qc)
        ksm = _head_stack(kc)
        pmat = _mm1(qsm, ksm, NT) * jnp.exp(d_mat - m_t)
        c_bd = c_scr[...]
        n_row = n_scr[...]
        num = w_inter * _mm1(qsm, c_bd) + _mm1(pmat, _head_stack(vc))
        den = w_inter * jnp.sum(qsm * n_row, axis=-1, keepdims=True) + jnp.sum(pmat, axis=-1, keepdims=True)
        h_sm = num / jnp.maximum(jnp.abs(den), jnp.exp(-m_t))
        o_ref[lo:lo + CHUNK, :] = _fold_heads(h_sm)
        m_new = jnp.concatenate(
            [m_t[h * CHUNK + CHUNK - 1:h * CHUNK + CHUNK, :] for h in range(N_HEADS)], axis=1)
        b_last = b_col[CHUNK - 1:CHUNK, :]
        w_end = jnp.exp(b_last - b_col + li_col[lo:lo + CHUNK] - m_new)
        d0 = jnp.exp(b_last + m_row - m_new)
        kw = kc * _expand_cols(w_end)
        d0_stack = jnp.concatenate(
            [jnp.broadcast_to(d0[:, h:h + 1], (HEAD_W, 1)) for h in range(N_HEADS)], axis=0)
        c_scr[...] = d0_stack * c_bd + jnp.where(same, _mm1(kw, vc, TN), 0.0)
        n_scr[...] = _expand_cols(d0) * n_row + jnp.sum(kw, axis=0, keepdims=True)
        m_scr[...] = m_new

    @pl.when(blk == pl.num_programs(1) - 1)
    def _():
        c_out_ref[...] = c_scr[...]
        n_out_ref[...] = n_scr[...]
        m_out_ref[...] = m_scr[...]


def _mlstm_prompt(p, gt, i_bias, f_bias, n_batch, seq):
    tb = min(256, seq)
    nb = seq // tb
    blk = lambda col: pl.BlockSpec((tb, GROUP_W), lambda b, i: (b * nb + i, col))
    r14 = lambda z: z.astype(f32).reshape(1, N_HEADS)
    c41 = lambda z: z.astype(f32).reshape(N_HEADS, 1)
    small = lambda shape: pl.BlockSpec(shape, lambda b, i: (0, 0))
    o, c_bd, n_row, m_row = pl.pallas_call(
        functools.partial(_mlstm_kernel, tb=tb),
        grid=(n_batch, nb),
        in_specs=[blk(11), blk(12), blk(13),
                  pl.BlockSpec((tb, 128), lambda b, i: (b * nb + i, GATE_COL // 128)),
                  pl.BlockSpec((16, tb), lambda b, i: (0, b * nb + i)),
                  small((1, N_HEADS)), small((1, N_HEADS)), small((N_HEADS, 1)), small((N_HEADS, 1))],
        out_specs=[
            pl.BlockSpec((tb, GROUP_W), lambda b, i: (b * nb + i, 0)),
            pl.BlockSpec((None, GROUP_W, GROUP_W), lambda b, i: (b, 0, 0)),
            pl.BlockSpec((None, 1, GROUP_W), lambda b, i: (b, 0, 0)),
            pl.BlockSpec((None, 1, N_HEADS), lambda b, i: (b, 0, 0)),
        ],
        out_shape=[jax.ShapeDtypeStruct((n_batch * seq, GROUP_W), f32),
                   jax.ShapeDtypeStruct((n_batch, GROUP_W, GROUP_W), f32),
                   jax.ShapeDtypeStruct((n_batch, 1, GROUP_W), f32),
                   jax.ShapeDtypeStruct((n_batch, 1, N_HEADS), f32)],
        scratch_shapes=[pltpu.VMEM((GROUP_W, GROUP_W), f32), pltpu.VMEM((1, GROUP_W), f32),
                        pltpu.VMEM((1, N_HEADS), f32)],
        compiler_params=_cparams("arbitrary", "arbitrary"),
        name="mlstm_prompt",
    )(p, p, p, p, gt, r14(i_bias), r14(f_bias), c41(i_bias), c41(f_bias))
    return (o, _bd_diag(c_bd), n_row.reshape(n_batch, N_HEADS, HEAD_W), m_row.reshape(n_batch, N_HEADS))


def _gdn_dec_prep_kernel(u_ref, buf_ref, cw_ref, q_ref, k_ref, v_ref):
    w = cw_ref[...]
    conv = u_ref[...] * w[3:4]
    for jj in range(CONV_W - 1):
        conv = conv + buf_ref[jj] * w[jj:jj + 1]
    qkv = _silu(conv)

    def l2n(x):
        return x * lax.rsqrt(_group_sum(x * x, HEAD_W) + EPS)

    q_ref[...] = l2n(qkv[:, 0:GROUP_W]) * (HEAD_W ** -0.5)
    k_ref[...] = l2n(qkv[:, GROUP_W:2 * GROUP_W])
    v_ref[...] = qkv[:, 2 * GROUP_W:3 * GROUP_W]


def _gdn_dec_prep(p, conv_buf, conv_w):
    n_b = p.shape[0]
    out = jax.ShapeDtypeStruct((n_b, GROUP_W), f32)
    return pl.pallas_call(
        _gdn_dec_prep_kernel,
        grid=(1,),
        in_specs=[pl.BlockSpec((n_b, 3 * GROUP_W), lambda i: (0, 0)),
                  pl.BlockSpec((CONV_W - 1, n_b, 3 * GROUP_W), lambda i: (0, 0, 0)),
                  pl.BlockSpec((CONV_W, 3 * GROUP_W), lambda i: (0, 0))],
        out_specs=[pl.BlockSpec((n_b, GROUP_W), lambda i: (0, 0))] * 3,
        out_shape=[out, out, out],
        compiler_params=_cparams("arbitrary"),
        name="gdn_dec_prep",
    )(p, jnp.swapaxes(conv_buf.astype(f32), 0, 1), conv_w.astype(f32))


def _rec_decode_kernel(cols_ref, vrows_ref, scal_ref, sg_ref, sh_ref, sc_ref,
                       o_ref, sg_out, sh_out, sc_out, sn_out, sm_out):
    cols = cols_ref[...]
    vrows = vrows_ref[...]
    scal = scal_ref[...]
    col = lambda k: cols[:, :, k:k + 1]
    vrow = lambda k: vrows[:, k:k + 1, :]
    sca = lambda k: scal[:, :, k:k + 1]

    q, k, v = col(0), col(1), vrow(0)
    s = sg_ref[...]
    g = -jnp.exp(sca(2)) * _softplus(sca(0) + sca(3))
    eg = jnp.exp(g)
    beta = jax.nn.sigmoid(sca(1))
    ks = jnp.sum(k * s, axis=1, keepdims=True)
    qs = jnp.sum(q * s, axis=1, keepdims=True)
    u = beta * (v - eg * ks)
    qk = jnp.sum(q * k, axis=1, keepdims=True)
    o_ref[:, 0:1, :] = eg * qs + qk * u
    sg_out[...] = eg * s + k * u

    lb = col(6)
    z = col(3)
    logf = jnp.log(lb + (1.0 - lb) * jax.nn.sigmoid(z))
    kc = (1.0 - lb) * jax.nn.sigmoid(-z)
    qc = _silu(col(2))
    vc = vrow(1)
    sh = sh_ref[...]
    ef = jnp.exp(logf)
    o_ref[:, 1:2, :] = (jnp.sum((qc * ef) * sh, axis=1, keepdims=True)
                        + jnp.sum(qc * kc, axis=1, keepdims=True) * vc)
    sh_out[...] = ef * sh + kc * vc

    qd = col(4)
    kd = col(5) * (HEAD_W ** -0.5)
    vd = vrow(2)
    li = sca(4) + sca(6)
    lf = _log_sigmoid(sca(5) + sca(7))
    m0 = sca(8)
    cs = sc_ref[...]
    n0 = col(7)
    inter = lf + m0
    m_t = jnp.maximum(inter, li)
    w_inter = jnp.exp(inter - m_t)
    qkd = jnp.sum(qd * kd, axis=1, keepdims=True) * jnp.exp(li - m_t)
    num = w_inter * jnp.sum(qd * cs, axis=1, keepdims=True) + qkd * vd
    den = w_inter * jnp.sum(qd * n0, axis=1, keepdims=True) + qkd
    o_ref[:, 2:3, :] = num / jnp.maximum(jnp.abs(den), jnp.exp(-m_t))
    w_end = jnp.exp(li - m_t)
    d0 = jnp.exp(lf + m0 - m_t)
    sc_out[...] = d0 * cs + (w_end * kd) * vd
    sn_out[...] = d0 * n0 + w_end * kd
    sm_out[...] = m_t


def _rec_decode(p, gq, gk, gv, a_log, dt_bias, lb, i_bias, f_bias, s_gdn, s_hgrn, s_c, s_n, s_m):
    n_b = p.shape[0]
    rows = n_b * N_HEADS
    rb = min(16, rows)
    rw = lambda z: z.astype(f32).reshape(rows, HEAD_W)
    blockp = lambda b: rw(p[:, b * GROUP_W:(b + 1) * GROUP_W])
    per_head = lambda z: jnp.tile(z.astype(f32), n_b)
    gate = lambda c: p[:, GATE_COL + c:GATE_COL + c + N_HEADS].reshape(rows)
    lb_rows = jnp.tile(lb.astype(f32).reshape(N_HEADS, HEAD_W), (n_b, 1))
    cols = jnp.stack([rw(gq), rw(gk), blockp(7), blockp(8), blockp(11), blockp(12), lb_rows, rw(s_n)], axis=-1)
    vrows = jnp.stack([rw(gv), blockp(9), blockp(13)], axis=1)
    scal = jnp.stack([gate(0), gate(4), per_head(a_log), per_head(dt_bias), gate(8), gate(12),
                      per_head(i_bias), per_head(f_bias), s_m.astype(f32).reshape(rows)], axis=-1).reshape(rows, 1, 9)
    st = lambda z: z.astype(f32).reshape(rows, HEAD_W, HEAD_W)
    args = [cols, vrows, scal, st(s_gdn), st(s_hgrn), st(s_c)]

    def spec(a):
        return pl.BlockSpec((rb,) + a.shape[1:], lambda i: (i, 0, 0))

    o_st = jax.ShapeDtypeStruct((rows, HEAD_W, HEAD_W), f32)
    outs = [jax.ShapeDtypeStruct((rows, 3, HEAD_W), f32), o_st, o_st, o_st,
            jax.ShapeDtypeStruct((rows, HEAD_W, 1), f32), jax.ShapeDtypeStruct((rows, 1, 1), f32)]
    o, sg, sh, sc, sn, sm = pl.pallas_call(
        _rec_decode_kernel,
        grid=(rows // rb,),
        in_specs=[spec(a) for a in args],
        out_specs=[spec(a) for a in outs],
        out_shape=outs,
        compiler_params=_cparams("arbitrary"),
        name="rec_decode",
    )(*args)
    s4 = lambda z: z.reshape(n_b, N_HEADS, HEAD_W, HEAD_W)
    o2 = lambda k: o[:, k, :].reshape(n_b, GROUP_W)
    return (o2(0), s4(sg), o2(1), s4(sh), o2(2), s4(sc),
            sn.reshape(n_b, N_HEADS, HEAD_W), sm.reshape(n_b, N_HEADS))


def _permute_w_in(w):
    d_in = w.shape[1]
    a_gate0 = 3 * GROUP_W
    d_gate0 = d_in - GROUP_W - 8
    main = jnp.concatenate([w[:, 0:a_gate0], w[:, a_gate0 + 8:d_gate0], w[:, d_gate0 + 8:]], axis=1)
    gates = jnp.concatenate([w[:, a_gate0:a_gate0 + 8], w[:, d_gate0:d_gate0 + 8]], axis=1)
    pad = jnp.zeros((w.shape[0], P_COLS - main.shape[1] - 16), w.dtype)
    return jnp.concatenate([main, gates, pad], axis=1).astype(bf16), gates.T.astype(bf16)


def kernel(x_prompt, x_sample, page_table, cache_k, cache_v, state_gdn_conv, state_gdn, state_hgrn, state_mlstm_C, state_mlstm_n, state_mlstm_m, attn_norm_g, w_in, gdn_conv_w, gdn_a_log, gdn_dt_bias, gdn_norm_g, diff_qk_norm_g, diff_lambda, diff_subln_g, rel_bias, hgrn_lb_logits, hgrn_norm_g, mlstm_i_bias, mlstm_f_bias, mlstm_norm_g, w_out, ffn_norm_g, ffn_w_gate, ffn_w_up, ffn_w_down, moe_router, moe_w_gate, moe_w_up, moe_w_down):
    depth = w_in.shape[0]
    n_bp, seq, _ = x_prompt.shape
    n_bs = x_sample.shape[0]
    n_pool, page = cache_k.shape[1], cache_k.shape[2]
    dt = x_prompt.dtype

    lb_p = jax.nn.softmax(hgrn_lb_logits.astype(f32), axis=0)
    lb_cum = jnp.cumsum(lb_p, axis=0)
    hgrn_lb = lb_cum - lb_cum[0:1]
    cache_k4 = jnp.transpose(cache_k, (0, 1, 3, 4, 5, 2)).reshape(depth, n_pool, GROUP_W, page)
    cache_v4 = jnp.transpose(cache_v, (0, 1, 3, 4, 2)).reshape(depth, n_pool, GROUP_W, page)

    xp = x_prompt.reshape(n_bp * seq, D_MODEL)
    xs = x_sample.reshape(n_bs, D_MODEL)
    outs_p, outs_s = [], []
    for l in range(depth):
        w_perm, w_gate_t = _permute_w_in(w_in[l])
        w_out_b = w_out[l].astype(bf16)
        gains = jnp.stack([jnp.tile(g.astype(f32), N_HEADS) for g in
                           (gdn_norm_g[l], diff_subln_g[l], hgrn_norm_g[l], mlstm_norm_g[l])])
        lam_init = 0.8 - 0.6 * math.exp(-0.3 * l)
        lam32 = diff_lambda[l].astype(f32)
        lam = jnp.exp(jnp.sum(lam32[0] * lam32[1])) - jnp.exp(jnp.sum(lam32[2] * lam32[3])) + lam_init
        if l % 2 == 0:
            ffn_w = (ffn_w_gate[l // 2].astype(bf16), ffn_w_up[l // 2].astype(bf16), ffn_w_down[l // 2].astype(bf16))
        else:
            router_pad = jnp.pad(moe_router[l // 2].astype(bf16), ((0, 0), (0, 128 - N_EXPERTS)))
            moe_w = (moe_w_gate[l // 2].astype(bf16), moe_w_up[l // 2].astype(bf16), moe_w_down[l // 2].astype(bf16))

        def channel_mix(x):
            if l % 2 == 0:
                return _ffn(x, ffn_norm_g[l], *ffn_w)
            return _moe(x, ffn_norm_g[l], router_pad, *moe_w)

        p, gt = _inproj(xp, attn_norm_g[l], w_perm, w_gate_t)
        qnt, kn, vt = _bprep(p, diff_qk_norm_g[l], True)
        ob = _attn_prompt(qnt, kn, vt, lam, rel_bias, n_bp, seq)
        oa, s_gdn = _gdn_prompt(p, gt, gdn_conv_w[l], gdn_a_log[l], gdn_dt_bias[l], n_bp, seq)
        oc, s_hgrn = _hgrn_prompt(p, hgrn_lb[l], n_bp, seq)
        od, s_c, s_n, s_m = _mlstm_prompt(p, gt, mlstm_i_bias[l], mlstm_f_bias[l], n_bp, seq)
        xp = _outproj(oa, ob, oc, od, p, xp, gains, w_out_b, 1.0 - lam_init)
        xp = channel_mix(xp)
        p3 = p.reshape(n_bp, seq, P_COLS)
        outs_p.append((
            kn.reshape(n_bp, seq, N_HEADS, 2, DKB).astype(dt),
            p3[:, :, 6 * GROUP_W:7 * GROUP_W].reshape(n_bp, seq, N_HEADS, HEAD_W).astype(dt),
            p3[:, seq - (CONV_W - 1):, 0:3 * GROUP_W].astype(dt),
            s_gdn.astype(dt), s_hgrn.astype(dt), s_c.astype(dt), s_n.astype(dt), s_m.astype(dt)))

        p, gt = _inproj(xs, attn_norm_g[l], w_perm, w_gate_t)
        qn, kn = _bprep(p, diff_qk_norm_g[l], False)
        vn = p[:, 6 * GROUP_W:7 * GROUP_W]
        ob = _attn_decode(qn, kn, vn, page_table, cache_k4, cache_v4, l, lam, rel_bias)
        u = p[:, 0:3 * GROUP_W]
        gq, gk, gv = _gdn_dec_prep(u, state_gdn_conv[l], gdn_conv_w[l])
        oa, s_gdn, oc, s_hgrn, od, s_c, s_n, s_m = _rec_decode(
            p, gq, gk, gv, gdn_a_log[l], gdn_dt_bias[l], hgrn_lb[l], mlstm_i_bias[l], mlstm_f_bias[l],
            state_gdn[l], state_hgrn[l], state_mlstm_C[l], state_mlstm_n[l], state_mlstm_m[l])
        xs = _outproj(oa, ob, oc, od, p, xs, gains, w_out_b, 1.0 - lam_init)
        xs = channel_mix(xs)
        conv_new = jnp.concatenate([state_gdn_conv[l][:, 1:].astype(dt), u[:, None, :].astype(dt)], axis=1)
        outs_s.append((
            kn.reshape(n_bs, 1, N_HEADS, 2, DKB).astype(dt),
            vn.reshape(n_bs, 1, N_HEADS, HEAD_W).astype(dt),
            conv_new, s_gdn.astype(dt), s_hgrn.astype(dt), s_c.astype(dt), s_n.astype(dt), s_m.astype(dt)))

    kp, vp, convp, gdnp, hgrnp, mcp, mnp_, mmp = [jnp.stack(z) for z in zip(*outs_p)]
    ks_, vs_, convs, gdns, hgrns, mcs, mns, mms = [jnp.stack(z) for z in zip(*outs_s)]
    return (xp.reshape(n_bp, seq, D_MODEL), xs.reshape(n_bs, 1, D_MODEL), kp, vp, ks_, vs_, convp, convs,
            gdnp, gdns, hgrnp, hgrns, mcp, mcs, mnp_, mns, mmp, mms)
```

```python
import functools
import math

import numpy as np
import jax
import jax.numpy as jnp
from jax import lax
from jax.experimental import pallas as pl
from jax.experimental.pallas import tpu as pltpu

f32 = jnp.float32
bf16 = jnp.bfloat16

D_MODEL = 1024
N_HEADS = 4
HEAD_W = 64
GROUP_W = N_HEADS * HEAD_W
DKB = 32
CONV_W = 4
CHUNK = 64
SUB = 16
NUM_BUCKETS = 32
MAX_DISTANCE = 128
N_EXPERTS = 8
EPS = 1e-6
NEG = -1e30
P_COLS = 4096
GATE_COL = 3840
VMEM_LIMIT = 56 * 1024 * 1024

NN = ((1,), (0,))
NT = ((1,), (1,))
TN = ((0,), (0,))


def _dg(a, b, dims=NN):
    return lax.dot_general(a, b, (dims, ((), ())), preferred_element_type=f32)


def _split(a):
    hi = a.astype(bf16)
    lo = (a - hi.astype(f32)).astype(bf16)
    return hi, lo


def _mm3(a, b, dims=NN):
    ah, al = _split(a)
    bh, bl = _split(b)
    return _dg(ah, bh, dims) + (_dg(ah, bl, dims) + _dg(al, bh, dims))


def _mm2(a, b01, dims=NN):
    ah, al = _split(a)
    return _dg(ah, b01, dims) + _dg(al, b01, dims)


def _mm2l(a01, b, dims=NN):
    bh, bl = _split(b)
    return _dg(a01, bh, dims) + _dg(a01, bl, dims)


def _mm1(a, b, dims=NN):
    return _dg(a.astype(bf16), b.astype(bf16), dims)


def _iota(shape, dim):
    return lax.broadcasted_iota(jnp.int32, shape, dim)


def _group_ones(width, group):
    r = _iota((width, width), 0) // group
    c = _iota((width, width), 1) // group
    return (r == c).astype(bf16)


def _group_sum(x, group):
    ones = _group_ones(x.shape[-1], group)
    hi = x.astype(bf16)
    r1 = x - hi.astype(f32)
    mid = r1.astype(bf16)
    lo = (r1 - mid.astype(f32)).astype(bf16)
    return _dg(hi, ones) + (_dg(mid, ones) + _dg(lo, ones))


def _recip(x):
    r = 1.0 / x
    return r * (2.0 - x * r)


def _silu(x):
    return x * jax.nn.sigmoid(x)


def _softplus(x):
    return jnp.maximum(x, 0.0) + jnp.log1p(jnp.exp(-jnp.abs(x)))


def _stack_cols(xc, n=N_HEADS, rows=HEAD_W):
    return jnp.concatenate([xc[:, h:h + 1] for h in range(n)], axis=0)


def _expand_cols(xc, n=N_HEADS, width=HEAD_W):
    r = xc.shape[0]
    return jnp.concatenate([jnp.broadcast_to(xc[:, h:h + 1], (r, width)) for h in range(n)], axis=1)


def _cat_rows(xr, lo, n=N_HEADS, width=HEAD_W):
    return jnp.concatenate([xr[h:h + 1, lo:lo + width] for h in range(n)], axis=1)


def _head_stack(x, n=N_HEADS, width=HEAD_W):
    lane_head = _iota(x.shape, 1) // width
    return jnp.concatenate([jnp.where(lane_head == h, x, 0.0) for h in range(n)], axis=0)


def _fold_heads(x_sm, n=N_HEADS):
    r = x_sm.shape[0] // n
    out = x_sm[0:r]
    for h in range(1, n):
        out = out + x_sm[h * r:(h + 1) * r]
    return out


def _bd_masks(n=GROUP_W, blk=CHUNK):
    r = _iota((n, n), 0)
    c = _iota((n, n), 1)
    same = (r // blk) == (c // blk)
    lower = same & ((r % blk) >= (c % blk))
    strict = same & ((r % blk) > (c % blk))
    return same, lower, strict


def _cparams(*sem):
    return pltpu.CompilerParams(dimension_semantics=sem, vmem_limit_bytes=VMEM_LIMIT)


def _inproj_kernel(x_ref, g_ref, w_ref, wgt_ref, p_ref, gt_ref, h_scr):
    @pl.when(pl.program_id(1) == 0)
    def _():
        x = x_ref[...]
        ms = jnp.mean(x * x, axis=-1, keepdims=True)
        h = ((x * lax.rsqrt(ms + EPS)) * g_ref[...]).astype(bf16)
        h_scr[...] = h
        gt_ref[...] = _dg(wgt_ref[...], h, NT)
    p_ref[...] = _dg(h_scr[...], w_ref[...], NN)


def _inproj(x, g, w_perm, w_gate_t):
    t = x.shape[0]
    tm = min(1024, t)
    tn = 1024
    return pl.pallas_call(
        _inproj_kernel,
        grid=(t // tm, P_COLS // tn),
        in_specs=[
            pl.BlockSpec((tm, D_MODEL), lambda i, j: (i, 0)),
            pl.BlockSpec((1, D_MODEL), lambda i, j: (0, 0)),
            pl.BlockSpec((D_MODEL, tn), lambda i, j: (0, j)),
            pl.BlockSpec((16, D_MODEL), lambda i, j: (0, 0)),
        ],
        out_specs=[
            pl.BlockSpec((tm, tn), lambda i, j: (i, j)),
            pl.BlockSpec((16, tm), lambda i, j: (0, i)),
        ],
        out_shape=[jax.ShapeDtypeStruct((t, P_COLS), f32), jax.ShapeDtypeStruct((16, t), f32)],
        scratch_shapes=[pltpu.VMEM((tm, D_MODEL), bf16)],
        compiler_params=_cparams("arbitrary", "arbitrary"),
        name="inproj",
    )(x, g.reshape(1, D_MODEL), w_perm, w_gate_t)


def _qk_gnorm(x, g):
    ms = _group_sum(x * x, DKB) * (1.0 / DKB)
    return (x * lax.rsqrt(ms + EPS)) * g


def _bprep_kernel(q_ref, k_ref, gq_ref, gk_ref, qn_ref, kn_ref):
    qn_ref[...] = _qk_gnorm(q_ref[...], gq_ref[...])
    kn_ref[...] = _qk_gnorm(k_ref[...], gk_ref[...])


def _bprep_t_kernel(q_ref, k_ref, v_ref, gq_ref, gk_ref, qnt_ref, kn_ref, vt_ref):
    qnt_ref[...] = _qk_gnorm(q_ref[...], gq_ref[...]).T
    kn_ref[...] = _qk_gnorm(k_ref[...], gk_ref[...])
    vt_ref[...] = v_ref[...].T


def _bprep(p, qk_norm_g, transposed):
    t = p.shape[0]
    tm = min(512, t)
    gq = jnp.tile(qk_norm_g[0], GROUP_W // DKB).reshape(1, GROUP_W)
    gk = jnp.tile(qk_norm_g[1], GROUP_W // DKB).reshape(1, GROUP_W)
    col = lambda c: pl.BlockSpec((tm, GROUP_W), lambda i: (i, c))
    gain = pl.BlockSpec((1, GROUP_W), lambda i: (0, 0))
    rows = pl.BlockSpec((tm, GROUP_W), lambda i: (i, 0))
    rows_t = pl.BlockSpec((GROUP_W, tm), lambda i: (0, i))
    if transposed:
        return pl.pallas_call(
            _bprep_t_kernel,
            grid=(t // tm,),
            in_specs=[col(4), col(5), col(6), gain, gain],
            out_specs=[rows_t, rows, rows_t],
            out_shape=[jax.ShapeDtypeStruct((GROUP_W, t), f32), jax.ShapeDtypeStruct((t, GROUP_W), f32),
                       jax.ShapeDtypeStruct((GROUP_W, t), f32)],
            compiler_params=_cparams("arbitrary"),
            name="bprep_t",
        )(p, p, p, gq, gk)
    return pl.pallas_call(
        _bprep_kernel,
        grid=(t // tm,),
        in_specs=[col(4), col(5), gain, gain],
        out_specs=[rows, rows],
        out_shape=[jax.ShapeDtypeStruct((t, GROUP_W), f32)] * 2,
        compiler_params=_cparams("arbitrary"),
        name="bprep",
    )(p, p, gq, gk)


def _outproj_kernel(oa_ref, ob_ref, oc_ref, od_ref, ag_ref, cg_ref, dg_ref, x_ref, g_ref, w_ref, y_ref, *, b_scale):
    def gnorm(x, g):
        ms = _group_sum(x * x, HEAD_W) * (1.0 / HEAD_W)
        return (x * lax.rsqrt(ms + EPS)) * g
    g = g_ref[...]
    mixes = (
        gnorm(oa_ref[...], g[0:1]) * _silu(ag_ref[...]),
        gnorm(ob_ref[...], g[1:2]) * b_scale,
        gnorm(oc_ref[...], g[2:3]) * jax.nn.sigmoid(cg_ref[...]),
        gnorm(od_ref[...], g[3:4]) * jax.nn.sigmoid(dg_ref[...]),
    )
    y = x_ref[...]
    for i, m in enumerate(mixes):
        y = y + _dg(m.astype(bf16), w_ref[i * GROUP_W:(i + 1) * GROUP_W, :], NN)
    y_ref[...] = y


def _outproj(oa, ob, oc, od, p, x, gains, w_out, b_scale):
    t = x.shape[0]
    tm = min(512, t)
    row = lambda i: (i, 0)
    return pl.pallas_call(
        functools.partial(_outproj_kernel, b_scale=b_scale),
        grid=(t // tm,),
        in_specs=[
            pl.BlockSpec((tm, GROUP_W), row), pl.BlockSpec((tm, GROUP_W), row),
            pl.BlockSpec((tm, GROUP_W), row), pl.BlockSpec((tm, GROUP_W), row),
            pl.BlockSpec((tm, GROUP_W), lambda i: (i, 3)),
            pl.BlockSpec((tm, GROUP_W), lambda i: (i, 10)),
            pl.BlockSpec((tm, GROUP_W), lambda i: (i, 14)),
            pl.BlockSpec((tm, D_MODEL), row),
            pl.BlockSpec((4, GROUP_W), lambda i: (0, 0)),
            pl.BlockSpec((D_MODEL, D_MODEL), lambda i: (0, 0)),
        ],
        out_specs=pl.BlockSpec((tm, D_MODEL), row),
        out_shape=jax.ShapeDtypeStruct((t, D_MODEL), f32),
        compiler_params=_cparams("arbitrary"),
        name="outproj",
    )(oa, ob, oc, od, p, p, p, x, gains, w_out)


def _ffn_kernel(x_ref, g_ref, wg_ref, wu_ref, wd_ref, y_ref, h_scr):
    @pl.when(pl.program_id(1) == 0)
    def _():
        x = x_ref[...]
        ms = jnp.mean(x * x, axis=-1, keepdims=True)
        h_scr[...] = ((x * lax.rsqrt(ms + EPS)) * g_ref[...]).astype(bf16)
        y_ref[...] = x

    h = h_scr[...]
    a = _silu(_dg(h, wg_ref[...])) * _dg(h, wu_ref[...])
    y_ref[...] += _dg(a.astype(bf16), wd_ref[...])


def _ffn(x, g, wg, wu, wd):
    t = x.shape[0]
    d_ff = wg.shape[1]
    tm = min(1024, t)
    tf = d_ff // 2
    return pl.pallas_call(
        _ffn_kernel,
        grid=(t // tm, d_ff // tf),
        in_specs=[
            pl.BlockSpec((tm, D_MODEL), lambda i, f: (i, 0)),
            pl.BlockSpec((1, D_MODEL), lambda i, f: (0, 0)),
            pl.BlockSpec((D_MODEL, tf), lambda i, f: (0, f)),
            pl.BlockSpec((D_MODEL, tf), lambda i, f: (0, f)),
            pl.BlockSpec((tf, D_MODEL), lambda i, f: (f, 0)),
        ],
        out_specs=pl.BlockSpec((tm, D_MODEL), lambda i, f: (i, 0)),
        out_shape=jax.ShapeDtypeStruct((t, D_MODEL), f32),
        scratch_shapes=[pltpu.VMEM((tm, D_MODEL), bf16)],
        compiler_params=_cparams("arbitrary", "arbitrary"),
        name="ffn",
    )(x, g.reshape(1, D_MODEL), wg, wu, wd)


LANE = 128
SUBL = D_MODEL // LANE
ROW_TILE = 256
SPARSE_MIN_TOKENS = 4096


def _tile_rows(x):
    return [x[:, j * LANE:(j + 1) * LANE] for j in range(SUBL)]


def _router_kernel(x_ref, g_ref, r_ref, h3_ref, meta_ref, cnt_ref, carry_scr):
    @pl.when(pl.program_id(0) == 0)
    def _():
        carry_scr[...] = jnp.zeros(carry_scr.shape, f32)

    x = x_ref[...]
    tm = x.shape[0]
    ms = jnp.mean(x * x, axis=-1, keepdims=True)
    h = (x * lax.rsqrt(ms + EPS)) * g_ref[...]
    for j, blk in enumerate(_tile_rows(h)):
        h3_ref[:, j, :] = blk
    logits = _dg(h.astype(bf16), r_ref[...])
    lane = _iota(logits.shape, 1)
    logits = jnp.where(lane < N_EXPERTS, logits, -jnp.inf)
    v1 = jnp.max(logits, axis=-1, keepdims=True)
    i1 = jnp.min(jnp.where(logits == v1, lane, LANE), axis=-1, keepdims=True)
    rest = jnp.where(lane == i1, -jnp.inf, logits)
    v2 = jnp.max(rest, axis=-1, keepdims=True)
    i2 = jnp.min(jnp.where(rest == v2, lane, LANE), axis=-1, keepdims=True)
    e2 = jnp.exp(v2 - v1)
    den = 1.0 + e2
    hit = ((lane == i1) | (lane == i2)).astype(f32)
    strict = (_iota((tm, tm), 0) > _iota((tm, tm), 1)).astype(bf16)
    before = _dg(strict, hit.astype(bf16)) + carry_scr[...]
    pos1 = jnp.sum(jnp.where(lane == i1, before, 0.0), axis=-1, keepdims=True)
    pos2 = jnp.sum(jnp.where(lane == i2, before, 0.0), axis=-1, keepdims=True)
    carry_scr[...] += jnp.sum(hit, axis=0, keepdims=True)
    meta = jnp.zeros(logits.shape, f32)
    for c, val in enumerate((i1.astype(f32), i2.astype(f32), 1.0 / den, e2 / den, pos1, pos2)):
        meta = jnp.where(lane == c, val, meta)
    meta_ref[...] = meta
    cnt_ref[...] = carry_scr[...]


def _router(x, g, router_pad):
    t = x.shape[0]
    tm = min(512, t)
    return pl.pallas_call(
        _router_kernel,
        grid=(t // tm,),
        in_specs=[
            pl.BlockSpec((tm, D_MODEL), lambda i: (i, 0)),
            pl.BlockSpec((1, D_MODEL), lambda i: (0, 0)),
            pl.BlockSpec((D_MODEL, LANE), lambda i: (0, 0)),
        ],
        out_specs=[pl.BlockSpec((tm, SUBL, LANE), lambda i: (i, 0, 0)),
                   pl.BlockSpec((tm, LANE), lambda i: (i, 0)),
                   pl.BlockSpec((1, LANE), lambda i: (0, 0))],
        out_shape=[jax.ShapeDtypeStruct((t, SUBL, LANE), f32), jax.ShapeDtypeStruct((t, LANE), f32),
                   jax.ShapeDtypeStruct((1, LANE), f32)],
        scratch_shapes=[pltpu.VMEM((1, LANE), f32)],
        compiler_params=_cparams("arbitrary"),
        name="router",
    )(x, g.reshape(1, D_MODEL), router_pad)


def _swiglu_bf16(x, wg, wu, wd):
    a = _silu(_dg(x, wg)) * _dg(x, wu)
    return _dg(a.astype(bf16), wd)


def _moe_dense_kernel(x_ref, h3_ref, meta_ref, wg_ref, wu_ref, wd_ref, y_ref, acc_scr):
    e = pl.program_id(1)
    f = pl.program_id(2)

    @pl.when((e == 0) & (f == 0))
    def _():
        acc_scr[...] = x_ref[...]

    meta = meta_ref[...]
    ef = e.astype(f32)
    cw = jnp.where(meta[:, 0:1] == ef, meta[:, 2:3], 0.0) + jnp.where(meta[:, 1:2] == ef, meta[:, 3:4], 0.0)
    h = jnp.concatenate([h3_ref[:, j, :] for j in range(SUBL)], axis=1).astype(bf16)
    acc_scr[...] += cw * _swiglu_bf16(h, wg_ref[...], wu_ref[...], wd_ref[...])

    @pl.when((e == pl.num_programs(1) - 1) & (f == pl.num_programs(2) - 1))
    def _():
        y_ref[...] = acc_scr[...]


def _moe_dense(x, h3, meta, wg, wu, wd):
    t = x.shape[0]
    n_e, _, d_ff = wg.shape
    tm = min(512, t)
    tf = d_ff // 2
    return pl.pallas_call(
        _moe_dense_kernel,
        grid=(t // tm, n_e, d_ff // tf),
        in_specs=[
            pl.BlockSpec((tm, D_MODEL), lambda i, e, f: (i, 0)),
            pl.BlockSpec((tm, SUBL, LANE), lambda i, e, f: (i, 0, 0)),
            pl.BlockSpec((tm, LANE), lambda i, e, f: (i, 0)),
            pl.BlockSpec((None, D_MODEL, tf), lambda i, e, f: (e, 0, f)),
            pl.BlockSpec((None, D_MODEL, tf), lambda i, e, f: (e, 0, f)),
            pl.BlockSpec((None, tf, D_MODEL), lambda i, e, f: (e, f, 0)),
        ],
        out_specs=pl.BlockSpec((tm, D_MODEL), lambda i, e, f: (i, 0)),
        out_shape=jax.ShapeDtypeStruct((t, D_MODEL), f32),
        scratch_shapes=[pltpu.VMEM((tm, D_MODEL), f32)],
        compiler_params=_cparams("arbitrary", "arbitrary", "arbitrary"),
        name="moe_dense",
    )(x, h3, meta, wg, wu, wd)


def _route_plan(meta, counts, t):
    cnt = counts[0, :N_EXPERTS].astype(jnp.int32)
    padded = ((cnt + ROW_TILE - 1) // ROW_TILE) * ROW_TILE
    ends = jnp.cumsum(padded)
    offs = ends - padded
    experts = jnp.arange(N_EXPERTS, dtype=jnp.int32)

    def dest(expert_col, rank_col):
        e = meta[:, expert_col].astype(jnp.int32)
        off = jnp.sum(jnp.where(e[:, None] == experts[None, :], offs[None, :], 0), axis=1)
        return off + meta[:, rank_col].astype(jnp.int32)

    n_rows = 2 * t + N_EXPERTS * ROW_TILE
    starts = jnp.arange(n_rows // ROW_TILE, dtype=jnp.int32) * ROW_TILE
    tile_expert = jnp.minimum(jnp.sum((starts[:, None] >= ends[None, :]).astype(jnp.int32), axis=1), N_EXPERTS - 1)
    n_used = (ends[N_EXPERTS - 1] // ROW_TILE).reshape(1)
    return dest(0, 4), dest(1, 5), tile_expert, n_used, n_rows


def _dispatch_kernel(d1_ref, d2_ref, h3_ref, zero_hbm, xs_hbm, sem, *, tm):
    del zero_hbm
    base = pl.program_id(0) * tm

    def issue(k, carry):
        src = h3_ref.at[pl.ds(k, 1)]
        pltpu.make_async_copy(src, xs_hbm.at[pl.ds(d1_ref[base + k], 1)], sem).start(priority=0)
        pltpu.make_async_copy(src, xs_hbm.at[pl.ds(d2_ref[base + k], 1)], sem).start(priority=1)
        return carry

    lax.fori_loop(0, tm, issue, 0, unroll=4)
    for _ in range(2):
        pltpu.make_async_copy(h3_ref, xs_hbm.at[pl.ds(0, tm)], sem).wait()


def _dispatch(h3, dest1, dest2, n_rows):
    t = h3.shape[0]
    tm = min(512, t)
    grid_spec = pltpu.PrefetchScalarGridSpec(
        num_scalar_prefetch=2,
        grid=(t // tm,),
        in_specs=[pl.BlockSpec((tm, SUBL, LANE), lambda i, d1, d2: (i, 0, 0)), pl.BlockSpec(memory_space=pl.ANY)],
        out_specs=pl.BlockSpec(memory_space=pl.ANY),
        scratch_shapes=[pltpu.SemaphoreType.DMA(())],
    )
    return pl.pallas_call(
        functools.partial(_dispatch_kernel, tm=tm),
        grid_spec=grid_spec,
        out_shape=jax.ShapeDtypeStruct((n_rows, SUBL, LANE), f32),
        input_output_aliases={3: 0},
        compiler_params=_cparams("arbitrary"),
        name="moe_dispatch",
    )(dest1, dest2, h3, jnp.zeros((n_rows, SUBL, LANE), f32))


def _experts_kernel(te_ref, nu_ref, xs_ref, wg_ref, wu_ref, wd_ref, ys_ref):
    del te_ref
    r = pl.program_id(0)

    @pl.when(r < nu_ref[0])
    def _():
        x = jnp.concatenate([xs_ref[:, j, :] for j in range(SUBL)], axis=1).astype(bf16)
        for j, blk in enumerate(_tile_rows(_swiglu_bf16(x, wg_ref[...], wu_ref[...], wd_ref[...]))):
            ys_ref[:, j, :] = blk

    @pl.when(r >= nu_ref[0])
    def _():
        ys_ref[...] = jnp.zeros(ys_ref.shape, f32)


def _experts(xs, tile_expert, n_used, wg, wu, wd):
    n_rows = xs.shape[0]
    d_ff = wg.shape[2]
    rows = pl.BlockSpec((ROW_TILE, SUBL, LANE), lambda r, te, nu: (r, 0, 0))
    w_in = pl.BlockSpec((None, D_MODEL, d_ff), lambda r, te, nu: (te[r], 0, 0), pipeline_mode=pl.Buffered(1))
    w_dn = pl.BlockSpec((None, d_ff, D_MODEL), lambda r, te, nu: (te[r], 0, 0), pipeline_mode=pl.Buffered(1))
    grid_spec = pltpu.PrefetchScalarGridSpec(
        num_scalar_prefetch=2,
        grid=(n_rows // ROW_TILE,),
        in_specs=[rows, w_in, w_in, w_dn],
        out_specs=rows,
    )
    return pl.pallas_call(
        _experts_kernel,
        grid_spec=grid_spec,
        out_shape=jax.ShapeDtypeStruct((n_rows, SUBL, LANE), f32),
        compiler_params=_cparams("arbitrary"),
        name="moe_experts",
    )(tile_expert, n_used, xs, wg, wu, wd)


def _combine_kernel(d1_ref, d2_ref, x_ref, meta_ref, ys_hbm, y_ref, buf, sem, *, tm):
    base = pl.program_id(0) * tm

    def issue(k, carry):
        t = base + k
        pltpu.make_async_copy(ys_hbm.at[pl.ds(d1_ref[t], 1)], buf.at[0, pl.ds(k, 1)], sem).start(priority=0)
        pltpu.make_async_copy(ys_hbm.at[pl.ds(d2_ref[t], 1)], buf.at[1, pl.ds(k, 1)], sem).start(priority=1)
        return carry

    lax.fori_loop(0, tm, issue, 0, unroll=4)
    for s in range(2):
        pltpu.make_async_copy(ys_hbm.at[pl.ds(0, tm)], buf.at[s], sem).wait()
    meta = meta_ref[...]
    g1 = meta[:, 2:3]
    g2 = meta[:, 3:4]
    for j in range(SUBL):
        sl = slice(j * LANE, (j + 1) * LANE)
        y_ref[:, sl] = x_ref[:, sl] + (g1 * buf[0, :, j, :] + g2 * buf[1, :, j, :])


def _combine(x, meta, ys, dest1, dest2):
    t = x.shape[0]
    tm = min(256, t)
    grid_spec = pltpu.PrefetchScalarGridSpec(
        num_scalar_prefetch=2,
        grid=(t // tm,),
        in_specs=[pl.BlockSpec((tm, D_MODEL), lambda i, d1, d2: (i, 0)),
                  pl.BlockSpec((tm, LANE), lambda i, d1, d2: (i, 0)),
                  pl.BlockSpec(memory_space=pl.ANY)],
        out_specs=pl.BlockSpec((tm, D_MODEL), lambda i, d1, d2: (i, 0)),
        scratch_shapes=[pltpu.VMEM((2, tm, SUBL, LANE), f32), pltpu.SemaphoreType.DMA(())],
    )
    return pl.pallas_call(
        functools.partial(_combine_kernel, tm=tm),
        grid_spec=grid_spec,
        out_shape=jax.ShapeDtypeStruct((t, D_MODEL), f32),
        compiler_params=_cparams("arbitrary"),
        name="moe_combine",
    )(dest1, dest2, x, meta, ys)


def _moe(x, g, router_pad, wg, wu, wd):
    t = x.shape[0]
    h3, meta, counts = _router(x, g, router_pad)
    if t < SPARSE_MIN_TOKENS:
        return _moe_dense(x, h3, meta, wg, wu, wd)
    dest1, dest2, tile_expert, n_used, n_rows = _route_plan(meta, counts, t)
    xs = _dispatch(h3, dest1, dest2, n_rows)
    ys = _experts(xs, tile_expert, n_used, wg, wu, wd)
    return _combine(x, meta, ys, dest1, dest2)


def _t5_bucket_np(n):
    n = np.maximum(n, 0)
    max_exact = NUM_BUCKETS // 2
    nf = np.maximum(n, 1).astype(np.float32)
    large = max_exact + (np.log(nf / np.float32(max_exact)) / np.float32(math.log(MAX_DISTANCE / max_exact))
                         * np.float32(NUM_BUCKETS - max_exact)).astype(np.int32)
    return np.where(n < max_exact, n, np.minimum(large, NUM_BUCKETS - 1))


def _shifted_bias(rel_bias):
    rb = rel_bias.astype(f32)
    return rb - rb[NUM_BUCKETS - 1:NUM_BUCKETS]


ACC_ROWS = HEAD_W + 8
LOG2E = math.log2(math.e)


def _attn_kernel(qi_ref, kj_ref, lam_ref, qt_ref, k_ref, vt_ref, toep_ref, o_ref, qs_scr, m_scr, acc_scr, *, tq):
    p = pl.program_id(1)
    i = qi_ref[p]
    j = kj_ref[p]
    n_hc = 2 * N_HEADS
    c2 = (DKB ** -0.5) * LOG2E

    @pl.when(j == 0)
    def _():
        qt = qt_ref[...] * c2
        row_grp = _iota(qt.shape, 0) // DKB
        for hc in range(n_hc):
            qs_scr[:, hc * tq:(hc + 1) * tq] = jnp.where(row_grp == hc, qt, 0.0).astype(bf16)
        m_scr[...] = jnp.full(m_scr.shape, NEG, f32)
        acc_scr[...] = jnp.zeros(acc_scr.shape, f32)

    def step(near):
        tk = k_ref.shape[0]
        st_all = _dg(k_ref[...].astype(bf16), qs_scr[...], NN)
        vt = vt_ref[...]
        ones = jnp.ones((ACC_ROWS - HEAD_W, tk), f32)
        for h in range(N_HEADS):
            vh = jnp.concatenate([vt[h * HEAD_W:(h + 1) * HEAD_W, :], ones], axis=0).astype(bf16)
            for hc in (2 * h, 2 * h + 1):
                s = st_all[:, hc * tq:(hc + 1) * tq]
                if near:
                    s = s + toep_ref[(i - j) * N_HEADS + h]
                m_old = m_scr[hc:hc + 1, :]
                m_new = jnp.maximum(m_old, jnp.max(s, axis=0, keepdims=True))
                pexp = jnp.exp2(s - m_new)
                acc_scr[hc] = jnp.exp2(m_old - m_new) * acc_scr[hc] + _dg(vh, pexp.astype(bf16), NN)
                m_scr[hc:hc + 1, :] = m_new

    @pl.when(i - j <= 1)
    def _():
        step(True)

    @pl.when(i - j > 1)
    def _():
        step(False)

    @pl.when(j == i)
    def _():
        lam = lam_ref[0]
        outs = []
        for h in range(N_HEADS):
            a0 = acc_scr[2 * h]
            a1 = acc_scr[2 * h + 1]
            outs.append(a0[0:HEAD_W] * _recip(a0[HEAD_W:HEAD_W + 1])
                        - lam * (a1[0:HEAD_W] * _recip(a1[HEAD_W:HEAD_W + 1])))
        o_ref[...] = jnp.concatenate(outs, axis=0).T


def _toeplitz_kernel(u_ref, o_ref):
    t = o_ref.shape[0]
    rows = jnp.broadcast_to(u_ref[...], (t, 2 * t))
    o_ref[...] = pltpu.roll(rows, 0, 1, stride=1, stride_axis=0)[:, t:2 * t]


def _toeplitz_bias_tiles(rel_bias, t):
    m = np.arange(2 * t)[None, :]
    dist = m - t + np.array([0, t])[:, None]
    tab = _shifted_bias(rel_bias)
    u = jnp.take(tab, jnp.asarray(_t5_bucket_np(dist)), axis=0)
    u = jnp.where(jnp.asarray(dist >= 0)[:, :, None], u * LOG2E, NEG)
    u = jnp.transpose(u, (0, 2, 1)).reshape(2 * N_HEADS, 1, 2 * t)
    return pl.pallas_call(
        _toeplitz_kernel,
        grid=(2 * N_HEADS,),
        in_specs=[pl.BlockSpec((None, 1, 2 * t), lambda i: (i, 0, 0))],
        out_specs=pl.BlockSpec((None, t, t), lambda i: (i, 0, 0)),
        out_shape=jax.ShapeDtypeStruct((2 * N_HEADS, t, t), f32),
        compiler_params=_cparams("arbitrary"),
        name="toeplitz_bias",
    )(u)


def _attn_prompt(qnt, kn, vt, lam, rel_bias, n_batch, seq):
    tq = min(512, seq)
    nq = seq // tq
    pairs =[(i, j) for i in range(nq) for j in range(i + 1)]
    qi = jnp.asarray(np.array([a for a, _ in pairs], np.int32))
    kj = jnp.asarray(np.array([b for _, b in pairs], np.int32))
    toep = _toeplitz_bias_tiles(rel_bias, tq)
    grid_spec = pltpu.PrefetchScalarGridSpec(
        num_scalar_prefetch=2,
        grid=(n_batch, len(pairs)),
        in_specs=[
            pl.BlockSpec(memory_space=pltpu.SMEM),
            pl.BlockSpec((GROUP_W, tq), lambda b_, p_, qi_, kj_: (0, b_ * nq + qi_[p_])),
            pl.BlockSpec((tq, GROUP_W), lambda b_, p_, qi_, kj_: (b_ * nq + kj_[p_], 0)),
            pl.BlockSpec((GROUP_W, tq), lambda b_, p_, qi_, kj_: (0, b_ * nq + kj_[p_])),
            pl.BlockSpec((2 * N_HEADS, tq, tq), lambda b_, p_, qi_, kj_: (0, 0, 0)),
        ],
        out_specs=pl.BlockSpec((tq, GROUP_W), lambda b_, p_, qi_, kj_: (b_ * nq + qi_[p_], 0)),
        scratch_shapes=[
            pltpu.VMEM((GROUP_W, 2 * N_HEADS * tq), bf16),
            pltpu.VMEM((2 * N_HEADS, tq), f32),
            pltpu.VMEM((2 * N_HEADS, ACC_ROWS, tq), f32),
        ],
    )
    return pl.pallas_call(
        functools.partial(_attn_kernel, tq=tq),
        grid_spec=grid_spec,
        out_shape=jax.ShapeDtypeStruct((n_batch * seq, GROUP_W), f32),
        compiler_params=_cparams("arbitrary", "arbitrary"),
        name="attn_prompt",
    )(qi, kj, lam.reshape(1), qnt, kn, vt, toep)


def _attn_decode_kernel(pt_ref, lam_ref, q_ref, kn_ref, vn_ref, blast_ref, bself_ref, *rest, pg, n_pages):
    k_refs = rest[:pg]
    v_refs = rest[pg:2 * pg]
    o_ref, qs_scr, s_scr, v_scr = rest[2 * pg:]
    t = pl.program_id(1)
    n_steps = n_pages // pg
    n_hc = 2 * N_HEADS
    page = k_refs[0].shape[1]
    scale = DKB ** -0.5
    rnd = lambda z: z.astype(bf16).astype(f32)

    @pl.when(t == 0)
    def _():
        q = jnp.broadcast_to(q_ref[...], (n_hc, GROUP_W))
        keep = (_iota(q.shape, 1) // DKB) == _iota(q.shape, 0)
        qs_scr[...] = jnp.where(keep, q, 0.0)

    qs_b = qs_scr[...].astype(bf16)
    parts = []
    for g in range(pg):
        s = _dg(qs_b, k_refs[g][...].astype(bf16), NN) * scale
        is_last = (t * pg + g) == (n_pages - 1)
        parts.append(s + jnp.where(is_last, blast_ref[...], 0.0))
        v_scr[t * pg + g] = v_refs[g][...].astype(bf16)
    s_scr[t] = jnp.concatenate(parts, axis=1)

    @pl.when(t == n_steps - 1)
    def _():
        s_all = s_scr[...]
        s_self = jnp.sum(rnd(qs_scr[...]) * rnd(kn_ref[...]), axis=-1, keepdims=True) * scale + bself_ref[...]
        m = jnp.maximum(jnp.max(jnp.max(s_all, axis=2, keepdims=True), axis=0), s_self)
        p = jnp.exp(s_all - m)
        p_self = jnp.exp(s_self - m)
        l = jnp.sum(jnp.sum(p, axis=2, keepdims=True), axis=0) + p_self
        inv_l = _recip(l)
        pn = p * inv_l
        pn_self = p_self * inv_l
        lam = lam_ref[0]
        rows = [pn[:, 2 * h:2 * h + 1, :] - lam * pn[:, 2 * h + 1:2 * h + 2, :] for h in range(N_HEADS)]
        s_scr[...] = jnp.concatenate(rows + [jnp.zeros_like(rows[0])] * N_HEADS, axis=1)
        rows_self = [pn_self[2 * h:2 * h + 1] - lam * pn_self[2 * h + 1:2 * h + 2] for h in range(N_HEADS)]
        a_self = jnp.concatenate(rows_self + [jnp.zeros_like(rows_self[0])] * N_HEADS, axis=0)

        def weighted_values(t2, acc):
            a = s_scr[t2].astype(bf16)
            for g in range(pg):
                acc = acc + _dg(a[:, g * page:(g + 1) * page], v_scr[t2 * pg + g], NT)
            return acc

        o = lax.fori_loop(0, n_steps, weighted_values, rnd(a_self) * rnd(vn_ref[...]))
        lane_head = _iota((1, GROUP_W), 1) // HEAD_W
        out = jnp.zeros((1, GROUP_W), f32)
        for h in range(N_HEADS):
            out = jnp.where(lane_head == h, o[h:h + 1], out)
        o_ref[...] = out


def _attn_decode(qn, kn, vn, page_table, cache_k, cache_v, layer, lam, rel_bias):
    n_b, n_pages = page_table.shape
    page = cache_k.shape[3]
    pg = min(16, n_pages)
    n_steps = n_pages // pg
    past = n_pages * page
    tab = _shifted_bias(rel_bias)
    d_last = past - ((n_pages - 1) * page + np.arange(page))
    blast = jnp.repeat(jnp.take(tab, jnp.asarray(_t5_bucket_np(d_last)), axis=0).T, 2, axis=0)
    bself = jnp.repeat(tab[0].reshape(N_HEADS, 1), 2, axis=0)

    def page_spec(g):
        return pl.BlockSpec((None, None, GROUP_W, page), lambda b_, t_, pt: (layer, pt[b_, t_ * pg + g], 0, 0))

    row = pl.BlockSpec((None, 1, GROUP_W), lambda b_, t_, pt: (b_, 0, 0))
    grid_spec = pltpu.PrefetchScalarGridSpec(
        num_scalar_prefetch=1,
        grid=(n_b, n_steps),
        in_specs=[pl.BlockSpec(memory_space=pltpu.SMEM), row, row, row,
                  pl.BlockSpec((2 * N_HEADS, page), lambda b_, t_, pt: (0, 0)),
                  pl.BlockSpec((2 * N_HEADS, 1), lambda b_, t_, pt: (0, 0))]
                 + [page_spec(g) for g in range(pg)] * 2,
        out_specs=row,
        scratch_shapes=[
            pltpu.VMEM((2 * N_HEADS, GROUP_W), f32),
            pltpu.VMEM((n_steps, 2 * N_HEADS, pg * page), f32),
            pltpu.VMEM((n_pages, GROUP_W, page), bf16),
        ],
    )
    r3 = lambda z: z.reshape(n_b, 1, GROUP_W)
    out = pl.pallas_call(
        functools.partial(_attn_decode_kernel, pg=pg, n_pages=n_pages),
        grid_spec=grid_spec,
        out_shape=jax.ShapeDtypeStruct((n_b, 1, GROUP_W), f32),
        compiler_params=_cparams("arbitrary", "arbitrary"),
        name="attn_decode",
    )(page_table, lam.reshape(1), r3(qn), r3(kn), r3(vn), blast, bself,
      *([cache_k] * pg), *([cache_v] * pg))
    return out.reshape(n_b, GROUP_W)


def _tri(n, dtype=f32):
    return (_iota((n, n), 0) >= _iota((n, n), 1)).astype(dtype)


def _block_tri_t(n, blk):
    r = _iota((n, n), 0)
    c = _iota((n, n), 1)
    return (((r // blk) == (c // blk)) & (r <= c)).astype(bf16)


def _block_tri(n, blk):
    r = _iota((n, n), 0)
    c = _iota((n, n), 1)
    return (((r // blk) == (c // blk)) & (r >= c)).astype(bf16)


def _gdn_kernel(u_ref, gc_ref, gr_ref, cw_ref, alr_ref, dtr_ref, alc_ref, dtc_ref, o_ref, s_out_ref, ext_scr, s_scr, *, tb):
    i = pl.program_id(1)

    @pl.when(i == 0)
    def _():
        ext_scr[0:8, :] = jnp.zeros((8, 3 * GROUP_W), f32)
        s_scr[...] = jnp.zeros(s_scr.shape, f32)

    ext_scr[8:8 + tb, :] = u_ref[...]
    w = cw_ref[...]
    conv = ext_scr[8:8 + tb, :] * w[3:4]
    for jj in range(1, CONV_W):
        conv = conv + ext_scr[8 - jj:8 - jj + tb, :] * w[3 - jj:4 - jj]
    ext_scr[0:8, :] = ext_scr[tb:tb + 8, :]
    qkv = _silu(conv)

    def l2n(x):
        return x * lax.rsqrt(_group_sum(x * x, HEAD_W) + EPS)

    q = l2n(qkv[:, 0:GROUP_W]) * (HEAD_W ** -0.5)
    k = l2n(qkv[:, GROUP_W:2 * GROUP_W])
    v = qkv[:, 2 * GROUP_W:3 * GROUP_W]

    gc = gc_ref[...]
    g_col = -jnp.exp(alr_ref[...]) * _softplus(gc[:, 0:4] + dtr_ref[...])
    beta_col = jax.nn.sigmoid(gc[:, 4:8])
    gr = gr_ref[...]
    g_row = -jnp.exp(alc_ref[...]) * _softplus(gr[0:4, :] + dtc_ref[...])
    g_row8 = jnp.concatenate([g_row, jnp.zeros_like(g_row)], axis=0)
    gcum_row = _mm2(g_row8, _block_tri_t(tb, CHUNK))

    same, lower, strict = _bd_masks()
    tri = _tri(CHUNK, bf16)
    r = _iota((GROUP_W, GROUP_W), 0)
    c = _iota((GROUP_W, GROUP_W), 1)
    eye = (r == c).astype(f32)

    chunks = range(tb // CHUNK)
    gcums, qks, m_bds = [], [], []
    for ch in chunks:
        lo = ch * CHUNK
        gcum = _mm2l(tri, g_col[lo:lo + CHUNK])
        g_stack = _stack_cols(gcum)
        g_cat = _cat_rows(gcum_row, lo)
        decay = jnp.exp(jnp.where(lower, g_stack - g_cat, NEG))
        ksm = _head_stack(k[lo:lo + CHUNK])
        kk = _mm1(ksm, ksm, NT)
        qks.append(_mm1(_head_stack(q[lo:lo + CHUNK]), ksm, NT) * decay)
        m_bds.append(_stack_cols(beta_col[lo:lo + CHUNK]) * kk * jnp.where(strict, decay, 0.0))
        gcums.append(gcum)

    def sibling(lev):
        return ((r >> (lev + 1)) == (c >> (lev + 1))) & (((r >> lev) & 1) == 1) & (((c >> lev) & 1) == 0)

    xs = [eye - jnp.where(sibling(0), m, 0.0) for m in m_bds]
    for lev in range(1, 6):
        sel = sibling(lev)
        xs = [x - _mm3(_mm2(x, jnp.where(sel, m, 0.0).astype(bf16)), x) for x, m in zip(xs, m_bds)]

    for ch in chunks:
        lo = ch * CHUNK
        qc, kc, vc = q[lo:lo + CHUNK], k[lo:lo + CHUNK], v[lo:lo + CHUNK]
        gcum, bcol = gcums[ch], beta_col[lo:lo + CHUNK]
        s_bd = s_scr[...]
        kq_s = _mm1(jnp.concatenate([kc, qc], axis=0), s_bd)
        ks, qs = kq_s[0:CHUNK], kq_s[CHUNK:2 * CHUNK]
        eg_all = _expand_cols(jnp.exp(gcum))
        rhs = _expand_cols(bcol) * (vc - eg_all * ks)
        u_sm = _mm3(xs[ch], _head_stack(rhs))
        o_sm = _mm1(qks[ch], u_sm)
        o_ref[lo:lo + CHUNK, :] = eg_all * qs + _fold_heads(o_sm)
        u_all = _fold_heads(u_sm)
        g_last = gcum[CHUNK - 1:CHUNK, :]
        kw = kc * _expand_cols(jnp.exp(g_last - gcum))
        d_stack = jnp.concatenate(
            [jnp.broadcast_to(jnp.exp(g_last[:, h:h + 1]), (HEAD_W, 1)) for h in range(N_HEADS)], axis=0)
        s_scr[...] = d_stack * s_bd + jnp.where(same, _mm1(kw, u_all, TN), 0.0)

    @pl.when(i == pl.num_programs(1) - 1)
    def _():
        s_out_ref[...] = s_scr[...]


def _gdn_prompt(p, gt, conv_w, a_log, dt_bias, n_batch, seq):
    tb = min(256, seq)
    nb = seq // tb
    r14 = lambda z: z.astype(f32).reshape(1, N_HEADS)
    c41 = lambda z: z.astype(f32).reshape(N_HEADS, 1)
    o, s_bd = pl.pallas_call(
        functools.partial(_gdn_kernel, tb=tb),
        grid=(n_batch, nb),
        in_specs=[
            pl.BlockSpec((tb, 3 * GROUP_W), lambda b, i: (b * nb + i, 0)),
            pl.BlockSpec((tb, 128), lambda b, i: (b * nb + i, GATE_COL // 128)),
            pl.BlockSpec((16, tb), lambda b, i: (0, b * nb + i)),
            pl.BlockSpec((CONV_W, 3 * GROUP_W), lambda b, i: (0, 0)),
            pl.BlockSpec((1, N_HEADS), lambda b, i: (0, 0)),
            pl.BlockSpec((1, N_HEADS), lambda b, i: (0, 0)),
            pl.BlockSpec((N_HEADS, 1), lambda b, i: (0, 0)),
            pl.BlockSpec((N_HEADS, 1), lambda b, i: (0, 0)),
        ],
        out_specs=[
            pl.BlockSpec((tb, GROUP_W), lambda b, i: (b * nb + i, 0)),
            pl.BlockSpec((None, GROUP_W, GROUP_W), lambda b, i: (b, 0, 0)),
        ],
        out_shape=[jax.ShapeDtypeStruct((n_batch * seq, GROUP_W), f32),
                   jax.ShapeDtypeStruct((n_batch, GROUP_W, GROUP_W), f32)],
        scratch_shapes=[pltpu.VMEM((tb + 8, 3 * GROUP_W), f32), pltpu.VMEM((GROUP_W, GROUP_W), f32)],
        compiler_params=_cparams("arbitrary", "arbitrary"),
        name="gdn_prompt",
    )(p, p, gt, conv_w.astype(f32), r14(a_log), r14(dt_bias), c41(a_log), c41(dt_bias))
    return o, _bd_diag(s_bd)


def _bd_diag(s_bd):
    n_b = s_bd.shape[0]
    s5 = s_bd.reshape(n_b, N_HEADS, HEAD_W, N_HEADS, HEAD_W)
    return jnp.stack([s5[:, h, :, h, :] for h in range(N_HEADS)], axis=1)


def _hgrn_kernel(q_ref, f_ref, i_ref, lb_ref, o_ref, s_out_ref, st_scr, q_scr, k_scr, b_scr, *, tb):
    blk = pl.program_id(1)

    @pl.when(blk == 0)
    def _():
        st_scr[...] = jnp.zeros(st_scr.shape, f32)

    lb = lb_ref[...]
    z = f_ref[...]
    logf = jnp.log(lb + (1.0 - lb) * jax.nn.sigmoid(z))
    q_scr[...] = _silu(q_ref[...])
    k_scr[...] = (1.0 - lb) * jax.nn.sigmoid(-z)
    b_scr[...] = _mm2l(_block_tri(tb, SUB), logf)

    same, _, _ = _bd_masks()
    ones_bd = _group_ones(GROUP_W, HEAD_W)
    row = _iota((SUB * SUB, GROUP_W), 0)
    tmask = (row % SUB) >= (row // SUB)

    def rep_t(x):
        return jnp.broadcast_to(x[None], (SUB, SUB, GROUP_W)).reshape(SUB * SUB, GROUP_W)

    def rep_j(x):
        return jnp.broadcast_to(x[:, None, :], (SUB, SUB, GROUP_W)).reshape(SUB * SUB, GROUP_W)

    def body(c, carry):
        r0 = pl.multiple_of(c * SUB, SUB)
        qs = q_scr[pl.ds(r0, SUB), :]
        ks = k_scr[pl.ds(r0, SUB), :]
        vs = i_ref[pl.ds(r0, SUB), :]
        bs = b_scr[pl.ds(r0, SUB), :]
        st = st_scr[...]
        o_inter = _mm1(qs * jnp.exp(bs), st, NT)
        wgt = rep_t(qs) * jnp.exp(jnp.where(tmask, rep_t(bs) - rep_j(bs), NEG)) * rep_j(ks)
        a = _mm2(wgt, ones_bd)
        o_diag = jnp.sum((a * rep_j(vs)).reshape(SUB, SUB, GROUP_W), axis=0)
        o_ref[pl.ds(r0, SUB), :] = o_inter + o_diag
        b_last = bs[SUB - 1:SUB, :]
        kw = ks * jnp.exp(b_last - bs)
        st_scr[...] = st * jnp.exp(b_last) + jnp.where(same, _mm1(vs, kw, TN), 0.0)
        return carry

    lax.fori_loop(0, tb // SUB, body, 0, unroll=4)

    @pl.when(blk == pl.num_programs(1) - 1)
    def _():
        s_out_ref[...] = st_scr[...]


def _hgrn_prompt(p, lb, n_batch, seq):
    tb = min(256, seq)
    nb = seq // tb
    blk = lambda col: pl.BlockSpec((tb, GROUP_W), lambda b, i: (b * nb + i, col))
    o, st = pl.pallas_call(
        functools.partial(_hgrn_kernel, tb=tb),
        grid=(n_batch, nb),
        in_specs=[blk(7), blk(8), blk(9), pl.BlockSpec((1, GROUP_W), lambda b, i: (0, 0))],
        out_specs=[
            pl.BlockSpec((tb, GROUP_W), lambda b, i: (b * nb + i, 0)),
            pl.BlockSpec((None, GROUP_W, GROUP_W), lambda b, i: (b, 0, 0)),
        ],
        out_shape=[jax.ShapeDtypeStruct((n_batch * seq, GROUP_W), f32),
                   jax.ShapeDtypeStruct((n_batch, GROUP_W, GROUP_W), f32)],
        scratch_shapes=[pltpu.VMEM((GROUP_W, GROUP_W), f32)] + [pltpu.VMEM((tb, GROUP_W), f32)] * 3,
        compiler_params=_cparams("arbitrary", "arbitrary"),
        name="hgrn_prompt",
    )(p, p, p, lb.astype(f32).reshape(1, GROUP_W))
    return o, jnp.swapaxes(_bd_diag(st), -1, -2)


def _log_sigmoid(x):
    return jnp.minimum(x, 0.0) - jnp.log1p(jnp.exp(-jnp.abs(x)))


def _mlstm_kernel(q_ref, k_ref, v_ref, gc_ref, gr_ref, ibr_ref, fbr_ref, ibc_ref, fbc_ref,
                  o_ref, c_out_ref, n_out_ref, m_out_ref, c_scr, n_scr, m_scr, *, tb):
    blk = pl.program_id(1)

    @pl.when(blk == 0)
    def _():
        c_scr[...] = jnp.zeros(c_scr.shape, f32)
        n_scr[...] = jnp.zeros(n_scr.shape, f32)
        m_scr[...] = jnp.zeros(m_scr.shape, f32)

    q = q_ref[...]
    k = k_ref[...] * (HEAD_W ** -0.5)
    v = v_ref[...]
    gc = gc_ref[...]
    li_col = gc[:, 8:12] + ibr_ref[...]
    lf_col = _log_sigmoid(gc[:, 12:16] + fbr_ref[...])
    gr = gr_ref[...]
    li_row = gr[8:12, :] + ibc_ref[...]
    lf_row = _log_sigmoid(gr[12:16, :] + fbc_ref[...])
    b_row = _mm2(jnp.concatenate([lf_row, jnp.zeros_like(lf_row)], axis=0), _block_tri_t(tb, CHUNK))

    same, lower, _ = _bd_masks()
    tri = _tri(CHUNK, bf16)

    for ch in range(tb // CHUNK):
        lo = ch * CHUNK
        qc, kc, vc = q[lo:lo + CHUNK], k[lo:lo + CHUNK], v[lo:lo + CHUNK]
        b_col = _mm2l(tri, lf_col[lo:lo + CHUNK])
        b_stack = _stack_cols(b_col)
        d_mat = jnp.where(lower, b_stack - _cat_rows(b_row, lo) + _cat_rows(li_row, lo), NEG)
        m_row = m_scr[...]
        m_stack = jnp.concatenate(
            [jnp.broadcast_to(m_row[:, h:h + 1], (CHUNK, 1)) for h in range(N_HEADS)], axis=0)
        inter = b_stack + m_stack
        m_t = jnp.maximum(inter, jnp.max(d_mat, axis=-1, keepdims=True))
        w_inter = jnp.exp(inter - m_t)
        qsm = _head_stack(qc)
        ksm = _head_stack(kc)
        pmat = _mm1(qsm, ksm, NT) * jnp.exp(d_mat - m_t)
        c_bd = c_scr[...]
        n_row = n_scr[...]
        num = w_inter * _mm1(qsm, c_bd) + _mm1(pmat, _head_stack(vc))
        den = w_inter * jnp.sum(qsm * n_row, axis=-1, keepdims=True) + jnp.sum(pmat, axis=-1, keepdims=True)
        h_sm = num / jnp.maximum(jnp.abs(den), jnp.exp(-m_t))
        o_ref[lo:lo + CHUNK, :] = _fold_heads(h_sm)
        m_new = jnp.concatenate(
            [m_t[h * CHUNK + CHUNK - 1:h * CHUNK + CHUNK, :] for h in range(N_HEADS)], axis=1)
        b_last = b_col[CHUNK - 1:CHUNK, :]
        w_end = jnp.exp(b_last - b_col + li_col[lo:lo + CHUNK] - m_new)
        d0 = jnp.exp(b_last + m_row - m_new)
        kw = kc * _expand_cols(w_end)
        d0_stack = jnp.concatenate(
            [jnp.broadcast_to(d0[:, h:h + 1], (HEAD_W, 1)) for h in range(N_HEADS)], axis=0)
        c_scr[...] = d0_stack * c_bd + jnp.where(same, _mm1(kw, vc, TN), 0.0)
        n_scr[...] = _expand_cols(d0) * n_row + jnp.sum(kw, axis=0, keepdims=True)
        m_scr[...] = m_new

    @pl.when(blk == pl.num_programs(1) - 1)
    def _():
        c_out_ref[...] = c_scr[...]
        n_out_ref[...] = n_scr[...]
        m_out_ref[...] = m_scr[...]


def _mlstm_prompt(p, gt, i_bias, f_bias, n_batch, seq):
    tb = min(256, seq)
    nb = seq // tb
    blk = lambda col: pl.BlockSpec((tb, GROUP_W), lambda b, i: (b * nb + i, col))
    r14 = lambda z: z.astype(f32).reshape(1, N_HEADS)
    c41 = lambda z: z.astype(f32).reshape(N_HEADS, 1)
    small = lambda shape: pl.BlockSpec(shape, lambda b, i: (0, 0))
    o, c_bd, n_row, m_row = pl.pallas_call(
        functools.partial(_mlstm_kernel, tb=tb),
        grid=(n_batch, nb),
        in_specs=[blk(11), blk(12), blk(13),
                  pl.BlockSpec((tb, 128), lambda b, i: (b * nb + i, GATE_COL // 128)),
                  pl.BlockSpec((16, tb), lambda b, i: (0, b * nb + i)),
                  small((1, N_HEADS)), small((1, N_HEADS)), small((N_HEADS, 1)), small((N_HEADS, 1))],
        out_specs=[
            pl.BlockSpec((tb, GROUP_W), lambda b, i: (b * nb + i, 0)),
            pl.BlockSpec((None, GROUP_W, GROUP_W), lambda b, i: (b, 0, 0)),
            pl.BlockSpec((None, 1, GROUP_W), lambda b, i: (b, 0, 0)),
            pl.BlockSpec((None, 1, N_HEADS), lambda b, i: (b, 0, 0)),
        ],
        out_shape=[jax.ShapeDtypeStruct((n_batch * seq, GROUP_W), f32),
                   jax.ShapeDtypeStruct((n_batch, GROUP_W, GROUP_W), f32),
                   jax.ShapeDtypeStruct((n_batch, 1, GROUP_W), f32),
                   jax.ShapeDtypeStruct((n_batch, 1, N_HEADS), f32)],
        scratch_shapes=[pltpu.VMEM((GROUP_W, GROUP_W), f32), pltpu.VMEM((1, GROUP_W), f32),
                        pltpu.VMEM((1, N_HEADS), f32)],
        compiler_params=_cparams("arbitrary", "arbitrary"),
        name="mlstm_prompt",
    )(p, p, p, p, gt, r14(i_bias), r14(f_bias), c41(i_bias), c41(f_bias))
    return (o, _bd_diag(c_bd), n_row.reshape(n_batch, N_HEADS, HEAD_W), m_row.reshape(n_batch, N_HEADS))


def _gdn_dec_prep_kernel(u_ref, buf_ref, cw_ref, q_ref, k_ref, v_ref):
    w = cw_ref[...]
    conv = u_ref[...] * w[3:4]
    for jj in range(CONV_W - 1):
        conv = conv + buf_ref[jj] * w[jj:jj + 1]
    qkv = _silu(conv)

    def l2n(x):
        return x * lax.rsqrt(_group_sum(x * x, HEAD_W) + EPS)

    q_ref[...] = l2n(qkv[:, 0:GROUP_W]) * (HEAD_W ** -0.5)
    k_ref[...] = l2n(qkv[:, GROUP_W:2 * GROUP_W])
    v_ref[...] = qkv[:, 2 * GROUP_W:3 * GROUP_W]


def _gdn_dec_prep(p, conv_buf, conv_w):
    n_b = p.shape[0]
    out = jax.ShapeDtypeStruct((n_b, GROUP_W), f32)
    return pl.pallas_call(
        _gdn_dec_prep_kernel,
        grid=(1,),
        in_specs=[pl.BlockSpec((n_b, 3 * GROUP_W), lambda i: (0, 0)),
                  pl.BlockSpec((CONV_W - 1, n_b, 3 * GROUP_W), lambda i: (0, 0, 0)),
                  pl.BlockSpec((CONV_W, 3 * GROUP_W), lambda i: (0, 0))],
        out_specs=[pl.BlockSpec((n_b, GROUP_W), lambda i: (0, 0))] * 3,
        out_shape=[out, out, out],
        compiler_params=_cparams("arbitrary"),
        name="gdn_dec_prep",
    )(p, jnp.swapaxes(conv_buf.astype(f32), 0, 1), conv_w.astype(f32))


def _rec_decode_kernel(cols_ref, vrows_ref, scal_ref, sg_ref, sh_ref, sc_ref,
                       o_ref, sg_out, sh_out, sc_out, sn_out, sm_out):
    cols = cols_ref[...]
    vrows = vrows_ref[...]
    scal = scal_ref[...]
    col = lambda k: cols[:, :, k:k + 1]
    vrow = lambda k: vrows[:, k:k + 1, :]
    sca = lambda k: scal[:, :, k:k + 1]

    q, k, v = col(0), col(1), vrow(0)
    s = sg_ref[...]
    g = -jnp.exp(sca(2)) * _softplus(sca(0) + sca(3))
    eg = jnp.exp(g)
    beta = jax.nn.sigmoid(sca(1))
    ks = jnp.sum(k * s, axis=1, keepdims=True)
    qs = jnp.sum(q * s, axis=1, keepdims=True)
    u = beta * (v - eg * ks)
    qk = jnp.sum(q * k, axis=1, keepdims=True)
    o_ref[:, 0:1, :] = eg * qs + qk * u
    sg_out[...] = eg * s + k * u

    lb = col(6)
    z = col(3)
    logf = jnp.log(lb + (1.0 - lb) * jax.nn.sigmoid(z))
    kc = (1.0 - lb) * jax.nn.sigmoid(-z)
    qc = _silu(col(2))
    vc = vrow(1)
    sh = sh_ref[...]
    ef = jnp.exp(logf)
    o_ref[:, 1:2, :] = (jnp.sum((qc * ef) * sh, axis=1, keepdims=True)
                        + jnp.sum(qc * kc, axis=1, keepdims=True) * vc)
    sh_out[...] = ef * sh + kc * vc

    qd = col(4)
    kd = col(5) * (HEAD_W ** -0.5)
    vd = vrow(2)
    li = sca(4) + sca(6)
    lf = _log_sigmoid(sca(5) + sca(7))
    m0 = sca(8)
    cs = sc_ref[...]
    n0 = col(7)
    inter = lf + m0
    m_t = jnp.maximum(inter, li)
    w_inter = jnp.exp(inter - m_t)
    qkd = jnp.sum(qd * kd, axis=1, keepdims=True) * jnp.exp(li - m_t)
    num = w_inter * jnp.sum(qd * cs, axis=1, keepdims=True) + qkd * vd
    den = w_inter * jnp.sum(qd * n0, axis=1, keepdims=True) + qkd
    o_ref[:, 2:3, :] = num / jnp.maximum(jnp.abs(den), jnp.exp(-m_t))
    w_end = jnp.exp(li - m_t)
    d0 = jnp.exp(lf + m0 - m_t)
    sc_out[...] = d0 * cs + (w_end * kd) * vd
    sn_out[...] = d0 * n0 + w_end * kd
    sm_out[...] = m_t


def _rec_decode(p, gq, gk, gv, a_log, dt_bias, lb, i_bias, f_bias, s_gdn, s_hgrn, s_c, s_n, s_m):
    n_b = p.shape[0]
    rows = n_b * N_HEADS
    rb = min(16, rows)
    rw = lambda z: z.astype(f32).reshape(rows, HEAD_W)
    blockp = lambda b: rw(p[:, b * GROUP_W:(b + 1) * GROUP_W])
    per_head = lambda z: jnp.tile(z.astype(f32), n_b)
    gate = lambda c: p[:, GATE_COL + c:GATE_COL + c + N_HEADS].reshape(rows)
    lb_rows = jnp.tile(lb.astype(f32).reshape(N_HEADS, HEAD_W), (n_b, 1))
    cols = jnp.stack([rw(gq), rw(gk), blockp(7), blockp(8), blockp(11), blockp(12), lb_rows, rw(s_n)], axis=-1)
    vrows = jnp.stack([rw(gv), blockp(9), blockp(13)], axis=1)
    scal = jnp.stack([gate(0), gate(4), per_head(a_log), per_head(dt_bias), gate(8), gate(12),
                      per_head(i_bias), per_head(f_bias), s_m.astype(f32).reshape(rows)], axis=-1).reshape(rows, 1, 9)
    st = lambda z: z.astype(f32).reshape(rows, HEAD_W, HEAD_W)
    args = [cols, vrows, scal, st(s_gdn), st(s_hgrn), st(s_c)]

    def spec(a):
        return pl.BlockSpec((rb,) + a.shape[1:], lambda i: (i, 0, 0))

    o_st = jax.ShapeDtypeStruct((rows, HEAD_W, HEAD_W), f32)
    outs = [jax.ShapeDtypeStruct((rows, 3, HEAD_W), f32), o_st, o_st, o_st,
            jax.ShapeDtypeStruct((rows, HEAD_W, 1), f32), jax.ShapeDtypeStruct((rows, 1, 1), f32)]
    o, sg, sh, sc, sn, sm = pl.pallas_call(
        _rec_decode_kernel,
        grid=(rows // rb,),
        in_specs=[spec(a) for a in args],
        out_specs=[spec(a) for a in outs],
        out_shape=outs,
        compiler_params=_cparams("arbitrary"),
        name="rec_decode",
    )(*args)
    s4 = lambda z: z.reshape(n_b, N_HEADS, HEAD_W, HEAD_W)
    o2 = lambda k: o[:, k, :].reshape(n_b, GROUP_W)
    return (o2(0), s4(sg), o2(1), s4(sh), o2(2), s4(sc),
            sn.reshape(n_b, N_HEADS, HEAD_W), sm.reshape(n_b, N_HEADS))


def _permute_w_in(w):
    d_in = w.shape[1]
    a_gate0 = 3 * GROUP_W
    d_gate0 = d_in - GROUP_W - 8
    main = jnp.concatenate([w[:, 0:a_gate0], w[:, a_gate0 + 8:d_gate0], w[:, d_gate0 + 8:]], axis=1)
    gates = jnp.concatenate([w[:, a_gate0:a_gate0 + 8], w[:, d_gate0:d_gate0 + 8]], axis=1)
    pad = jnp.zeros((w.shape[0], P_COLS - main.shape[1] - 16), w.dtype)
    return jnp.concatenate([main, gates, pad], axis=1).astype(bf16), gates.T.astype(bf16)


def kernel(x_prompt, x_sample, page_table, cache_k, cache_v, state_gdn_conv, state_gdn, state_hgrn, state_mlstm_C, state_mlstm_n, state_mlstm_m, attn_norm_g, w_in, gdn_conv_w, gdn_a_log, gdn_dt_bias, gdn_norm_g, diff_qk_norm_g, diff_lambda, diff_subln_g, rel_bias, hgrn_lb_logits, hgrn_norm_g, mlstm_i_bias, mlstm_f_bias, mlstm_norm_g, w_out, ffn_norm_g, ffn_w_gate, ffn_w_up, ffn_w_down, moe_router, moe_w_gate, moe_w_up, moe_w_down):
    depth = w_in.shape[0]
    n_bp, seq, _ = x_prompt.shape
    n_bs = x_sample.shape[0]
    n_pool, page = cache_k.shape[1], cache_k.shape[2]
    dt = x_prompt.dtype

    lb_p = jax.nn.softmax(hgrn_lb_logits.astype(f32), axis=0)
    lb_cum = jnp.cumsum(lb_p, axis=0)
    hgrn_lb = lb_cum - lb_cum[0:1]
    cache_k4 = jnp.transpose(cache_k, (0, 1, 3, 4, 5, 2)).reshape(depth, n_pool, GROUP_W, page)
    cache_v4 = jnp.transpose(cache_v, (0, 1, 3, 4, 2)).reshape(depth, n_pool, GROUP_W, page)

    xp = x_prompt.reshape(n_bp * seq, D_MODEL)
    xs = x_sample.reshape(n_bs, D_MODEL)
    outs_p, outs_s = [], []
    for l in range(depth):
        w_perm, w_gate_t = _permute_w_in(w_in[l])
        w_out_b = w_out[l].astype(bf16)
        gains = jnp.stack([jnp.tile(g.astype(f32), N_HEADS) for g in
                           (gdn_norm_g[l], diff_subln_g[l], hgrn_norm_g[l], mlstm_norm_g[l])])
        lam_init = 0.8 - 0.6 * math.exp(-0.3 * l)
        lam32 = diff_lambda[l].astype(f32)
        lam = jnp.exp(jnp.sum(lam32[0] * lam32[1])) - jnp.exp(jnp.sum(lam32[2] * lam32[3])) + lam_init
        if l % 2 == 0:
            ffn_w = (ffn_w_gate[l // 2].astype(bf16), ffn_w_up[l // 2].astype(bf16), ffn_w_down[l // 2].astype(bf16))
        else:
            router_pad = jnp.pad(moe_router[l // 2].astype(bf16), ((0, 0), (0, 128 - N_EXPERTS)))
            moe_w = (moe_w_gate[l // 2].astype(bf16), moe_w_up[l // 2].astype(bf16), moe_w_down[l // 2].astype(bf16))

        def channel_mix(x):
            if l % 2 == 0:
                return _ffn(x, ffn_norm_g[l], *ffn_w)
            return _moe(x, ffn_norm_g[l], router_pad, *moe_w)

        p, gt = _inproj(xp, attn_norm_g[l], w_perm, w_gate_t)
        qnt, kn, vt = _bprep(p, diff_qk_norm_g[l], True)
        ob = _attn_prompt(qnt, kn, vt, lam, rel_bias, n_bp, seq)
        oa, s_gdn = _gdn_prompt(p, gt, gdn_conv_w[l], gdn_a_log[l], gdn_dt_bias[l], n_bp, seq)
        oc, s_hgrn = _hgrn_prompt(p, hgrn_lb[l], n_bp, seq)
        od, s_c, s_n, s_m = _mlstm_prompt(p, gt, mlstm_i_bias[l], mlstm_f_bias[l], n_bp, seq)
        xp = _outproj(oa, ob, oc, od, p, xp, gains, w_out_b, 1.0 - lam_init)
        xp = channel_mix(xp)
        p3 = p.reshape(n_bp, seq, P_COLS)
        outs_p.append((
            kn.reshape(n_bp, seq, N_HEADS, 2, DKB).astype(dt),
            p3[:, :, 6 * GROUP_W:7 * GROUP_W].reshape(n_bp, seq, N_HEADS, HEAD_W).astype(dt),
            p3[:, seq - (CONV_W - 1):, 0:3 * GROUP_W].astype(dt),
            s_gdn.astype(dt), s_hgrn.astype(dt), s_c.astype(dt), s_n.astype(dt), s_m.astype(dt)))

        p, gt = _inproj(xs, attn_norm_g[l], w_perm, w_gate_t)
        qn, kn = _bprep(p, diff_qk_norm_g[l], False)
        vn = p[:, 6 * GROUP_W:7 * GROUP_W]
        ob = _attn_decode(qn, kn, vn, page_table, cache_k4, cache_v4, l, lam, rel_bias)
        u = p[:, 0:3 * GROUP_W]
        gq, gk, gv = _gdn_dec_prep(u, state_gdn_conv[l], gdn_conv_w[l])
        oa, s_gdn, oc, s_hgrn, od, s_c, s_n, s_m = _rec_decode(
            p, gq, gk, gv, gdn_a_log[l], gdn_dt_bias[l], hgrn_lb[l], mlstm_i_bias[l], mlstm_f_bias[l],
            state_gdn[l], state_hgrn[l], state_mlstm_C[l], state_mlstm_n[l], state_mlstm_m[l])
        xs = _outproj(oa, ob, oc, od, p, xs, gains, w_out_b, 1.0 - lam_init)
        xs = channel_mix(xs)
        conv_new = jnp.concatenate([state_gdn_conv[l][:, 1:].astype(dt), u[:, None, :].astype(dt)], axis=1)
        outs_s.append((
            kn.reshape(n_bs, 1, N_HEADS, 2, DKB).astype(dt),
            vn.reshape(n_bs, 1, N_HEADS, HEAD_W).astype(dt),
            conv_new, s_gdn.astype(dt), s_hgrn.astype(dt), s_c.astype(dt), s_n.astype(dt), s_m.astype(dt)))

    kp, vp, convp, gdnp, hgrnp, mcp, mnp_, mmp = [jnp.stack(z) for z in zip(*outs_p)]
    ks_, vs_, convs, gdns, hgrns, mcs, mns, mms = [jnp.stack(z) for z in zip(*outs_s)]
    return (xp.reshape(n_bp, seq, D_MODEL), xs.reshape(n_bs, 1, D_MODEL), kp, vp, ks_, vs_, convp, convs,
            gdnp, gdns, hgrnp, hgrns, mcp, mcs, mnp_, mns, mmp, mms)
```

```python
import functools
import math

import numpy as np
import jax
import jax.numpy as jnp
from jax import lax
from jax.experimental import pallas as pl
from jax.experimental.pallas import tpu as pltpu

f32 = jnp.float32
bf16 = jnp.bfloat16

D_MODEL = 1024
N_HEADS = 4
HEAD_W = 64
GROUP_W = N_HEADS * HEAD_W
DKB = 32
CONV_W = 4
CHUNK = 64
SUB = 16
NUM_BUCKETS = 32
MAX_DISTANCE = 128
N_EXPERTS = 8
EPS = 1e-6
NEG = -1e30
P_COLS = 4096
GATE_COL = 3840
VMEM_LIMIT = 56 * 1024 * 1024

NN = ((1,), (0,))
NT = ((1,), (1,))
TN = ((0,), (0,))


def _dg(a, b, dims=NN):
    return lax.dot_general(a, b, (dims, ((), ())), preferred_element_type=f32)


def _split(a):
    hi = a.astype(bf16)
    lo = (a - hi.astype(f32)).astype(bf16)
    return hi, lo


def _mm3(a, b, dims=NN):
    ah, al = _split(a)
    bh, bl = _split(b)
    return _dg(ah, bh, dims) + (_dg(ah, bl, dims) + _dg(al, bh, dims))


def _mm2(a, b01, dims=NN):
    ah, al = _split(a)
    return _dg(ah, b01, dims) + _dg(al, b01, dims)


def _mm2l(a01, b, dims=NN):
    bh, bl = _split(b)
    return _dg(a01, bh, dims) + _dg(a01, bl, dims)


def _mm1(a, b, dims=NN):
    return _dg(a.astype(bf16), b.astype(bf16), dims)


def _iota(shape, dim):
    return lax.broadcasted_iota(jnp.int32, shape, dim)


def _group_ones(width, group):
    r = _iota((width, width), 0) // group
    c = _iota((width, width), 1) // group
    return (r == c).astype(bf16)


def _group_sum(x, group):
    ones = _group_ones(x.shape[-1], group)
    hi = x.astype(bf16)
    r1 = x - hi.astype(f32)
    mid = r1.astype(bf16)
    lo = (r1 - mid.astype(f32)).astype(bf16)
    return _dg(hi, ones) + (_dg(mid, ones) + _dg(lo, ones))


def _recip(x):
    r = 1.0 / x
    return r * (2.0 - x * r)


def _silu(x):
    return x * jax.nn.sigmoid(x)


def _softplus(x):
    return jnp.maximum(x, 0.0) + jnp.log1p(jnp.exp(-jnp.abs(x)))


def _stack_cols(xc, n=N_HEADS, rows=HEAD_W):
    return jnp.concatenate([xc[:, h:h + 1] for h in range(n)], axis=0)


def _expand_cols(xc, n=N_HEADS, width=HEAD_W):
    r = xc.shape[0]
    return jnp.concatenate([jnp.broadcast_to(xc[:, h:h + 1], (r, width)) for h in range(n)], axis=1)


def _cat_rows(xr, lo, n=N_HEADS, width=HEAD_W):
    return jnp.concatenate([xr[h:h + 1, lo:lo + width] for h in range(n)], axis=1)


def _head_stack(x, n=N_HEADS, width=HEAD_W):
    lane_head = _iota(x.shape, 1) // width
    return jnp.concatenate([jnp.where(lane_head == h, x, 0.0) for h in range(n)], axis=0)


def _fold_heads(x_sm, n=N_HEADS):
    r = x_sm.shape[0] // n
    out = x_sm[0:r]
    for h in range(1, n):
        out = out + x_sm[h * r:(h + 1) * r]
    return out


def _bd_masks(n=GROUP_W, blk=CHUNK):
    r = _iota((n, n), 0)
    c = _iota((n, n), 1)
    same = (r // blk) == (c // blk)
    lower = same & ((r % blk) >= (c % blk))
    strict = same & ((r % blk) > (c % blk))
    return same, lower, strict


def _cparams(*sem):
    return pltpu.CompilerParams(dimension_semantics=sem, vmem_limit_bytes=VMEM_LIMIT)


def _inproj_kernel(x_ref, g_ref, w_ref, wgt_ref, p_ref, gt_ref, h_scr):
    @pl.when(pl.program_id(1) == 0)
    def _():
        x = x_ref[...]
        ms = jnp.mean(x * x, axis=-1, keepdims=True)
        h = ((x * lax.rsqrt(ms + EPS)) * g_ref[...]).astype(bf16)
        h_scr[...] = h
        gt_ref[...] = _dg(wgt_ref[...], h, NT)
    p_ref[...] = _dg(h_scr[...], w_ref[...], NN)


def _inproj(x, g, w_perm, w_gate_t):
    t = x.shape[0]
    tm = min(1024, t)
    tn = 1024
    return pl.pallas_call(
        _inproj_kernel,
        grid=(t // tm, P_COLS // tn),
        in_specs=[
            pl.BlockSpec((tm, D_MODEL), lambda i, j: (i, 0)),
            pl.BlockSpec((1, D_MODEL), lambda i, j: (0, 0)),
            pl.BlockSpec((D_MODEL, tn), lambda i, j: (0, j)),
            pl.BlockSpec((16, D_MODEL), lambda i, j: (0, 0)),
        ],
        out_specs=[
            pl.BlockSpec((tm, tn), lambda i, j: (i, j)),
            pl.BlockSpec((16, tm), lambda i, j: (0, i)),
        ],
        out_shape=[jax.ShapeDtypeStruct((t, P_COLS), f32), jax.ShapeDtypeStruct((16, t), f32)],
        scratch_shapes=[pltpu.VMEM((tm, D_MODEL), bf16)],
        compiler_params=_cparams("arbitrary", "arbitrary"),
        name="inproj",
    )(x, g.reshape(1, D_MODEL), w_perm, w_gate_t)


def _qk_gnorm(x, g):
    ms = _group_sum(x * x, DKB) * (1.0 / DKB)
    return (x * lax.rsqrt(ms + EPS)) * g


def _bprep_kernel(q_ref, k_ref, gq_ref, gk_ref, qn_ref, kn_ref):
    qn_ref[...] = _qk_gnorm(q_ref[...], gq_ref[...])
    kn_ref[...] = _qk_gnorm(k_ref[...], gk_ref[...])


def _bprep_t_kernel(q_ref, k_ref, v_ref, gq_ref, gk_ref, qnt_ref, kn_ref, vt_ref):
    qnt_ref[...] = _qk_gnorm(q_ref[...], gq_ref[...]).T
    kn_ref[...] = _qk_gnorm(k_ref[...], gk_ref[...])
    vt_ref[...] = v_ref[...].T


def _bprep(p, qk_norm_g, transposed):
    t = p.shape[0]
    tm = min(512, t)
    gq = jnp.tile(qk_norm_g[0], GROUP_W // DKB).reshape(1, GROUP_W)
    gk = jnp.tile(qk_norm_g[1], GROUP_W // DKB).reshape(1, GROUP_W)
    col = lambda c: pl.BlockSpec((tm, GROUP_W), lambda i: (i, c))
    gain = pl.BlockSpec((1, GROUP_W), lambda i: (0, 0))
    rows = pl.BlockSpec((tm, GROUP_W), lambda i: (i, 0))
    rows_t = pl.BlockSpec((GROUP_W, tm), lambda i: (0, i))
    if transposed:
        return pl.pallas_call(
            _bprep_t_kernel,
            grid=(t // tm,),
            in_specs=[col(4), col(5), col(6), gain, gain],
            out_specs=[rows_t, rows, rows_t],
            out_shape=[jax.ShapeDtypeStruct((GROUP_W, t), f32), jax.ShapeDtypeStruct((t, GROUP_W), f32),
                       jax.ShapeDtypeStruct((GROUP_W, t), f32)],
            compiler_params=_cparams("arbitrary"),
            name="bprep_t",
        )(p, p, p, gq, gk)
    return pl.pallas_call(
        _bprep_kernel,
        grid=(t // tm,),
        in_specs=[col(4), col(5), gain, gain],
        out_specs=[rows, rows],
        out_shape=[jax.ShapeDtypeStruct((t, GROUP_W), f32)] * 2,
        compiler_params=_cparams("arbitrary"),
        name="bprep",
    )(p, p, gq, gk)


def _outproj_kernel(oa_ref, ob_ref, oc_ref, od_ref, ag_ref, cg_ref, dg_ref, x_ref, g_ref, w_ref, y_ref, *, b_scale):
    def gnorm(x, g):
        ms = _group_sum(x * x, HEAD_W) * (1.0 / HEAD_W)
        return (x * lax.rsqrt(ms + EPS)) * g
    g = g_ref[...]
    mixes = (
        gnorm(oa_ref[...], g[0:1]) * _silu(ag_ref[...]),
        gnorm(ob_ref[...], g[1:2]) * b_scale,
        gnorm(oc_ref[...], g[2:3]) * jax.nn.sigmoid(cg_ref[...]),
        gnorm(od_ref[...], g[3:4]) * jax.nn.sigmoid(dg_ref[...]),
    )
    y = x_ref[...]
    for i, m in enumerate(mixes):
        y = y + _dg(m.astype(bf16), w_ref[i * GROUP_W:(i + 1) * GROUP_W, :], NN)
    y_ref[...] = y


def _outproj(oa, ob, oc, od, p, x, gains, w_out, b_scale):
    t = x.shape[0]
    tm = min(512, t)
    row = lambda i: (i, 0)
    return pl.pallas_call(
        functools.partial(_outproj_kernel, b_scale=b_scale),
        grid=(t // tm,),
        in_specs=[
            pl.BlockSpec((tm, GROUP_W), row), pl.BlockSpec((tm, GROUP_W), row),
            pl.BlockSpec((tm, GROUP_W), row), pl.BlockSpec((tm, GROUP_W), row),
            pl.BlockSpec((tm, GROUP_W), lambda i: (i, 3)),
            pl.BlockSpec((tm, GROUP_W), lambda i: (i, 10)),
            pl.BlockSpec((tm, GROUP_W), lambda i: (i, 14)),
            pl.BlockSpec((tm, D_MODEL), row),
            pl.BlockSpec((4, GROUP_W), lambda i: (0, 0)),
            pl.BlockSpec((D_MODEL, D_MODEL), lambda i: (0, 0)),
        ],
        out_specs=pl.BlockSpec((tm, D_MODEL), row),
        out_shape=jax.ShapeDtypeStruct((t, D_MODEL), f32),
        compiler_params=_cparams("arbitrary"),
        name="outproj",
    )(oa, ob, oc, od, p, p, p, x, gains, w_out)


def _ffn_kernel(x_ref, g_ref, wg_ref, wu_ref, wd_ref, y_ref, h_scr):
    @pl.when(pl.program_id(1) == 0)
    def _():
        x = x_ref[...]
        ms = jnp.mean(x * x, axis=-1, keepdims=True)
        h_scr[...] = ((x * lax.rsqrt(ms + EPS)) * g_ref[...]).astype(bf16)
        y_ref[...] = x

    h = h_scr[...]
    a = _silu(_dg(h, wg_ref[...])) * _dg(h, wu_ref[...])
    y_ref[...] += _dg(a.astype(bf16), wd_ref[...])


def _ffn(x, g, wg, wu, wd):
    t = x.shape[0]
    d_ff = wg.shape[1]
    tm = min(1024, t)
    tf = d_ff // 2
    return pl.pallas_call(
        _ffn_kernel,
        grid=(t // tm, d_ff // tf),
        in_specs=[
            pl.BlockSpec((tm, D_MODEL), lambda i, f: (i, 0)),
            pl.BlockSpec((1, D_MODEL), lambda i, f: (0, 0)),
            pl.BlockSpec((D_MODEL, tf), lambda i, f: (0, f)),
            pl.BlockSpec((D_MODEL, tf), lambda i, f: (0, f)),
            pl.BlockSpec((tf, D_MODEL), lambda i, f: (f, 0)),
        ],
        out_specs=pl.BlockSpec((tm, D_MODEL), lambda i, f: (i, 0)),
        out_shape=jax.ShapeDtypeStruct((t, D_MODEL), f32),
        scratch_shapes=[pltpu.VMEM((tm, D_MODEL), bf16)],
        compiler_params=_cparams("arbitrary", "arbitrary"),
        name="ffn",
    )(x, g.reshape(1, D_MODEL), wg, wu, wd)


LANE = 128
SUBL = D_MODEL // LANE
ROW_TILE = 256
SPARSE_MIN_TOKENS = 4096


def _tile_rows(x):
    return [x[:, j * LANE:(j + 1) * LANE] for j in range(SUBL)]


def _router_kernel(x_ref, g_ref, r_ref, h3_ref, meta_ref, cnt_ref, carry_scr):
    @pl.when(pl.program_id(0) == 0)
    def _():
        carry_scr[...] = jnp.zeros(carry_scr.shape, f32)

    x = x_ref[...]
    tm = x.shape[0]
    ms = jnp.mean(x * x, axis=-1, keepdims=True)
    h = (x * lax.rsqrt(ms + EPS)) * g_ref[...]
    for j, blk in enumerate(_tile_rows(h)):
        h3_ref[:, j, :] = blk
    logits = _dg(h.astype(bf16), r_ref[...])
    lane = _iota(logits.shape, 1)
    logits = jnp.where(lane < N_EXPERTS, logits, -jnp.inf)
    v1 = jnp.max(logits, axis=-1, keepdims=True)
    i1 = jnp.min(jnp.where(logits == v1, lane, LANE), axis=-1, keepdims=True)
    rest = jnp.where(lane == i1, -jnp.inf, logits)
    v2 = jnp.max(rest, axis=-1, keepdims=True)
    i2 = jnp.min(jnp.where(rest == v2, lane, LANE), axis=-1, keepdims=True)
    e2 = jnp.exp(v2 - v1)
    den = 1.0 + e2
    hit = ((lane == i1) | (lane == i2)).astype(f32)
    strict = (_iota((tm, tm), 0) > _iota((tm, tm), 1)).astype(bf16)
    before = _dg(strict, hit.astype(bf16)) + carry_scr[...]
    pos1 = jnp.sum(jnp.where(lane == i1, before, 0.0), axis=-1, keepdims=True)
    pos2 = jnp.sum(jnp.where(lane == i2, before, 0.0), axis=-1, keepdims=True)
    carry_scr[...] += jnp.sum(hit, axis=0, keepdims=True)
    meta = jnp.zeros(logits.shape, f32)
    for c, val in enumerate((i1.astype(f32), i2.astype(f32), 1.0 / den, e2 / den, pos1, pos2)):
        meta = jnp.where(lane == c, val, meta)
    meta_ref[...] = meta
    cnt_ref[...] = carry_scr[...]


def _router(x, g, router_pad):
    t = x.shape[0]
    tm = min(512, t)
    return pl.pallas_call(
        _router_kernel,
        grid=(t // tm,),
        in_specs=[
            pl.BlockSpec((tm, D_MODEL), lambda i: (i, 0)),
            pl.BlockSpec((1, D_MODEL), lambda i: (0, 0)),
            pl.BlockSpec((D_MODEL, LANE), lambda i: (0, 0)),
        ],
        out_specs=[pl.BlockSpec((tm, SUBL, LANE), lambda i: (i, 0, 0)),
                   pl.BlockSpec((tm, LANE), lambda i: (i, 0)),
                   pl.BlockSpec((1, LANE), lambda i: (0, 0))],
        out_shape=[jax.ShapeDtypeStruct((t, SUBL, LANE), f32), jax.ShapeDtypeStruct((t, LANE), f32),
                   jax.ShapeDtypeStruct((1, LANE), f32)],
        scratch_shapes=[pltpu.VMEM((1, LANE), f32)],
        compiler_params=_cparams("arbitrary"),
        name="router",
    )(x, g.reshape(1, D_MODEL), router_pad)


def _swiglu_bf16(x, wg, wu, wd):
    a = _silu(_dg(x, wg)) * _dg(x, wu)
    return _dg(a.astype(bf16), wd)


def _moe_dense_kernel(x_ref, h3_ref, meta_ref, wg_ref, wu_ref, wd_ref, y_ref, acc_scr):
    e = pl.program_id(1)
    f = pl.program_id(2)

    @pl.when((e == 0) & (f == 0))
    def _():
        acc_scr[...] = x_ref[...]

    meta = meta_ref[...]
    ef = e.astype(f32)
    cw = jnp.where(meta[:, 0:1] == ef, meta[:, 2:3], 0.0) + jnp.where(meta[:, 1:2] == ef, meta[:, 3:4], 0.0)
    h = jnp.concatenate([h3_ref[:, j, :] for j in range(SUBL)], axis=1).astype(bf16)
    acc_scr[...] += cw * _swiglu_bf16(h, wg_ref[...], wu_ref[...], wd_ref[...])

    @pl.when((e == pl.num_programs(1) - 1) & (f == pl.num_programs(2) - 1))
    def _():
        y_ref[...] = acc_scr[...]


def _moe_dense(x, h3, meta, wg, wu, wd):
    t = x.shape[0]
    n_e, _, d_ff = wg.shape
    tm = min(512, t)
    tf = d_ff // 2
    return pl.pallas_call(
        _moe_dense_kernel,
        grid=(t // tm, n_e, d_ff // tf),
        in_specs=[
            pl.BlockSpec((tm, D_MODEL), lambda i, e, f: (i, 0)),
            pl.BlockSpec((tm, SUBL, LANE), lambda i, e, f: (i, 0, 0)),
            pl.BlockSpec((tm, LANE), lambda i, e, f: (i, 0)),
            pl.BlockSpec((None, D_MODEL, tf), lambda i, e, f: (e, 0, f)),
            pl.BlockSpec((None, D_MODEL, tf), lambda i, e, f: (e, 0, f)),
            pl.BlockSpec((None, tf, D_MODEL), lambda i, e, f: (e, f, 0)),
        ],
        out_specs=pl.BlockSpec((tm, D_MODEL), lambda i, e, f: (i, 0)),
        out_shape=jax.ShapeDtypeStruct((t, D_MODEL), f32),
        scratch_shapes=[pltpu.VMEM((tm, D_MODEL), f32)],
        compiler_params=_cparams("arbitrary", "arbitrary", "arbitrary"),
        name="moe_dense",
    )(x, h3, meta, wg, wu, wd)


def _route_plan(meta, counts, t):
    cnt = counts[0, :N_EXPERTS].astype(jnp.int32)
    padded = ((cnt + ROW_TILE - 1) // ROW_TILE) * ROW_TILE
    ends = jnp.cumsum(padded)
    offs = ends - padded
    experts = jnp.arange(N_EXPERTS, dtype=jnp.int32)

    def dest(expert_col, rank_col):
        e = meta[:, expert_col].astype(jnp.int32)
        off = jnp.sum(jnp.where(e[:, None] == experts[None, :], offs[None, :], 0), axis=1)
        return off + meta[:, rank_col].astype(jnp.int32)

    n_rows = 2 * t + N_EXPERTS * ROW_TILE
    starts = jnp.arange(n_rows // ROW_TILE, dtype=jnp.int32) * ROW_TILE
    tile_expert = jnp.minimum(jnp.sum((starts[:, None] >= ends[None, :]).astype(jnp.int32), axis=1), N_EXPERTS - 1)
    n_used = (ends[N_EXPERTS - 1] // ROW_TILE).reshape(1)
    return dest(0, 4), dest(1, 5), tile_expert, n_used, n_rows


def _dispatch_kernel(d1_ref, d2_ref, h3_ref, zero_hbm, xs_hbm, sem, *, tm):
    del zero_hbm
    base = pl.program_id(0) * tm

    def issue(k, carry):
        src = h3_ref.at[pl.ds(k, 1)]
        pltpu.make_async_copy(src, xs_hbm.at[pl.ds(d1_ref[base + k], 1)], sem).start(priority=0)
        pltpu.make_async_copy(src, xs_hbm.at[pl.ds(d2_ref[base + k], 1)], sem).start(priority=1)
        return carry

    lax.fori_loop(0, tm, issue, 0, unroll=4)
    for _ in range(2):
        pltpu.make_async_copy(h3_ref, xs_hbm.at[pl.ds(0, tm)], sem).wait()


def _dispatch(h3, dest1, dest2, n_rows):
    t = h3.shape[0]
    tm = min(512, t)
    grid_spec = pltpu.PrefetchScalarGridSpec(
        num_scalar_prefetch=2,
        grid=(t // tm,),
        in_specs=[pl.BlockSpec((tm, SUBL, LANE), lambda i, d1, d2: (i, 0, 0)), pl.BlockSpec(memory_space=pl.ANY)],
        out_specs=pl.BlockSpec(memory_space=pl.ANY),
        scratch_shapes=[pltpu.SemaphoreType.DMA(())],
    )
    return pl.pallas_call(
        functools.partial(_dispatch_kernel, tm=tm),
        grid_spec=grid_spec,
        out_shape=jax.ShapeDtypeStruct((n_rows, SUBL, LANE), f32),
        input_output_aliases={3: 0},
        compiler_params=_cparams("arbitrary"),
        name="moe_dispatch",
    )(dest1, dest2, h3, jnp.zeros((n_rows, SUBL, LANE), f32))


def _experts_kernel(te_ref, nu_ref, xs_ref, wg_ref, wu_ref, wd_ref, ys_ref):
    del te_ref
    r = pl.program_id(0)

    @pl.when(r < nu_ref[0])
    def _():
        x = jnp.concatenate([xs_ref[:, j, :] for j in range(SUBL)], axis=1).astype(bf16)
        for j, blk in enumerate(_tile_rows(_swiglu_bf16(x, wg_ref[...], wu_ref[...], wd_ref[...]))):
            ys_ref[:, j, :] = blk

    @pl.when(r >= nu_ref[0])
    def _():
        ys_ref[...] = jnp.zeros(ys_ref.shape, f32)


def _experts(xs, tile_expert, n_used, wg, wu, wd):
    n_rows = xs.shape[0]
    d_ff = wg.shape[2]
    rows = pl.BlockSpec((ROW_TILE, SUBL, LANE), lambda r, te, nu: (r, 0, 0))
    w_in = pl.BlockSpec((None, D_MODEL, d_ff), lambda r, te, nu: (te[r], 0, 0), pipeline_mode=pl.Buffered(1))
    w_dn = pl.BlockSpec((None, d_ff, D_MODEL), lambda r, te, nu: (te[r], 0, 0), pipeline_mode=pl.Buffered(1))
    grid_spec = pltpu.PrefetchScalarGridSpec(
        num_scalar_prefetch=2,
        grid=(n_rows // ROW_TILE,),
        in_specs=[rows, w_in, w_in, w_dn],
        out_specs=rows,
    )
    return pl.pallas_call(
        _experts_kernel,
        grid_spec=grid_spec,
        out_shape=jax.ShapeDtypeStruct((n_rows, SUBL, LANE), f32),
        compiler_params=_cparams("arbitrary"),
        name="moe_experts",
    )(tile_expert, n_used, xs, wg, wu, wd)


def _combine_kernel(d1_ref, d2_ref, x_ref, meta_ref, ys_hbm, y_ref, buf, sem, *, tm):
    base = pl.program_id(0) * tm

    def issue(k, carry):
        t = base + k
        pltpu.make_async_copy(ys_hbm.at[pl.ds(d1_ref[t], 1)], buf.at[0, pl.ds(k, 1)], sem).start(priority=0)
        pltpu.make_async_copy(ys_hbm.at[pl.ds(d2_ref[t], 1)], buf.at[1, pl.ds(k, 1)], sem).start(priority=1)
        return carry

    lax.fori_loop(0, tm, issue, 0, unroll=4)
    for s in range(2):
        pltpu.make_async_copy(ys_hbm.at[pl.ds(0, tm)], buf.at[s], sem).wait()
    meta = meta_ref[...]
    g1 = meta[:, 2:3]
    g2 = meta[:, 3:4]
    for j in range(SUBL):
        sl = slice(j * LANE, (j + 1) * LANE)
        y_ref[:, sl] = x_ref[:, sl] + (g1 * buf[0, :, j, :] + g2 * buf[1, :, j, :])


def _combine(x, meta, ys, dest1, dest2):
    t = x.shape[0]
    tm = min(256, t)
    grid_spec = pltpu.PrefetchScalarGridSpec(
        num_scalar_prefetch=2,
        grid=(t // tm,),
        in_specs=[pl.BlockSpec((tm, D_MODEL), lambda i, d1, d2: (i, 0)),
                  pl.BlockSpec((tm, LANE), lambda i, d1, d2: (i, 0)),
                  pl.BlockSpec(memory_space=pl.ANY)],
        out_specs=pl.BlockSpec((tm, D_MODEL), lambda i, d1, d2: (i, 0)),
        scratch_shapes=[pltpu.VMEM((2, tm, SUBL, LANE), f32), pltpu.SemaphoreType.DMA(())],
    )
    return pl.pallas_call(
        functools.partial(_combine_kernel, tm=tm),
        grid_spec=grid_spec,
        out_shape=jax.ShapeDtypeStruct((t, D_MODEL), f32),
        compiler_params=_cparams("arbitrary"),
        name="moe_combine",
    )(dest1, dest2, x, meta, ys)


def _moe(x, g, router_pad, wg, wu, wd):
    t = x.shape[0]
    h3, meta, counts = _router(x, g, router_pad)
    if t < SPARSE_MIN_TOKENS:
        return _moe_dense(x, h3, meta, wg, wu, wd)
    dest1, dest2, tile_expert, n_used, n_rows = _route_plan(meta, counts, t)
    xs = _dispatch(h3, dest1, dest2, n_rows)
    ys = _experts(xs, tile_expert, n_used, wg, wu, wd)
    return _combine(x, meta, ys, dest1, dest2)


def _t5_bucket_np(n):
    n = np.maximum(n, 0)
    max_exact = NUM_BUCKETS // 2
    nf = np.maximum(n, 1).astype(np.float32)
    large = max_exact + (np.log(nf / np.float32(max_exact)) / np.float32(math.log(MAX_DISTANCE / max_exact))
                         * np.float32(NUM_BUCKETS - max_exact)).astype(np.int32)
    return np.where(n < max_exact, n, np.minimum(large, NUM_BUCKETS - 1))


def _shifted_bias(rel_bias):
    rb = rel_bias.astype(f32)
    return rb - rb[NUM_BUCKETS - 1:NUM_BUCKETS]


ACC_ROWS = HEAD_W + 8
LOG2E = math.log2(math.e)


def _attn_kernel(qi_ref, kj_ref, lam_ref, qt_ref, k_ref, vt_ref, toep_ref, o_ref, qs_scr, m_scr, acc_scr, *, tq):
    p = pl.program_id(1)
    i = qi_ref[p]
    j = kj_ref[p]
    n_hc = 2 * N_HEADS
    c2 = (DKB ** -0.5) * LOG2E

    @pl.when(j == 0)
    def _():
        qt = qt_ref[...] * c2
        row_grp = _iota(qt.shape, 0) // DKB
        for hc in range(n_hc):
            qs_scr[:, hc * tq:(hc + 1) * tq] = jnp.where(row_grp == hc, qt, 0.0).astype(bf16)
        m_scr[...] = jnp.full(m_scr.shape, NEG, f32)
        acc_scr[...] = jnp.zeros(acc_scr.shape, f32)

    def step(near):
        tk = k_ref.shape[0]
        st_all = _dg(k_ref[...].astype(bf16), qs_scr[...], NN)
        vt = vt_ref[...]
        ones = jnp.ones((ACC_ROWS - HEAD_W, tk), f32)
        for h in range(N_HEADS):
            vh = jnp.concatenate([vt[h * HEAD_W:(h + 1) * HEAD_W, :], ones], axis=0).astype(bf16)
            for hc in (2 * h, 2 * h + 1):
                s = st_all[:, hc * tq:(hc + 1) * tq]
                if near:
                    s = s + toep_ref[(i - j) * N_HEADS + h]
                m_old = m_scr[hc:hc + 1, :]
                m_new = jnp.maximum(m_old, jnp.max(s, axis=0, keepdims=True))
                pexp = jnp.exp2(s - m_new)
                acc_scr[hc] = jnp.exp2(m_old - m_new) * acc_scr[hc] + _dg(vh, pexp.astype(bf16), NN)
                m_scr[hc:hc + 1, :] = m_new

    @pl.when(i - j <= 1)
    def _():
        step(True)

    @pl.when(i - j > 1)
    def _():
        step(False)

    @pl.when(j == i)
    def _():
        lam = lam_ref[0]
        outs = []
        for h in range(N_HEADS):
            a0 = acc_scr[2 * h]
            a1 = acc_scr[2 * h + 1]
            outs.append(a0[0:HEAD_W] * _recip(a0[HEAD_W:HEAD_W + 1])
                        - lam * (a1[0:HEAD_W] * _recip(a1[HEAD_W:HEAD_W + 1])))
        o_ref[...] = jnp.concatenate(outs, axis=0).T


def _toeplitz_kernel(u_ref, o_ref):
    t = o_ref.shape[0]
    rows = jnp.broadcast_to(u_ref[...], (t, 2 * t))
    o_ref[...] = pltpu.roll(rows, 0, 1, stride=1, stride_axis=0)[:, t:2 * t]


def _toeplitz_bias_tiles(rel_bias, t):
    m = np.arange(2 * t)[None, :]
    dist = m - t + np.array([0, t])[:, None]
    tab = _shifted_bias(rel_bias)
    u = jnp.take(tab, jnp.asarray(_t5_bucket_np(dist)), axis=0)
    u = jnp.where(jnp.asarray(dist >= 0)[:, :, None], u * LOG2E, NEG)
    u = jnp.transpose(u, (0, 2, 1)).reshape(2 * N_HEADS, 1, 2 * t)
    return pl.pallas_call(
        _toeplitz_kernel,
        grid=(2 * N_HEADS,),
        in_specs=[pl.BlockSpec((None, 1, 2 * t), lambda i: (i, 0, 0))],
        out_specs=pl.BlockSpec((None, t, t), lambda i: (i, 0, 0)),
        out_shape=jax.ShapeDtypeStruct((2 * N_HEADS, t, t), f32),
        compiler_params=_cparams("arbitrary"),
        name="toeplitz_bias",
    )(u)


def _attn_prompt(qnt, kn, vt, lam, rel_bias, n_batch, seq):
    tq = min(512, seq)
    nq = seq // tq
    pairs =[(i, j) for i in range(nq) for j in range(i + 1)]
    qi = jnp.asarray(np.array([a for a, _ in pairs], np.int32))
    kj = jnp.asarray(np.array([b for _, b in pairs], np.int32))
    toep = _toeplitz_bias_tiles(rel_bias, tq)
    grid_spec = pltpu.PrefetchScalarGridSpec(
        num_scalar_prefetch=2,
        grid=(n_batch, len(pairs)),
        in_specs=[
            pl.BlockSpec(memory_space=pltpu.SMEM),
            pl.BlockSpec((GROUP_W, tq), lambda b_, p_, qi_, kj_: (0, b_ * nq + qi_[p_])),
            pl.BlockSpec((tq, GROUP_W), lambda b_, p_, qi_, kj_: (b_ * nq + kj_[p_], 0)),
            pl.BlockSpec((GROUP_W, tq), lambda b_, p_, qi_, kj_: (0, b_ * nq + kj_[p_])),
            pl.BlockSpec((2 * N_HEADS, tq, tq), lambda b_, p_, qi_, kj_: (0, 0, 0)),
        ],
        out_specs=pl.BlockSpec((tq, GROUP_W), lambda b_, p_, qi_, kj_: (b_ * nq + qi_[p_], 0)),
        scratch_shapes=[
            pltpu.VMEM((GROUP_W, 2 * N_HEADS * tq), bf16),
            pltpu.VMEM((2 * N_HEADS, tq), f32),
            pltpu.VMEM((2 * N_HEADS, ACC_ROWS, tq), f32),
        ],
    )
    return pl.pallas_call(
        functools.partial(_attn_kernel, tq=tq),
        grid_spec=grid_spec,
        out_shape=jax.ShapeDtypeStruct((n_batch * seq, GROUP_W), f32),
        compiler_params=_cparams("arbitrary", "arbitrary"),
        name="attn_prompt",
    )(qi, kj, lam.reshape(1), qnt, kn, vt, toep)


def _attn_decode_kernel(pt_ref, lam_ref, q_ref, kn_ref, vn_ref, blast_ref, bself_ref, *rest, pg, n_pages):
    k_refs = rest[:pg]
    v_refs = rest[pg:2 * pg]
    o_ref, qs_scr, s_scr, v_scr = rest[2 * pg:]
    t = pl.program_id(1)
    n_steps = n_pages // pg
    n_hc = 2 * N_HEADS
    page = k_refs[0].shape[1]
    scale = DKB ** -0.5
    rnd = lambda z: z.astype(bf16).astype(f32)

    @pl.when(t == 0)
    def _():
        q = jnp.broadcast_to(q_ref[...], (n_hc, GROUP_W))
        keep = (_iota(q.shape, 1) // DKB) == _iota(q.shape, 0)
        qs_scr[...] = jnp.where(keep, q, 0.0)

    qs_b = qs_scr[...].astype(bf16)
    parts = []
    for g in range(pg):
        s = _dg(qs_b, k_refs[g][...].astype(bf16), NN) * scale
        is_last = (t * pg + g) == (n_pages - 1)
        parts.append(s + jnp.where(is_last, blast_ref[...], 0.0))
        v_scr[t * pg + g] = v_refs[g][...].astype(bf16)
    s_scr[t] = jnp.concatenate(parts, axis=1)

    @pl.when(t == n_steps - 1)
    def _():
        s_all = s_scr[...]
        s_self = jnp.sum(rnd(qs_scr[...]) * rnd(kn_ref[...]), axis=-1, keepdims=True) * scale + bself_ref[...]
        m = jnp.maximum(jnp.max(jnp.max(s_all, axis=2, keepdims=True), axis=0), s_self)
        p = jnp.exp(s_all - m)
        p_self = jnp.exp(s_self - m)
        l = jnp.sum(jnp.sum(p, axis=2, keepdims=True), axis=0) + p_self
        inv_l = _recip(l)
        pn = p * inv_l
        pn_self = p_self * inv_l
        lam = lam_ref[0]
        rows = [pn[:, 2 * h:2 * h + 1, :] - lam * pn[:, 2 * h + 1:2 * h + 2, :] for h in range(N_HEADS)]
        s_scr[...] = jnp.concatenate(rows + [jnp.zeros_like(rows[0])] * N_HEADS, axis=1)
        rows_self = [pn_self[2 * h:2 * h + 1] - lam * pn_self[2 * h + 1:2 * h + 2] for h in range(N_HEADS)]
        a_self = jnp.concatenate(rows_self + [jnp.zeros_like(rows_self[0])] * N_HEADS, axis=0)

        def weighted_values(t2, acc):
            a = s_scr[t2].astype(bf16)
            for g in range(pg):
                acc = acc + _dg(a[:, g * page:(g + 1) * page], v_scr[t2 * pg + g], NT)
            return acc

        o = lax.fori_loop(0, n_steps, weighted_values, rnd(a_self) * rnd(vn_ref[...]))
        lane_head = _iota((1, GROUP_W), 1) // HEAD_W
        out = jnp.zeros((1, GROUP_W), f32)
        for h in range(N_HEADS):
            out = jnp.where(lane_head == h, o[h:h + 1], out)
        o_ref[...] = out


def _attn_decode(qn, kn, vn, page_table, cache_k, cache_v, layer, lam, rel_bias):
    n_b, n_pages = page_table.shape
    page = cache_k.shape[3]
    pg = min(16, n_pages)
    n_steps = n_pages // pg
    past = n_pages * page
    tab = _shifted_bias(rel_bias)
    d_last = past - ((n_pages - 1) * page + np.arange(page))
    blast = jnp.repeat(jnp.take(tab, jnp.asarray(_t5_bucket_np(d_last)), axis=0).T, 2, axis=0)
    bself = jnp.repeat(tab[0].reshape(N_HEADS, 1), 2, axis=0)

    def page_spec(g):
        return pl.BlockSpec((None, None, GROUP_W, page), lambda b_, t_, pt: (layer, pt[b_, t_ * pg + g], 0, 0))

    row = pl.BlockSpec((None, 1, GROUP_W), lambda b_, t_, pt: (b_, 0, 0))
    grid_spec = pltpu.PrefetchScalarGridSpec(
        num_scalar_prefetch=1,
        grid=(n_b, n_steps),
        in_specs=[pl.BlockSpec(memory_space=pltpu.SMEM), row, row, row,
                  pl.BlockSpec((2 * N_HEADS, page), lambda b_, t_, pt: (0, 0)),
                  pl.BlockSpec((2 * N_HEADS, 1), lambda b_, t_, pt: (0, 0))]
                 + [page_spec(g) for g in range(pg)] * 2,
        out_specs=row,
        scratch_shapes=[
            pltpu.VMEM((2 * N_HEADS, GROUP_W), f32),
            pltpu.VMEM((n_steps, 2 * N_HEADS, pg * page), f32),
            pltpu.VMEM((n_pages, GROUP_W, page), bf16),
        ],
    )
    r3 = lambda z: z.reshape(n_b, 1, GROUP_W)
    out = pl.pallas_call(
        functools.partial(_attn_decode_kernel, pg=pg, n_pages=n_pages),
        grid_spec=grid_spec,
        out_shape=jax.ShapeDtypeStruct((n_b, 1, GROUP_W), f32),
        compiler_params=_cparams("arbitrary", "arbitrary"),
        name="attn_decode",
    )(page_table, lam.reshape(1), r3(qn), r3(kn), r3(vn), blast, bself,
      *([cache_k] * pg), *([cache_v] * pg))
    return out.reshape(n_b, GROUP_W)


def _tri(n, dtype=f32):
    return (_iota((n, n), 0) >= _iota((n, n), 1)).astype(dtype)


def _block_tri_t(n, blk):
    r = _iota((n, n), 0)
    c = _iota((n, n), 1)
    return (((r // blk) == (c // blk)) & (r <= c)).astype(bf16)


def _block_tri(n, blk):
    r = _iota((n, n), 0)
    c = _iota((n, n), 1)
    return (((r // blk) == (c // blk)) & (r >= c)).astype(bf16)


def _gdn_kernel(u_ref, gc_ref, gr_ref, cw_ref, alr_ref, dtr_ref, alc_ref, dtc_ref, o_ref, s_out_ref, ext_scr, s_scr, *, tb):
    i = pl.program_id(1)

    @pl.when(i == 0)
    def _():
        ext_scr[0:8, :] = jnp.zeros((8, 3 * GROUP_W), f32)
        s_scr[...] = jnp.zeros(s_scr.shape, f32)

    ext_scr[8:8 + tb, :] = u_ref[...]
    w = cw_ref[...]
    conv = ext_scr[8:8 + tb, :] * w[3:4]
    for jj in range(1, CONV_W):
        conv = conv + ext_scr[8 - jj:8 - jj + tb, :] * w[3 - jj:4 - jj]
    ext_scr[0:8, :] = ext_scr[tb:tb + 8, :]
    qkv = _silu(conv)

    def l2n(x):
        return x * lax.rsqrt(_group_sum(x * x, HEAD_W) + EPS)

    q = l2n(qkv[:, 0:GROUP_W]) * (HEAD_W ** -0.5)
    k = l2n(qkv[:, GROUP_W:2 * GROUP_W])
    v = qkv[:, 2 * GROUP_W:3 * GROUP_W]

    gc = gc_ref[...]
    g_col = -jnp.exp(alr_ref[...]) * _softplus(gc[:, 0:4] + dtr_ref[...])
    beta_col = jax.nn.sigmoid(gc[:, 4:8])
    gr = gr_ref[...]
    g_row = -jnp.exp(alc_ref[...]) * _softplus(gr[0:4, :] + dtc_ref[...])
    g_row8 = jnp.concatenate([g_row, jnp.zeros_like(g_row)], axis=0)
    gcum_row = _mm2(g_row8, _block_tri_t(tb, CHUNK))

    same, lower, strict = _bd_masks()
    tri = _tri(CHUNK, bf16)
    r = _iota((GROUP_W, GROUP_W), 0)
    c = _iota((GROUP_W, GROUP_W), 1)
    eye = (r == c).astype(f32)

    chunks = range(tb // CHUNK)
    gcums, qks, m_bds = [], [], []
    for ch in chunks:
        lo = ch * CHUNK
        gcum = _mm2l(tri, g_col[lo:lo + CHUNK])
        g_stack = _stack_cols(gcum)
        g_cat = _cat_rows(gcum_row, lo)
        decay = jnp.exp(jnp.where(lower, g_stack - g_cat, NEG))
        ksm = _head_stack(k[lo:lo + CHUNK])
        kk = _mm1(ksm, ksm, NT)
        qks.append(_mm1(_head_stack(q[lo:lo + CHUNK]), ksm, NT) * decay)
        m_bds.append(_stack_cols(beta_col[lo:lo + CHUNK]) * kk * jnp.where(strict, decay, 0.0))
        gcums.append(gcum)

    def sibling(lev):
        return ((r >> (lev + 1)) == (c >> (lev + 1))) & (((r >> lev) & 1) == 1) & (((c >> lev) & 1) == 0)

    xs = [eye - jnp.where(sibling(0), m, 0.0) for m in m_bds]
    for lev in range(1, 6):
        sel = sibling(lev)
        xs = [x - _mm2(_mm2(x, jnp.where(sel, m, 0.0).astype(bf16)), x.astype(bf16)) for x, m in zip(xs, m_bds)]

    for ch in chunks:
        lo = ch * CHUNK
        qc, kc, vc = q[lo:lo + CHUNK], k[lo:lo + CHUNK], v[lo:lo + CHUNK]
        gcum, bcol = gcums[ch], beta_col[lo:lo + CHUNK]
        s_bd = s_scr[...]
        kq_s = _mm1(jnp.concatenate([kc, qc], axis=0), s_bd)
        ks, qs = kq_s[0:CHUNK], kq_s[CHUNK:2 * CHUNK]
        eg_all = _expand_cols(jnp.exp(gcum))
        rhs = _expand_cols(bcol) * (vc - eg_all * ks)
        u_sm = _mm3(xs[ch], _head_stack(rhs))
        o_sm = _mm1(qks[ch], u_sm)
        o_ref[lo:lo + CHUNK, :] = eg_all * qs + _fold_heads(o_sm)
        u_all = _fold_heads(u_sm)
        g_last = gcum[CHUNK - 1:CHUNK, :]
        kw = kc * _expand_cols(jnp.exp(g_last - gcum))
        d_stack = jnp.concatenate(
            [jnp.broadcast_to(jnp.exp(g_last[:, h:h + 1]), (HEAD_W, 1)) for h in range(N_HEADS)], axis=0)
        s_scr[...] = d_stack * s_bd + jnp.where(same, _mm1(kw, u_all, TN), 0.0)

    @pl.when(i == pl.num_programs(1) - 1)
    def _():
        s_out_ref[...] = s_scr[...]


def _gdn_prompt(p, gt, conv_w, a_log, dt_bias, n_batch, seq):
    tb = min(256, seq)
    nb = seq // tb
    r14 = lambda z: z.astype(f32).reshape(1, N_HEADS)
    c41 = lambda z: z.astype(f32).reshape(N_HEADS, 1)
    o, s_bd = pl.pallas_call(
        functools.partial(_gdn_kernel, tb=tb),
        grid=(n_batch, nb),
        in_specs=[
            pl.BlockSpec((tb, 3 * GROUP_W), lambda b, i: (b * nb + i, 0)),
            pl.BlockSpec((tb, 128), lambda b, i: (b * nb + i, GATE_COL // 128)),
            pl.BlockSpec((16, tb), lambda b, i: (0, b * nb + i)),
            pl.BlockSpec((CONV_W, 3 * GROUP_W), lambda b, i: (0, 0)),
            pl.BlockSpec((1, N_HEADS), lambda b, i: (0, 0)),
            pl.BlockSpec((1, N_HEADS), lambda b, i: (0, 0)),
            pl.BlockSpec((N_HEADS, 1), lambda b, i: (0, 0)),
            pl.BlockSpec((N_HEADS, 1), lambda b, i: (0, 0)),
        ],
        out_specs=[
            pl.BlockSpec((tb, GROUP_W), lambda b, i: (b * nb + i, 0)),
            pl.BlockSpec((None, GROUP_W, GROUP_W), lambda b, i: (b, 0, 0)),
        ],
        out_shape=[jax.ShapeDtypeStruct((n_batch * seq, GROUP_W), f32),
                   jax.ShapeDtypeStruct((n_batch, GROUP_W, GROUP_W), f32)],
        scratch_shapes=[pltpu.VMEM((tb + 8, 3 * GROUP_W), f32), pltpu.VMEM((GROUP_W, GROUP_W), f32)],
        compiler_params=_cparams("arbitrary", "arbitrary"),
        name="gdn_prompt",
    )(p, p, gt, conv_w.astype(f32), r14(a_log), r14(dt_bias), c41(a_log), c41(dt_bias))
    return o, _bd_diag(s_bd)


def _bd_diag(s_bd):
    n_b = s_bd.shape[0]
    s5 = s_bd.reshape(n_b, N_HEADS, HEAD_W, N_HEADS, HEAD_W)
    return jnp.stack([s5[:, h, :, h, :] for h in range(N_HEADS)], axis=1)


def _hgrn_kernel(q_ref, f_ref, i_ref, lb_ref, o_ref, s_out_ref, st_scr, q_scr, k_scr, b_scr, *, tb):
    blk = pl.program_id(1)

    @pl.when(blk == 0)
    def _():
        st_scr[...] = jnp.zeros(st_scr.shape, f32)

    lb = lb_ref[...]
    z = f_ref[...]
    logf = jnp.log(lb + (1.0 - lb) * jax.nn.sigmoid(z))
    q_scr[...] = _silu(q_ref[...])
    k_scr[...] = (1.0 - lb) * jax.nn.sigmoid(-z)
    b_scr[...] = _mm2l(_block_tri(tb, SUB), logf)

    same, _, _ = _bd_masks()
    ones_bd = _group_ones(GROUP_W, HEAD_W)
    row = _iota((SUB * SUB, GROUP_W), 0)
    tmask = (row % SUB) >= (row // SUB)

    def rep_t(x):
        return jnp.broadcast_to(x[None], (SUB, SUB, GROUP_W)).reshape(SUB * SUB, GROUP_W)

    def rep_j(x):
        return jnp.broadcast_to(x[:, None, :], (SUB, SUB, GROUP_W)).reshape(SUB * SUB, GROUP_W)

    def body(c, carry):
        r0 = pl.multiple_of(c * SUB, SUB)
        qs = q_scr[pl.ds(r0, SUB), :]
        ks = k_scr[pl.ds(r0, SUB), :]
        vs = i_ref[pl.ds(r0, SUB), :]
        bs = b_scr[pl.ds(r0, SUB), :]
        st = st_scr[...]
        o_inter = _mm1(qs * jnp.exp(bs), st, NT)
        wgt = rep_t(qs) * jnp.exp(jnp.where(tmask, rep_t(bs) - rep_j(bs), NEG)) * rep_j(ks)
        a = _mm2(wgt, ones_bd)
        o_diag = jnp.sum((a * rep_j(vs)).reshape(SUB, SUB, GROUP_W), axis=0)
        o_ref[pl.ds(r0, SUB), :] = o_inter + o_diag
        b_last = bs[SUB - 1:SUB, :]
        kw = ks * jnp.exp(b_last - bs)
        st_scr[...] = st * jnp.exp(b_last) + jnp.where(same, _mm1(vs, kw, TN), 0.0)
        return carry

    lax.fori_loop(0, tb // SUB, body, 0, unroll=4)

    @pl.when(blk == pl.num_programs(1) - 1)
    def _():
        s_out_ref[...] = st_scr[...]


def _hgrn_prompt(p, lb, n_batch, seq):
    tb = min(256, seq)
    nb = seq // tb
    blk = lambda col: pl.BlockSpec((tb, GROUP_W), lambda b, i: (b * nb + i, col))
    o, st = pl.pallas_call(
        functools.partial(_hgrn_kernel, tb=tb),
        grid=(n_batch, nb),
        in_specs=[blk(7), blk(8), blk(9), pl.BlockSpec((1, GROUP_W), lambda b, i: (0, 0))],
        out_specs=[
            pl.BlockSpec((tb, GROUP_W), lambda b, i: (b * nb + i, 0)),
            pl.BlockSpec((None, GROUP_W, GROUP_W), lambda b, i: (b, 0, 0)),
        ],
        out_shape=[jax.ShapeDtypeStruct((n_batch * seq, GROUP_W), f32),
                   jax.ShapeDtypeStruct((n_batch, GROUP_W, GROUP_W), f32)],
        scratch_shapes=[pltpu.VMEM((GROUP_W, GROUP_W), f32)] + [pltpu.VMEM((tb, GROUP_W), f32)] * 3,
        compiler_params=_cparams("arbitrary", "arbitrary"),
        name="hgrn_prompt",
    )(p, p, p, lb.astype(f32).reshape(1, GROUP_W))
    return o, jnp.swapaxes(_bd_diag(st), -1, -2)


def _log_sigmoid(x):
    return jnp.minimum(x, 0.0) - jnp.log1p(jnp.exp(-jnp.abs(x)))


def _mlstm_kernel(q_ref, k_ref, v_ref, gc_ref, gr_ref, ibr_ref, fbr_ref, ibc_ref, fbc_ref,
                  o_ref, c_out_ref, n_out_ref, m_out_ref, c_scr, n_scr, m_scr, *, tb):
    blk = pl.program_id(1)

    @pl.when(blk == 0)
    def _():
        c_scr[...] = jnp.zeros(c_scr.shape, f32)
        n_scr[...] = jnp.zeros(n_scr.shape, f32)
        m_scr[...] = jnp.zeros(m_scr.shape, f32)

    q = q_ref[...]
    k = k_ref[...] * (HEAD_W ** -0.5)
    v = v_ref[...]
    gc = gc_ref[...]
    li_col = gc[:, 8:12] + ibr_ref[...]
    lf_col = _log_sigmoid(gc[:, 12:16] + fbr_ref[...])
    gr = gr_ref[...]
    li_row = gr[8:12, :] + ibc_ref[...]
    lf_row = _log_sigmoid(gr[12:16, :] + fbc_ref[...])
    b_row = _mm2(jnp.concatenate([lf_row, jnp.zeros_like(lf_row)], axis=0), _block_tri_t(tb, CHUNK))

    same, lower, _ = _bd_masks()
    tri = _tri(CHUNK, bf16)

    for ch in range(tb // CHUNK):
        lo = ch * CHUNK
        qc, kc, vc = q[lo:lo + CHUNK], k[lo:lo + CHUNK], v[lo:lo + CHUNK]
        b_col = _mm2l(tri, lf_col[lo:lo + CHUNK])
        b_stack = _stack_cols(b_col)
        d_mat = jnp.where(lower, b_stack - _cat_rows(b_row, lo) + _cat_rows(li_row, lo), NEG)
        m_row = m_scr[...]
        m_stack = jnp.concatenate(
            [jnp.broadcast_to(m_row[:, h:h + 1], (CHUNK, 1)) for h in range(N_HEADS)], axis=0)
        inter = b_stack + m_stack
        m_t = jnp.maximum(inter, jnp.max(d_mat, axis=-1, keepdims=True))
        w_inter = jnp.exp(inter - m_t)
        qsm = _head_stack(qc)
        ksm = _head_stack(kc)
        pmat = _mm1(qsm, ksm, NT) * jnp.exp(d_mat - m_t)
        c_bd = c_scr[...]
        n_row = n_scr[...]
        num = w_inter * _mm1(qsm, c_bd) + _mm1(pmat, _head_stack(vc))
        den = w_inter * jnp.sum(qsm * n_row, axis=-1, keepdims=True) + jnp.sum(pmat, axis=-1, keepdims=True)
        h_sm = num / jnp.maximum(jnp.abs(den), jnp.exp(-m_t))
        o_ref[lo:lo + CHUNK, :] = _fold_heads(h_sm)
        m_new = jnp.concatenate(
            [m_t[h * CHUNK + CHUNK - 1:h * CHUNK + CHUNK, :] for h in range(N_HEADS)], axis=1)
        b_last = b_col[CHUNK - 1:CHUNK, :]
        w_end = jnp.exp(b_last - b_col + li_col[lo:lo + CHUNK] - m_new)
        d0 = jnp.exp(b_last + m_row - m_new)
        kw = kc * _expand_cols(w_end)
        d0_stack = jnp.concatenate(
            [jnp.broadcast_to(d0[:, h:h + 1], (HEAD_W, 1)) for h in range(N_HEADS)], axis=0)
        c_scr[...] = d0_stack * c_bd + jnp.where(same, _mm1(kw, vc, TN), 0.0)
        n_scr[...] = _expand_cols(d0) * n_row + jnp.sum(kw, axis=0, keepdims=True)
        m_scr[...] = m_new

    @pl.when(blk == pl.num_programs(1) - 1)
    def _():
        c_out_ref[...] = c_scr[...]
        n_out_ref[...] = n_scr[...]
        m_out_ref[...] = m_scr[...]


def _mlstm_prompt(p, gt, i_bias, f_bias, n_batch, seq):
    tb = min(256, seq)
    nb = seq // tb
    blk = lambda col: pl.BlockSpec((tb, GROUP_W), lambda b, i: (b * nb + i, col))
    r14 = lambda z: z.astype(f32).reshape(1, N_HEADS)
    c41 = lambda z: z.astype(f32).reshape(N_HEADS, 1)
    small = lambda shape: pl.BlockSpec(shape, lambda b, i: (0, 0))
    o, c_bd, n_row, m_row = pl.pallas_call(
        functools.partial(_mlstm_kernel, tb=tb),
        grid=(n_batch, nb),
        in_specs=[blk(11), blk(12), blk(13),
                  pl.BlockSpec((tb, 128), lambda b, i: (b * nb + i, GATE_COL // 128)),
                  pl.BlockSpec((16, tb), lambda b, i: (0, b * nb + i)),
                  small((1, N_HEADS)), small((1, N_HEADS)), small((N_HEADS, 1)), small((N_HEADS, 1))],
        out_specs=[
            pl.BlockSpec((tb, GROUP_W), lambda b, i: (b * nb + i, 0)),
            pl.BlockSpec((None, GROUP_W, GROUP_W), lambda b, i: (b, 0, 0)),
            pl.BlockSpec((None, 1, GROUP_W), lambda b, i: (b, 0, 0)),
            pl.BlockSpec((None, 1, N_HEADS), lambda b, i: (b, 0, 0)),
        ],
        out_shape=[jax.ShapeDtypeStruct((n_batch * seq, GROUP_W), f32),
                   jax.ShapeDtypeStruct((n_batch, GROUP_W, GROUP_W), f32),
                   jax.ShapeDtypeStruct((n_batch, 1, GROUP_W), f32),
                   jax.ShapeDtypeStruct((n_batch, 1, N_HEADS), f32)],
        scratch_shapes=[pltpu.VMEM((GROUP_W, GROUP_W), f32), pltpu.VMEM((1, GROUP_W), f32),
                        pltpu.VMEM((1, N_HEADS), f32)],
        compiler_params=_cparams("arbitrary", "arbitrary"),
        name="mlstm_prompt",
    )(p, p, p, p, gt, r14(i_bias), r14(f_bias), c41(i_bias), c41(f_bias))
    return (o, _bd_diag(c_bd), n_row.reshape(n_batch, N_HEADS, HEAD_W), m_row.reshape(n_batch, N_HEADS))


def _gdn_dec_prep_kernel(u_ref, buf_ref, cw_ref, q_ref, k_ref, v_ref):
    w = cw_ref[...]
    conv = u_ref[...] * w[3:4]
    for jj in range(CONV_W - 1):
        conv = conv + buf_ref[jj] * w[jj:jj + 1]
    qkv = _silu(conv)

    def l2n(x):
        return x * lax.rsqrt(_group_sum(x * x, HEAD_W) + EPS)

    q_ref[...] = l2n(qkv[:, 0:GROUP_W]) * (HEAD_W ** -0.5)
    k_ref[...] = l2n(qkv[:, GROUP_W:2 * GROUP_W])
    v_ref[...] = qkv[:, 2 * GROUP_W:3 * GROUP_W]


def _gdn_dec_prep(p, conv_buf, conv_w):
    n_b = p.shape[0]
    out = jax.ShapeDtypeStruct((n_b, GROUP_W), f32)
    return pl.pallas_call(
        _gdn_dec_prep_kernel,
        grid=(1,),
        in_specs=[pl.BlockSpec((n_b, 3 * GROUP_W), lambda i: (0, 0)),
                  pl.BlockSpec((CONV_W - 1, n_b, 3 * GROUP_W), lambda i: (0, 0, 0)),
                  pl.BlockSpec((CONV_W, 3 * GROUP_W), lambda i: (0, 0))],
        out_specs=[pl.BlockSpec((n_b, GROUP_W), lambda i: (0, 0))] * 3,
        out_shape=[out, out, out],
        compiler_params=_cparams("arbitrary"),
        name="gdn_dec_prep",
    )(p, jnp.swapaxes(conv_buf.astype(f32), 0, 1), conv_w.astype(f32))


def _rec_decode_kernel(cols_ref, vrows_ref, scal_ref, sg_ref, sh_ref, sc_ref,
                       o_ref, sg_out, sh_out, sc_out, sn_out, sm_out):
    cols = cols_ref[...]
    vrows = vrows_ref[...]
    scal = scal_ref[...]
    col = lambda k: cols[:, :, k:k + 1]
    vrow = lambda k: vrows[:, k:k + 1, :]
    sca = lambda k: scal[:, :, k:k + 1]

    q, k, v = col(0), col(1), vrow(0)
    s = sg_ref[...]
    g = -jnp.exp(sca(2)) * _softplus(sca(0) + sca(3))
    eg = jnp.exp(g)
    beta = jax.nn.sigmoid(sca(1))
    ks = jnp.sum(k * s, axis=1, keepdims=True)
    qs = jnp.sum(q * s, axis=1, keepdims=True)
    u = beta * (v - eg * ks)
    qk = jnp.sum(q * k, axis=1, keepdims=True)
    o_ref[:, 0:1, :] = eg * qs + qk * u
    sg_out[...] = eg * s + k * u

    lb = col(6)
    z = col(3)
    logf = jnp.log(lb + (1.0 - lb) * jax.nn.sigmoid(z))
    kc = (1.0 - lb) * jax.nn.sigmoid(-z)
    qc = _silu(col(2))
    vc = vrow(1)
    sh = sh_ref[...]
    ef = jnp.exp(logf)
    o_ref[:, 1:2, :] = (jnp.sum((qc * ef) * sh, axis=1, keepdims=True)
                        + jnp.sum(qc * kc, axis=1, keepdims=True) * vc)
    sh_out[...] = ef * sh + kc * vc

    qd = col(4)
    kd = col(5) * (HEAD_W ** -0.5)
    vd = vrow(2)
    li = sca(4) + sca(6)
    lf = _log_sigmoid(sca(5) + sca(7))
    m0 = sca(8)
    cs = sc_ref[...]
    n0 = col(7)
    inter = lf + m0
    m_t = jnp.maximum(inter, li)
    w_inter = jnp.exp(inter - m_t)
    qkd = jnp.sum(qd * kd, axis=1, keepdims=True) * jnp.exp(li - m_t)
    num = w_inter * jnp.sum(qd * cs, axis=1, keepdims=True) + qkd * vd
    den = w_inter * jnp.sum(qd * n0, axis=1, keepdims=True) + qkd
    o_ref[:, 2:3, :] = num / jnp.maximum(jnp.abs(den), jnp.exp(-m_t))
    w_end = jnp.exp(li - m_t)
    d0 = jnp.exp(lf + m0 - m_t)
    sc_out[...] = d0 * cs + (w_end * kd) * vd
    sn_out[...] = d0 * n0 + w_end * kd
    sm_out[...] = m_t


def _rec_decode(p, gq, gk, gv, a_log, dt_bias, lb, i_bias, f_bias, s_gdn, s_hgrn, s_c, s_n, s_m):
    n_b = p.shape[0]
    rows = n_b * N_HEADS
    rb = min(16, rows)
    rw = lambda z: z.astype(f32).reshape(rows, HEAD_W)
    blockp = lambda b: rw(p[:, b * GROUP_W:(b + 1) * GROUP_W])
    per_head = lambda z: jnp.tile(z.astype(f32), n_b)
    gate = lambda c: p[:, GATE_COL + c:GATE_COL + c + N_HEADS].reshape(rows)
    lb_rows = jnp.tile(lb.astype(f32).reshape(N_HEADS, HEAD_W), (n_b, 1))
    cols = jnp.stack([rw(gq), rw(gk), blockp(7), blockp(8), blockp(11), blockp(12), lb_rows, rw(s_n)], axis=-1)
    vrows = jnp.stack([rw(gv), blockp(9), blockp(13)], axis=1)
    scal = jnp.stack([gate(0), gate(4), per_head(a_log), per_head(dt_bias), gate(8), gate(12),
                      per_head(i_bias), per_head(f_bias), s_m.astype(f32).reshape(rows)], axis=-1).reshape(rows, 1, 9)
    st = lambda z: z.astype(f32).reshape(rows, HEAD_W, HEAD_W)
    args = [cols, vrows, scal, st(s_gdn), st(s_hgrn), st(s_c)]

    def spec(a):
        return pl.BlockSpec((rb,) + a.shape[1:], lambda i: (i, 0, 0))

    o_st = jax.ShapeDtypeStruct((rows, HEAD_W, HEAD_W), f32)
    outs = [jax.ShapeDtypeStruct((rows, 3, HEAD_W), f32), o_st, o_st, o_st,
            jax.ShapeDtypeStruct((rows, HEAD_W, 1), f32), jax.ShapeDtypeStruct((rows, 1, 1), f32)]
    o, sg, sh, sc, sn, sm = pl.pallas_call(
        _rec_decode_kernel,
        grid=(rows // rb,),
        in_specs=[spec(a) for a in args],
        out_specs=[spec(a) for a in outs],
        out_shape=outs,
        compiler_params=_cparams("arbitrary"),
        name="rec_decode",
    )(*args)
    s4 = lambda z: z.reshape(n_b, N_HEADS, HEAD_W, HEAD_W)
    o2 = lambda k: o[:, k, :].reshape(n_b, GROUP_W)
    return (o2(0), s4(sg), o2(1), s4(sh), o2(2), s4(sc),
            sn.reshape(n_b, N_HEADS, HEAD_W), sm.reshape(n_b, N_HEADS))


def _permute_w_in(w):
    d_in = w.shape[1]
    a_gate0 = 3 * GROUP_W
    d_gate0 = d_in - GROUP_W - 8
    main = jnp.concatenate([w[:, 0:a_gate0], w[:, a_gate0 + 8:d_gate0], w[:, d_gate0 + 8:]], axis=1)
    gates = jnp.concatenate([w[:, a_gate0:a_gate0 + 8], w[:, d_gate0:d_gate0 + 8]], axis=1)
    pad = jnp.zeros((w.shape[0], P_COLS - main.shape[1] - 16), w.dtype)
    return jnp.concatenate([main, gates, pad], axis=1).astype(bf16), gates.T.astype(bf16)


def kernel(x_prompt, x_sample, page_table, cache_k, cache_v, state_gdn_conv, state_gdn, state_hgrn, state_mlstm_C, state_mlstm_n, state_mlstm_m, attn_norm_g, w_in, gdn_conv_w, gdn_a_log, gdn_dt_bias, gdn_norm_g, diff_qk_norm_g, diff_lambda, diff_subln_g, rel_bias, hgrn_lb_logits, hgrn_norm_g, mlstm_i_bias, mlstm_f_bias, mlstm_norm_g, w_out, ffn_norm_g, ffn_w_gate, ffn_w_up, ffn_w_down, moe_router, moe_w_gate, moe_w_up, moe_w_down):
    depth = w_in.shape[0]
    n_bp, seq, _ = x_prompt.shape
    n_bs = x_sample.shape[0]
    n_pool, page = cache_k.shape[1], cache_k.shape[2]
    dt = x_prompt.dtype

    lb_p = jax.nn.softmax(hgrn_lb_logits.astype(f32), axis=0)
    lb_cum = jnp.cumsum(lb_p, axis=0)
    hgrn_lb = lb_cum - lb_cum[0:1]
    cache_k4 = jnp.transpose(cache_k, (0, 1, 3, 4, 5, 2)).reshape(depth, n_pool, GROUP_W, page)
    cache_v4 = jnp.transpose(cache_v, (0, 1, 3, 4, 2)).reshape(depth, n_pool, GROUP_W, page)

    xp = x_prompt.reshape(n_bp * seq, D_MODEL)
    xs = x_sample.reshape(n_bs, D_MODEL)
    outs_p, outs_s = [], []
    for l in range(depth):
        w_perm, w_gate_t = _permute_w_in(w_in[l])
        w_out_b = w_out[l].astype(bf16)
        gains = jnp.stack([jnp.tile(g.astype(f32), N_HEADS) for g in
                           (gdn_norm_g[l], diff_subln_g[l], hgrn_norm_g[l], mlstm_norm_g[l])])
        lam_init = 0.8 - 0.6 * math.exp(-0.3 * l)
        lam32 = diff_lambda[l].astype(f32)
        lam = jnp.exp(jnp.sum(lam32[0] * lam32[1])) - jnp.exp(jnp.sum(lam32[2] * lam32[3])) + lam_init
        if l % 2 == 0:
            ffn_w = (ffn_w_gate[l // 2].astype(bf16), ffn_w_up[l // 2].astype(bf16), ffn_w_down[l // 2].astype(bf16))
        else:
            router_pad = jnp.pad(moe_router[l // 2].astype(bf16), ((0, 0), (0, 128 - N_EXPERTS)))
            moe_w = (moe_w_gate[l // 2].astype(bf16), moe_w_up[l // 2].astype(bf16), moe_w_down[l // 2].astype(bf16))

        def channel_mix(x):
            if l % 2 == 0:
                return _ffn(x, ffn_norm_g[l], *ffn_w)
            return _moe(x, ffn_norm_g[l], router_pad, *moe_w)

        p, gt = _inproj(xp, attn_norm_g[l], w_perm, w_gate_t)
        qnt, kn, vt = _bprep(p, diff_qk_norm_g[l], True)
        ob = _attn_prompt(qnt, kn, vt, lam, rel_bias, n_bp, seq)
        oa, s_gdn = _gdn_prompt(p, gt, gdn_conv_w[l], gdn_a_log[l], gdn_dt_bias[l], n_bp, seq)
        oc, s_hgrn = _hgrn_prompt(p, hgrn_lb[l], n_bp, seq)
        od, s_c, s_n, s_m = _mlstm_prompt(p, gt, mlstm_i_bias[l], mlstm_f_bias[l], n_bp, seq)
        xp = _outproj(oa, ob, oc, od, p, xp, gains, w_out_b, 1.0 - lam_init)
        xp = channel_mix(xp)
        p3 = p.reshape(n_bp, seq, P_COLS)
        outs_p.append((
            kn.reshape(n_bp, seq, N_HEADS, 2, DKB).astype(dt),
            p3[:, :, 6 * GROUP_W:7 * GROUP_W].reshape(n_bp, seq, N_HEADS, HEAD_W).astype(dt),
            p3[:, seq - (CONV_W - 1):, 0:3 * GROUP_W].astype(dt),
            s_gdn.astype(dt), s_hgrn.astype(dt), s_c.astype(dt), s_n.astype(dt), s_m.astype(dt)))

        p, gt = _inproj(xs, attn_norm_g[l], w_perm, w_gate_t)
        qn, kn = _bprep(p, diff_qk_norm_g[l], False)
        vn = p[:, 6 * GROUP_W:7 * GROUP_W]
        ob = _attn_decode(qn, kn, vn, page_table, cache_k4, cache_v4, l, lam, rel_bias)
        u = p[:, 0:3 * GROUP_W]
        gq, gk, gv = _gdn_dec_prep(u, state_gdn_conv[l], gdn_conv_w[l])
        oa, s_gdn, oc, s_hgrn, od, s_c, s_n, s_m = _rec_decode(
            p, gq, gk, gv, gdn_a_log[l], gdn_dt_bias[l], hgrn_lb[l], mlstm_i_bias[l], mlstm_f_bias[l],
            state_gdn[l], state_hgrn[l], state_mlstm_C[l], state_mlstm_n[l], state_mlstm_m[l])
        xs = _outproj(oa, ob, oc, od, p, xs, gains, w_out_b, 1.0 - lam_init)
        xs = channel_mix(xs)
        conv_new = jnp.concatenate([state_gdn_conv[l][:, 1:].astype(dt), u[:, None, :].astype(dt)], axis=1)
        outs_s.append((
            kn.reshape(n_bs, 1, N_HEADS, 2, DKB).astype(dt),
            vn.reshape(n_bs, 1, N_HEADS, HEAD_W).astype(dt),
            conv_new, s_gdn.astype(dt), s_hgrn.astype(dt), s_c.astype(dt), s_n.astype(dt), s_m.astype(dt)))

    kp, vp, convp, gdnp, hgrnp, mcp, mnp_, mmp = [jnp.stack(z) for z in zip(*outs_p)]
    ks_, vs_, convs, gdns, hgrns, mcs, mns, mms = [jnp.stack(z) for z in zip(*outs_s)]
    return (xp.reshape(n_bp, seq, D_MODEL), xs.reshape(n_bs, 1, D_MODEL), kp, vp, ks_, vs_, convp, convs,
            gdnp, gdns, hgrnp, hgrns, mcp, mcs, mnp_, mns, mmp, mms)
```

```python
import functools
import math

import numpy as np
import jax
import jax.numpy as jnp
from jax import lax
from jax.experimental import pallas as pl
from jax.experimental.pallas import tpu as pltpu

f32 = jnp.float32
bf16 = jnp.bfloat16

D_MODEL = 1024
N_HEADS = 4
HEAD_W = 64
GROUP_W = N_HEADS * HEAD_W
DKB = 32
CONV_W = 4
CHUNK = 64
SUB = 16
NUM_BUCKETS = 32
MAX_DISTANCE = 128
N_EXPERTS = 8
EPS = 1e-6
NEG = -1e30
P_COLS = 4096
GATE_COL = 3840
VMEM_LIMIT = 56 * 1024 * 1024

NN = ((1,), (0,))
NT = ((1,), (1,))
TN = ((0,), (0,))


def _dg(a, b, dims=NN):
    return lax.dot_general(a, b, (dims, ((), ())), preferred_element_type=f32)


def _split(a):
    hi = a.astype(bf16)
    lo = (a - hi.astype(f32)).astype(bf16)
    return hi, lo


def _mm3(a, b, dims=NN):
    ah, al = _split(a)
    bh, bl = _split(b)
    return _dg(ah, bh, dims) + (_dg(ah, bl, dims) + _dg(al, bh, dims))


def _mm2(a, b01, dims=NN):
    ah, al = _split(a)
    return _dg(ah, b01, dims) + _dg(al, b01, dims)


def _mm2l(a01, b, dims=NN):
    bh, bl = _split(b)
    return _dg(a01, bh, dims) + _dg(a01, bl, dims)


def _mm1(a, b, dims=NN):
    return _dg(a.astype(bf16), b.astype(bf16), dims)


def _iota(shape, dim):
    return lax.broadcasted_iota(jnp.int32, shape, dim)


def _group_ones(width, group):
    r = _iota((width, width), 0) // group
    c = _iota((width, width), 1) // group
    return (r == c).astype(bf16)


def _group_sum(x, group):
    ones = _group_ones(x.shape[-1], group)
    hi = x.astype(bf16)
    r1 = x - hi.astype(f32)
    mid = r1.astype(bf16)
    lo = (r1 - mid.astype(f32)).astype(bf16)
    return _dg(hi, ones) + (_dg(mid, ones) + _dg(lo, ones))


def _recip(x):
    r = 1.0 / x
    return r * (2.0 - x * r)


def _silu(x):
    return x * jax.nn.sigmoid(x)


def _softplus(x):
    return jnp.maximum(x, 0.0) + jnp.log1p(jnp.exp(-jnp.abs(x)))


def _stack_cols(xc, n=N_HEADS, rows=HEAD_W):
    return jnp.concatenate([xc[:, h:h + 1] for h in range(n)], axis=0)


def _expand_cols(xc, n=N_HEADS, width=HEAD_W):
    r = xc.shape[0]
    return jnp.concatenate([jnp.broadcast_to(xc[:, h:h + 1], (r, width)) for h in range(n)], axis=1)


def _cat_rows(xr, lo, n=N_HEADS, width=HEAD_W):
    return jnp.concatenate([xr[h:h + 1, lo:lo + width] for h in range(n)], axis=1)


def _head_stack(x, n=N_HEADS, width=HEAD_W):
    lane_head = _iota(x.shape, 1) // width
    return jnp.concatenate([jnp.where(lane_head == h, x, 0.0) for h in range(n)], axis=0)


def _fold_heads(x_sm, n=N_HEADS):
    r = x_sm.shape[0] // n
    out = x_sm[0:r]
    for h in range(1, n):
        out = out + x_sm[h * r:(h + 1) * r]
    return out


def _bd_masks(n=GROUP_W, blk=CHUNK):
    r = _iota((n, n), 0)
    c = _iota((n, n), 1)
    same = (r // blk) == (c // blk)
    lower = same & ((r % blk) >= (c % blk))
    strict = same & ((r % blk) > (c % blk))
    return same, lower, strict


def _cparams(*sem):
    return pltpu.CompilerParams(dimension_semantics=sem, vmem_limit_bytes=VMEM_LIMIT)


def _inproj_kernel(x_ref, g_ref, w_ref, wgt_ref, p_ref, gt_ref, h_scr):
    @pl.when(pl.program_id(1) == 0)
    def _():
        x = x_ref[...]
        ms = jnp.mean(x * x, axis=-1, keepdims=True)
        h = ((x * lax.rsqrt(ms + EPS)) * g_ref[...]).astype(bf16)
        h_scr[...] = h
        gt_ref[...] = _dg(wgt_ref[...], h, NT)
    p_ref[...] = _dg(h_scr[...], w_ref[...], NN)


def _inproj(x, g, w_perm, w_gate_t):
    t = x.shape[0]
    tm = min(1024, t)
    tn = 1024
    return pl.pallas_call(
        _inproj_kernel,
        grid=(t // tm, P_COLS // tn),
        in_specs=[
            pl.BlockSpec((tm, D_MODEL), lambda i, j: (i, 0)),
            pl.BlockSpec((1, D_MODEL), lambda i, j: (0, 0)),
            pl.BlockSpec((D_MODEL, tn), lambda i, j: (0, j)),
            pl.BlockSpec((16, D_MODEL), lambda i, j: (0, 0)),
        ],
        out_specs=[
            pl.BlockSpec((tm, tn), lambda i, j: (i, j)),
            pl.BlockSpec((16, tm), lambda i, j: (0, i)),
        ],
        out_shape=[jax.ShapeDtypeStruct((t, P_COLS), f32), jax.ShapeDtypeStruct((16, t), f32)],
        scratch_shapes=[pltpu.VMEM((tm, D_MODEL), bf16)],
        compiler_params=_cparams("arbitrary", "arbitrary"),
        name="inproj",
    )(x, g.reshape(1, D_MODEL), w_perm, w_gate_t)


def _qk_gnorm(x, g):
    ms = _group_sum(x * x, DKB) * (1.0 / DKB)
    return (x * lax.rsqrt(ms + EPS)) * g


def _bprep_kernel(q_ref, k_ref, gq_ref, gk_ref, qn_ref, kn_ref):
    qn_ref[...] = _qk_gnorm(q_ref[...], gq_ref[...])
    kn_ref[...] = _qk_gnorm(k_ref[...], gk_ref[...])


def _bprep_t_kernel(q_ref, k_ref, v_ref, gq_ref, gk_ref, qnt_ref, kn_ref, vt_ref):
    qnt_ref[...] = _qk_gnorm(q_ref[...], gq_ref[...]).T
    kn_ref[...] = _qk_gnorm(k_ref[...], gk_ref[...])
    vt_ref[...] = v_ref[...].T


def _bprep(p, qk_norm_g, transposed):
    t = p.shape[0]
    tm = min(512, t)
    gq = jnp.tile(qk_norm_g[0], GROUP_W // DKB).reshape(1, GROUP_W)
    gk = jnp.tile(qk_norm_g[1], GROUP_W // DKB).reshape(1, GROUP_W)
    col = lambda c: pl.BlockSpec((tm, GROUP_W), lambda i: (i, c))
    gain = pl.BlockSpec((1, GROUP_W), lambda i: (0, 0))
    rows = pl.BlockSpec((tm, GROUP_W), lambda i: (i, 0))
    rows_t = pl.BlockSpec((GROUP_W, tm), lambda i: (0, i))
    if transposed:
        return pl.pallas_call(
            _bprep_t_kernel,
            grid=(t // tm,),
            in_specs=[col(4), col(5), col(6), gain, gain],
            out_specs=[rows_t, rows, rows_t],
            out_shape=[jax.ShapeDtypeStruct((GROUP_W, t), f32), jax.ShapeDtypeStruct((t, GROUP_W), f32),
                       jax.ShapeDtypeStruct((GROUP_W, t), f32)],
            compiler_params=_cparams("arbitrary"),
            name="bprep_t",
        )(p, p, p, gq, gk)
    return pl.pallas_call(
        _bprep_kernel,
        grid=(t // tm,),
        in_specs=[col(4), col(5), gain, gain],
        out_specs=[rows, rows],
        out_shape=[jax.ShapeDtypeStruct((t, GROUP_W), f32)] * 2,
        compiler_params=_cparams("arbitrary"),
        name="bprep",
    )(p, p, gq, gk)


def _outproj_kernel(oa_ref, ob_ref, oc_ref, od_ref, ag_ref, cg_ref, dg_ref, x_ref, g_ref, w_ref, y_ref, *, b_scale):
    def gnorm(x, g):
        ms = _group_sum(x * x, HEAD_W) * (1.0 / HEAD_W)
        return (x * lax.rsqrt(ms + EPS)) * g
    g = g_ref[...]
    mixes = (
        gnorm(oa_ref[...], g[0:1]) * _silu(ag_ref[...]),
        gnorm(ob_ref[...], g[1:2]) * b_scale,
        gnorm(oc_ref[...], g[2:3]) * jax.nn.sigmoid(cg_ref[...]),
        gnorm(od_ref[...], g[3:4]) * jax.nn.sigmoid(dg_ref[...]),
    )
    y = x_ref[...]
    for i, m in enumerate(mixes):
        y = y + _dg(m.astype(bf16), w_ref[i * GROUP_W:(i + 1) * GROUP_W, :], NN)
    y_ref[...] = y


def _outproj(oa, ob, oc, od, p, x, gains, w_out, b_scale):
    t = x.shape[0]
    tm = min(512, t)
    row = lambda i: (i, 0)
    return pl.pallas_call(
        functools.partial(_outproj_kernel, b_scale=b_scale),
        grid=(t // tm,),
        in_specs=[
            pl.BlockSpec((tm, GROUP_W), row), pl.BlockSpec((tm, GROUP_W), row),
            pl.BlockSpec((tm, GROUP_W), row), pl.BlockSpec((tm, GROUP_W), row),
            pl.BlockSpec((tm, GROUP_W), lambda i: (i, 3)),
            pl.BlockSpec((tm, GROUP_W), lambda i: (i, 10)),
            pl.BlockSpec((tm, GROUP_W), lambda i: (i, 14)),
            pl.BlockSpec((tm, D_MODEL), row),
            pl.BlockSpec((4, GROUP_W), lambda i: (0, 0)),
            pl.BlockSpec((D_MODEL, D_MODEL), lambda i: (0, 0)),
        ],
        out_specs=pl.BlockSpec((tm, D_MODEL), row),
        out_shape=jax.ShapeDtypeStruct((t, D_MODEL), f32),
        compiler_params=_cparams("arbitrary"),
        name="outproj",
    )(oa, ob, oc, od, p, p, p, x, gains, w_out)


def _ffn_kernel(x_ref, g_ref, wg_ref, wu_ref, wd_ref, y_ref, h_scr):
    @pl.when(pl.program_id(1) == 0)
    def _():
        x = x_ref[...]
        ms = jnp.mean(x * x, axis=-1, keepdims=True)
        h_scr[...] = ((x * lax.rsqrt(ms + EPS)) * g_ref[...]).astype(bf16)
        y_ref[...] = x

    h = h_scr[...]
    a = _silu(_dg(h, wg_ref[...])) * _dg(h, wu_ref[...])
    y_ref[...] += _dg(a.astype(bf16), wd_ref[...])


def _ffn(x, g, wg, wu, wd):
    t = x.shape[0]
    d_ff = wg.shape[1]
    tm = min(1024, t)
    tf = d_ff // 2
    return pl.pallas_call(
        _ffn_kernel,
        grid=(t // tm, d_ff // tf),
        in_specs=[
            pl.BlockSpec((tm, D_MODEL), lambda i, f: (i, 0)),
            pl.BlockSpec((1, D_MODEL), lambda i, f: (0, 0)),
            pl.BlockSpec((D_MODEL, tf), lambda i, f: (0, f)),
            pl.BlockSpec((D_MODEL, tf), lambda i, f: (0, f)),
            pl.BlockSpec((tf, D_MODEL), lambda i, f: (f, 0)),
        ],
        out_specs=pl.BlockSpec((tm, D_MODEL), lambda i, f: (i, 0)),
        out_shape=jax.ShapeDtypeStruct((t, D_MODEL), f32),
        scratch_shapes=[pltpu.VMEM((tm, D_MODEL), bf16)],
        compiler_params=_cparams("arbitrary", "arbitrary"),
        name="ffn",
    )(x, g.reshape(1, D_MODEL), wg, wu, wd)


LANE = 128
SUBL = D_MODEL // LANE
ROW_TILE = 256
SPARSE_MIN_TOKENS = 4096


def _tile_rows(x):
    return [x[:, j * LANE:(j + 1) * LANE] for j in range(SUBL)]


def _router_kernel(x_ref, g_ref, r_ref, h3_ref, meta_ref, cnt_ref, carry_scr):
    @pl.when(pl.program_id(0) == 0)
    def _():
        carry_scr[...] = jnp.zeros(carry_scr.shape, f32)

    x = x_ref[...]
    tm = x.shape[0]
    ms = jnp.mean(x * x, axis=-1, keepdims=True)
    h = (x * lax.rsqrt(ms + EPS)) * g_ref[...]
    for j, blk in enumerate(_tile_rows(h)):
        h3_ref[:, j, :] = blk
    logits = _dg(h.astype(bf16), r_ref[...])
    lane = _iota(logits.shape, 1)
    logits = jnp.where(lane < N_EXPERTS, logits, -jnp.inf)
    v1 = jnp.max(logits, axis=-1, keepdims=True)
    i1 = jnp.min(jnp.where(logits == v1, lane, LANE), axis=-1, keepdims=True)
    rest = jnp.where(lane == i1, -jnp.inf, logits)
    v2 = jnp.max(rest, axis=-1, keepdims=True)
    i2 = jnp.min(jnp.where(rest == v2, lane, LANE), axis=-1, keepdims=True)
    e2 = jnp.exp(v2 - v1)
    den = 1.0 + e2
    hit = ((lane == i1) | (lane == i2)).astype(f32)
    strict = (_iota((tm, tm), 0) > _iota((tm, tm), 1)).astype(bf16)
    before = _dg(strict, hit.astype(bf16)) + carry_scr[...]
    pos1 = jnp.sum(jnp.where(lane == i1, before, 0.0), axis=-1, keepdims=True)
    pos2 = jnp.sum(jnp.where(lane == i2, before, 0.0), axis=-1, keepdims=True)
    carry_scr[...] += jnp.sum(hit, axis=0, keepdims=True)
    meta = jnp.zeros(logits.shape, f32)
    for c, val in enumerate((i1.astype(f32), i2.astype(f32), 1.0 / den, e2 / den, pos1, pos2)):
        meta = jnp.where(lane == c, val, meta)
    meta_ref[...] = meta
    cnt_ref[...] = carry_scr[...]


def _router(x, g, router_pad):
    t = x.shape[0]
    tm = min(512, t)
    return pl.pallas_call(
        _router_kernel,
        grid=(t // tm,),
        in_specs=[
            pl.BlockSpec((tm, D_MODEL), lambda i: (i, 0)),
            pl.BlockSpec((1, D_MODEL), lambda i: (0, 0)),
            pl.BlockSpec((D_MODEL, LANE), lambda i: (0, 0)),
        ],
        out_specs=[pl.BlockSpec((tm, SUBL, LANE), lambda i: (i, 0, 0)),
                   pl.BlockSpec((tm, LANE), lambda i: (i, 0)),
                   pl.BlockSpec((1, LANE), lambda i: (0, 0))],
        out_shape=[jax.ShapeDtypeStruct((t, SUBL, LANE), f32), jax.ShapeDtypeStruct((t, LANE), f32),
                   jax.ShapeDtypeStruct((1, LANE), f32)],
        scratch_shapes=[pltpu.VMEM((1, LANE), f32)],
        compiler_params=_cparams("arbitrary"),
        name="router",
    )(x, g.reshape(1, D_MODEL), router_pad)


def _swiglu_bf16(x, wg, wu, wd):
    a = _silu(_dg(x, wg)) * _dg(x, wu)
    return _dg(a.astype(bf16), wd)


def _moe_dense_kernel(x_ref, h3_ref, meta_ref, wg_ref, wu_ref, wd_ref, y_ref, acc_scr):
    e = pl.program_id(1)
    f = pl.program_id(2)

    @pl.when((e == 0) & (f == 0))
    def _():
        acc_scr[...] = x_ref[...]

    meta = meta_ref[...]
    ef = e.astype(f32)
    cw = jnp.where(meta[:, 0:1] == ef, meta[:, 2:3], 0.0) + jnp.where(meta[:, 1:2] == ef, meta[:, 3:4], 0.0)
    h = jnp.concatenate([h3_ref[:, j, :] for j in range(SUBL)], axis=1).astype(bf16)
    acc_scr[...] += cw * _swiglu_bf16(h, wg_ref[...], wu_ref[...], wd_ref[...])

    @pl.when((e == pl.num_programs(1) - 1) & (f == pl.num_programs(2) - 1))
    def _():
        y_ref[...] = acc_scr[...]


def _moe_dense(x, h3, meta, wg, wu, wd):
    t = x.shape[0]
    n_e, _, d_ff = wg.shape
    tm = min(512, t)
    tf = d_ff // 2
    return pl.pallas_call(
        _moe_dense_kernel,
        grid=(t // tm, n_e, d_ff // tf),
        in_specs=[
            pl.BlockSpec((tm, D_MODEL), lambda i, e, f: (i, 0)),
            pl.BlockSpec((tm, SUBL, LANE), lambda i, e, f: (i, 0, 0)),
            pl.BlockSpec((tm, LANE), lambda i, e, f: (i, 0)),
            pl.BlockSpec((None, D_MODEL, tf), lambda i, e, f: (e, 0, f)),
            pl.BlockSpec((None, D_MODEL, tf), lambda i, e, f: (e, 0, f)),
            pl.BlockSpec((None, tf, D_MODEL), lambda i, e, f: (e, f, 0)),
        ],
        out_specs=pl.BlockSpec((tm, D_MODEL), lambda i, e, f: (i, 0)),
        out_shape=jax.ShapeDtypeStruct((t, D_MODEL), f32),
        scratch_shapes=[pltpu.VMEM((tm, D_MODEL), f32)],
        compiler_params=_cparams("arbitrary", "arbitrary", "arbitrary"),
        name="moe_dense",
    )(x, h3, meta, wg, wu, wd)


def _route_plan(meta, counts, t):
    cnt = counts[0, :N_EXPERTS].astype(jnp.int32)
    padded = ((cnt + ROW_TILE - 1) // ROW_TILE) * ROW_TILE
    ends = jnp.cumsum(padded)
    offs = ends - padded
    experts = jnp.arange(N_EXPERTS, dtype=jnp.int32)

    def dest(expert_col, rank_col):
        e = meta[:, expert_col].astype(jnp.int32)
        off = jnp.sum(jnp.where(e[:, None] == experts[None, :], offs[None, :], 0), axis=1)
        return off + meta[:, rank_col].astype(jnp.int32)

    n_rows = 2 * t + N_EXPERTS * ROW_TILE
    starts = jnp.arange(n_rows // ROW_TILE, dtype=jnp.int32) * ROW_TILE
    tile_expert = jnp.minimum(jnp.sum((starts[:, None] >= ends[None, :]).astype(jnp.int32), axis=1), N_EXPERTS - 1)
    n_used = (ends[N_EXPERTS - 1] // ROW_TILE).reshape(1)
    return dest(0, 4), dest(1, 5), tile_expert, n_used, n_rows


def _dispatch_kernel(d1_ref, d2_ref, h3_ref, zero_hbm, xs_hbm, sem, *, tm):
    del zero_hbm
    base = pl.program_id(0) * tm

    def issue(k, carry):
        src = h3_ref.at[pl.ds(k, 1)]
        pltpu.make_async_copy(src, xs_hbm.at[pl.ds(d1_ref[base + k], 1)], sem).start(priority=0)
        pltpu.make_async_copy(src, xs_hbm.at[pl.ds(d2_ref[base + k], 1)], sem).start(priority=1)
        return carry

    lax.fori_loop(0, tm, issue, 0, unroll=4)
    for _ in range(2):
        pltpu.make_async_copy(h3_ref, xs_hbm.at[pl.ds(0, tm)], sem).wait()


def _dispatch(h3, dest1, dest2, n_rows):
    t = h3.shape[0]
    tm = min(512, t)
    grid_spec = pltpu.PrefetchScalarGridSpec(
        num_scalar_prefetch=2,
        grid=(t // tm,),
        in_specs=[pl.BlockSpec((tm, SUBL, LANE), lambda i, d1, d2: (i, 0, 0)), pl.BlockSpec(memory_space=pl.ANY)],
        out_specs=pl.BlockSpec(memory_space=pl.ANY),
        scratch_shapes=[pltpu.SemaphoreType.DMA(())],
    )
    return pl.pallas_call(
        functools.partial(_dispatch_kernel, tm=tm),
        grid_spec=grid_spec,
        out_shape=jax.ShapeDtypeStruct((n_rows, SUBL, LANE), f32),
        input_output_aliases={3: 0},
        compiler_params=_cparams("arbitrary"),
        name="moe_dispatch",
    )(dest1, dest2, h3, jnp.zeros((n_rows, SUBL, LANE), f32))


def _experts_kernel(te_ref, nu_ref, xs_ref, wg_ref, wu_ref, wd_ref, ys_ref):
    del te_ref
    r = pl.program_id(0)

    @pl.when(r < nu_ref[0])
    def _():
        x = jnp.concatenate([xs_ref[:, j, :] for j in range(SUBL)], axis=1).astype(bf16)
        for j, blk in enumerate(_tile_rows(_swiglu_bf16(x, wg_ref[...], wu_ref[...], wd_ref[...]))):
            ys_ref[:, j, :] = blk

    @pl.when(r >= nu_ref[0])
    def _():
        ys_ref[...] = jnp.zeros(ys_ref.shape, f32)


def _experts(xs, tile_expert, n_used, wg, wu, wd):
    n_rows = xs.shape[0]
    d_ff = wg.shape[2]
    rows = pl.BlockSpec((ROW_TILE, SUBL, LANE), lambda r, te, nu: (r, 0, 0))
    w_in = pl.BlockSpec((None, D_MODEL, d_ff), lambda r, te, nu: (te[r], 0, 0), pipeline_mode=pl.Buffered(1))
    w_dn = pl.BlockSpec((None, d_ff, D_MODEL), lambda r, te, nu: (te[r], 0, 0), pipeline_mode=pl.Buffered(1))
    grid_spec = pltpu.PrefetchScalarGridSpec(
        num_scalar_prefetch=2,
        grid=(n_rows // ROW_TILE,),
        in_specs=[rows, w_in, w_in, w_dn],
        out_specs=rows,
    )
    return pl.pallas_call(
        _experts_kernel,
        grid_spec=grid_spec,
        out_shape=jax.ShapeDtypeStruct((n_rows, SUBL, LANE), f32),
        compiler_params=_cparams("arbitrary"),
        name="moe_experts",
    )(tile_expert, n_used, xs, wg, wu, wd)


def _combine_kernel(d1_ref, d2_ref, x_ref, meta_ref, ys_hbm, y_ref, buf, sem, *, tm):
    base = pl.program_id(0) * tm

    def issue(k, carry):
        t = base + k
        pltpu.make_async_copy(ys_hbm.at[pl.ds(d1_ref[t], 1)], buf.at[0, pl.ds(k, 1)], sem).start(priority=0)
        pltpu.make_async_copy(ys_hbm.at[pl.ds(d2_ref[t], 1)], buf.at[1, pl.ds(k, 1)], sem).start(priority=1)
        return carry

    lax.fori_loop(0, tm, issue, 0, unroll=4)
    for s in range(2):
        pltpu.make_async_copy(ys_hbm.at[pl.ds(0, tm)], buf.at[s], sem).wait()
    meta = meta_ref[...]
    g1 = meta[:, 2:3]
    g2 = meta[:, 3:4]
    for j in range(SUBL):
        sl = slice(j * LANE, (j + 1) * LANE)
        y_ref[:, sl] = x_ref[:, sl] + (g1 * buf[0, :, j, :] + g2 * buf[1, :, j, :])


def _combine(x, meta, ys, dest1, dest2):
    t = x.shape[0]
    tm = min(256, t)
    grid_spec = pltpu.PrefetchScalarGridSpec(
        num_scalar_prefetch=2,
        grid=(t // tm,),
        in_specs=[pl.BlockSpec((tm, D_MODEL), lambda i, d1, d2: (i, 0)),
                  pl.BlockSpec((tm, LANE), lambda i, d1, d2: (i, 0)),
                  pl.BlockSpec(memory_space=pl.ANY)],
        out_specs=pl.BlockSpec((tm, D_MODEL), lambda i, d1, d2: (i, 0)),
        scratch_shapes=[pltpu.VMEM((2, tm, SUBL, LANE), f32), pltpu.SemaphoreType.DMA(())],
    )
    return pl.pallas_call(
        functools.partial(_combine_kernel, tm=tm),
        grid_spec=grid_spec,
        out_shape=jax.ShapeDtypeStruct((t, D_MODEL), f32),
        compiler_params=_cparams("arbitrary"),
        name="moe_combine",
    )(dest1, dest2, x, meta, ys)


def _moe(x, g, router_pad, wg, wu, wd):
    t = x.shape[0]
    h3, meta, counts = _router(x, g, router_pad)
    if t < SPARSE_MIN_TOKENS:
        return _moe_dense(x, h3, meta, wg, wu, wd)
    dest1, dest2, tile_expert, n_used, n_rows = _route_plan(meta, counts, t)
    xs = _dispatch(h3, dest1, dest2, n_rows)
    ys = _experts(xs, tile_expert, n_used, wg, wu, wd)
    return _combine(x, meta, ys, dest1, dest2)


def _t5_bucket_np(n):
    n = np.maximum(n, 0)
    max_exact = NUM_BUCKETS // 2
    nf = np.maximum(n, 1).astype(np.float32)
    large = max_exact + (np.log(nf / np.float32(max_exact)) / np.float32(math.log(MAX_DISTANCE / max_exact))
                         * np.float32(NUM_BUCKETS - max_exact)).astype(np.int32)
    return np.where(n < max_exact, n, np.minimum(large, NUM_BUCKETS - 1))


def _shifted_bias(rel_bias):
    rb = rel_bias.astype(f32)
    return rb - rb[NUM_BUCKETS - 1:NUM_BUCKETS]


ACC_ROWS = HEAD_W + 8
LOG2E = math.log2(math.e)


def _attn_kernel(qi_ref, kj_ref, lam_ref, qt_ref, k_ref, vt_ref, toep_ref, o_ref, qs_scr, m_scr, acc_scr, *, tq):
    p = pl.program_id(1)
    i = qi_ref[p]
    j = kj_ref[p]
    n_hc = 2 * N_HEADS
    c2 = (DKB ** -0.5) * LOG2E

    @pl.when(j == 0)
    def _():
        qt = qt_ref[...] * c2
        row_grp = _iota(qt.shape, 0) // DKB
        for hc in range(n_hc):
            qs_scr[:, hc * tq:(hc + 1) * tq] = jnp.where(row_grp == hc, qt, 0.0).astype(bf16)
        m_scr[...] = jnp.full(m_scr.shape, NEG, f32)
        acc_scr[...] = jnp.zeros(acc_scr.shape, f32)

    def step(near):
        tk = k_ref.shape[0]
        st_all = _dg(k_ref[...].astype(bf16), qs_scr[...], NN)
        vt = vt_ref[...]
        ones = jnp.ones((ACC_ROWS - HEAD_W, tk), f32)
        for h in range(N_HEADS):
            vh = jnp.concatenate([vt[h * HEAD_W:(h + 1) * HEAD_W, :], ones], axis=0).astype(bf16)
            for hc in (2 * h, 2 * h + 1):
                s = st_all[:, hc * tq:(hc + 1) * tq]
                if near:
                    s = s + toep_ref[(i - j) * N_HEADS + h]
                m_old = m_scr[hc:hc + 1, :]
                m_new = jnp.maximum(m_old, jnp.max(s, axis=0, keepdims=True))
                pexp = jnp.exp2(s - m_new)
                acc_scr[hc] = jnp.exp2(m_old - m_new) * acc_scr[hc] + _dg(vh, pexp.astype(bf16), NN)
                m_scr[hc:hc + 1, :] = m_new

    @pl.when(i - j <= 1)
    def _():
        step(True)

    @pl.when(i - j > 1)
    def _():
        step(False)

    @pl.when(j == i)
    def _():
        lam = lam_ref[0]
        outs = []
        for h in range(N_HEADS):
            a0 = acc_scr[2 * h]
            a1 = acc_scr[2 * h + 1]
            outs.append(a0[0:HEAD_W] * _recip(a0[HEAD_W:HEAD_W + 1])
                        - lam * (a1[0:HEAD_W] * _recip(a1[HEAD_W:HEAD_W + 1])))
        o_ref[...] = jnp.concatenate(outs, axis=0).T


def _toeplitz_kernel(u_ref, o_ref):
    t = o_ref.shape[0]
    rows = jnp.broadcast_to(u_ref[...], (t, 2 * t))
    o_ref[...] = pltpu.roll(rows, 0, 1, stride=1, stride_axis=0)[:, t:2 * t]


def _toeplitz_bias_tiles(rel_bias, t):
    m = np.arange(2 * t)[None, :]
    dist = m - t + np.array([0, t])[:, None]
    tab = _shifted_bias(rel_bias)
    u = jnp.take(tab, jnp.asarray(_t5_bucket_np(dist)), axis=0)
    u = jnp.where(jnp.asarray(dist >= 0)[:, :, None], u * LOG2E, NEG)
    u = jnp.transpose(u, (0, 2, 1)).reshape(2 * N_HEADS, 1, 2 * t)
    return pl.pallas_call(
        _toeplitz_kernel,
        grid=(2 * N_HEADS,),
        in_specs=[pl.BlockSpec((None, 1, 2 * t), lambda i: (i, 0, 0))],
        out_specs=pl.BlockSpec((None, t, t), lambda i: (i, 0, 0)),
        out_shape=jax.ShapeDtypeStruct((2 * N_HEADS, t, t), f32),
        compiler_params=_cparams("arbitrary"),
        name="toeplitz_bias",
    )(u)


def _attn_prompt(qnt, kn, vt, lam, rel_bias, n_batch, seq):
    tq = min(512, seq)
    nq = seq // tq
    pairs =[(i, j) for i in range(nq) for j in range(i + 1)]
    qi = jnp.asarray(np.array([a for a, _ in pairs], np.int32))
    kj = jnp.asarray(np.array([b for _, b in pairs], np.int32))
    toep = _toeplitz_bias_tiles(rel_bias, tq)
    grid_spec = pltpu.PrefetchScalarGridSpec(
        num_scalar_prefetch=2,
        grid=(n_batch, len(pairs)),
        in_specs=[
            pl.BlockSpec(memory_space=pltpu.SMEM),
            pl.BlockSpec((GROUP_W, tq), lambda b_, p_, qi_, kj_: (0, b_ * nq + qi_[p_])),
            pl.BlockSpec((tq, GROUP_W), lambda b_, p_, qi_, kj_: (b_ * nq + kj_[p_], 0)),
            pl.BlockSpec((GROUP_W, tq), lambda b_, p_, qi_, kj_: (0, b_ * nq + kj_[p_])),
            pl.BlockSpec((2 * N_HEADS, tq, tq), lambda b_, p_, qi_, kj_: (0, 0, 0)),
        ],
        out_specs=pl.BlockSpec((tq, GROUP_W), lambda b_, p_, qi_, kj_: (b_ * nq + qi_[p_], 0)),
        scratch_shapes=[
            pltpu.VMEM((GROUP_W, 2 * N_HEADS * tq), bf16),
            pltpu.VMEM((2 * N_HEADS, tq), f32),
            pltpu.VMEM((2 * N_HEADS, ACC_ROWS, tq), f32),
        ],
    )
    return pl.pallas_call(
        functools.partial(_attn_kernel, tq=tq),
        grid_spec=grid_spec,
        out_shape=jax.ShapeDtypeStruct((n_batch * seq, GROUP_W), f32),
        compiler_params=_cparams("arbitrary", "arbitrary"),
        name="attn_prompt",
    )(qi, kj, lam.reshape(1), qnt, kn, vt, toep)


def _attn_decode_kernel(pt_ref, lam_ref, q_ref, kn_ref, vn_ref, blast_ref, bself_ref, *rest, pg, n_pages):
    k_refs = rest[:pg]
    v_refs = rest[pg:2 * pg]
    o_ref, qs_scr, s_scr, v_scr = rest[2 * pg:]
    t = pl.program_id(1)
    n_steps = n_pages // pg
    n_hc = 2 * N_HEADS
    page = k_refs[0].shape[1]
    scale = DKB ** -0.5
    rnd = lambda z: z.astype(bf16).astype(f32)

    @pl.when(t == 0)
    def _():
        q = jnp.broadcast_to(q_ref[...], (n_hc, GROUP_W))
        keep = (_iota(q.shape, 1) // DKB) == _iota(q.shape, 0)
        qs_scr[...] = jnp.where(keep, q, 0.0)

    qs_b = qs_scr[...].astype(bf16)
    parts = []
    for g in range(pg):
        s = _dg(qs_b, k_refs[g][...].astype(bf16), NN) * scale
        is_last = (t * pg + g) == (n_pages - 1)
        parts.append(s + jnp.where(is_last, blast_ref[...], 0.0))
        v_scr[t * pg + g] = v_refs[g][...].astype(bf16)
    s_scr[t] = jnp.concatenate(parts, axis=1)

    @pl.when(t == n_steps - 1)
    def _():
        s_all = s_scr[...]
        s_self = jnp.sum(rnd(qs_scr[...]) * rnd(kn_ref[...]), axis=-1, keepdims=True) * scale + bself_ref[...]
        m = jnp.maximum(jnp.max(jnp.max(s_all, axis=2, keepdims=True), axis=0), s_self)
        p = jnp.exp(s_all - m)
        p_self = jnp.exp(s_self - m)
        l = jnp.sum(jnp.sum(p, axis=2, keepdims=True), axis=0) + p_self
        inv_l = _recip(l)
        pn = p * inv_l
        pn_self = p_self * inv_l
        lam = lam_ref[0]
        rows = [pn[:, 2 * h:2 * h + 1, :] - lam * pn[:, 2 * h + 1:2 * h + 2, :] for h in range(N_HEADS)]
        s_scr[...] = jnp.concatenate(rows + [jnp.zeros_like(rows[0])] * N_HEADS, axis=1)
        rows_self = [pn_self[2 * h:2 * h + 1] - lam * pn_self[2 * h + 1:2 * h + 2] for h in range(N_HEADS)]
        a_self = jnp.concatenate(rows_self + [jnp.zeros_like(rows_self[0])] * N_HEADS, axis=0)

        def weighted_values(t2, acc):
            a = s_scr[t2].astype(bf16)
            for g in range(pg):
                acc = acc + _dg(a[:, g * page:(g + 1) * page], v_scr[t2 * pg + g], NT)
            return acc

        o = lax.fori_loop(0, n_steps, weighted_values, rnd(a_self) * rnd(vn_ref[...]))
        lane_head = _iota((1, GROUP_W), 1) // HEAD_W
        out = jnp.zeros((1, GROUP_W), f32)
        for h in range(N_HEADS):
            out = jnp.where(lane_head == h, o[h:h + 1], out)
        o_ref[...] = out


def _attn_decode(qn, kn, vn, page_table, cache_k, cache_v, layer, lam, rel_bias):
    n_b, n_pages = page_table.shape
    page = cache_k.shape[3]
    pg = min(32, n_pages)
    n_steps = n_pages // pg
    past = n_pages * page
    tab = _shifted_bias(rel_bias)
    d_last = past - ((n_pages - 1) * page + np.arange(page))
    blast = jnp.repeat(jnp.take(tab, jnp.asarray(_t5_bucket_np(d_last)), axis=0).T, 2, axis=0)
    bself = jnp.repeat(tab[0].reshape(N_HEADS, 1), 2, axis=0)

    def page_spec(g):
        return pl.BlockSpec((None, None, GROUP_W, page), lambda b_, t_, pt: (layer, pt[b_, t_ * pg + g], 0, 0))

    row = pl.BlockSpec((None, 1, GROUP_W), lambda b_, t_, pt: (b_, 0, 0))
    grid_spec = pltpu.PrefetchScalarGridSpec(
        num_scalar_prefetch=1,
        grid=(n_b, n_steps),
        in_specs=[pl.BlockSpec(memory_space=pltpu.SMEM), row, row, row,
                  pl.BlockSpec((2 * N_HEADS, page), lambda b_, t_, pt: (0, 0)),
                  pl.BlockSpec((2 * N_HEADS, 1), lambda b_, t_, pt: (0, 0))]
                 + [page_spec(g) for g in range(pg)] * 2,
        out_specs=row,
        scratch_shapes=[
            pltpu.VMEM((2 * N_HEADS, GROUP_W), f32),
            pltpu.VMEM((n_steps, 2 * N_HEADS, pg * page), f32),
            pltpu.VMEM((n_pages, GROUP_W, page), bf16),
        ],
    )
    r3 = lambda z: z.reshape(n_b, 1, GROUP_W)
    out = pl.pallas_call(
        functools.partial(_attn_decode_kernel, pg=pg, n_pages=n_pages),
        grid_spec=grid_spec,
        out_shape=jax.ShapeDtypeStruct((n_b, 1, GROUP_W), f32),
        compiler_params=_cparams("arbitrary", "arbitrary"),
        name="attn_decode",
    )(page_table, lam.reshape(1), r3(qn), r3(kn), r3(vn), blast, bself,
      *([cache_k] * pg), *([cache_v] * pg))
    return out.reshape(n_b, GROUP_W)


def _tri(n, dtype=f32):
    return (_iota((n, n), 0) >= _iota((n, n), 1)).astype(dtype)


def _block_tri_t(n, blk):
    r = _iota((n, n), 0)
    c = _iota((n, n), 1)
    return (((r // blk) == (c // blk)) & (r <= c)).astype(bf16)


def _block_tri(n, blk):
    r = _iota((n, n), 0)
    c = _iota((n, n), 1)
    return (((r // blk) == (c // blk)) & (r >= c)).astype(bf16)


def _gdn_kernel(u_ref, gc_ref, gr_ref, cw_ref, alr_ref, dtr_ref, alc_ref, dtc_ref, o_ref, s_out_ref, ext_scr, s_scr, *, tb):
    i = pl.program_id(1)

    @pl.when(i == 0)
    def _():
        ext_scr[0:8, :] = jnp.zeros((8, 3 * GROUP_W), f32)
        s_scr[...] = jnp.zeros(s_scr.shape, f32)

    ext_scr[8:8 + tb, :] = u_ref[...]
    w = cw_ref[...]
    conv = ext_scr[8:8 + tb, :] * w[3:4]
    for jj in range(1, CONV_W):
        conv = conv + ext_scr[8 - jj:8 - jj + tb, :] * w[3 - jj:4 - jj]
    ext_scr[0:8, :] = ext_scr[tb:tb + 8, :]
    qkv = _silu(conv)

    def l2n(x):
        return x * lax.rsqrt(_group_sum(x * x, HEAD_W) + EPS)

    q = l2n(qkv[:, 0:GROUP_W]) * (HEAD_W ** -0.5)
    k = l2n(qkv[:, GROUP_W:2 * GROUP_W])
    v = qkv[:, 2 * GROUP_W:3 * GROUP_W]

    gc = gc_ref[...]
    g_col = -jnp.exp(alr_ref[...]) * _softplus(gc[:, 0:4] + dtr_ref[...])
    beta_col = jax.nn.sigmoid(gc[:, 4:8])
    gr = gr_ref[...]
    g_row = -jnp.exp(alc_ref[...]) * _softplus(gr[0:4, :] + dtc_ref[...])
    g_row8 = jnp.concatenate([g_row, jnp.zeros_like(g_row)], axis=0)
    gcum_row = _mm2(g_row8, _block_tri_t(tb, CHUNK))

    same, lower, strict = _bd_masks()
    tri = _tri(CHUNK, bf16)
    r = _iota((GROUP_W, GROUP_W), 0)
    c = _iota((GROUP_W, GROUP_W), 1)
    eye = (r == c).astype(f32)

    chunks = range(tb // CHUNK)
    gcums, qks, m_bds = [], [], []
    for ch in chunks:
        lo = ch * CHUNK
        gcum = _mm2l(tri, g_col[lo:lo + CHUNK])
        g_stack = _stack_cols(gcum)
        g_cat = _cat_rows(gcum_row, lo)
        decay = jnp.exp(jnp.where(lower, g_stack - g_cat, NEG))
        ksm = _head_stack(k[lo:lo + CHUNK])
        kk = _mm1(ksm, ksm, NT)
        qks.append(_mm1(_head_stack(q[lo:lo + CHUNK]), ksm, NT) * decay)
        m_bds.append(_stack_cols(beta_col[lo:lo + CHUNK]) * kk * jnp.where(strict, decay, 0.0))
        gcums.append(gcum)

    def sibling(lev):
        return ((r >> (lev + 1)) == (c >> (lev + 1))) & (((r >> lev) & 1) == 1) & (((c >> lev) & 1) == 0)

    xs = [eye - jnp.where(sibling(0), m, 0.0) for m in m_bds]
    for lev in range(1, 6):
        sel = sibling(lev)
        xs = [x - _mm2(_mm2(x, jnp.where(sel, m, 0.0).astype(bf16)), x.astype(bf16)) for x, m in zip(xs, m_bds)]

    for ch in chunks:
        lo = ch * CHUNK
        qc, kc, vc = q[lo:lo + CHUNK], k[lo:lo + CHUNK], v[lo:lo + CHUNK]
        gcum, bcol = gcums[ch], beta_col[lo:lo + CHUNK]
        s_bd = s_scr[...]
        kq_s = _mm1(jnp.concatenate([kc, qc], axis=0), s_bd)
        ks, qs = kq_s[0:CHUNK], kq_s[CHUNK:2 * CHUNK]
        eg_all = _expand_cols(jnp.exp(gcum))
        rhs = _expand_cols(bcol) * (vc - eg_all * ks)
        u_sm = _mm2(xs[ch], _head_stack(rhs).astype(bf16))
        o_sm = _mm1(qks[ch], u_sm)
        o_ref[lo:lo + CHUNK, :] = eg_all * qs + _fold_heads(o_sm)
        u_all = _fold_heads(u_sm)
        g_last = gcum[CHUNK - 1:CHUNK, :]
        kw = kc * _expand_cols(jnp.exp(g_last - gcum))
        d_stack = jnp.concatenate(
            [jnp.broadcast_to(jnp.exp(g_last[:, h:h + 1]), (HEAD_W, 1)) for h in range(N_HEADS)], axis=0)
        s_scr[...] = d_stack * s_bd + jnp.where(same, _mm1(kw, u_all, TN), 0.0)

    @pl.when(i == pl.num_programs(1) - 1)
    def _():
        s_out_ref[...] = s_scr[...]


def _gdn_prompt(p, gt, conv_w, a_log, dt_bias, n_batch, seq):
    tb = min(256, seq)
    nb = seq // tb
    r14 = lambda z: z.astype(f32).reshape(1, N_HEADS)
    c41 = lambda z: z.astype(f32).reshape(N_HEADS, 1)
    o, s_bd = pl.pallas_call(
        functools.partial(_gdn_kernel, tb=tb),
        grid=(n_batch, nb),
        in_specs=[
            pl.BlockSpec((tb, 3 * GROUP_W), lambda b, i: (b * nb + i, 0)),
            pl.BlockSpec((tb, 128), lambda b, i: (b * nb + i, GATE_COL // 128)),
            pl.BlockSpec((16, tb), lambda b, i: (0, b * nb + i)),
            pl.BlockSpec((CONV_W, 3 * GROUP_W), lambda b, i: (0, 0)),
            pl.BlockSpec((1, N_HEADS), lambda b, i: (0, 0)),
            pl.BlockSpec((1, N_HEADS), lambda b, i: (0, 0)),
            pl.BlockSpec((N_HEADS, 1), lambda b, i: (0, 0)),
            pl.BlockSpec((N_HEADS, 1), lambda b, i: (0, 0)),
        ],
        out_specs=[
            pl.BlockSpec((tb, GROUP_W), lambda b, i: (b * nb + i, 0)),
            pl.BlockSpec((None, GROUP_W, GROUP_W), lambda b, i: (b, 0, 0)),
        ],
        out_shape=[jax.ShapeDtypeStruct((n_batch * seq, GROUP_W), f32),
                   jax.ShapeDtypeStruct((n_batch, GROUP_W, GROUP_W), f32)],
        scratch_shapes=[pltpu.VMEM((tb + 8, 3 * GROUP_W), f32), pltpu.VMEM((GROUP_W, GROUP_W), f32)],
        compiler_params=_cparams("arbitrary", "arbitrary"),
        name="gdn_prompt",
    )(p, p, gt, conv_w.astype(f32), r14(a_log), r14(dt_bias), c41(a_log), c41(dt_bias))
    return o, _bd_diag(s_bd)


def _bd_diag(s_bd):
    n_b = s_bd.shape[0]
    s5 = s_bd.reshape(n_b, N_HEADS, HEAD_W, N_HEADS, HEAD_W)
    return jnp.stack([s5[:, h, :, h, :] for h in range(N_HEADS)], axis=1)


def _hgrn_kernel(q_ref, f_ref, i_ref, lb_ref, o_ref, s_out_ref, st_scr, q_scr, k_scr, b_scr, *, tb):
    blk = pl.program_id(1)

    @pl.when(blk == 0)
    def _():
        st_scr[...] = jnp.zeros(st_scr.shape, f32)

    lb = lb_ref[...]
    z = f_ref[...]
    logf = jnp.log(lb + (1.0 - lb) * jax.nn.sigmoid(z))
    q_scr[...] = _silu(q_ref[...])
    k_scr[...] = (1.0 - lb) * jax.nn.sigmoid(-z)
    b_scr[...] = _mm2l(_block_tri(tb, SUB), logf)

    same, _, _ = _bd_masks()
    ones_bd = _group_ones(GROUP_W, HEAD_W)
    row = _iota((SUB * SUB, GROUP_W), 0)
    tmask = (row % SUB) >= (row // SUB)

    def rep_t(x):
        return jnp.broadcast_to(x[None], (SUB, SUB, GROUP_W)).reshape(SUB * SUB, GROUP_W)

    def rep_j(x):
        return jnp.broadcast_to(x[:, None, :], (SUB, SUB, GROUP_W)).reshape(SUB * SUB, GROUP_W)

    def body(c, carry):
        r0 = pl.multiple_of(c * SUB, SUB)
        qs = q_scr[pl.ds(r0, SUB), :]
        ks = k_scr[pl.ds(r0, SUB), :]
        vs = i_ref[pl.ds(r0, SUB), :]
        bs = b_scr[pl.ds(r0, SUB), :]
        st = st_scr[...]
        o_inter = _mm1(qs * jnp.exp(bs), st, NT)
        wgt = rep_t(qs) * jnp.exp(jnp.where(tmask, rep_t(bs) - rep_j(bs), NEG)) * rep_j(ks)
        a = _mm2(wgt, ones_bd)
        o_diag = jnp.sum((a * rep_j(vs)).reshape(SUB, SUB, GROUP_W), axis=0)
        o_ref[pl.ds(r0, SUB), :] = o_inter + o_diag
        b_last = bs[SUB - 1:SUB, :]
        kw = ks * jnp.exp(b_last - bs)
        st_scr[...] = st * jnp.exp(b_last) + jnp.where(same, _mm1(vs, kw, TN), 0.0)
        return carry

    lax.fori_loop(0, tb // SUB, body, 0, unroll=4)

    @pl.when(blk == pl.num_programs(1) - 1)
    def _():
        s_out_ref[...] = st_scr[...]


def _hgrn_prompt(p, lb, n_batch, seq):
    tb = min(256, seq)
    nb = seq // tb
    blk = lambda col: pl.BlockSpec((tb, GROUP_W), lambda b, i: (b * nb + i, col))
    o, st = pl.pallas_call(
        functools.partial(_hgrn_kernel, tb=tb),
        grid=(n_batch, nb),
        in_specs=[blk(7), blk(8), blk(9), pl.BlockSpec((1, GROUP_W), lambda b, i: (0, 0))],
        out_specs=[
            pl.BlockSpec((tb, GROUP_W), lambda b, i: (b * nb + i, 0)),
            pl.BlockSpec((None, GROUP_W, GROUP_W), lambda b, i: (b, 0, 0)),
        ],
        out_shape=[jax.ShapeDtypeStruct((n_batch * seq, GROUP_W), f32),
                   jax.ShapeDtypeStruct((n_batch, GROUP_W, GROUP_W), f32)],
        scratch_shapes=[pltpu.VMEM((GROUP_W, GROUP_W), f32)] + [pltpu.VMEM((tb, GROUP_W), f32)] * 3,
        compiler_params=_cparams("arbitrary", "arbitrary"),
        name="hgrn_prompt",
    )(p, p, p, lb.astype(f32).reshape(1, GROUP_W))
    return o, jnp.swapaxes(_bd_diag(st), -1, -2)


def _log_sigmoid(x):
    return jnp.minimum(x, 0.0) - jnp.log1p(jnp.exp(-jnp.abs(x)))


def _mlstm_kernel(q_ref, k_ref, v_ref, gc_ref, gr_ref, ibr_ref, fbr_ref, ibc_ref, fbc_ref,
                  o_ref, c_out_ref, n_out_ref, m_out_ref, c_scr, n_scr, m_scr, *, tb):
    blk = pl.program_id(1)

    @pl.when(blk == 0)
    def _():
        c_scr[...] = jnp.zeros(c_scr.shape, f32)
        n_scr[...] = jnp.zeros(n_scr.shape, f32)
        m_scr[...] = jnp.zeros(m_scr.shape, f32)

    q = q_ref[...]
    k = k_ref[...] * (HEAD_W ** -0.5)
    v = v_ref[...]
    gc = gc_ref[...]
    li_col = gc[:, 8:12] + ibr_ref[...]
    lf_col = _log_sigmoid(gc[:, 12:16] + fbr_ref[...])
    gr = gr_ref[...]
    li_row = gr[8:12, :] + ibc_ref[...]
    lf_row = _log_sigmoid(gr[12:16, :] + fbc_ref[...])
    b_row = _mm2(jnp.concatenate([lf_row, jnp.zeros_like(lf_row)], axis=0), _block_tri_t(tb, CHUNK))

    same, lower, _ = _bd_masks()
    tri = _tri(CHUNK, bf16)

    for ch in range(tb // CHUNK):
        lo = ch * CHUNK
        qc, kc, vc = q[lo:lo + CHUNK], k[lo:lo + CHUNK], v[lo:lo + CHUNK]
        b_col = _mm2l(tri, lf_col[lo:lo + CHUNK])
        b_stack = _stack_cols(b_col)
        d_mat = jnp.where(lower, b_stack - _cat_rows(b_row, lo) + _cat_rows(li_row, lo), NEG)
        m_row = m_scr[...]
        m_stack = jnp.concatenate(
            [jnp.broadcast_to(m_row[:, h:h + 1], (CHUNK, 1)) for h in range(N_HEADS)], axis=0)
        inter = b_stack + m_stack
        m_t = jnp.maximum(inter, jnp.max(d_mat, axis=-1, keepdims=True))
        w_inter = jnp.exp(inter - m_t)
        qsm = _head_stack(qc)
        ksm = _head_stack(kc)
        pmat = _mm1(qsm, ksm, NT) * jnp.exp(d_mat - m_t)
        c_bd = c_scr[...]
        n_row = n_scr[...]
        num = w_inter * _mm1(qsm, c_bd) + _mm1(pmat, _head_stack(vc))
        den = w_inter * jnp.sum(qsm * n_row, axis=-1, keepdims=True) + jnp.sum(pmat, axis=-1, keepdims=True)
        h_sm = num / jnp.maximum(jnp.abs(den), jnp.exp(-m_t))
        o_ref[lo:lo + CHUNK, :] = _fold_heads(h_sm)
        m_new = jnp.concatenate(
            [m_t[h * CHUNK + CHUNK - 1:h * CHUNK + CHUNK, :] for h in range(N_HEADS)], axis=1)
        b_last = b_col[CHUNK - 1:CHUNK, :]
        w_end = jnp.exp(b_last - b_col + li_col[lo:lo + CHUNK] - m_new)
        d0 = jnp.exp(b_last + m_row - m_new)
        kw = kc * _expand_cols(w_end)
        d0_stack = jnp.concatenate(
            [jnp.broadcast_to(d0[:, h:h + 1], (HEAD_W, 1)) for h in range(N_HEADS)], axis=0)
        c_scr[...] = d0_stack * c_bd + jnp.where(same, _mm1(kw, vc, TN), 0.0)
        n_scr[...] = _expand_cols(d0) * n_row + jnp.sum(kw, axis=0, keepdims=True)
        m_scr[...] = m_new

    @pl.when(blk == pl.num_programs(1) - 1)
    def _():
        c_out_ref[...] = c_scr[...]
        n_out_ref[...] = n_scr[...]
        m_out_ref[...] = m_scr[...]


def _mlstm_prompt(p, gt, i_bias, f_bias, n_batch, seq):
    tb = min(256, seq)
    nb = seq // tb
    blk = lambda col: pl.BlockSpec((tb, GROUP_W), lambda b, i: (b * nb + i, col))
    r14 = lambda z: z.astype(f32).reshape(1, N_HEADS)
    c41 = lambda z: z.astype(f32).reshape(N_HEADS, 1)
    small = lambda shape: pl.BlockSpec(shape, lambda b, i: (0, 0))
    o, c_bd, n_row, m_row = pl.pallas_call(
        functools.partial(_mlstm_kernel, tb=tb),
        grid=(n_batch, nb),
        in_specs=[blk(11), blk(12), blk(13),
                  pl.BlockSpec((tb, 128), lambda b, i: (b * nb + i, GATE_COL // 128)),
                  pl.BlockSpec((16, tb), lambda b, i: (0, b * nb + i)),
                  small((1, N_HEADS)), small((1, N_HEADS)), small((N_HEADS, 1)), small((N_HEADS, 1))],
        out_specs=[
            pl.BlockSpec((tb, GROUP_W), lambda b, i: (b * nb + i, 0)),
            pl.BlockSpec((None, GROUP_W, GROUP_W), lambda b, i: (b, 0, 0)),
            pl.BlockSpec((None, 1, GROUP_W), lambda b, i: (b, 0, 0)),
            pl.BlockSpec((None, 1, N_HEADS), lambda b, i: (b, 0, 0)),
        ],
        out_shape=[jax.ShapeDtypeStruct((n_batch * seq, GROUP_W), f32),
                   jax.ShapeDtypeStruct((n_batch, GROUP_W, GROUP_W), f32),
                   jax.ShapeDtypeStruct((n_batch, 1, GROUP_W), f32),
                   jax.ShapeDtypeStruct((n_batch, 1, N_HEADS), f32)],
        scratch_shapes=[pltpu.VMEM((GROUP_W, GROUP_W), f32), pltpu.VMEM((1, GROUP_W), f32),
                        pltpu.VMEM((1, N_HEADS), f32)],
        compiler_params=_cparams("arbitrary", "arbitrary"),
        name="mlstm_prompt",
    )(p, p, p, p, gt, r14(i_bias), r14(f_bias), c41(i_bias), c41(f_bias))
    return (o, _bd_diag(c_bd), n_row.reshape(n_batch, N_HEADS, HEAD_W), m_row.reshape(n_batch, N_HEADS))


def _gdn_dec_prep_kernel(u_ref, buf_ref, cw_ref, q_ref, k_ref, v_ref):
    w = cw_ref[...]
    conv = u_ref[...] * w[3:4]
    for jj in range(CONV_W - 1):
        conv = conv + buf_ref[jj] * w[jj:jj + 1]
    qkv = _silu(conv)

    def l2n(x):
        return x * lax.rsqrt(_group_sum(x * x, HEAD_W) + EPS)

    q_ref[...] = l2n(qkv[:, 0:GROUP_W]) * (HEAD_W ** -0.5)
    k_ref[...] = l2n(qkv[:, GROUP_W:2 * GROUP_W])
    v_ref[...] = qkv[:, 2 * GROUP_W:3 * GROUP_W]


def _gdn_dec_prep(p, conv_buf, conv_w):
    n_b = p.shape[0]
    out = jax.ShapeDtypeStruct((n_b, GROUP_W), f32)
    return pl.pallas_call(
        _gdn_dec_prep_kernel,
        grid=(1,),
        in_specs=[pl.BlockSpec((n_b, 3 * GROUP_W), lambda i: (0, 0)),
                  pl.BlockSpec((CONV_W - 1, n_b, 3 * GROUP_W), lambda i: (0, 0, 0)),
                  pl.BlockSpec((CONV_W, 3 * GROUP_W), lambda i: (0, 0))],
        out_specs=[pl.BlockSpec((n_b, GROUP_W), lambda i: (0, 0))] * 3,
        out_shape=[out, out, out],
        compiler_params=_cparams("arbitrary"),
        name="gdn_dec_prep",
    )(p, jnp.swapaxes(conv_buf.astype(f32), 0, 1), conv_w.astype(f32))


def _rec_decode_kernel(cols_ref, vrows_ref, scal_ref, sg_ref, sh_ref, sc_ref,
                       o_ref, sg_out, sh_out, sc_out, sn_out, sm_out):
    cols = cols_ref[...]
    vrows = vrows_ref[...]
    scal = scal_ref[...]
    col = lambda k: cols[:, :, k:k + 1]
    vrow = lambda k: vrows[:, k:k + 1, :]
    sca = lambda k: scal[:, :, k:k + 1]

    q, k, v = col(0), col(1), vrow(0)
    s = sg_ref[...]
    g = -jnp.exp(sca(2)) * _softplus(sca(0) + sca(3))
    eg = jnp.exp(g)
    beta = jax.nn.sigmoid(sca(1))
    ks = jnp.sum(k * s, axis=1, keepdims=True)
    qs = jnp.sum(q * s, axis=1, keepdims=True)
    u = beta * (v - eg * ks)
    qk = jnp.sum(q * k, axis=1, keepdims=True)
    o_ref[:, 0:1, :] = eg * qs + qk * u
    sg_out[...] = eg * s + k * u

    lb = col(6)
    z = col(3)
    logf = jnp.log(lb + (1.0 - lb) * jax.nn.sigmoid(z))
    kc = (1.0 - lb) * jax.nn.sigmoid(-z)
    qc = _silu(col(2))
    vc = vrow(1)
    sh = sh_ref[...]
    ef = jnp.exp(logf)
    o_ref[:, 1:2, :] = (jnp.sum((qc * ef) * sh, axis=1, keepdims=True)
                        + jnp.sum(qc * kc, axis=1, keepdims=True) * vc)
    sh_out[...] = ef * sh + kc * vc

    qd = col(4)
    kd = col(5) * (HEAD_W ** -0.5)
    vd = vrow(2)
    li = sca(4) + sca(6)
    lf = _log_sigmoid(sca(5) + sca(7))
    m0 = sca(8)
    cs = sc_ref[...]
    n0 = col(7)
    inter = lf + m0
    m_t = jnp.maximum(inter, li)
    w_inter = jnp.exp(inter - m_t)
    qkd = jnp.sum(qd * kd, axis=1, keepdims=True) * jnp.exp(li - m_t)
    num = w_inter * jnp.sum(qd * cs, axis=1, keepdims=True) + qkd * vd
    den = w_inter * jnp.sum(qd * n0, axis=1, keepdims=True) + qkd
    o_ref[:, 2:3, :] = num / jnp.maximum(jnp.abs(den), jnp.exp(-m_t))
    w_end = jnp.exp(li - m_t)
    d0 = jnp.exp(lf + m0 - m_t)
    sc_out[...] = d0 * cs + (w_end * kd) * vd
    sn_out[...] = d0 * n0 + w_end * kd
    sm_out[...] = m_t


def _rec_decode(p, gq, gk, gv, a_log, dt_bias, lb, i_bias, f_bias, s_gdn, s_hgrn, s_c, s_n, s_m):
    n_b = p.shape[0]
    rows = n_b * N_HEADS
    rb = min(16, rows)
    rw = lambda z: z.astype(f32).reshape(rows, HEAD_W)
    blockp = lambda b: rw(p[:, b * GROUP_W:(b + 1) * GROUP_W])
    per_head = lambda z: jnp.tile(z.astype(f32), n_b)
    gate = lambda c: p[:, GATE_COL + c:GATE_COL + c + N_HEADS].reshape(rows)
    lb_rows = jnp.tile(lb.astype(f32).reshape(N_HEADS, HEAD_W), (n_b, 1))
    cols = jnp.stack([rw(gq), rw(gk), blockp(7), blockp(8), blockp(11), blockp(12), lb_rows, rw(s_n)], axis=-1)
    vrows = jnp.stack([rw(gv), blockp(9), blockp(13)], axis=1)
    scal = jnp.stack([gate(0), gate(4), per_head(a_log), per_head(dt_bias), gate(8), gate(12),
                      per_head(i_bias), per_head(f_bias), s_m.astype(f32).reshape(rows)], axis=-1).reshape(rows, 1, 9)
    st = lambda z: z.astype(f32).reshape(rows, HEAD_W, HEAD_W)
    args = [cols, vrows, scal, st(s_gdn), st(s_hgrn), st(s_c)]

    def spec(a):
        return pl.BlockSpec((rb,) + a.shape[1:], lambda i: (i, 0, 0))

    o_st = jax.ShapeDtypeStruct((rows, HEAD_W, HEAD_W), f32)
    outs = [jax.ShapeDtypeStruct((rows, 3, HEAD_W), f32), o_st, o_st, o_st,
            jax.ShapeDtypeStruct((rows, HEAD_W, 1), f32), jax.ShapeDtypeStruct((rows, 1, 1), f32)]
    o, sg, sh, sc, sn, sm = pl.pallas_call(
        _rec_decode_kernel,
        grid=(rows // rb,),
        in_specs=[spec(a) for a in args],
        out_specs=[spec(a) for a in outs],
        out_shape=outs,
        compiler_params=_cparams("arbitrary"),
        name="rec_decode",
    )(*args)
    s4 = lambda z: z.reshape(n_b, N_HEADS, HEAD_W, HEAD_W)
    o2 = lambda k: o[:, k, :].reshape(n_b, GROUP_W)
    return (o2(0), s4(sg), o2(1), s4(sh), o2(2), s4(sc),
            sn.reshape(n_b, N_HEADS, HEAD_W), sm.reshape(n_b, N_HEADS))


def _permute_w_in(w):
    d_in = w.shape[1]
    a_gate0 = 3 * GROUP_W
    d_gate0 = d_in - GROUP_W - 8
    main = jnp.concatenate([w[:, 0:a_gate0], w[:, a_gate0 + 8:d_gate0], w[:, d_gate0 + 8:]], axis=1)
    gates = jnp.concatenate([w[:, a_gate0:a_gate0 + 8], w[:, d_gate0:d_gate0 + 8]], axis=1)
    pad = jnp.zeros((w.shape[0], P_COLS - main.shape[1] - 16), w.dtype)
    return jnp.concatenate([main, gates, pad], axis=1).astype(bf16), gates.T.astype(bf16)


def kernel(x_prompt, x_sample, page_table, cache_k, cache_v, state_gdn_conv, state_gdn, state_hgrn, state_mlstm_C, state_mlstm_n, state_mlstm_m, attn_norm_g, w_in, gdn_conv_w, gdn_a_log, gdn_dt_bias, gdn_norm_g, diff_qk_norm_g, diff_lambda, diff_subln_g, rel_bias, hgrn_lb_logits, hgrn_norm_g, mlstm_i_bias, mlstm_f_bias, mlstm_norm_g, w_out, ffn_norm_g, ffn_w_gate, ffn_w_up, ffn_w_down, moe_router, moe_w_gate, moe_w_up, moe_w_down):
    depth = w_in.shape[0]
    n_bp, seq, _ = x_prompt.shape
    n_bs = x_sample.shape[0]
    n_pool, page = cache_k.shape[1], cache_k.shape[2]
    dt = x_prompt.dtype

    lb_p = jax.nn.softmax(hgrn_lb_logits.astype(f32), axis=0)
    lb_cum = jnp.cumsum(lb_p, axis=0)
    hgrn_lb = lb_cum - lb_cum[0:1]
    cache_k4 = jnp.transpose(cache_k, (0, 1, 3, 4, 5, 2)).reshape(depth, n_pool, GROUP_W, page)
    cache_v4 = jnp.transpose(cache_v, (0, 1, 3, 4, 2)).reshape(depth, n_pool, GROUP_W, page)

    xp = x_prompt.reshape(n_bp * seq, D_MODEL)
    xs = x_sample.reshape(n_bs, D_MODEL)
    outs_p, outs_s = [], []
    for l in range(depth):
        w_perm, w_gate_t = _permute_w_in(w_in[l])
        w_out_b = w_out[l].astype(bf16)
        gains = jnp.stack([jnp.tile(g.astype(f32), N_HEADS) for g in
                           (gdn_norm_g[l], diff_subln_g[l], hgrn_norm_g[l], mlstm_norm_g[l])])
        lam_init = 0.8 - 0.6 * math.exp(-0.3 * l)
        lam32 = diff_lambda[l].astype(f32)
        lam = jnp.exp(jnp.sum(lam32[0] * lam32[1])) - jnp.exp(jnp.sum(lam32[2] * lam32[3])) + lam_init
        if l % 2 == 0:
            ffn_w = (ffn_w_gate[l // 2].astype(bf16), ffn_w_up[l // 2].astype(bf16), ffn_w_down[l // 2].astype(bf16))
        else:
            router_pad = jnp.pad(moe_router[l // 2].astype(bf16), ((0, 0), (0, 128 - N_EXPERTS)))
            moe_w = (moe_w_gate[l // 2].astype(bf16), moe_w_up[l // 2].astype(bf16), moe_w_down[l // 2].astype(bf16))

        def channel_mix(x):
            if l % 2 == 0:
                return _ffn(x, ffn_norm_g[l], *ffn_w)
            return _moe(x, ffn_norm_g[l], router_pad, *moe_w)

        p, gt = _inproj(xp, attn_norm_g[l], w_perm, w_gate_t)
        qnt, kn, vt = _bprep(p, diff_qk_norm_g[l], True)
        ob = _attn_prompt(qnt, kn, vt, lam, rel_bias, n_bp, seq)
        oa, s_gdn = _gdn_prompt(p, gt, gdn_conv_w[l], gdn_a_log[l], gdn_dt_bias[l], n_bp, seq)
        oc, s_hgrn = _hgrn_prompt(p, hgrn_lb[l], n_bp, seq)
        od, s_c, s_n, s_m = _mlstm_prompt(p, gt, mlstm_i_bias[l], mlstm_f_bias[l], n_bp, seq)
        xp = _outproj(oa, ob, oc, od, p, xp, gains, w_out_b, 1.0 - lam_init)
        xp = channel_mix(xp)
        p3 = p.reshape(n_bp, seq, P_COLS)
        outs_p.append((
            kn.reshape(n_bp, seq, N_HEADS, 2, DKB).astype(dt),
            p3[:, :, 6 * GROUP_W:7 * GROUP_W].reshape(n_bp, seq, N_HEADS, HEAD_W).astype(dt),
            p3[:, seq - (CONV_W - 1):, 0:3 * GROUP_W].astype(dt),
            s_gdn.astype(dt), s_hgrn.astype(dt), s_c.astype(dt), s_n.astype(dt), s_m.astype(dt)))

        p, gt = _inproj(xs, attn_norm_g[l], w_perm, w_gate_t)
        qn, kn = _bprep(p, diff_qk_norm_g[l], False)
        vn = p[:, 6 * GROUP_W:7 * GROUP_W]
        ob = _attn_decode(qn, kn, vn, page_table, cache_k4, cache_v4, l, lam, rel_bias)
        u = p[:, 0:3 * GROUP_W]
        gq, gk, gv = _gdn_dec_prep(u, state_gdn_conv[l], gdn_conv_w[l])
        oa, s_gdn, oc, s_hgrn, od, s_c, s_n, s_m = _rec_decode(
            p, gq, gk, gv, gdn_a_log[l], gdn_dt_bias[l], hgrn_lb[l], mlstm_i_bias[l], mlstm_f_bias[l],
            state_gdn[l], state_hgrn[l], state_mlstm_C[l], state_mlstm_n[l], state_mlstm_m[l])
        xs = _outproj(oa, ob, oc, od, p, xs, gains, w_out_b, 1.0 - lam_init)
        xs = channel_mix(xs)
        conv_new = jnp.concatenate([state_gdn_conv[l][:, 1:].astype(dt), u[:, None, :].astype(dt)], axis=1)
        outs_s.append((
            kn.reshape(n_bs, 1, N_HEADS, 2, DKB).astype(dt),
            vn.reshape(n_bs, 1, N_HEADS, HEAD_W).astype(dt),
            conv_new, s_gdn.astype(dt), s_hgrn.astype(dt), s_c.astype(dt), s_n.astype(dt), s_m.astype(dt)))

    kp, vp, convp, gdnp, hgrnp, mcp, mnp_, mmp = [jnp.stack(z) for z in zip(*outs_p)]
    ks_, vs_, convs, gdns, hgrns, mcs, mns, mms = [jnp.stack(z) for z in zip(*outs_s)]
    return (xp.reshape(n_bp, seq, D_MODEL), xs.reshape(n_bs, 1, D_MODEL), kp, vp, ks_, vs_, convp, convs,
            gdnp, gdns, hgrnp, hgrns, mcp, mcs, mnp_, mns, mmp, mms)
```
